```python
import jax, jax.numpy as jnp
from jax import lax
import numpy as np

D_MODEL = 2048
BATCH = 1
SEQ = 8192
DEPTH = 2

F32 = jnp.float32
N_META = 16
CHUNK = 64
D_MIX = D_MODEL
D_POOL = D_MODEL // 4
POOL_WINDOWS = (2, 4, 8, 16)
POOL_GROUP = D_POOL // len(POOL_WINDOWS)
D_MLSTM = 3 * D_MODEL // 8
MLSTM_HEADS = 6
MLSTM_HEAD_DIM = D_MLSTM // MLSTM_HEADS
CONV_K = 4
D_HGRN = D_MIX - D_POOL - D_MLSTM
HGRN_HEADS = 6
HGRN_HEAD_DIM = D_HGRN // HGRN_HEADS
MIX_SIZES = (D_POOL, D_MLSTM, D_MLSTM, D_MLSTM, D_MLSTM, MLSTM_HEADS, MLSTM_HEADS,
             D_HGRN, D_HGRN, D_HGRN, D_HGRN)
D_IN = sum(MIX_SIZES)
N_GROUPS = 4
EXPERTS_PER_GROUP = 8
N_EXPERTS = N_GROUPS * EXPERTS_PER_GROUP
TOP_K = 2
D_EXPERT = D_MODEL // 4
ALPHA = (2 * DEPTH) ** 0.25
BETA = (8 * DEPTH) ** -0.25
LN_EPS = 1e-5

kernel_name = "hymba_pool_mlstm_hgrn2_hmoe_deepnorm"


def layer_norm(x, g, b):
    xf = x.astype(F32)
    mu = xf.mean(-1, keepdims=True)
    var = jnp.square(xf - mu).mean(-1, keepdims=True)
    return ((xf - mu) * lax.rsqrt(var + LN_EPS) * g + b).astype(x.dtype)


def split_heads(t, n_heads):
    b, l, d = t.shape
    return t.reshape(b, l, n_heads, d // n_heads).transpose(0, 2, 1, 3)


def merge_heads(t):
    b, h, l, d = t.shape
    return t.transpose(0, 2, 1, 3).reshape(b, l, h * d)


def causal_conv(x, w, b):
    y = lax.conv_general_dilated(x, w[:, None, :], window_strides=(1,), padding=[(CONV_K - 1, 0)],
                                 dimension_numbers=('NWC', 'WIO', 'NWC'), feature_group_count=x.shape[-1])
    return y + b


def chunked_scan(step, init, seqs):
    carry, h_meta = step(init, tuple(s[:, :, :N_META] for s in seqs))

    def to_chunks(s):
        b, h, l = s.shape[:3]
        s = s[:, :, N_META:].reshape((b, h, (l - N_META) // CHUNK, CHUNK) + s.shape[3:])
        return jnp.moveaxis(s, 2, 0)

    _, h = lax.scan(step, carry, tuple(to_chunks(s) for s in seqs))
    h = jnp.moveaxis(h, 0, 2)
    h = h.reshape(h.shape[:2] + (-1,) + h.shape[4:])
    return jnp.concatenate([h_meta, h], axis=2)


def pool_mixer(u, pool_w, pool_scale):
    uf = u.astype(F32)
    l = u.shape[1]
    pos = jnp.arange(1, l + 1, dtype=F32)[None, :, None]
    diffs = []
    for gi, w in enumerate(POOL_WINDOWS):
        ug = uf[..., gi * POOL_GROUP:(gi + 1) * POOL_GROUP]
        cs = jnp.cumsum(jnp.pad(ug, ((0, 0), (w, 0), (0, 0))), axis=1)
        mean = (cs[:, w:] - cs[:, :-w]) / jnp.minimum(pos, float(w))
        diffs.append(mean - ug)
    d = jnp.stack(diffs, axis=2)
    y = jnp.einsum('blgc,gcd->blgd', d, pool_w.astype(F32))
    return y.reshape(u.shape) * pool_scale


def mlstm_chunk(carry, inp):
    c_mat, n_vec, m = carry
    q, k, v, ig, lf = inp
    c = q.shape[2]
    g = jnp.cumsum(lf, axis=-1)
    causal = jnp.tril(jnp.ones((c, c), bool))
    dmat = jnp.where(causal, g[..., :, None] - g[..., None, :] + ig[..., None, :], -jnp.inf)
    m_inter = g + m[..., None]
    m_t = jnp.maximum(dmat.max(-1), m_inter)
    s = jnp.einsum('bhtd,bhsd->bhts', q, k) * jnp.exp(dmat - m_t[..., None])
    inter = jnp.exp(m_inter - m_t)
    num = s @ v + inter[..., None] * jnp.einsum('bhtd,bhde->bhte', q, c_mat)
    den = s.sum(-1) + inter * jnp.einsum('bhtd,bhd->bht', q, n_vec)
    h = num / jnp.maximum(jnp.abs(den), jnp.exp(-m_t))[..., None]
    g_last = g[..., -1]
    w = g_last[..., None] - g + ig
    m_new = jnp.maximum(g_last + m, w.max(-1))
    decay = jnp.exp(g_last + m - m_new)
    wk = k * jnp.exp(w - m_new[..., None])[..., None]
    c_new = decay[..., None, None] * c_mat + jnp.einsum('bhsd,bhse->bhde', wk, v)
    n_new = decay[..., None] * n_vec + wk.sum(2)
    return (c_new, n_new, m_new), h


def mlstm_mixer(mq, mk, mv, mo, mi, mf, conv_w, conv_b, ig_b, fg_b, norm_g):
    qk = jax.nn.silu(causal_conv(jnp.concatenate([mq, mk], -1), conv_w, conv_b))
    q = split_heads(qk[..., :D_MLSTM].astype(F32), MLSTM_HEADS)
    k = split_heads(qk[..., D_MLSTM:].astype(F32), MLSTM_HEADS) * (MLSTM_HEAD_DIM ** -0.5)
    v = split_heads(mv.astype(F32), MLSTM_HEADS)
    ig = jnp.swapaxes((mi + ig_b).astype(F32), 1, 2)
    lf = jax.nn.log_sigmoid(jnp.swapaxes((mf + fg_b).astype(F32), 1, 2))
    b = q.shape[0]
    init = (jnp.zeros((b, MLSTM_HEADS, MLSTM_HEAD_DIM, MLSTM_HEAD_DIM), F32),
            jnp.zeros((b, MLSTM_HEADS, MLSTM_HEAD_DIM), F32),
            jnp.full((b, MLSTM_HEADS), -jnp.inf, F32))
    h = chunked_scan(mlstm_chunk, init, (q, k, v, ig, lf))
    mu = h.mean(-1, keepdims=True)
    var = jnp.square(h - mu).mean(-1, keepdims=True)
    h = merge_heads((h - mu) * lax.rsqrt(var + LN_EPS))
    return h * norm_g * jax.nn.sigmoid(mo.astype(F32))


def hgrn_chunk(s_mat, inp):
    q, k, v, lf = inp
    c = q.shape[2]
    b = jnp.cumsum(lf, axis=2)
    causal = jnp.tril(jnp.ones((c, c), bool))[:, :, None]
    decay = jnp.exp(jnp.where(causal, b[:, :, :, None, :] - b[:, :, None, :, :], -jnp.inf))
    a = jnp.einsum('bhtd,bhsd,bhtsd->bhts', q, k, decay)
    o = a @ v + jnp.einsum('bhtd,bhde->bhte', q * jnp.exp(b), s_mat)
    b_last = b[:, :, -1:]
    s_new = jnp.exp(b_last[:, :, 0])[..., None] * s_mat + jnp.einsum('bhsd,bhse->bhde', k * jnp.exp(b_last - b), v)
    return s_new, o


def hgrn_mixer(hq, hf, hi, hg, lb, norm_g):
    zf = hf.astype(F32)
    log_f = jnp.logaddexp(jnp.log(lb), jnp.log1p(-lb) + jax.nn.log_sigmoid(zf))
    k = (1.0 - lb) * jax.nn.sigmoid(-zf)
    q = split_heads(jax.nn.silu(hq.astype(F32)), HGRN_HEADS)
    v = split_heads(hi.astype(F32), HGRN_HEADS)
    b = q.shape[0]
    init = jnp.zeros((b, HGRN_HEADS, HGRN_HEAD_DIM, HGRN_HEAD_DIM), F32)
    o = chunked_scan(hgrn_chunk, init, (q, split_heads(k, HGRN_HEADS), v, split_heads(log_f, HGRN_HEADS)))
    o = o * lax.rsqrt(jnp.square(o).mean(-1, keepdims=True) + LN_EPS)
    return merge_heads(o) * norm_g * jax.nn.silu(hg.astype(F32))


def hybrid_mixer(h, w_in, conv_w, conv_b, ig_b, fg_b, mlstm_norm_g, pool_w, pool_scale, lb, hgrn_norm_g, w_out):
    p = h @ w_in
    splits = np.cumsum(MIX_SIZES)[:-1].tolist()
    u_pool, mq, mk, mv, mo, mi, mf, hq, hf, hi, hg = jnp.split(p, splits, axis=-1)
    y_pool = pool_mixer(u_pool, pool_w, pool_scale)
    y_m = mlstm_mixer(mq, mk, mv, mo, mi, mf, conv_w, conv_b, ig_b, fg_b, mlstm_norm_g)
    y_h = hgrn_mixer(hq, hf, hi, hg, lb, hgrn_norm_g)
    y = jnp.concatenate([y_pool, y_m, y_h], -1).astype(h.dtype)
    return y @ w_out


def hier_moe(h, w_rg, b_rg, w_re, b_re, w1, w3, w2):
    bsz, l, d = h.shape
    xt = h.reshape(bsz * l, d)
    pg = jax.nn.softmax((xt @ w_rg).astype(F32) + b_rg, axis=-1)
    p_grp, g_idx = lax.top_k(pg, 1)
    le = ((xt @ w_re).astype(F32) + b_re).reshape(-1, N_GROUPS, EXPERTS_PER_GROUP)
    le_sel = jnp.take_along_axis(le, g_idx[:, :, None], axis=1)[:, 0]
    p_exp, e_idx = lax.top_k(jax.nn.softmax(le_sel, axis=-1), TOP_K)
    wts = p_grp * p_exp / p_exp.sum(-1, keepdims=True)
    expert_id = g_idx * EXPERTS_PER_GROUP + e_idx
    combine = (jax.nn.one_hot(expert_id, N_EXPERTS, dtype=F32) * wts[..., None]).sum(1)
    y = jnp.zeros((xt.shape[0], d), F32)
    for e in range(N_EXPERTS):
        he = jax.nn.silu(xt @ w1[e]) * (xt @ w3[e])
        y = y + combine[:, e:e + 1] * (he @ w2[e])
    return y.reshape(bsz, l, d).astype(h.dtype)


def setup_inputs(seed: int = 0) -> dict:
    key = jax.random.key(seed)
    ks = jax.random.split(key, 24)
    n = lambda k, shape, s: jax.random.normal(k, shape, F32) * s
    gain = lambda k, shape: 1.0 + n(k, shape, 0.02)
    fg_base = jnp.linspace(3.0, 6.0, MLSTM_HEADS, dtype=F32)
    return {
        "x": n(ks[0], (BATCH, SEQ, D_MODEL), 1.0),
        "meta_tokens": n(ks[1], (N_META, D_MODEL), 1.0),
        "emb_ln_g": gain(ks[2], (D_MODEL,)),
        "emb_ln_b": n(ks[3], (D_MODEL,), 0.02),
        "hgrn_lb_logits": n(ks[4], (DEPTH, D_HGRN), 0.5),
        "w_in": n(ks[5], (DEPTH, D_MODEL, D_IN), D_MODEL ** -0.5),
        "conv_w": n(ks[6], (DEPTH, CONV_K, 2 * D_MLSTM), CONV_K ** -0.5),
        "conv_b": n(ks[7], (DEPTH, 2 * D_MLSTM), 0.02),
        "ig_b": n(ks[8], (DEPTH, MLSTM_HEADS), 0.1),
        "fg_b": fg_base + n(ks[9], (DEPTH, MLSTM_HEADS), 0.1),
        "mlstm_norm_g": gain(ks[10], (DEPTH, D_MLSTM)),
        "pool_w": n(ks[11], (DEPTH, len(POOL_WINDOWS), POOL_GROUP, POOL_GROUP), POOL_GROUP ** -0.5),
        "pool_scale": gain(ks[12], (DEPTH, D_POOL)),
        "hgrn_norm_g": gain(ks[13], (DEPTH, D_HGRN)),
        "w_out": n(ks[14], (DEPTH, D_MIX, D_MODEL), BETA * D_MIX ** -0.5),
        "ln1_g": gain(ks[15], (DEPTH, D_MODEL)),
        "ln1_b": n(ks[16], (DEPTH, D_MODEL), 0.02),
        "w_router_group": n(ks[17], (DEPTH, D_MODEL, N_GROUPS), D_MODEL ** -0.5),
        "b_router_group": n(ks[18], (DEPTH, N_GROUPS), 0.01),
        "w_router_expert": n(ks[19], (DEPTH, D_MODEL, N_EXPERTS), D_MODEL ** -0.5),
        "b_router_expert": n(ks[20], (DEPTH, N_EXPERTS), 0.01),
        "w1": n(ks[21], (DEPTH, N_EXPERTS, D_MODEL, D_EXPERT), D_MODEL ** -0.5),
        "w3": n(ks[22], (DEPTH, N_EXPERTS, D_MODEL, D_EXPERT), D_MODEL ** -0.5),
        "w2": n(ks[23], (DEPTH, N_EXPERTS, D_EXPERT, D_MODEL), BETA * D_EXPERT ** -0.5),
        "ln2_g": gain(jax.random.fold_in(key, 101), (DEPTH, D_MODEL)),
        "ln2_b": n(jax.random.fold_in(key, 102), (DEPTH, D_MODEL), 0.02),
    }


def reference(x, meta_tokens, emb_ln_g, emb_ln_b, hgrn_lb_logits, w_in, conv_w, conv_b, ig_b, fg_b,
              mlstm_norm_g, pool_w, pool_scale, hgrn_norm_g, w_out, ln1_g, ln1_b,
              w_router_group, b_router_group, w_router_expert, b_router_expert, w1, w3, w2,
              ln2_g, ln2_b):
    bsz = x.shape[0]
    meta = jnp.broadcast_to(meta_tokens[None].astype(x.dtype), (bsz, N_META, D_MODEL))
    h = layer_norm(jnp.concatenate([meta, x], axis=1), emb_ln_g, emb_ln_b)
    lb_all = jnp.cumsum(jax.nn.softmax(hgrn_lb_logits.astype(F32), axis=0), axis=0)
    lb_all = lb_all - lb_all[0]
    for l in range(DEPTH):
        y = hybrid_mixer(h, w_in[l], conv_w[l], conv_b[l], ig_b[l], fg_b[l], mlstm_norm_g[l],
                         pool_w[l], pool_scale[l], lb_all[l], hgrn_norm_g[l], w_out[l])
        h = layer_norm(ALPHA * h + y, ln1_g[l], ln1_b[l])
        y = hier_moe(h, w_router_group[l], b_router_group[l], w_router_expert[l], b_router_expert[l],
                     w1[l], w3[l], w2[l])
        h = layer_norm(ALPHA * h + y, ln2_g[l], ln2_b[l])
    return h[:, N_META:]
```

```python
import functools
import math

import jax
import jax.numpy as jnp
import numpy as np
from jax import lax
from jax.experimental import pallas as pl
from jax.experimental.pallas import tpu as pltpu

F32 = jnp.float32
BF16 = jnp.bfloat16

D_MODEL = 2048
SEQ = 8192
DEPTH = 2
N_META = 16
CHUNK = 64
D_POOL = D_MODEL // 4
POOL_WINDOWS = (2, 4, 8, 16)
POOL_GROUP = D_POOL // len(POOL_WINDOWS)
D_MLSTM = 3 * D_MODEL // 8
N_HEADS = 6
HEAD_DIM = D_MLSTM // N_HEADS
CONV_K = 4
D_HGRN = D_MODEL - D_POOL - D_MLSTM
N_GROUPS = 4
EXPERTS_PER_GROUP = 8
N_EXPERTS = N_GROUPS * EXPERTS_PER_GROUP
D_EXPERT = D_MODEL // 4
ALPHA = (2 * DEPTH) ** 0.25
LN_EPS = 1e-5
NEG_INF = float("-inf")

LANES = 128
LP = SEQ + CHUNK
META_ROW0 = SEQ
META_PAD = CHUNK - N_META
N_CHUNKS = LP // CHUNK
LP_EMBED = 17 * 512

COL_MQK = 0
COL_MV = 2 * D_MLSTM
COL_MO = COL_MV + D_MLSTM
COL_HQ = COL_MO + D_MLSTM
COL_HF = COL_HQ + D_HGRN
COL_HI = COL_HF + D_HGRN
COL_HG = COL_HI + D_HGRN
COL_POOL = COL_HG + D_HGRN
COL_GATE = COL_POOL + D_POOL
P_COLS = COL_GATE + 2 * LANES

TM_BIG = 1376
TN_IN = 768
TM_MID = 688
TM_LN = 192
TM_FINAL = 256
TM_EXPERT = 256
N_ASSIGN = 2 * LP
SLOT_STRIDE = -(-LP // math.lcm(TM_LN, TM_FINAL)) * math.lcm(TM_LN, TM_FINAL)
Y_ROWS = SLOT_STRIDE + LP
assert 0 < SLOT_STRIDE - LP <= TM_EXPERT
N_EXPERT_TILES = -(-(N_ASSIGN + N_EXPERTS * (TM_EXPERT - 1)) // TM_EXPERT)
VMEM_LIMIT = 56 * 1024 * 1024


def _cparams(sem):
    return pltpu.CompilerParams(dimension_semantics=sem, vmem_limit_bytes=VMEM_LIMIT)


def _sigmoid(x):
    return 1.0 / (1.0 + jnp.exp(-x))


def _log_sigmoid(x):
    return jnp.minimum(x, 0.0) - jnp.log1p(jnp.exp(-jnp.abs(x)))


def _layer_norm_rows(x, g, b):
    mu = jnp.mean(x, axis=-1, keepdims=True)
    xc = x - mu
    var = jnp.mean(xc * xc, axis=-1, keepdims=True)
    return xc * lax.rsqrt(var + LN_EPS) * g + b


def _dot(a, b):
    return jnp.dot(a, b, preferred_element_type=F32)


def _dot_nt(a, b):
    return lax.dot_general(a, b, (((1,), (1,)), ((), ())), preferred_element_type=F32)


def _dot_tn(a, b):
    return lax.dot_general(a, b, (((0,), (0,)), ((), ())), preferred_element_type=F32)


def _split_bf16(x):
    hi = x.astype(BF16)
    lo = (x - hi.astype(F32)).astype(BF16)
    return hi, lo


def _chunk_cumsum(x):
    r = lax.broadcasted_iota(jnp.int32, (CHUNK, CHUNK), 0)
    c = lax.broadcasted_iota(jnp.int32, (CHUNK, CHUNK), 1)
    tri = jnp.where(r >= c, 1.0, 0.0).astype(BF16)
    hi, lo = _split_bf16(x)
    return _dot(tri, hi) + _dot(tri, lo)


def _embed_kernel(x_ref, meta_ref, g_ref, b_ref, h_ref, hb_ref):
    i = pl.program_id(0)
    g = g_ref[...]
    b = b_ref[...]

    @pl.when(i < SEQ // 512)
    def _():
        y = _layer_norm_rows(x_ref[...], g, b)
        h_ref[...] = y
        hb_ref[...] = y.astype(BF16)

    @pl.when(i == SEQ // 512)
    def _():
        y = _layer_norm_rows(meta_ref[...], g, b)
        row = lax.broadcasted_iota(jnp.int32, (CHUNK, D_MODEL), 0)
        y = jnp.where(row >= META_PAD, y, 0.0)
        h_ref[0:CHUNK, :] = y
        hb_ref[0:CHUNK, :] = y.astype(BF16)
        h_ref[CHUNK:512, :] = jnp.zeros((512 - CHUNK, D_MODEL), F32)
        hb_ref[CHUNK:512, :] = jnp.zeros((512 - CHUNK, D_MODEL), BF16)


def _embed(x2d, meta_blk, g, b):
    nx = SEQ // 512
    return pl.pallas_call(
        _embed_kernel,
        grid=(nx + 1,),
        in_specs=[
            pl.BlockSpec((512, D_MODEL), lambda i: (jnp.minimum(i, nx - 1), 0)),
            pl.BlockSpec((CHUNK, D_MODEL), lambda i: (0, 0)),
            pl.BlockSpec((1, D_MODEL), lambda i: (0, 0)),
            pl.BlockSpec((1, D_MODEL), lambda i: (0, 0)),
        ],
        out_specs=[
            pl.BlockSpec((512, D_MODEL), lambda i: (i, 0)),
            pl.BlockSpec((512, D_MODEL), lambda i: (i, 0)),
        ],
        out_shape=[
            jax.ShapeDtypeStruct((LP_EMBED, D_MODEL), F32),
            jax.ShapeDtypeStruct((LP_EMBED, D_MODEL), BF16),
        ],
        compiler_params=_cparams(("arbitrary",)),
        name="embed_ln",
    )(x2d, meta_blk, g, b)


def _matmul_kernel(x_ref, w_ref, o_ref):
    o_ref[...] = _dot(x_ref[...], w_ref[...])


def _in_proj(hb, w):
    return pl.pallas_call(
        _matmul_kernel,
        grid=(LP // TM_BIG, P_COLS // TN_IN),
        in_specs=[
            pl.BlockSpec((TM_BIG, D_MODEL), lambda i, j: (i, 0)),
            pl.BlockSpec((D_MODEL, TN_IN), lambda i, j: (0, j)),
        ],
        out_specs=pl.BlockSpec((TM_BIG, TN_IN), lambda i, j: (i, j)),
        out_shape=jax.ShapeDtypeStruct((LP, P_COLS), F32),
        compiler_params=_cparams(("arbitrary", "arbitrary")),
        name="in_proj",
    )(hb, w)


POOL_HALO = 16


def _pool_kernel(u_ref, halo_ref, pw_ref, ps_ref, y_ref, ubuf):
    i = pl.program_id(0)
    u = u_ref[...]
    ubuf[0:POOL_HALO, :] = halo_ref[...]
    ubuf[POOL_HALO:POOL_HALO + TM_MID, :] = u
    row = i * TM_MID + lax.broadcasted_iota(jnp.int32, (TM_MID, POOL_GROUP), 0)
    pos = jnp.where(row >= META_ROW0 + META_PAD, row - (META_ROW0 + META_PAD) + 1, 2 * POOL_HALO)
    scale = ps_ref[...]
    for gi, w in enumerate(POOL_WINDOWS):
        cs = slice(gi * POOL_GROUP, (gi + 1) * POOL_GROUP)
        ug = u[:, cs]
        acc = ug
        for j in range(1, w):
            acc = acc + ubuf[POOL_HALO - j:POOL_HALO - j + TM_MID, cs]
        div = jnp.minimum(pos, w).astype(F32)
        d = acc / div - ug
        yg = _dot(d.astype(BF16), pw_ref[gi].astype(BF16))
        y_ref[:, cs] = (yg * scale[:, cs]).astype(BF16)


def _pool(p, pool_w, pool_scale):
    blocks_per_tile = TM_MID // POOL_HALO
    last_meta_block = (LP - POOL_HALO) // POOL_HALO

    def halo_map(i):
        return (jnp.where(i == 0, last_meta_block, i * blocks_per_tile - 1), COL_POOL // D_POOL)

    return pl.pallas_call(
        _pool_kernel,
        grid=(LP // TM_MID,),
        in_specs=[
            pl.BlockSpec((TM_MID, D_POOL), lambda i: (i, COL_POOL // D_POOL)),
            pl.BlockSpec((POOL_HALO, D_POOL), halo_map),
            pl.BlockSpec((len(POOL_WINDOWS), POOL_GROUP, POOL_GROUP), lambda i: (0, 0, 0)),
            pl.BlockSpec((1, D_POOL), lambda i: (0, 0)),
        ],
        out_specs=pl.BlockSpec((TM_MID, D_POOL), lambda i: (i, 0)),
        out_shape=jax.ShapeDtypeStruct((LP, D_POOL), BF16),
        scratch_shapes=[pltpu.VMEM((POOL_HALO + TM_MID, D_POOL), F32)],
        compiler_params=_cparams(("arbitrary",)),
        name="pool_mixer",
    )(p, p, pool_w, pool_scale)


def _chunk_block(c):
    return (c + N_CHUNKS - 1) % N_CHUNKS


CONV_HALO = 8


def _mlstm_kernel(mqk_ref, mv_ref, mo_ref, gt_ref, cw_ref, cb_ref, gb_ref, ng_ref, y_ref,
                  c_sc, n_sc, m_sc, xbuf):
    c = pl.program_id(0)

    @pl.when(c == 0)
    def _():
        c_sc[...] = jnp.zeros(c_sc.shape, F32)
        n_sc[...] = jnp.zeros(n_sc.shape, F32)
        m_sc[...] = jnp.full(m_sc.shape, NEG_INF, F32)
        xbuf[0:CONV_HALO, :] = jnp.zeros((CONV_HALO, 2 * D_MLSTM), F32)

    x = mqk_ref[...]
    xbuf[CONV_HALO:CONV_HALO + CHUNK, :] = x
    cw = cw_ref[...]
    conv = cb_ref[...] + cw[CONV_K - 1:CONV_K, :] * x
    for j in range(CONV_K - 1):
        off = CONV_HALO - (CONV_K - 1) + j
        conv = conv + cw[j:j + 1, :] * xbuf[off:off + CHUNK, :]
    xbuf[0:CONV_HALO, :] = x[CHUNK - CONV_HALO:CHUNK, :]
    qk = conv * _sigmoid(conv)
    v_all = mv_ref[...]
    og_all = _sigmoid(mo_ref[...])
    ng = ng_ref[...]

    z = gt_ref[...] + gb_ref[...]
    lane = lax.broadcasted_iota(jnp.int32, (CHUNK, LANES), 1)
    row = lax.broadcasted_iota(jnp.int32, (CHUNK, LANES), 0)
    valid = row >= jnp.where(c > 0, 0, META_PAD)
    ig = jnp.where(valid, z, NEG_INF)
    lf = jnp.where(valid, _log_sigmoid(z), 0.0)
    g_cum = _chunk_cumsum(lf)
    gcol = jnp.where(lane < N_HEADS, ig, g_cum)
    grow = gcol.T

    r64 = lax.broadcasted_iota(jnp.int32, (CHUNK, CHUNK), 0)
    c64 = lax.broadcasted_iota(jnp.int32, (CHUNK, CHUNK), 1)
    causal = r64 >= c64
    k_scale = HEAD_DIM ** -0.5

    for h in range(N_HEADS):
        hs = slice(h * HEAD_DIM, (h + 1) * HEAD_DIM)
        q = qk[:, hs]
        k = qk[:, D_MLSTM + h * HEAD_DIM:D_MLSTM + (h + 1) * HEAD_DIM] * k_scale
        v = v_all[:, hs].astype(BF16)
        qb = q.astype(BF16)
        g_t = gcol[:, N_HEADS + h:N_HEADS + h + 1]
        ig_t = gcol[:, h:h + 1]
        g_s = grow[N_HEADS + h:N_HEADS + h + 1, :]
        ig_s = grow[h:h + 1, :]
        m_prev = m_sc[h:h + 1, 0:1]
        n_prev = n_sc[h:h + 1, :]
        c_prev = c_sc[h]

        dmat = jnp.where(causal, g_t - g_s + ig_s, NEG_INF)
        m_inter = g_t + m_prev
        m_t = jnp.maximum(jnp.max(dmat, axis=1, keepdims=True), m_inter)
        m_ts = jnp.where(m_t == NEG_INF, 0.0, m_t)
        s = _dot_nt(qb, k.astype(BF16)) * jnp.exp(dmat - m_ts)
        inter = jnp.exp(m_inter - m_ts)
        num = _dot(s.astype(BF16), v) + inter * _dot(qb, c_prev.astype(BF16))
        den = jnp.sum(s, axis=1, keepdims=True) + inter * jnp.sum(q * n_prev, axis=1, keepdims=True)
        hh = num / jnp.maximum(jnp.abs(den), jnp.exp(-m_ts))

        g_last = gcol[CHUNK - 1:CHUNK, N_HEADS + h:N_HEADS + h + 1]
        w_col = g_last - g_t + ig_t
        w_row = g_last - g_s + ig_s
        m_new = jnp.maximum(g_last + m_prev, jnp.max(w_row, axis=1, keepdims=True))
        decay = jnp.exp(g_last + m_prev - m_new)
        wk = k * jnp.exp(w_col - m_new)
        c_sc[h] = decay * c_prev + _dot_tn(wk.astype(BF16), v)
        n_sc[h:h + 1, :] = decay * n_prev + jnp.sum(wk, axis=0, keepdims=True)
        m_sc[h:h + 1, :] = jnp.broadcast_to(m_new, (1, LANES))

        mu = jnp.mean(hh, axis=1, keepdims=True)
        hc = hh - mu
        var = jnp.mean(hc * hc, axis=1, keepdims=True)
        y = hc * lax.rsqrt(var + LN_EPS) * ng[:, hs] * og_all[:, hs]
        y_ref[:, hs] = y.astype(BF16)


def _mlstm(p, conv_w, conv_b, gate_bias, norm_g):
    def col(block_w, start):
        return lambda c: (_chunk_block(c), start // block_w)

    const2 = lambda c: (0, 0)
    return pl.pallas_call(
        _mlstm_kernel,
        grid=(N_CHUNKS,),
        in_specs=[
            pl.BlockSpec((CHUNK, 2 * D_MLSTM), col(2 * D_MLSTM, COL_MQK)),
            pl.BlockSpec((CHUNK, D_MLSTM), col(D_MLSTM, COL_MV)),
            pl.BlockSpec((CHUNK, D_MLSTM), col(D_MLSTM, COL_MO)),
            pl.BlockSpec((CHUNK, LANES), col(LANES, COL_GATE)),
            pl.BlockSpec((CONV_K, 2 * D_MLSTM), const2),
            pl.BlockSpec((1, 2 * D_MLSTM), const2),
            pl.BlockSpec((1, LANES), const2),
            pl.BlockSpec((1, D_MLSTM), const2),
        ],
        out_specs=pl.BlockSpec((CHUNK, D_MLSTM), lambda c: (_chunk_block(c), 0)),
        out_shape=jax.ShapeDtypeStruct((LP, D_MLSTM), BF16),
        scratch_shapes=[
            pltpu.VMEM((N_HEADS, HEAD_DIM, HEAD_DIM), F32),
            pltpu.VMEM((8, HEAD_DIM), F32),
            pltpu.VMEM((8, LANES), F32),
            pltpu.VMEM((CONV_HALO + CHUNK, 2 * D_MLSTM), F32),
        ],
        compiler_params=_cparams(("arbitrary",)),
        name="mlstm_mixer",
    )(p, p, p, p, conv_w, conv_b, gate_bias, norm_g)


N_LEVELS = 6


def _hgrn_tables():
    t = np.arange(CHUNK)
    sel = np.zeros((N_LEVELS * CHUNK, CHUNK), np.float32)
    mask = np.zeros((N_LEVELS + 1, CHUNK, CHUNK), np.float32)
    upper = np.zeros((CHUNK, LANES), np.float32)
    for l in range(N_LEVELS):
        half = 1 << l
        ref_row = (t // (2 * half)) * (2 * half) + half - 1
        sel[l * CHUNK + t, ref_row] = 1.0
        is_upper = (t // half) % 2 == 1
        upper[:, l] = is_upper
        same = (t[:, None] // (2 * half)) == (t[None, :] // (2 * half))
        mask[l] = same & is_upper[:, None] & ~is_upper[None, :]
    mask[N_LEVELS] = np.eye(CHUNK)
    return sel, mask, upper


def _hgrn_kernel(layer, hq_ref, hf_ref, hi_ref, hg_ref, lbl_ref, ng_ref, sel_ref, mask_ref, up_ref,
                 y_ref, st_sc):
    c = pl.program_id(0)

    @pl.when(c == 0)
    def _():
        st_sc[...] = jnp.zeros(st_sc.shape, F32)

    lbl = lbl_ref[...]
    e = jnp.exp(lbl - jnp.max(lbl, axis=0, keepdims=True))
    sm = e / jnp.sum(e, axis=0, keepdims=True)
    lb = jnp.sum(sm[0:layer + 1, :], axis=0, keepdims=True) - sm[0:1, :]

    z = hf_ref[...]
    a = jnp.log(lb)
    bb = jnp.log1p(-lb) + _log_sigmoid(z)
    mx = jnp.maximum(a, bb)
    log_f = mx + jnp.log(jnp.exp(a - mx) + jnp.exp(bb - mx))
    kk = (1.0 - lb) * _sigmoid(-z)
    hq = hq_ref[...]
    q = hq * _sigmoid(hq)
    v = hi_ref[...].astype(BF16)
    hg = hg_ref[...]
    gate = hg * _sigmoid(hg) * ng_ref[...]

    b = _chunk_cumsum(log_f)
    b_hi, b_lo = _split_bf16(b)
    sel = sel_ref[...]
    refs = _dot(sel, b_hi) + _dot(sel, b_lo)
    up = up_ref[...]

    amats = [None] * N_HEADS
    for l in range(N_LEVELS + 1):
        if l < N_LEVELS:
            ref_l = refs[l * CHUNK:(l + 1) * CHUNK, :]
            sign = 2.0 * up[:, l:l + 1] - 1.0
            zl = jnp.exp(sign * (b - ref_l))
            ql = (q * zl).astype(BF16)
            kl = (kk * zl).astype(BF16)
        else:
            ql = q.astype(BF16)
            kl = kk.astype(BF16)
        ml = mask_ref[l]
        for h in range(N_HEADS):
            hs = slice(h * HEAD_DIM, (h + 1) * HEAD_DIM)
            part = ml * _dot_nt(ql[:, hs], kl[:, hs])
            amats[h] = part if amats[h] is None else amats[h] + part

    b_last = b[CHUNK - 1:CHUNK, :]
    qe = (q * jnp.exp(b)).astype(BF16)
    kd = (kk * jnp.exp(b_last - b)).astype(BF16)
    e_last = jnp.exp(b_last)
    for h in range(N_HEADS):
        hs = slice(h * HEAD_DIM, (h + 1) * HEAD_DIM)
        st = st_sc[h]
        o = _dot(amats[h].astype(BF16), v[:, hs]) + _dot_nt(qe[:, hs], st.astype(BF16))
        st_sc[h] = e_last[:, hs] * st + _dot_tn(v[:, hs], kd[:, hs])
        o = o * lax.rsqrt(jnp.mean(o * o, axis=1, keepdims=True) + LN_EPS)
        y_ref[:, hs] = (o * gate[:, hs]).astype(BF16)


def _hgrn(p, lb_logits, norm_g, layer):
    def col(start):
        return lambda c: (_chunk_block(c), start // D_HGRN)

    sel, mask, upper = _hgrn_tables()
    const2 = lambda c: (0, 0)
    return pl.pallas_call(
        functools.partial(_hgrn_kernel, layer),
        grid=(N_CHUNKS,),
        in_specs=[
            pl.BlockSpec((CHUNK, D_HGRN), col(COL_HQ)),
            pl.BlockSpec((CHUNK, D_HGRN), col(COL_HF)),
            pl.BlockSpec((CHUNK, D_HGRN), col(COL_HI)),
            pl.BlockSpec((CHUNK, D_HGRN), col(COL_HG)),
            pl.BlockSpec((DEPTH, D_HGRN), const2),
            pl.BlockSpec((1, D_HGRN), const2),
            pl.BlockSpec((N_LEVELS * CHUNK, CHUNK), const2),
            pl.BlockSpec((N_LEVELS + 1, CHUNK, CHUNK), lambda c: (0, 0, 0)),
            pl.BlockSpec((CHUNK, LANES), const2),
        ],
        out_specs=pl.BlockSpec((CHUNK, D_HGRN), lambda c: (_chunk_block(c), 0)),
        out_shape=jax.ShapeDtypeStruct((LP, D_HGRN), BF16),
        scratch_shapes=[pltpu.VMEM((N_HEADS, HEAD_DIM, HEAD_DIM), F32)],
        compiler_params=_cparams(("arbitrary",)),
        name="hgrn_mixer",
    )(p, p, p, p, lb_logits, norm_g, jnp.asarray(sel, BF16), jnp.asarray(mask, F32), jnp.asarray(upper, F32))


def _zero_pad_rows(y, row0):
    row = row0 + lax.broadcasted_iota(jnp.int32, y.shape, 0)
    is_pad = jnp.logical_and(row >= META_ROW0, row < META_ROW0 + META_PAD)
    return jnp.where(is_pad, 0.0, y)


def _first_argmax(x, lane, valid):
    xm = jnp.where(valid, x, NEG_INF)
    mx = jnp.max(xm, axis=1, keepdims=True)
    idx = jnp.min(jnp.where(jnp.logical_and(valid, xm == mx), lane, float(LANES)), axis=1, keepdims=True)
    return mx, idx


def _out_router_kernel(yp_ref, ym_ref, yh_ref, h_ref, wo_ref, g_ref, b_ref, wrh_ref, wrl_ref, br_ref,
                       h1_ref, eid_ref, wts_ref):
    i = pl.program_id(0)
    acc = _dot(yp_ref[...], wo_ref[0:D_POOL, :])
    acc = acc + _dot(ym_ref[...], wo_ref[D_POOL:D_POOL + D_MLSTM, :])
    acc = acc + _dot(yh_ref[...], wo_ref[D_POOL + D_MLSTM:D_MODEL, :])
    h1 = _layer_norm_rows(ALPHA * h_ref[...] + acc, g_ref[...], b_ref[...])
    h1 = _zero_pad_rows(h1, i * TM_MID)
    h1_ref[...] = h1

    x_hi, x_lo = _split_bf16(h1)
    wrh = wrh_ref[...]
    logits = _dot(x_hi, wrh) + _dot(x_lo, wrh) + _dot(x_hi, wrl_ref[...]) + br_ref[...]
    lane = lax.broadcasted_iota(jnp.int32, logits.shape, 1).astype(F32)

    is_grp = lane < N_GROUPS
    g_max, g_idx = _first_argmax(logits, lane, is_grp)
    g_exp = jnp.where(is_grp, jnp.exp(logits - g_max), 0.0)
    p_grp = 1.0 / jnp.sum(g_exp, axis=1, keepdims=True)

    e_lo = N_GROUPS + g_idx * EXPERTS_PER_GROUP
    in_grp = jnp.logical_and(lane >= e_lo, lane < e_lo + EXPERTS_PER_GROUP)
    e_max, e1 = _first_argmax(logits, lane, in_grp)
    e_exp = jnp.where(in_grp, jnp.exp(logits - e_max), 0.0)
    p_exp = e_exp / jnp.sum(e_exp, axis=1, keepdims=True)
    p1, _ = _first_argmax(p_exp, lane, in_grp)
    rest = jnp.logical_and(in_grp, lane != e1)
    p2, e2 = _first_argmax(p_exp, lane, rest)
    psum = p1 + p2
    w1 = p_grp * p1 / psum
    w2 = p_grp * p2 / psum
    eid = jnp.where(lane == 0.0, e1 - N_GROUPS, jnp.where(lane == 1.0, e2 - N_GROUPS, 0.0))
    eid_ref[...] = eid.astype(jnp.int32)
    wts_ref[...] = jnp.where(lane == 0.0, w1, jnp.where(lane == 1.0, w2, 0.0))


def _out_router(y_pool, y_m, y_h, h, w_out, g, b, wr_hi, wr_lo, br):
    row = lambda i: (i, 0)
    const2 = lambda i: (0, 0)
    return pl.pallas_call(
        _out_router_kernel,
        grid=(LP // TM_MID,),
        in_specs=[
            pl.BlockSpec((TM_MID, D_POOL), row),
            pl.BlockSpec((TM_MID, D_MLSTM), row),
            pl.BlockSpec((TM_MID, D_HGRN), row),
            pl.BlockSpec((TM_MID, D_MODEL), row),
            pl.BlockSpec((D_MODEL, D_MODEL), const2, pipeline_mode=pl.Buffered(1)),
            pl.BlockSpec((1, D_MODEL), const2),
            pl.BlockSpec((1, D_MODEL), const2),
            pl.BlockSpec((D_MODEL, LANES), const2),
            pl.BlockSpec((D_MODEL, LANES), const2),
            pl.BlockSpec((1, LANES), const2),
        ],
        out_specs=[
            pl.BlockSpec((TM_MID, D_MODEL), row),
            pl.BlockSpec((TM_MID, LANES), row),
            pl.BlockSpec((TM_MID, LANES), row),
        ],
        out_shape=[
            jax.ShapeDtypeStruct((LP, D_MODEL), F32),
            jax.ShapeDtypeStruct((LP, LANES), jnp.int32),
            jax.ShapeDtypeStruct((LP, LANES), F32),
        ],
        compiler_params=_cparams(("arbitrary",)),
        name="out_proj_ln_router",
    )(y_pool, y_m, y_h, h, w_out, g, b, wr_hi, wr_lo, br)


def _expert_kernel(tile_expert, tile_rows, n_tiles, tok_cur_ref, tok_next_ref, dst_ref,
                   h_hbm, w1_ref, w3_ref, w2_ref, y_hbm,
                   xbuf, ybuf, w1b, w3b, w2b, gsem, ssem):
    i = pl.program_id(0)
    n_used = n_tiles[0]
    slot = i % 2

    def gather_copy(tok_ref, r, s):
        return pltpu.make_async_copy(h_hbm.at[pl.ds(tok_ref[0, 0, r], 1)], xbuf.at[s, pl.ds(r, 1)], gsem.at[s])

    def scatter_copy(r, s):
        return pltpu.make_async_copy(ybuf.at[s, pl.ds(r, 1)], y_hbm.at[pl.ds(dst_ref[0, 0, r], 1)], ssem.at[s])

    def start_gather(tok_ref, s):
        def body(r, carry):
            gather_copy(tok_ref, r, s).start()
            return carry
        lax.fori_loop(0, TM_EXPERT, body, 0)

    def wait_scatter(n, s):
        def body(r, carry):
            scatter_copy(0, s).wait()
            return carry
        lax.fori_loop(0, n, body, 0)

    @pl.when(i == 0)
    def _():
        gap = SLOT_STRIDE - LP
        ybuf[1, 0:gap, :] = jnp.zeros((gap, D_MODEL), F32)
        fill = pltpu.make_async_copy(ybuf.at[1, pl.ds(0, gap)], y_hbm.at[pl.ds(LP, gap)], ssem.at[1])
        fill.start()
        fill.wait()

    @pl.when(jnp.logical_and(i == 0, n_used > 0))
    def _():
        start_gather(tok_cur_ref, 0)

    @pl.when(i + 1 < n_used)
    def _():
        start_gather(tok_next_ref, 1 - slot)

    @pl.when(i < n_used)
    def _():
        first_of_expert = jnp.logical_or(i == 0, tile_expert[i] != tile_expert[jnp.maximum(i - 1, 0)])

        @pl.when(first_of_expert)
        def _():
            w1b[...] = w1_ref[0].astype(BF16)
            w3b[...] = w3_ref[0].astype(BF16)
            w2b[...] = w2_ref[0].astype(BF16)

        def wait_body(r, carry):
            gather_copy(tok_cur_ref, 0, slot).wait()
            return carry
        lax.fori_loop(0, TM_EXPERT, wait_body, 0)

        @pl.when(i >= 2)
        def _():
            wait_scatter(tile_rows[jnp.maximum(i - 2, 0)], slot)

        x = xbuf[slot].astype(BF16)
        a = _dot(x, w1b[...])
        g = _dot(x, w3b[...])
        he = (a * _sigmoid(a) * g).astype(BF16)
        ybuf[slot] = _dot(he, w2b[...])

        def scatter_body(r, carry):
            scatter_copy(r, slot).start()
            return carry
        lax.fori_loop(0, tile_rows[i], scatter_body, 0)

    @pl.when(jnp.logical_and(i == n_used - 1, i >= 1))
    def _():
        wait_scatter(tile_rows[jnp.maximum(i - 1, 0)], 1 - slot)

    @pl.when(i == n_used - 1)
    def _():
        wait_scatter(tile_rows[i], slot)


def _experts(h1, w1, w3, w2, tile_expert, tile_rows, n_tiles, row_token, row_dst):
    def cur(i, te, tr, nt):
        return (i, 0, 0)

    def nxt(i, te, tr, nt):
        return (jnp.minimum(i + 1, N_EXPERT_TILES - 1), 0, 0)

    def wmap(i, te, tr, nt):
        return (te[i], 0, 0)

    grid_spec = pltpu.PrefetchScalarGridSpec(
        num_scalar_prefetch=3,
        grid=(N_EXPERT_TILES,),
        in_specs=[
            pl.BlockSpec((1, 1, TM_EXPERT), cur, memory_space=pltpu.SMEM),
            pl.BlockSpec((1, 1, TM_EXPERT), nxt, memory_space=pltpu.SMEM),
            pl.BlockSpec((1, 1, TM_EXPERT), cur, memory_space=pltpu.SMEM),
            pl.BlockSpec(memory_space=pl.ANY),
            pl.BlockSpec((1, D_MODEL, D_EXPERT), wmap),
            pl.BlockSpec((1, D_MODEL, D_EXPERT), wmap),
            pl.BlockSpec((1, D_EXPERT, D_MODEL), wmap),
        ],
        out_specs=pl.BlockSpec(memory_space=pl.ANY),
        scratch_shapes=[
            pltpu.VMEM((2, TM_EXPERT, D_MODEL), F32),
            pltpu.VMEM((2, TM_EXPERT, D_MODEL), F32),
            pltpu.VMEM((D_MODEL, D_EXPERT), BF16),
            pltpu.VMEM((D_MODEL, D_EXPERT), BF16),
            pltpu.VMEM((D_EXPERT, D_MODEL), BF16),
            pltpu.SemaphoreType.DMA((2,)),
            pltpu.SemaphoreType.DMA((2,)),
        ],
    )
    return pl.pallas_call(
        _expert_kernel,
        grid_spec=grid_spec,
        out_shape=jax.ShapeDtypeStruct((Y_ROWS, D_MODEL), F32),
        compiler_params=_cparams(("arbitrary",)),
        name="moe_experts",
    )(tile_expert, tile_rows, n_tiles, row_token, row_token, row_dst, h1, w1, w3, w2)


def _dispatch_tables(eid):
    flat_e = eid.reshape(-1)
    onehot = (flat_e[:, None] == jnp.arange(N_EXPERTS, dtype=jnp.int32)[None, :]).astype(jnp.int32)
    counts = jnp.sum(onehot, axis=0)
    rank = jnp.sum((jnp.cumsum(onehot, axis=0) - onehot) * onehot, axis=1)
    padded = ((counts + TM_EXPERT - 1) // TM_EXPERT) * TM_EXPERT
    pad_end = jnp.cumsum(padded)
    pad_start = pad_end - padded
    dest = pad_start[flat_e] + rank
    a = jnp.arange(N_ASSIGN, dtype=jnp.int32)
    n_rows = N_EXPERT_TILES * TM_EXPERT
    row_token = jnp.zeros((n_rows,), jnp.int32).at[dest].set(a // 2)
    row_dst = jnp.zeros((n_rows,), jnp.int32).at[dest].set((a % 2) * SLOT_STRIDE + a // 2)
    tile_row0 = jnp.arange(N_EXPERT_TILES, dtype=jnp.int32) * TM_EXPERT
    tile_expert = jnp.minimum(jnp.searchsorted(pad_end, tile_row0, side="right"), N_EXPERTS - 1).astype(jnp.int32)
    tile_rows = jnp.clip(counts[tile_expert] - (tile_row0 - pad_start[tile_expert]), 0, TM_EXPERT).astype(jnp.int32)
    n_tiles = (pad_end[-1] // TM_EXPERT).astype(jnp.int32).reshape(1)
    shape3 = (N_EXPERT_TILES, 1, TM_EXPERT)
    return tile_expert, tile_rows, n_tiles, row_token.reshape(shape3), row_dst.reshape(shape3)


def _combine_kernel(tm, final, h1_ref, y0_ref, y1_ref, wts_ref, g_ref, b_ref, *out_refs):
    i = pl.program_id(0)
    wts = wts_ref[...]
    y = wts[:, 0:1] * y0_ref[...] + wts[:, 1:2] * y1_ref[...]
    h2 = _layer_norm_rows(ALPHA * h1_ref[...] + y, g_ref[...], b_ref[...])
    if final:
        out_refs[0][...] = h2
    else:
        h2 = _zero_pad_rows(h2, i * tm)
        out_refs[0][...] = h2
        out_refs[1][...] = h2.astype(BF16)


def _combine(h1, y_tok, wts, g, b, final):
    tm = TM_FINAL if final else TM_LN
    rows = SEQ if final else LP
    row = lambda i: (i, 0)
    second = lambda i: (i + SLOT_STRIDE // tm, 0)
    const2 = lambda i: (0, 0)
    if final:
        out_specs = [pl.BlockSpec((tm, D_MODEL), row)]
        out_shape = [jax.ShapeDtypeStruct((rows, D_MODEL), F32)]
    else:
        out_specs = [pl.BlockSpec((tm, D_MODEL), row), pl.BlockSpec((tm, D_MODEL), row)]
        out_shape = [jax.ShapeDtypeStruct((rows, D_MODEL), F32), jax.ShapeDtypeStruct((rows, D_MODEL), BF16)]
    return pl.pallas_call(
        functools.partial(_combine_kernel, tm, final),
        grid=(rows // tm,),
        in_specs=[
            pl.BlockSpec((tm, D_MODEL), row),
            pl.BlockSpec((tm, D_MODEL), row),
            pl.BlockSpec((tm, D_MODEL), second),
            pl.BlockSpec((tm, LANES), row),
            pl.BlockSpec((1, D_MODEL), const2),
            pl.BlockSpec((1, D_MODEL), const2),
        ],
        out_specs=out_specs,
        out_shape=out_shape,
        compiler_params=_cparams(("arbitrary",)),
        name="moe_combine_ln_final" if final else "moe_combine_ln",
    )(h1, y_tok, y_tok, wts, g, b)


def _in_proj_weight(w_in_l):
    o_pool, o_mq = 0, D_POOL
    o_mv = o_mq + 2 * D_MLSTM
    o_mi = o_mv + 2 * D_MLSTM
    o_hq = o_mi + 2 * N_HEADS
    parts = [
        w_in_l[:, o_mq:o_mi],
        w_in_l[:, o_hq:],
        w_in_l[:, o_pool:o_mq],
        w_in_l[:, o_mi:o_hq],
        jnp.zeros((D_MODEL, 2 * LANES - 2 * N_HEADS), w_in_l.dtype),
    ]
    return jnp.concatenate(parts, axis=1).astype(BF16)


def _pad_lanes(v):
    return jnp.pad(v, ((0, 0), (0, LANES - v.shape[1])))


def kernel(x, meta_tokens, emb_ln_g, emb_ln_b, hgrn_lb_logits, w_in, conv_w, conv_b, ig_b, fg_b,
           mlstm_norm_g, pool_w, pool_scale, hgrn_norm_g, w_out, ln1_g, ln1_b,
           w_router_group, b_router_group, w_router_expert, b_router_expert, w1, w3, w2,
           ln2_g, ln2_b):
    assert x.shape == (1, SEQ, D_MODEL) and x.dtype == F32
    row2 = lambda v: v.reshape(1, -1)
    meta_blk = jnp.pad(meta_tokens.astype(F32), ((META_PAD, 0), (0, 0)))
    h, hb = _embed(x.reshape(SEQ, D_MODEL), meta_blk, row2(emb_ln_g), row2(emb_ln_b))

    out = None
    for l in range(DEPTH):
        p = _in_proj(hb, _in_proj_weight(w_in[l]))
        y_pool = _pool(p, pool_w[l], row2(pool_scale[l]))
        gate_bias = _pad_lanes(jnp.concatenate([ig_b[l], fg_b[l]]).reshape(1, -1))
        y_m = _mlstm(p, conv_w[l], row2(conv_b[l]), gate_bias, row2(mlstm_norm_g[l]))
        y_h = _hgrn(p, hgrn_lb_logits, row2(hgrn_norm_g[l]), l)

        w_r = _pad_lanes(jnp.concatenate([w_router_group[l], w_router_expert[l]], axis=1))
        wr_hi, wr_lo = _split_bf16(w_r)
        b_r = _pad_lanes(jnp.concatenate([b_router_group[l], b_router_expert[l]]).reshape(1, -1))
        h1, eid, wts = _out_router(y_pool, y_m, y_h, h, w_out[l].astype(BF16), row2(ln1_g[l]), row2(ln1_b[l]),
                                   wr_hi, wr_lo, b_r)

        tables = _dispatch_tables(eid[:, 0:2])
        y_tok = _experts(h1, w1[l], w3[l], w2[l], *tables)
        if l + 1 < DEPTH:
            h, hb = _combine(h1, y_tok, wts, row2(ln2_g[l]), row2(ln2_b[l]), final=False)
        else:
            (out,) = _combine(h1, y_tok, wts, row2(ln2_g[l]), row2(ln2_b[l]), final=True)
    return out.reshape(1, SEQ, D_MODEL)
```

```python
import functools
import math

import jax
import jax.numpy as jnp
import numpy as np
from jax import lax
from jax.experimental import pallas as pl
from jax.experimental.pallas import tpu as pltpu

F32 = jnp.float32
BF16 = jnp.bfloat16

D_MODEL = 2048
SEQ = 8192
DEPTH = 2
N_META = 16
CHUNK = 64
D_POOL = D_MODEL // 4
POOL_WINDOWS = (2, 4, 8, 16)
POOL_GROUP = D_POOL // len(POOL_WINDOWS)
D_MLSTM = 3 * D_MODEL // 8
N_HEADS = 6
HEAD_DIM = D_MLSTM // N_HEADS
CONV_K = 4
D_HGRN = D_MODEL - D_POOL - D_MLSTM
N_GROUPS = 4
EXPERTS_PER_GROUP = 8
N_EXPERTS = N_GROUPS * EXPERTS_PER_GROUP
D_EXPERT = D_MODEL // 4
ALPHA = (2 * DEPTH) ** 0.25
LN_EPS = 1e-5
NEG_INF = float("-inf")

LANES = 128
LP = SEQ + CHUNK
META_ROW0 = SEQ
META_PAD = CHUNK - N_META
N_CHUNKS = LP // CHUNK
LP_EMBED = 17 * 512

COL_MQK = 0
COL_MV = 2 * D_MLSTM
COL_MO = COL_MV + D_MLSTM
COL_HQ = COL_MO + D_MLSTM
COL_HF = COL_HQ + D_HGRN
COL_HI = COL_HF + D_HGRN
COL_HG = COL_HI + D_HGRN
COL_POOL = COL_HG + D_HGRN
COL_GATE = COL_POOL + D_POOL
P_COLS = COL_GATE + 2 * LANES

TM_BIG = 2752
TN_IN = 256
TM_MID = 688
TM_LN = 192
TM_FINAL = 256
TM_EXPERT = 256
N_ASSIGN = 2 * LP
N_EXPERT_TILES = (N_ASSIGN + N_EXPERTS * (TM_EXPERT - 1)) // TM_EXPERT + 1
VMEM_LIMIT = 56 * 1024 * 1024


def _cparams(sem):
    return pltpu.CompilerParams(dimension_semantics=sem, vmem_limit_bytes=VMEM_LIMIT)


def _sigmoid(x):
    return 1.0 / (1.0 + jnp.exp(-x))


def _log_sigmoid(x):
    return jnp.minimum(x, 0.0) - jnp.log1p(jnp.exp(-jnp.abs(x)))


def _layer_norm_rows(x, g, b):
    mu = jnp.mean(x, axis=-1, keepdims=True)
    xc = x - mu
    var = jnp.mean(xc * xc, axis=-1, keepdims=True)
    return xc * lax.rsqrt(var + LN_EPS) * g + b


def _dot(a, b):
    return jnp.dot(a, b, preferred_element_type=F32)


def _dot_nt(a, b):
    return lax.dot_general(a, b, (((1,), (1,)), ((), ())), preferred_element_type=F32)


def _dot_tn(a, b):
    return lax.dot_general(a, b, (((0,), (0,)), ((), ())), preferred_element_type=F32)


def _split_bf16(x):
    hi = x.astype(BF16)
    lo = (x - hi.astype(F32)).astype(BF16)
    return hi, lo


def _chunk_cumsum(x):
    r = lax.broadcasted_iota(jnp.int32, (CHUNK, CHUNK), 0)
    c = lax.broadcasted_iota(jnp.int32, (CHUNK, CHUNK), 1)
    tri = jnp.where(r >= c, 1.0, 0.0).astype(BF16)
    hi, lo = _split_bf16(x)
    return _dot(tri, hi) + _dot(tri, lo)


def _embed_kernel(x_ref, meta_ref, g_ref, b_ref, h_ref, hb_ref):
    i = pl.program_id(0)
    g = g_ref[...]
    b = b_ref[...]

    @pl.when(i < SEQ // 512)
    def _():
        y = _layer_norm_rows(x_ref[...], g, b)
        h_ref[...] = y
        hb_ref[...] = y.astype(BF16)

    @pl.when(i == SEQ // 512)
    def _():
        y = _layer_norm_rows(meta_ref[...], g, b)
        row = lax.broadcasted_iota(jnp.int32, (CHUNK, D_MODEL), 0)
        y = jnp.where(row >= META_PAD, y, 0.0)
        h_ref[0:CHUNK, :] = y
        hb_ref[0:CHUNK, :] = y.astype(BF16)
        h_ref[CHUNK:512, :] = jnp.zeros((512 - CHUNK, D_MODEL), F32)
        hb_ref[CHUNK:512, :] = jnp.zeros((512 - CHUNK, D_MODEL), BF16)


def _embed(x2d, meta_blk, g, b):
    nx = SEQ // 512
    return pl.pallas_call(
        _embed_kernel,
        grid=(nx + 1,),
        in_specs=[
            pl.BlockSpec((512, D_MODEL), lambda i: (jnp.minimum(i, nx - 1), 0)),
            pl.BlockSpec((CHUNK, D_MODEL), lambda i: (0, 0)),
            pl.BlockSpec((1, D_MODEL), lambda i: (0, 0)),
            pl.BlockSpec((1, D_MODEL), lambda i: (0, 0)),
        ],
        out_specs=[
            pl.BlockSpec((512, D_MODEL), lambda i: (i, 0)),
            pl.BlockSpec((512, D_MODEL), lambda i: (i, 0)),
        ],
        out_shape=[
            jax.ShapeDtypeStruct((LP_EMBED, D_MODEL), F32),
            jax.ShapeDtypeStruct((LP_EMBED, D_MODEL), BF16),
        ],
        compiler_params=_cparams(("arbitrary",)),
        name="embed_ln",
    )(x2d, meta_blk, g, b)


GATE_COLS = 2 * N_HEADS
W_IN_GATE0 = D_POOL + 4 * D_MLSTM
N_TILES_A = W_IN_GATE0 // TN_IN
N_TILES_H = 4 * D_HGRN // TN_IN
N_TILES_POOL = D_POOL // TN_IN


def _in_proj_kernel(x_ref, wa_ref, wb_ref, o_ref):
    j = pl.program_id(1)
    shifted = jnp.logical_and(j >= N_TILES_A, j < N_TILES_A + N_TILES_H)

    @pl.when(jnp.logical_not(shifted))
    def _():
        o_ref[...] = _dot(x_ref[...], wa_ref[0].astype(BF16))

    @pl.when(shifted)
    def _():
        w = jnp.concatenate([wa_ref[0], wb_ref[0]], axis=1)[:, GATE_COLS:GATE_COLS + TN_IN]
        o_ref[...] = _dot(x_ref[...], w.astype(BF16))


def _in_proj(hb, w_in, layer):
    n_main = N_TILES_A + N_TILES_H

    def wa_map(i, j):
        return (layer, 0, jnp.where(j < n_main, j, N_TILES_A))

    def wb_map(i, j):
        shifted = jnp.logical_and(j >= N_TILES_A, j < n_main)
        return (layer, 0, jnp.where(shifted, (j + 1) * (TN_IN // LANES), 0))

    def out_map(i, j):
        return (i, jnp.where(j < N_TILES_POOL, j + n_main - N_TILES_POOL, jnp.where(j < n_main, j - N_TILES_POOL, j)))

    return pl.pallas_call(
        _in_proj_kernel,
        grid=(LP // TM_BIG, P_COLS // TN_IN),
        in_specs=[
            pl.BlockSpec((TM_BIG, D_MODEL), lambda i, j: (i, 0)),
            pl.BlockSpec((1, D_MODEL, TN_IN), wa_map),
            pl.BlockSpec((1, D_MODEL, LANES), wb_map),
        ],
        out_specs=pl.BlockSpec((TM_BIG, TN_IN), out_map),
        out_shape=jax.ShapeDtypeStruct((LP, P_COLS), F32),
        compiler_params=_cparams(("arbitrary", "arbitrary")),
        name="in_proj",
    )(hb, w_in, w_in)


POOL_HALO = 16


def _pool_kernel(u_ref, halo_ref, pw_ref, ps_ref, y_ref, ubuf):
    i = pl.program_id(0)
    u = u_ref[...]
    ubuf[0:POOL_HALO, :] = halo_ref[...]
    ubuf[POOL_HALO:POOL_HALO + TM_MID, :] = u
    row = i * TM_MID + lax.broadcasted_iota(jnp.int32, (TM_MID, POOL_GROUP), 0)
    pos = jnp.where(row >= META_ROW0 + META_PAD, row - (META_ROW0 + META_PAD) + 1, 2 * POOL_HALO)
    scale = ps_ref[...]
    for gi, w in enumerate(POOL_WINDOWS):
        cs = slice(gi * POOL_GROUP, (gi + 1) * POOL_GROUP)
        ug = u[:, cs]
        acc = ug
        for j in range(1, w):
            acc = acc + ubuf[POOL_HALO - j:POOL_HALO - j + TM_MID, cs]
        div = jnp.minimum(pos, w).astype(F32)
        d = acc / div - ug
        yg = _dot(d.astype(BF16), pw_ref[gi].astype(BF16))
        y_ref[:, cs] = (yg * scale[:, cs]).astype(BF16)


def _pool(p, pool_w, pool_scale):
    blocks_per_tile = TM_MID // POOL_HALO
    last_meta_block = (LP - POOL_HALO) // POOL_HALO

    def halo_map(i):
        return (jnp.where(i == 0, last_meta_block, i * blocks_per_tile - 1), COL_POOL // D_POOL)

    return pl.pallas_call(
        _pool_kernel,
        grid=(LP // TM_MID,),
        in_specs=[
            pl.BlockSpec((TM_MID, D_POOL), lambda i: (i, COL_POOL // D_POOL)),
            pl.BlockSpec((POOL_HALO, D_POOL), halo_map),
            pl.BlockSpec((len(POOL_WINDOWS), POOL_GROUP, POOL_GROUP), lambda i: (0, 0, 0)),
            pl.BlockSpec((1, D_POOL), lambda i: (0, 0)),
        ],
        out_specs=pl.BlockSpec((TM_MID, D_POOL), lambda i: (i, 0)),
        out_shape=jax.ShapeDtypeStruct((LP, D_POOL), BF16),
        scratch_shapes=[pltpu.VMEM((POOL_HALO + TM_MID, D_POOL), F32)],
        compiler_params=_cparams(("arbitrary",)),
        name="pool_mixer",
    )(p, p, pool_w, pool_scale)


def _chunk_block(c):
    return (c + N_CHUNKS - 1) % N_CHUNKS


CONV_HALO = 8


def _mlstm_kernel(mqk_ref, mv_ref, mo_ref, gt_ref, cw_ref, cb_ref, gb_ref, ng_ref, y_ref,
                  c_sc, n_sc, m_sc, xbuf):
    c = pl.program_id(0)

    @pl.when(c == 0)
    def _():
        c_sc[...] = jnp.zeros(c_sc.shape, F32)
        n_sc[...] = jnp.zeros(n_sc.shape, F32)
        m_sc[...] = jnp.full(m_sc.shape, NEG_INF, F32)
        xbuf[0:CONV_HALO, :] = jnp.zeros((CONV_HALO, 2 * D_MLSTM), F32)

    x = mqk_ref[...]
    xbuf[CONV_HALO:CONV_HALO + CHUNK, :] = x
    cw = cw_ref[...]
    conv = cb_ref[...] + cw[CONV_K - 1:CONV_K, :] * x
    for j in range(CONV_K - 1):
        off = CONV_HALO - (CONV_K - 1) + j
        conv = conv + cw[j:j + 1, :] * xbuf[off:off + CHUNK, :]
    xbuf[0:CONV_HALO, :] = x[CHUNK - CONV_HALO:CHUNK, :]
    qk = conv * _sigmoid(conv)
    v_all = mv_ref[...]
    og_all = _sigmoid(mo_ref[...])
    ng = ng_ref[...]

    z = gt_ref[...] + gb_ref[...]
    lane = lax.broadcasted_iota(jnp.int32, (CHUNK, LANES), 1)
    row = lax.broadcasted_iota(jnp.int32, (CHUNK, LANES), 0)
    valid = row >= jnp.where(c > 0, 0, META_PAD)
    ig = jnp.where(valid, z, NEG_INF)
    lf = jnp.where(valid, _log_sigmoid(z), 0.0)
    g_cum = _chunk_cumsum(lf)
    gcol = jnp.where(lane < N_HEADS, ig, g_cum)
    grow = gcol.T

    r64 = lax.broadcasted_iota(jnp.int32, (CHUNK, CHUNK), 0)
    c64 = lax.broadcasted_iota(jnp.int32, (CHUNK, CHUNK), 1)
    causal = r64 >= c64
    k_scale = HEAD_DIM ** -0.5

    for h in range(N_HEADS):
        hs = slice(h * HEAD_DIM, (h + 1) * HEAD_DIM)
        q = qk[:, hs]
        k = qk[:, D_MLSTM + h * HEAD_DIM:D_MLSTM + (h + 1) * HEAD_DIM] * k_scale
        v = v_all[:, hs].astype(BF16)
        qb = q.astype(BF16)
        g_t = gcol[:, N_HEADS + h:N_HEADS + h + 1]
        ig_t = gcol[:, h:h + 1]
        g_s = grow[N_HEADS + h:N_HEADS + h + 1, :]
        ig_s = grow[h:h + 1, :]
        m_prev = m_sc[h:h + 1, 0:1]
        n_prev = n_sc[h:h + 1, :]
        c_prev = c_sc[h]

        dmat = jnp.where(causal, g_t - g_s + ig_s, NEG_INF)
        m_inter = g_t + m_prev
        m_t = jnp.maximum(jnp.max(dmat, axis=1, keepdims=True), m_inter)
        m_ts = jnp.where(m_t == NEG_INF, 0.0, m_t)
        s = _dot_nt(qb, k.astype(BF16)) * jnp.exp(dmat - m_ts)
        inter = jnp.exp(m_inter - m_ts)
        num = _dot(s.astype(BF16), v) + inter * _dot(qb, c_prev.astype(BF16))
        den = jnp.sum(s, axis=1, keepdims=True) + inter * jnp.sum(q * n_prev, axis=1, keepdims=True)
        hh = num / jnp.maximum(jnp.abs(den), jnp.exp(-m_ts))

        g_last = gcol[CHUNK - 1:CHUNK, N_HEADS + h:N_HEADS + h + 1]
        w_col = g_last - g_t + ig_t
        w_row = g_last - g_s + ig_s
        m_new = jnp.maximum(g_last + m_prev, jnp.max(w_row, axis=1, keepdims=True))
        decay = jnp.exp(g_last + m_prev - m_new)
        wk = k * jnp.exp(w_col - m_new)
        c_sc[h] = decay * c_prev + _dot_tn(wk.astype(BF16), v)
        n_sc[h:h + 1, :] = decay * n_prev + jnp.sum(wk, axis=0, keepdims=True)
        m_sc[h:h + 1, :] = jnp.broadcast_to(m_new, (1, LANES))

        mu = jnp.mean(hh, axis=1, keepdims=True)
        hc = hh - mu
        var = jnp.mean(hc * hc, axis=1, keepdims=True)
        y = hc * lax.rsqrt(var + LN_EPS) * ng[:, hs] * og_all[:, hs]
        y_ref[:, hs] = y.astype(BF16)


def _mlstm(p, conv_w, conv_b, gate_bias, norm_g):
    def col(block_w, start):
        return lambda c: (_chunk_block(c), start // block_w)

    const2 = lambda c: (0, 0)
    return pl.pallas_call(
        _mlstm_kernel,
        grid=(N_CHUNKS,),
        in_specs=[
            pl.BlockSpec((CHUNK, 2 * D_MLSTM), col(2 * D_MLSTM, COL_MQK)),
            pl.BlockSpec((CHUNK, D_MLSTM), col(D_MLSTM, COL_MV)),
            pl.BlockSpec((CHUNK, D_MLSTM), col(D_MLSTM, COL_MO)),
            pl.BlockSpec((CHUNK, LANES), col(LANES, COL_GATE)),
            pl.BlockSpec((CONV_K, 2 * D_MLSTM), const2),
            pl.BlockSpec((1, 2 * D_MLSTM), const2),
            pl.BlockSpec((1, LANES), const2),
            pl.BlockSpec((1, D_MLSTM), const2),
        ],
        out_specs=pl.BlockSpec((CHUNK, D_MLSTM), lambda c: (_chunk_block(c), 0)),
        out_shape=jax.ShapeDtypeStruct((LP, D_MLSTM), BF16),
        scratch_shapes=[
            pltpu.VMEM((N_HEADS, HEAD_DIM, HEAD_DIM), F32),
            pltpu.VMEM((8, HEAD_DIM), F32),
            pltpu.VMEM((8, LANES), F32),
            pltpu.VMEM((CONV_HALO + CHUNK, 2 * D_MLSTM), F32),
        ],
        compiler_params=_cparams(("arbitrary",)),
        name="mlstm_mixer",
    )(p, p, p, p, conv_w, conv_b, gate_bias, norm_g)


N_LEVELS = 6


def _hgrn_tables():
    t = np.arange(CHUNK)
    sel = np.zeros((N_LEVELS * CHUNK, CHUNK), np.float32)
    mask = np.zeros((N_LEVELS + 1, CHUNK, CHUNK), np.float32)
    upper = np.zeros((CHUNK, LANES), np.float32)
    for l in range(N_LEVELS):
        half = 1 << l
        ref_row = (t // (2 * half)) * (2 * half) + half - 1
        sel[l * CHUNK + t, ref_row] = 1.0
        is_upper = (t // half) % 2 == 1
        upper[:, l] = is_upper
        same = (t[:, None] // (2 * half)) == (t[None, :] // (2 * half))
        mask[l] = same & is_upper[:, None] & ~is_upper[None, :]
    mask[N_LEVELS] = np.eye(CHUNK)
    return sel, mask, upper


def _hgrn_kernel(layer, hq_ref, hf_ref, hi_ref, hg_ref, lbl_ref, ng_ref, sel_ref, mask_ref, up_ref,
                 y_ref, st_sc):
    c = pl.program_id(0)

    @pl.when(c == 0)
    def _():
        st_sc[...] = jnp.zeros(st_sc.shape, F32)

    lbl = lbl_ref[...]
    e = jnp.exp(lbl - jnp.max(lbl, axis=0, keepdims=True))
    sm = e / jnp.sum(e, axis=0, keepdims=True)
    lb = jnp.sum(sm[0:layer + 1, :], axis=0, keepdims=True) - sm[0:1, :]

    z = hf_ref[...]
    a = jnp.log(lb)
    bb = jnp.log1p(-lb) + _log_sigmoid(z)
    mx = jnp.maximum(a, bb)
    log_f = mx + jnp.log(jnp.exp(a - mx) + jnp.exp(bb - mx))
    kk = (1.0 - lb) * _sigmoid(-z)
    hq = hq_ref[...]
    q = hq * _sigmoid(hq)
    v = hi_ref[...].astype(BF16)
    hg = hg_ref[...]
    gate = hg * _sigmoid(hg) * ng_ref[...]

    b = _chunk_cumsum(log_f)
    b_hi, b_lo = _split_bf16(b)
    sel = sel_ref[...]
    refs = _dot(sel, b_hi) + _dot(sel, b_lo)
    up = up_ref[...]

    amats = [None] * N_HEADS
    for l in range(N_LEVELS + 1):
        if l < N_LEVELS:
            ref_l = refs[l * CHUNK:(l + 1) * CHUNK, :]
            sign = 2.0 * up[:, l:l + 1] - 1.0
            zl = jnp.exp(sign * (b - ref_l))
            ql = (q * zl).astype(BF16)
            kl = (kk * zl).astype(BF16)
        else:
            ql = q.astype(BF16)
            kl = kk.astype(BF16)
        ml = mask_ref[l]
        for h in range(N_HEADS):
            hs = slice(h * HEAD_DIM, (h + 1) * HEAD_DIM)
            part = ml * _dot_nt(ql[:, hs], kl[:, hs])
            amats[h] = part if amats[h] is None else amats[h] + part

    b_last = b[CHUNK - 1:CHUNK, :]
    qe = (q * jnp.exp(b)).astype(BF16)
    kd = (kk * jnp.exp(b_last - b)).astype(BF16)
    e_last = jnp.exp(b_last)
    for h in range(N_HEADS):
        hs = slice(h * HEAD_DIM, (h + 1) * HEAD_DIM)
        st = st_sc[h]
        o = _dot(amats[h].astype(BF16), v[:, hs]) + _dot_nt(qe[:, hs], st.astype(BF16))
        st_sc[h] = e_last[:, hs] * st + _dot_tn(v[:, hs], kd[:, hs])
        o = o * lax.rsqrt(jnp.mean(o * o, axis=1, keepdims=True) + LN_EPS)
        y_ref[:, hs] = (o * gate[:, hs]).astype(BF16)


def _hgrn(p, lb_logits, norm_g, layer):
    def col(start):
        return lambda c: (_chunk_block(c), start // D_HGRN)

    sel, mask, upper = _hgrn_tables()
    const2 = lambda c: (0, 0)
    return pl.pallas_call(
        functools.partial(_hgrn_kernel, layer),
        grid=(N_CHUNKS,),
        in_specs=[
            pl.BlockSpec((CHUNK, D_HGRN), col(COL_HQ)),
            pl.BlockSpec((CHUNK, D_HGRN), col(COL_HF)),
            pl.BlockSpec((CHUNK, D_HGRN), col(COL_HI)),
            pl.BlockSpec((CHUNK, D_HGRN), col(COL_HG)),
            pl.BlockSpec((DEPTH, D_HGRN), const2),
            pl.BlockSpec((1, D_HGRN), const2),
            pl.BlockSpec((N_LEVELS * CHUNK, CHUNK), const2),
            pl.BlockSpec((N_LEVELS + 1, CHUNK, CHUNK), lambda c: (0, 0, 0)),
            pl.BlockSpec((CHUNK, LANES), const2),
        ],
        out_specs=pl.BlockSpec((CHUNK, D_HGRN), lambda c: (_chunk_block(c), 0)),
        out_shape=jax.ShapeDtypeStruct((LP, D_HGRN), BF16),
        scratch_shapes=[pltpu.VMEM((N_HEADS, HEAD_DIM, HEAD_DIM), F32)],
        compiler_params=_cparams(("arbitrary",)),
        name="hgrn_mixer",
    )(p, p, p, p, lb_logits, norm_g, jnp.asarray(sel, BF16), jnp.asarray(mask, F32), jnp.asarray(upper, F32))


def _zero_pad_rows(y, row0):
    row = row0 + lax.broadcasted_iota(jnp.int32, y.shape, 0)
    is_pad = jnp.logical_and(row >= META_ROW0, row < META_ROW0 + META_PAD)
    return jnp.where(is_pad, 0.0, y)


def _first_argmax(x, lane, valid):
    xm = jnp.where(valid, x, NEG_INF)
    mx = jnp.max(xm, axis=1, keepdims=True)
    idx = jnp.min(jnp.where(jnp.logical_and(valid, xm == mx), lane, float(LANES)), axis=1, keepdims=True)
    return mx, idx


def _out_router_kernel(yp_ref, ym_ref, yh_ref, h_ref, wo_ref, g_ref, b_ref, wrh_ref, wrl_ref, br_ref,
                       h1_ref, eid_ref, wts_ref):
    i = pl.program_id(0)
    acc = _dot(yp_ref[...], wo_ref[0:D_POOL, :])
    acc = acc + _dot(ym_ref[...], wo_ref[D_POOL:D_POOL + D_MLSTM, :])
    acc = acc + _dot(yh_ref[...], wo_ref[D_POOL + D_MLSTM:D_MODEL, :])
    h1 = _layer_norm_rows(ALPHA * h_ref[...] + acc, g_ref[...], b_ref[...])
    h1 = _zero_pad_rows(h1, i * TM_MID)
    h1_ref[...] = h1

    x_hi, x_lo = _split_bf16(h1)
    wrh = wrh_ref[...]
    logits = _dot(x_hi, wrh) + _dot(x_lo, wrh) + _dot(x_hi, wrl_ref[...]) + br_ref[...]
    lane = lax.broadcasted_iota(jnp.int32, logits.shape, 1).astype(F32)

    is_grp = lane < N_GROUPS
    g_max, g_idx = _first_argmax(logits, lane, is_grp)
    g_exp = jnp.where(is_grp, jnp.exp(logits - g_max), 0.0)
    p_grp = 1.0 / jnp.sum(g_exp, axis=1, keepdims=True)

    e_lo = N_GROUPS + g_idx * EXPERTS_PER_GROUP
    in_grp = jnp.logical_and(lane >= e_lo, lane < e_lo + EXPERTS_PER_GROUP)
    e_max, e1 = _first_argmax(logits, lane, in_grp)
    e_exp = jnp.where(in_grp, jnp.exp(logits - e_max), 0.0)
    p_exp = e_exp / jnp.sum(e_exp, axis=1, keepdims=True)
    p1, _ = _first_argmax(p_exp, lane, in_grp)
    rest = jnp.logical_and(in_grp, lane != e1)
    p2, e2 = _first_argmax(p_exp, lane, rest)
    psum = p1 + p2
    w1 = p_grp * p1 / psum
    w2 = p_grp * p2 / psum
    eid = jnp.where(lane == 0.0, e1 - N_GROUPS, jnp.where(lane == 1.0, e2 - N_GROUPS, 0.0))
    eid_ref[...] = eid.astype(jnp.int32)
    wts_ref[...] = jnp.where(lane == 0.0, w1, jnp.where(lane == 1.0, w2, 0.0))


def _out_router(y_pool, y_m, y_h, h, w_out, g, b, wr_hi, wr_lo, br):
    row = lambda i: (i, 0)
    const2 = lambda i: (0, 0)
    return pl.pallas_call(
        _out_router_kernel,
        grid=(LP // TM_MID,),
        in_specs=[
            pl.BlockSpec((TM_MID, D_POOL), row),
            pl.BlockSpec((TM_MID, D_MLSTM), row),
            pl.BlockSpec((TM_MID, D_HGRN), row),
            pl.BlockSpec((TM_MID, D_MODEL), row),
            pl.BlockSpec((D_MODEL, D_MODEL), const2, pipeline_mode=pl.Buffered(1)),
            pl.BlockSpec((1, D_MODEL), const2),
            pl.BlockSpec((1, D_MODEL), const2),
            pl.BlockSpec((D_MODEL, LANES), const2),
            pl.BlockSpec((D_MODEL, LANES), const2),
            pl.BlockSpec((1, LANES), const2),
        ],
        out_specs=[
            pl.BlockSpec((TM_MID, D_MODEL), row),
            pl.BlockSpec((TM_MID, LANES), row),
            pl.BlockSpec((TM_MID, LANES), row),
        ],
        out_shape=[
            jax.ShapeDtypeStruct((LP, D_MODEL), F32),
            jax.ShapeDtypeStruct((LP, LANES), jnp.int32),
            jax.ShapeDtypeStruct((LP, LANES), F32),
        ],
        compiler_params=_cparams(("arbitrary",)),
        name="out_proj_ln_router",
    )(y_pool, y_m, y_h, h, w_out, g, b, wr_hi, wr_lo, br)


ROUTE_ROWS = 256
SORTED_ROWS = N_EXPERT_TILES * TM_EXPERT
GATHER_UNROLL = 8


def _route_kernel(e_ref, dest_ref, tab_ref):
    e = e_ref[...]
    lane = lax.broadcasted_iota(jnp.int32, (ROUTE_ROWS, LANES), 1)
    kk = lax.broadcasted_iota(jnp.int32, (LANES, LANES), 0)
    ll = lax.broadcasted_iota(jnp.int32, (LANES, LANES), 1)
    before_lane = jnp.where(kk < ll, 1.0, 0.0).astype(BF16)
    all_lanes = jnp.ones((LANES, LANES), BF16)
    rr = lax.broadcasted_iota(jnp.int32, (ROUTE_ROWS, ROUTE_ROWS), 0)
    cc = lax.broadcasted_iota(jnp.int32, (ROUTE_ROWS, ROUTE_ROWS), 1)
    before_row = jnp.where(cc < rr, 1.0, 0.0).astype(BF16)

    lane1 = lane[0:1, :]
    rank = jnp.zeros((ROUTE_ROWS, LANES), F32)
    counts = jnp.zeros((1, LANES), F32)
    masks = []
    for x in range(N_EXPERTS):
        m = jnp.where(e == x, 1.0, 0.0)
        mb = m.astype(BF16)
        in_row = _dot(mb, before_lane)
        row_tot = _dot(mb, all_lanes)
        rows_before = _dot(before_row, row_tot.astype(BF16))
        rank = rank + m * (in_row + rows_before)
        total = rows_before[ROUTE_ROWS - 1:ROUTE_ROWS, :] + row_tot[ROUTE_ROWS - 1:ROUTE_ROWS, :]
        counts = counts + jnp.where(lane1 == x, total, 0.0)
        masks.append(m)

    padded = jnp.floor((counts + (TM_EXPERT - 1)) * (1.0 / TM_EXPERT)) * TM_EXPERT
    p_hi, p_lo = _split_bf16(padded)
    start = _dot(p_hi, before_lane) + _dot(p_lo, before_lane)
    end = start + padded

    dest = rank
    tile0 = (lane1 * TM_EXPERT).astype(F32)
    n_before = jnp.zeros((1, LANES), F32)
    for x in range(N_EXPERTS):
        dest = dest + masks[x] * start[:, x:x + 1]
        n_before = n_before + jnp.where(end[:, x:x + 1] <= tile0, 1.0, 0.0)
    dest_ref[...] = dest.astype(jnp.int32)

    tile_expert = jnp.minimum(n_before, float(N_EXPERTS - 1))
    n_tiles = end[:, N_EXPERTS - 1:N_EXPERTS] * (1.0 / TM_EXPERT)
    row = lax.broadcasted_iota(jnp.int32, (8, LANES), 0)
    tab = jnp.where(row == 0, tile_expert, jnp.where(row == 1, n_tiles, 0.0))
    tab_ref[...] = tab.astype(jnp.int32)


def _route(e_grid):
    return pl.pallas_call(
        _route_kernel,
        out_shape=[
            jax.ShapeDtypeStruct((ROUTE_ROWS, LANES), jnp.int32),
            jax.ShapeDtypeStruct((8, LANES), jnp.int32),
        ],
        compiler_params=pltpu.CompilerParams(vmem_limit_bytes=VMEM_LIMIT),
        name="moe_route",
    )(e_grid)


def _invert_kernel(dest_ref, tok_ref):
    def zero(p, carry):
        tok_ref[p] = 0
        return carry
    lax.fori_loop(0, SORTED_ROWS, zero, 0, unroll=GATHER_UNROLL)

    def place(a, carry):
        row = lax.shift_right_logical(a, LANES.bit_length() - 1)
        tok_ref[dest_ref[row, a & (LANES - 1)]] = lax.shift_right_logical(a, 1)
        return carry
    lax.fori_loop(0, N_ASSIGN, place, 0, unroll=GATHER_UNROLL)


def _invert(dest):
    used_rows = -(-N_ASSIGN // (8 * LANES)) * 8
    return pl.pallas_call(
        _invert_kernel,
        grid=(1,),
        in_specs=[pl.BlockSpec((used_rows, LANES), lambda i: (0, 0), memory_space=pltpu.SMEM)],
        out_specs=pl.BlockSpec((SORTED_ROWS,), lambda i: (0,), memory_space=pltpu.SMEM),
        out_shape=jax.ShapeDtypeStruct((SORTED_ROWS,), jnp.int32),
        name="moe_invert",
    )(dest)


def _expert_kernel(tile_expert, n_tiles, tok_cur_ref, tok_next_ref, h_hbm, w1_ref, w3_ref, w2_ref, y_ref,
                   xbuf, w1b, w3b, w2b, gsem):
    i = pl.program_id(0)
    n_used = n_tiles[0]
    slot = i % 2

    def start_gather(tok_ref, s):
        def body(r, carry):
            src = h_hbm.at[pl.ds(tok_ref[0, 0, r], 1)]
            pltpu.make_async_copy(src, xbuf.at[s, pl.ds(r, 1)], gsem.at[s]).start()
            return carry
        lax.fori_loop(0, TM_EXPERT, body, 0, unroll=GATHER_UNROLL)

    def wait_gather(s):
        pltpu.make_async_copy(h_hbm.at[pl.ds(0, TM_EXPERT)], xbuf.at[s], gsem.at[s]).wait()

    @pl.when(i == 0)
    def _():
        start_gather(tok_cur_ref, 0)

    @pl.when(i < n_used)
    def _():
        wait_gather(slot)
        start_gather(tok_next_ref, 1 - slot)
        first_of_expert = jnp.logical_or(i == 0, tile_expert[i] != tile_expert[jnp.maximum(i - 1, 0)])

        @pl.when(first_of_expert)
        def _():
            w1b[...] = w1_ref[0, 0].astype(BF16)
            w3b[...] = w3_ref[0, 0].astype(BF16)
            w2b[...] = w2_ref[0, 0].astype(BF16)

        x = xbuf[slot].astype(BF16)
        a = _dot(x, w1b[...])
        g = _dot(x, w3b[...])
        he = (a * _sigmoid(a) * g).astype(BF16)
        y_ref[...] = _dot(he, w2b[...])

    @pl.when(i == n_used)
    def _():
        wait_gather(slot)

    @pl.when(i >= n_used)
    def _():
        y_ref[...] = jnp.zeros(y_ref.shape, F32)


def _experts(h1, w1, w3, w2, layer, tile_expert, n_tiles, row_token):
    def cur(i, te, nt):
        return (i, 0, 0)

    def nxt(i, te, nt):
        return (jnp.minimum(i + 1, N_EXPERT_TILES - 1), 0, 0)

    def wmap(i, te, nt):
        return (layer, te[i], 0, 0)

    grid_spec = pltpu.PrefetchScalarGridSpec(
        num_scalar_prefetch=2,
        grid=(N_EXPERT_TILES,),
        in_specs=[
            pl.BlockSpec((1, 1, TM_EXPERT), cur, memory_space=pltpu.SMEM),
            pl.BlockSpec((1, 1, TM_EXPERT), nxt, memory_space=pltpu.SMEM),
            pl.BlockSpec(memory_space=pl.ANY),
            pl.BlockSpec((1, 1, D_MODEL, D_EXPERT), wmap),
            pl.BlockSpec((1, 1, D_MODEL, D_EXPERT), wmap),
            pl.BlockSpec((1, 1, D_EXPERT, D_MODEL), wmap),
        ],
        out_specs=pl.BlockSpec((TM_EXPERT, D_MODEL), lambda i, te, nt: (i, 0)),
        scratch_shapes=[
            pltpu.VMEM((2, TM_EXPERT, D_MODEL), F32),
            pltpu.VMEM((D_MODEL, D_EXPERT), BF16),
            pltpu.VMEM((D_MODEL, D_EXPERT), BF16),
            pltpu.VMEM((D_EXPERT, D_MODEL), BF16),
            pltpu.SemaphoreType.DMA((2,)),
        ],
    )
    return pl.pallas_call(
        _expert_kernel,
        grid_spec=grid_spec,
        out_shape=jax.ShapeDtypeStruct((SORTED_ROWS, D_MODEL), F32),
        compiler_params=_cparams(("arbitrary",)),
        name="moe_experts",
    )(tile_expert, n_tiles, row_token, row_token, h1, w1, w3, w2)


def _combine_kernel(tm, n_steps, final, pos_cur_ref, pos_next_ref, h1_ref, wts_ref, g_ref, b_ref, y_hbm,
                    *rest):
    out_refs, (ybuf, sem) = rest[:-2], rest[-2:]
    i = pl.program_id(0)
    slot = i % 2

    def start_gather(pos_ref, s):
        def body(r, carry):
            for j in range(2):
                src = y_hbm.at[pl.ds(pos_ref[0, 0, 2 * r + j], 1)]
                pltpu.make_async_copy(src, ybuf.at[s, j, pl.ds(r, 1)], sem.at[s]).start()
            return carry
        lax.fori_loop(0, tm, body, 0, unroll=GATHER_UNROLL // 2)

    @pl.when(i == 0)
    def _():
        start_gather(pos_cur_ref, 0)

    for j in range(2):
        pltpu.make_async_copy(y_hbm.at[pl.ds(0, tm)], ybuf.at[slot, j], sem.at[slot]).wait()

    @pl.when(i + 1 < n_steps)
    def _():
        start_gather(pos_next_ref, 1 - slot)

    wts = wts_ref[...]
    y = wts[:, 0:1] * ybuf[slot, 0] + wts[:, 1:2] * ybuf[slot, 1]
    h2 = _layer_norm_rows(ALPHA * h1_ref[...] + y, g_ref[...], b_ref[...])
    if final:
        out_refs[0][...] = h2
    else:
        h2 = _zero_pad_rows(h2, i * tm)
        out_refs[0][...] = h2
        out_refs[1][...] = h2.astype(BF16)


def _combine(h1, y_sorted, pos, wts, g, b, final):
    tm = TM_FINAL if final else TM_LN
    rows = SEQ if final else LP
    n_steps = rows // tm
    pos3 = pos[:2 * rows].reshape(n_steps, 1, 2 * tm)
    row = lambda i: (i, 0)
    const2 = lambda i: (0, 0)
    if final:
        out_specs = [pl.BlockSpec((tm, D_MODEL), row)]
        out_shape = [jax.ShapeDtypeStruct((rows, D_MODEL), F32)]
    else:
        out_specs = [pl.BlockSpec((tm, D_MODEL), row), pl.BlockSpec((tm, D_MODEL), row)]
        out_shape = [jax.ShapeDtypeStruct((rows, D_MODEL), F32), jax.ShapeDtypeStruct((rows, D_MODEL), BF16)]
    return pl.pallas_call(
        functools.partial(_combine_kernel, tm, n_steps, final),
        grid=(n_steps,),
        in_specs=[
            pl.BlockSpec((1, 1, 2 * tm), lambda i: (i, 0, 0), memory_space=pltpu.SMEM),
            pl.BlockSpec((1, 1, 2 * tm), lambda i: (jnp.minimum(i + 1, n_steps - 1), 0, 0), memory_space=pltpu.SMEM),
            pl.BlockSpec((tm, D_MODEL), row),
            pl.BlockSpec((tm, LANES), row),
            pl.BlockSpec((1, D_MODEL), const2),
            pl.BlockSpec((1, D_MODEL), const2),
            pl.BlockSpec(memory_space=pl.ANY),
        ],
        out_specs=out_specs,
        out_shape=out_shape,
        scratch_shapes=[pltpu.VMEM((2, 2, tm, D_MODEL), F32), pltpu.SemaphoreType.DMA((2,))],
        compiler_params=_cparams(("arbitrary",)),
        name="moe_combine_ln_final" if final else "moe_combine_ln",
    )(pos3, pos3, h1, wts, g, b, y_sorted)


def _pad_lanes(v):
    return jnp.pad(v, ((0, 0), (0, LANES - v.shape[1])))


def kernel(x, meta_tokens, emb_ln_g, emb_ln_b, hgrn_lb_logits, w_in, conv_w, conv_b, ig_b, fg_b,
           mlstm_norm_g, pool_w, pool_scale, hgrn_norm_g, w_out, ln1_g, ln1_b,
           w_router_group, b_router_group, w_router_expert, b_router_expert, w1, w3, w2,
           ln2_g, ln2_b):
    assert x.shape == (1, SEQ, D_MODEL) and x.dtype == F32
    row2 = lambda v: v.reshape(1, -1)
    meta_blk = jnp.pad(meta_tokens.astype(F32), ((META_PAD, 0), (0, 0)))
    h, hb = _embed(x.reshape(SEQ, D_MODEL), meta_blk, row2(emb_ln_g), row2(emb_ln_b))

    out = None
    for l in range(DEPTH):
        p = _in_proj(hb, w_in, l)
        y_pool = _pool(p, pool_w[l], row2(pool_scale[l]))
        gate_bias = _pad_lanes(jnp.concatenate([ig_b[l], fg_b[l]]).reshape(1, -1))
        y_m = _mlstm(p, conv_w[l], row2(conv_b[l]), gate_bias, row2(mlstm_norm_g[l]))
        y_h = _hgrn(p, hgrn_lb_logits, row2(hgrn_norm_g[l]), l)

        w_r = _pad_lanes(jnp.concatenate([w_router_group[l], w_router_expert[l]], axis=1))
        wr_hi, wr_lo = _split_bf16(w_r)
        b_r = _pad_lanes(jnp.concatenate([b_router_group[l], b_router_expert[l]]).reshape(1, -1))
        h1, eid, wts = _out_router(y_pool, y_m, y_h, h, w_out[l].astype(BF16), row2(ln1_g[l]), row2(ln1_b[l]),
                                   wr_hi, wr_lo, b_r)

        e_flat = jnp.pad(eid[:, 0:2].reshape(-1), (0, ROUTE_ROWS * LANES - N_ASSIGN), constant_values=-1)
        dest, tab = _route(e_flat.reshape(ROUTE_ROWS, LANES))
        row_token = _invert(dest).reshape(N_EXPERT_TILES, 1, TM_EXPERT)
        y_sorted = _experts(h1, w1, w3, w2, l, tab[0, :N_EXPERT_TILES], tab[1, :1], row_token)
        pos = dest.reshape(-1)
        if l + 1 < DEPTH:
            h, hb = _combine(h1, y_sorted, pos, wts, row2(ln2_g[l]), row2(ln2_b[l]), final=False)
        else:
            (out,) = _combine(h1, y_sorted, pos, wts, row2(ln2_g[l]), row2(ln2_b[l]), final=True)
    return out.reshape(1, SEQ, D_MODEL)
```

```python
import functools

import jax
import jax.numpy as jnp
import numpy as np
from jax import lax
from jax.experimental import pallas as pl
from jax.experimental.pallas import tpu as pltpu

F32 = jnp.float32
BF16 = jnp.bfloat16

D_MODEL = 2048
SEQ = 8192
DEPTH = 2
N_META = 16
CHUNK = 64
D_POOL = D_MODEL // 4
POOL_WINDOWS = (2, 4, 8, 16)
POOL_GROUP = D_POOL // len(POOL_WINDOWS)
D_MLSTM = 3 * D_MODEL // 8
N_HEADS = 6
HEAD_DIM = D_MLSTM // N_HEADS
CONV_K = 4
D_HGRN = D_MODEL - D_POOL - D_MLSTM
N_GROUPS = 4
EXPERTS_PER_GROUP = 8
N_EXPERTS = N_GROUPS * EXPERTS_PER_GROUP
D_EXPERT = D_MODEL // 4
ALPHA = (2 * DEPTH) ** 0.25
LN_EPS = 1e-5
NEG_INF = float("-inf")

LANES = 128
LP = SEQ + CHUNK
META_ROW0 = SEQ
META_PAD = CHUNK - N_META
N_CHUNKS = LP // CHUNK
LP_EMBED = 17 * 512

COL_MQK = 0
COL_MV = 2 * D_MLSTM
COL_MO = COL_MV + D_MLSTM
COL_HQ = COL_MO + D_MLSTM
COL_HF = COL_HQ + D_HGRN
COL_HI = COL_HF + D_HGRN
COL_HG = COL_HI + D_HGRN
COL_POOL = COL_HG + D_HGRN
COL_GATE = COL_POOL + D_POOL
P_COLS = COL_GATE + 2 * LANES

TM_BIG = 2752
TN_IN = 256
TM_MID = 688
TM_LN = 192
TM_FINAL = 256
TM_EXPERT = 256
N_ASSIGN = 2 * LP
N_EXPERT_TILES = (N_ASSIGN + N_EXPERTS * (TM_EXPERT - 1)) // TM_EXPERT + 1
VMEM_LIMIT = 56 * 1024 * 1024


def _cparams(sem):
    return pltpu.CompilerParams(dimension_semantics=sem, vmem_limit_bytes=VMEM_LIMIT)


def _sigmoid(x):
    return 1.0 / (1.0 + jnp.exp(-x))


def _log_sigmoid(x):
    return jnp.minimum(x, 0.0) - jnp.log1p(jnp.exp(-jnp.abs(x)))


def _layer_norm_rows(x, g, b):
    mu = jnp.mean(x, axis=-1, keepdims=True)
    xc = x - mu
    var = jnp.mean(xc * xc, axis=-1, keepdims=True)
    return xc * lax.rsqrt(var + LN_EPS) * g + b


def _dot(a, b):
    return jnp.dot(a, b, preferred_element_type=F32)


def _dot_nt(a, b):
    return lax.dot_general(a, b, (((1,), (1,)), ((), ())), preferred_element_type=F32)


def _dot_tn(a, b):
    return lax.dot_general(a, b, (((0,), (0,)), ((), ())), preferred_element_type=F32)


def _split_bf16(x):
    hi = x.astype(BF16)
    lo = (x - hi.astype(F32)).astype(BF16)
    return hi, lo


def _chunk_cumsum(x):
    r = lax.broadcasted_iota(jnp.int32, (CHUNK, CHUNK), 0)
    c = lax.broadcasted_iota(jnp.int32, (CHUNK, CHUNK), 1)
    tri = jnp.where(r >= c, 1.0, 0.0).astype(BF16)
    hi, lo = _split_bf16(x)
    return _dot(tri, hi) + _dot(tri, lo)


def _embed_kernel(x_ref, meta_ref, g_ref, b_ref, h_ref, hb_ref):
    i = pl.program_id(0)
    g = g_ref[...]
    b = b_ref[...]

    @pl.when(i < SEQ // 512)
    def _():
        y = _layer_norm_rows(x_ref[...], g, b)
        h_ref[...] = y
        hb_ref[...] = y.astype(BF16)

    @pl.when(i == SEQ // 512)
    def _():
        y = _layer_norm_rows(meta_ref[...], g, b)
        row = lax.broadcasted_iota(jnp.int32, (CHUNK, D_MODEL), 0)
        y = jnp.where(row >= META_PAD, y, 0.0)
        h_ref[0:CHUNK, :] = y
        hb_ref[0:CHUNK, :] = y.astype(BF16)
        h_ref[CHUNK:512, :] = jnp.zeros((512 - CHUNK, D_MODEL), F32)
        hb_ref[CHUNK:512, :] = jnp.zeros((512 - CHUNK, D_MODEL), BF16)


def _embed(x2d, meta_blk, g, b):
    nx = SEQ // 512
    return pl.pallas_call(
        _embed_kernel,
        grid=(nx + 1,),
        in_specs=[
            pl.BlockSpec((512, D_MODEL), lambda i: (jnp.minimum(i, nx - 1), 0)),
            pl.BlockSpec((CHUNK, D_MODEL), lambda i: (0, 0)),
            pl.BlockSpec((1, D_MODEL), lambda i: (0, 0)),
            pl.BlockSpec((1, D_MODEL), lambda i: (0, 0)),
        ],
        out_specs=[
            pl.BlockSpec((512, D_MODEL), lambda i: (i, 0)),
            pl.BlockSpec((512, D_MODEL), lambda i: (i, 0)),
        ],
        out_shape=[
            jax.ShapeDtypeStruct((LP_EMBED, D_MODEL), F32),
            jax.ShapeDtypeStruct((LP_EMBED, D_MODEL), BF16),
        ],
        compiler_params=_cparams(("arbitrary",)),
        name="embed_ln",
    )(x2d, meta_blk, g, b)


GATE_COLS = 2 * N_HEADS
W_IN_GATE0 = D_POOL + 4 * D_MLSTM
N_TILES_A = W_IN_GATE0 // TN_IN
N_TILES_H = 4 * D_HGRN // TN_IN
N_TILES_POOL = D_POOL // TN_IN


def _in_proj_kernel(x_ref, wa_ref, wb_ref, o_ref):
    j = pl.program_id(1)
    shifted = jnp.logical_and(j >= N_TILES_A, j < N_TILES_A + N_TILES_H)

    @pl.when(jnp.logical_not(shifted))
    def _():
        o_ref[...] = _dot(x_ref[...], wa_ref[0].astype(BF16))

    @pl.when(shifted)
    def _():
        w = jnp.concatenate([wa_ref[0], wb_ref[0]], axis=1)[:, GATE_COLS:GATE_COLS + TN_IN]
        o_ref[...] = _dot(x_ref[...], w.astype(BF16))


def _in_proj(hb, w_in, layer):
    n_main = N_TILES_A + N_TILES_H

    def wa_map(i, j):
        return (layer, 0, jnp.where(j < n_main, j, N_TILES_A))

    def wb_map(i, j):
        shifted = jnp.logical_and(j >= N_TILES_A, j < n_main)
        return (layer, 0, jnp.where(shifted, (j + 1) * (TN_IN // LANES), 0))

    def out_map(i, j):
        return (i, jnp.where(j < N_TILES_POOL, j + n_main - N_TILES_POOL, jnp.where(j < n_main, j - N_TILES_POOL, j)))

    return pl.pallas_call(
        _in_proj_kernel,
        grid=(LP // TM_BIG, P_COLS // TN_IN),
        in_specs=[
            pl.BlockSpec((TM_BIG, D_MODEL), lambda i, j: (i, 0)),
            pl.BlockSpec((1, D_MODEL, TN_IN), wa_map),
            pl.BlockSpec((1, D_MODEL, LANES), wb_map),
        ],
        out_specs=pl.BlockSpec((TM_BIG, TN_IN), out_map),
        out_shape=jax.ShapeDtypeStruct((LP, P_COLS), F32),
        compiler_params=_cparams(("arbitrary", "arbitrary")),
        name="in_proj",
    )(hb, w_in, w_in)


POOL_HALO = 16


def _pool_kernel(u_ref, halo_ref, pw_ref, ps_ref, y_ref, ubuf):
    i = pl.program_id(0)
    u = u_ref[...]
    ubuf[0:POOL_HALO, :] = halo_ref[...]
    ubuf[POOL_HALO:POOL_HALO + TM_MID, :] = u
    row = i * TM_MID + lax.broadcasted_iota(jnp.int32, (TM_MID, POOL_GROUP), 0)
    pos = jnp.where(row >= META_ROW0 + META_PAD, row - (META_ROW0 + META_PAD) + 1, 2 * POOL_HALO)
    scale = ps_ref[...]
    for gi, w in enumerate(POOL_WINDOWS):
        cs = slice(gi * POOL_GROUP, (gi + 1) * POOL_GROUP)
        ug = u[:, cs]
        acc = ug
        for j in range(1, w):
            acc = acc + ubuf[POOL_HALO - j:POOL_HALO - j + TM_MID, cs]
        div = jnp.minimum(pos, w).astype(F32)
        d = acc / div - ug
        yg = _dot(d.astype(BF16), pw_ref[gi].astype(BF16))
        y_ref[:, cs] = (yg * scale[:, cs]).astype(BF16)


def _pool(p, pool_w, pool_scale):
    blocks_per_tile = TM_MID // POOL_HALO
    last_meta_block = (LP - POOL_HALO) // POOL_HALO

    def halo_map(i):
        return (jnp.where(i == 0, last_meta_block, i * blocks_per_tile - 1), COL_POOL // D_POOL)

    return pl.pallas_call(
        _pool_kernel,
        grid=(LP // TM_MID,),
        in_specs=[
            pl.BlockSpec((TM_MID, D_POOL), lambda i: (i, COL_POOL // D_POOL)),
            pl.BlockSpec((POOL_HALO, D_POOL), halo_map),
            pl.BlockSpec((len(POOL_WINDOWS), POOL_GROUP, POOL_GROUP), lambda i: (0, 0, 0)),
            pl.BlockSpec((1, D_POOL), lambda i: (0, 0)),
        ],
        out_specs=pl.BlockSpec((TM_MID, D_POOL), lambda i: (i, 0)),
        out_shape=jax.ShapeDtypeStruct((LP, D_POOL), BF16),
        scratch_shapes=[pltpu.VMEM((POOL_HALO + TM_MID, D_POOL), F32)],
        compiler_params=_cparams(("arbitrary",)),
        name="pool_mixer",
    )(p, p, pool_w, pool_scale)


def _chunk_block(c):
    return (c + N_CHUNKS - 1) % N_CHUNKS


CONV_HALO = 8


def _mlstm_kernel(mqk_ref, mv_ref, mo_ref, gt_ref, cw_ref, cb_ref, gb_ref, ng_ref, y_ref,
                  c_sc, n_sc, m_sc, xbuf):
    c = pl.program_id(0)

    @pl.when(c == 0)
    def _():
        c_sc[...] = jnp.zeros(c_sc.shape, F32)
        n_sc[...] = jnp.zeros(n_sc.shape, F32)
        m_sc[...] = jnp.full(m_sc.shape, NEG_INF, F32)
        xbuf[0:CONV_HALO, :] = jnp.zeros((CONV_HALO, 2 * D_MLSTM), F32)

    x = mqk_ref[...]
    xbuf[CONV_HALO:CONV_HALO + CHUNK, :] = x
    cw = cw_ref[...]
    conv = cb_ref[...] + cw[CONV_K - 1:CONV_K, :] * x
    for j in range(CONV_K - 1):
        off = CONV_HALO - (CONV_K - 1) + j
        conv = conv + cw[j:j + 1, :] * xbuf[off:off + CHUNK, :]
    xbuf[0:CONV_HALO, :] = x[CHUNK - CONV_HALO:CHUNK, :]
    qk = conv * _sigmoid(conv)
    v_all = mv_ref[...]
    og_all = _sigmoid(mo_ref[...])
    ng = ng_ref[...]

    z = gt_ref[...] + gb_ref[...]
    lane = lax.broadcasted_iota(jnp.int32, (CHUNK, LANES), 1)
    row = lax.broadcasted_iota(jnp.int32, (CHUNK, LANES), 0)
    valid = row >= jnp.where(c > 0, 0, META_PAD)
    ig = jnp.where(valid, z, NEG_INF)
    lf = jnp.where(valid, _log_sigmoid(z), 0.0)
    g_cum = _chunk_cumsum(lf)
    gcol = jnp.where(lane < N_HEADS, ig, g_cum)
    grow = gcol.T

    r64 = lax.broadcasted_iota(jnp.int32, (CHUNK, CHUNK), 0)
    c64 = lax.broadcasted_iota(jnp.int32, (CHUNK, CHUNK), 1)
    causal = r64 >= c64
    k_scale = HEAD_DIM ** -0.5

    for h in range(N_HEADS):
        hs = slice(h * HEAD_DIM, (h + 1) * HEAD_DIM)
        q = qk[:, hs]
        k = qk[:, D_MLSTM + h * HEAD_DIM:D_MLSTM + (h + 1) * HEAD_DIM] * k_scale
        v = v_all[:, hs].astype(BF16)
        qb = q.astype(BF16)
        g_t = gcol[:, N_HEADS + h:N_HEADS + h + 1]
        ig_t = gcol[:, h:h + 1]
        g_s = grow[N_HEADS + h:N_HEADS + h + 1, :]
        ig_s = grow[h:h + 1, :]
        m_prev = m_sc[h:h + 1, 0:1]
        n_prev = n_sc[h:h + 1, :]
        c_prev = c_sc[h]

        dmat = jnp.where(causal, g_t - g_s + ig_s, NEG_INF)
        m_inter = g_t + m_prev
        m_t = jnp.maximum(jnp.max(dmat, axis=1, keepdims=True), m_inter)
        m_ts = jnp.where(m_t == NEG_INF, 0.0, m_t)
        s = _dot_nt(qb, k.astype(BF16)) * jnp.exp(dmat - m_ts)
        inter = jnp.exp(m_inter - m_ts)
        num = _dot(s.astype(BF16), v) + inter * _dot(qb, c_prev.astype(BF16))
        den = jnp.sum(s, axis=1, keepdims=True) + inter * jnp.sum(q * n_prev, axis=1, keepdims=True)
        hh = num / jnp.maximum(jnp.abs(den), jnp.exp(-m_ts))

        g_last = gcol[CHUNK - 1:CHUNK, N_HEADS + h:N_HEADS + h + 1]
        w_col = g_last - g_t + ig_t
        w_row = g_last - g_s + ig_s
        m_new = jnp.maximum(g_last + m_prev, jnp.max(w_row, axis=1, keepdims=True))
        decay = jnp.exp(g_last + m_prev - m_new)
        wk = k * jnp.exp(w_col - m_new)
        c_sc[h] = decay * c_prev + _dot_tn(wk.astype(BF16), v)
        n_sc[h:h + 1, :] = decay * n_prev + jnp.sum(wk, axis=0, keepdims=True)
        m_sc[h:h + 1, :] = jnp.broadcast_to(m_new, (1, LANES))

        mu = jnp.mean(hh, axis=1, keepdims=True)
        hc = hh - mu
        var = jnp.mean(hc * hc, axis=1, keepdims=True)
        y = hc * lax.rsqrt(var + LN_EPS) * ng[:, hs] * og_all[:, hs]
        y_ref[:, hs] = y.astype(BF16)


def _mlstm(p, conv_w, conv_b, gate_bias, norm_g):
    def col(block_w, start):
        return lambda c: (_chunk_block(c), start // block_w)

    const2 = lambda c: (0, 0)
    return pl.pallas_call(
        _mlstm_kernel,
        grid=(N_CHUNKS,),
        in_specs=[
            pl.BlockSpec((CHUNK, 2 * D_MLSTM), col(2 * D_MLSTM, COL_MQK)),
            pl.BlockSpec((CHUNK, D_MLSTM), col(D_MLSTM, COL_MV)),
            pl.BlockSpec((CHUNK, D_MLSTM), col(D_MLSTM, COL_MO)),
            pl.BlockSpec((CHUNK, LANES), col(LANES, COL_GATE)),
            pl.BlockSpec((CONV_K, 2 * D_MLSTM), const2),
            pl.BlockSpec((1, 2 * D_MLSTM), const2),
            pl.BlockSpec((1, LANES), const2),
            pl.BlockSpec((1, D_MLSTM), const2),
        ],
        out_specs=pl.BlockSpec((CHUNK, D_MLSTM), lambda c: (_chunk_block(c), 0)),
        out_shape=jax.ShapeDtypeStruct((LP, D_MLSTM), BF16),
        scratch_shapes=[
            pltpu.VMEM((N_HEADS, HEAD_DIM, HEAD_DIM), F32),
            pltpu.VMEM((8, HEAD_DIM), F32),
            pltpu.VMEM((8, LANES), F32),
            pltpu.VMEM((CONV_HALO + CHUNK, 2 * D_MLSTM), F32),
        ],
        compiler_params=_cparams(("arbitrary",)),
        name="mlstm_mixer",
    )(p, p, p, p, conv_w, conv_b, gate_bias, norm_g)


N_LEVELS = 6


def _hgrn_tables():
    t = np.arange(CHUNK)
    sel = np.zeros((N_LEVELS * CHUNK, CHUNK), np.float32)
    mask = np.zeros((N_LEVELS + 1, CHUNK, CHUNK), np.float32)
    upper = np.zeros((CHUNK, LANES), np.float32)
    for l in range(N_LEVELS):
        half = 1 << l
        ref_row = (t // (2 * half)) * (2 * half) + half - 1
        sel[l * CHUNK + t, ref_row] = 1.0
        is_upper = (t // half) % 2 == 1
        upper[:, l] = is_upper
        same = (t[:, None] // (2 * half)) == (t[None, :] // (2 * half))
        mask[l] = same & is_upper[:, None] & ~is_upper[None, :]
    mask[N_LEVELS] = np.eye(CHUNK)
    return sel, mask, upper


def _hgrn_kernel(layer, hq_ref, hf_ref, hi_ref, hg_ref, lbl_ref, ng_ref, sel_ref, mask_ref, up_ref,
                 y_ref, st_sc):
    c = pl.program_id(0)

    @pl.when(c == 0)
    def _():
        st_sc[...] = jnp.zeros(st_sc.shape, F32)

    lbl = lbl_ref[...]
    e = jnp.exp(lbl - jnp.max(lbl, axis=0, keepdims=True))
    sm = e / jnp.sum(e, axis=0, keepdims=True)
    lb = jnp.sum(sm[0:layer + 1, :], axis=0, keepdims=True) - sm[0:1, :]

    z = hf_ref[...]
    a = jnp.log(lb)
    bb = jnp.log1p(-lb) + _log_sigmoid(z)
    mx = jnp.maximum(a, bb)
    log_f = mx + jnp.log(jnp.exp(a - mx) + jnp.exp(bb - mx))
    kk = (1.0 - lb) * _sigmoid(-z)
    hq = hq_ref[...]
    q = hq * _sigmoid(hq)
    v = hi_ref[...].astype(BF16)
    hg = hg_ref[...]
    gate = hg * _sigmoid(hg) * ng_ref[...]

    b = _chunk_cumsum(log_f)
    b_hi, b_lo = _split_bf16(b)
    sel = sel_ref[...]
    refs = _dot(sel, b_hi) + _dot(sel, b_lo)
    up = up_ref[...]

    amats = [None] * N_HEADS
    for l in range(N_LEVELS + 1):
        if l < N_LEVELS:
            ref_l = refs[l * CHUNK:(l + 1) * CHUNK, :]
            sign = 2.0 * up[:, l:l + 1] - 1.0
            zl = jnp.exp(sign * (b - ref_l))
            ql = (q * zl).astype(BF16)
            kl = (kk * zl).astype(BF16)
        else:
            ql = q.astype(BF16)
            kl = kk.astype(BF16)
        ml = mask_ref[l]
        for h in range(N_HEADS):
            hs = slice(h * HEAD_DIM, (h + 1) * HEAD_DIM)
            part = ml * _dot_nt(ql[:, hs], kl[:, hs])
            amats[h] = part if amats[h] is None else amats[h] + part

    b_last = b[CHUNK - 1:CHUNK, :]
    qe = (q * jnp.exp(b)).astype(BF16)
    kd = (kk * jnp.exp(b_last - b)).astype(BF16)
    e_last = jnp.exp(b_last)
    for h in range(N_HEADS):
        hs = slice(h * HEAD_DIM, (h + 1) * HEAD_DIM)
        st = st_sc[h]
        o = _dot(amats[h].astype(BF16), v[:, hs]) + _dot_nt(qe[:, hs], st.astype(BF16))
        st_sc[h] = e_last[:, hs] * st + _dot_tn(v[:, hs], kd[:, hs])
        o = o * lax.rsqrt(jnp.mean(o * o, axis=1, keepdims=True) + LN_EPS)
        y_ref[:, hs] = (o * gate[:, hs]).astype(BF16)


def _hgrn(p, lb_logits, norm_g, layer):
    def col(start):
        return lambda c: (_chunk_block(c), start // D_HGRN)

    sel, mask, upper = _hgrn_tables()
    const2 = lambda c: (0, 0)
    return pl.pallas_call(
        functools.partial(_hgrn_kernel, layer),
        grid=(N_CHUNKS,),
        in_specs=[
            pl.BlockSpec((CHUNK, D_HGRN), col(COL_HQ)),
            pl.BlockSpec((CHUNK, D_HGRN), col(COL_HF)),
            pl.BlockSpec((CHUNK, D_HGRN), col(COL_HI)),
            pl.BlockSpec((CHUNK, D_HGRN), col(COL_HG)),
            pl.BlockSpec((DEPTH, D_HGRN), const2),
            pl.BlockSpec((1, D_HGRN), const2),
            pl.BlockSpec((N_LEVELS * CHUNK, CHUNK), const2),
            pl.BlockSpec((N_LEVELS + 1, CHUNK, CHUNK), lambda c: (0, 0, 0)),
            pl.BlockSpec((CHUNK, LANES), const2),
        ],
        out_specs=pl.BlockSpec((CHUNK, D_HGRN), lambda c: (_chunk_block(c), 0)),
        out_shape=jax.ShapeDtypeStruct((LP, D_HGRN), BF16),
        scratch_shapes=[pltpu.VMEM((N_HEADS, HEAD_DIM, HEAD_DIM), F32)],
        compiler_params=_cparams(("arbitrary",)),
        name="hgrn_mixer",
    )(p, p, p, p, lb_logits, norm_g, jnp.asarray(sel, BF16), jnp.asarray(mask, F32), jnp.asarray(upper, F32))


def _zero_pad_rows(y, row0):
    row = row0 + lax.broadcasted_iota(jnp.int32, y.shape, 0)
    is_pad = jnp.logical_and(row >= META_ROW0, row < META_ROW0 + META_PAD)
    return jnp.where(is_pad, 0.0, y)


def _first_argmax(x, lane, valid):
    xm = jnp.where(valid, x, NEG_INF)
    mx = jnp.max(xm, axis=1, keepdims=True)
    idx = jnp.min(jnp.where(jnp.logical_and(valid, xm == mx), lane, float(LANES)), axis=1, keepdims=True)
    return mx, idx


TOKEN_ROWS = D_MODEL // LANES
GATHER_PITCH = 24


def _store_token_linear(ref, x):
    n = x.shape[0]
    for k in range(TOKEN_ROWS):
        ref[pl.ds(k, n, stride=TOKEN_ROWS), :] = x[:, k * LANES:(k + 1) * LANES]


def _load_gathered(ref, n):
    return jnp.concatenate([ref[pl.ds(k, n, stride=GATHER_PITCH), :] for k in range(TOKEN_ROWS)], axis=1)


def _out_router_kernel(yp_ref, ym_ref, yh_ref, h_ref, wo_ref, g_ref, b_ref, wrh_ref, wrl_ref, br_ref,
                       h1_ref, hlin_ref, eid_ref, wts_ref):
    i = pl.program_id(0)
    acc = _dot(yp_ref[...], wo_ref[0:D_POOL, :])
    acc = acc + _dot(ym_ref[...], wo_ref[D_POOL:D_POOL + D_MLSTM, :])
    acc = acc + _dot(yh_ref[...], wo_ref[D_POOL + D_MLSTM:D_MODEL, :])
    h1 = _layer_norm_rows(ALPHA * h_ref[...] + acc, g_ref[...], b_ref[...])
    h1 = _zero_pad_rows(h1, i * TM_MID)
    h1_ref[...] = h1
    _store_token_linear(hlin_ref, h1)

    x_hi, x_lo = _split_bf16(h1)
    wrh = wrh_ref[...]
    logits = _dot(x_hi, wrh) + _dot(x_lo, wrh) + _dot(x_hi, wrl_ref[...]) + br_ref[...]
    lane = lax.broadcasted_iota(jnp.int32, logits.shape, 1).astype(F32)

    is_grp = lane < N_GROUPS
    g_max, g_idx = _first_argmax(logits, lane, is_grp)
    g_exp = jnp.where(is_grp, jnp.exp(logits - g_max), 0.0)
    p_grp = 1.0 / jnp.sum(g_exp, axis=1, keepdims=True)

    e_lo = N_GROUPS + g_idx * EXPERTS_PER_GROUP
    in_grp = jnp.logical_and(lane >= e_lo, lane < e_lo + EXPERTS_PER_GROUP)
    e_max, e1 = _first_argmax(logits, lane, in_grp)
    e_exp = jnp.where(in_grp, jnp.exp(logits - e_max), 0.0)
    p_exp = e_exp / jnp.sum(e_exp, axis=1, keepdims=True)
    p1, _ = _first_argmax(p_exp, lane, in_grp)
    rest = jnp.logical_and(in_grp, lane != e1)
    p2, e2 = _first_argmax(p_exp, lane, rest)
    psum = p1 + p2
    w1 = p_grp * p1 / psum
    w2 = p_grp * p2 / psum
    eid = jnp.where(lane == 0.0, e1 - N_GROUPS, jnp.where(lane == 1.0, e2 - N_GROUPS, 0.0))
    eid_ref[...] = eid.astype(jnp.int32)
    wts_ref[...] = jnp.where(lane == 0.0, w1, jnp.where(lane == 1.0, w2, 0.0))


def _out_router(y_pool, y_m, y_h, h, w_out, g, b, wr_hi, wr_lo, br):
    row = lambda i: (i, 0)
    const2 = lambda i: (0, 0)
    return pl.pallas_call(
        _out_router_kernel,
        grid=(LP // TM_MID,),
        in_specs=[
            pl.BlockSpec((TM_MID, D_POOL), row),
            pl.BlockSpec((TM_MID, D_MLSTM), row),
            pl.BlockSpec((TM_MID, D_HGRN), row),
            pl.BlockSpec((TM_MID, D_MODEL), row),
            pl.BlockSpec((D_MODEL, D_MODEL), const2, pipeline_mode=pl.Buffered(1)),
            pl.BlockSpec((1, D_MODEL), const2),
            pl.BlockSpec((1, D_MODEL), const2),
            pl.BlockSpec((D_MODEL, LANES), const2),
            pl.BlockSpec((D_MODEL, LANES), const2),
            pl.BlockSpec((1, LANES), const2),
        ],
        out_specs=[
            pl.BlockSpec((TM_MID, D_MODEL), row),
            pl.BlockSpec((TM_MID * TOKEN_ROWS, LANES), row),
            pl.BlockSpec((TM_MID, LANES), row),
            pl.BlockSpec((TM_MID, LANES), row),
        ],
        out_shape=[
            jax.ShapeDtypeStruct((LP, D_MODEL), F32),
            jax.ShapeDtypeStruct((LP * TOKEN_ROWS, LANES), F32),
            jax.ShapeDtypeStruct((LP, LANES), jnp.int32),
            jax.ShapeDtypeStruct((LP, LANES), F32),
        ],
        compiler_params=_cparams(("arbitrary",)),
        name="out_proj_ln_router",
    )(y_pool, y_m, y_h, h, w_out, g, b, wr_hi, wr_lo, br)


ROUTE_ROWS = 256
SORTED_ROWS = N_EXPERT_TILES * TM_EXPERT
GATHER_UNROLL = 8


def _route_kernel(e_ref, dest_ref, tab_ref):
    e = e_ref[...]
    lane = lax.broadcasted_iota(jnp.int32, (ROUTE_ROWS, LANES), 1)
    kk = lax.broadcasted_iota(jnp.int32, (LANES, LANES), 0)
    ll = lax.broadcasted_iota(jnp.int32, (LANES, LANES), 1)
    before_lane = jnp.where(kk < ll, 1.0, 0.0).astype(BF16)
    all_lanes = jnp.ones((LANES, LANES), BF16)
    rr = lax.broadcasted_iota(jnp.int32, (ROUTE_ROWS, ROUTE_ROWS), 0)
    cc = lax.broadcasted_iota(jnp.int32, (ROUTE_ROWS, ROUTE_ROWS), 1)
    before_row = jnp.where(cc < rr, 1.0, 0.0).astype(BF16)

    lane1 = lane[0:1, :]
    rank = jnp.zeros((ROUTE_ROWS, LANES), F32)
    counts = jnp.zeros((1, LANES), F32)
    masks = []
    for x in range(N_EXPERTS):
        m = jnp.where(e == x, 1.0, 0.0)
        mb = m.astype(BF16)
        in_row = _dot(mb, before_lane)
        row_tot = _dot(mb, all_lanes)
        rows_before = _dot(before_row, row_tot.astype(BF16))
        rank = rank + m * (in_row + rows_before)
        total = rows_before[ROUTE_ROWS - 1:ROUTE_ROWS, :] + row_tot[ROUTE_ROWS - 1:ROUTE_ROWS, :]
        counts = counts + jnp.where(lane1 == x, total, 0.0)
        masks.append(m)

    padded = jnp.floor((counts + (TM_EXPERT - 1)) * (1.0 / TM_EXPERT)) * TM_EXPERT
    p_hi, p_lo = _split_bf16(padded)
    start = _dot(p_hi, before_lane) + _dot(p_lo, before_lane)
    end = start + padded

    dest = rank
    tile0 = (lane1 * TM_EXPERT).astype(F32)
    n_before = jnp.zeros((1, LANES), F32)
    for x in range(N_EXPERTS):
        dest = dest + masks[x] * start[:, x:x + 1]
        n_before = n_before + jnp.where(end[:, x:x + 1] <= tile0, 1.0, 0.0)
    dest_ref[...] = dest.astype(jnp.int32)

    tile_expert = jnp.minimum(n_before, float(N_EXPERTS - 1))
    n_tiles = end[:, N_EXPERTS - 1:N_EXPERTS] * (1.0 / TM_EXPERT)
    row = lax.broadcasted_iota(jnp.int32, (8, LANES), 0)
    tab = jnp.where(row == 0, tile_expert, jnp.where(row == 1, n_tiles, 0.0))
    tab_ref[...] = tab.astype(jnp.int32)


def _route(e_grid):
    return pl.pallas_call(
        _route_kernel,
        out_shape=[
            jax.ShapeDtypeStruct((ROUTE_ROWS, LANES), jnp.int32),
            jax.ShapeDtypeStruct((8, LANES), jnp.int32),
        ],
        compiler_params=pltpu.CompilerParams(vmem_limit_bytes=VMEM_LIMIT),
        name="moe_route",
    )(e_grid)


def _invert_kernel(dest_ref, tok_ref):
    def zero(p, carry):
        tok_ref[p] = 0
        return carry
    lax.fori_loop(0, SORTED_ROWS, zero, 0, unroll=GATHER_UNROLL)

    def place(a, carry):
        row = lax.shift_right_logical(a, LANES.bit_length() - 1)
        tok_ref[dest_ref[row, a & (LANES - 1)]] = lax.shift_right_logical(a, 1)
        return carry
    lax.fori_loop(0, N_ASSIGN, place, 0, unroll=GATHER_UNROLL)


def _invert(dest):
    used_rows = -(-N_ASSIGN // (8 * LANES)) * 8
    return pl.pallas_call(
        _invert_kernel,
        grid=(1,),
        in_specs=[pl.BlockSpec((used_rows, LANES), lambda i: (0, 0), memory_space=pltpu.SMEM)],
        out_specs=pl.BlockSpec((SORTED_ROWS,), lambda i: (0,), memory_space=pltpu.SMEM),
        out_shape=jax.ShapeDtypeStruct((SORTED_ROWS,), jnp.int32),
        name="moe_invert",
    )(dest)


def _expert_kernel(tile_expert, n_tiles, tok_cur_ref, tok_next_ref, h_hbm, w1_ref, w3_ref, w2_ref, y_ref,
                   xbuf, w1b, w3b, w2b, gsem):
    i = pl.program_id(0)
    n_used = n_tiles[0]
    slot = i % 2

    def start_gather(tok_ref, s):
        def body(r, carry):
            src = h_hbm.at[pl.ds(pl.multiple_of(tok_ref[0, 0, r] * TOKEN_ROWS, TOKEN_ROWS), TOKEN_ROWS)]
            dst = xbuf.at[s, pl.ds(pl.multiple_of(r * GATHER_PITCH, 8), TOKEN_ROWS)]
            pltpu.make_async_copy(src, dst, gsem.at[s]).start()
            return carry
        lax.fori_loop(0, TM_EXPERT, body, 0, unroll=GATHER_UNROLL)

    def wait_gather(s):
        n = TM_EXPERT * TOKEN_ROWS
        pltpu.make_async_copy(h_hbm.at[pl.ds(0, n)], xbuf.at[s, pl.ds(0, n)], gsem.at[s]).wait()

    @pl.when(i == 0)
    def _():
        start_gather(tok_cur_ref, 0)

    @pl.when(i < n_used)
    def _():
        wait_gather(slot)
        start_gather(tok_next_ref, 1 - slot)
        first_of_expert = jnp.logical_or(i == 0, tile_expert[i] != tile_expert[jnp.maximum(i - 1, 0)])

        @pl.when(first_of_expert)
        def _():
            w1b[...] = w1_ref[0, 0].astype(BF16)
            w3b[...] = w3_ref[0, 0].astype(BF16)
            w2b[...] = w2_ref[0, 0].astype(BF16)

        x = _load_gathered(xbuf.at[slot], TM_EXPERT).astype(BF16)
        a = _dot(x, w1b[...])
        g = _dot(x, w3b[...])
        he = (a * _sigmoid(a) * g).astype(BF16)
        _store_token_linear(y_ref, _dot(he, w2b[...]))

    @pl.when(i == n_used)
    def _():
        wait_gather(slot)

    @pl.when(i >= n_used)
    def _():
        y_ref[...] = jnp.zeros(y_ref.shape, F32)


def _experts(h1, w1, w3, w2, layer, tile_expert, n_tiles, row_token):
    def cur(i, te, nt):
        return (i, 0, 0)

    def nxt(i, te, nt):
        return (jnp.minimum(i + 1, N_EXPERT_TILES - 1), 0, 0)

    def wmap(i, te, nt):
        return (layer, te[i], 0, 0)

    grid_spec = pltpu.PrefetchScalarGridSpec(
        num_scalar_prefetch=2,
        grid=(N_EXPERT_TILES,),
        in_specs=[
            pl.BlockSpec((1, 1, TM_EXPERT), cur, memory_space=pltpu.SMEM),
            pl.BlockSpec((1, 1, TM_EXPERT), nxt, memory_space=pltpu.SMEM),
            pl.BlockSpec(memory_space=pl.ANY),
            pl.BlockSpec((1, 1, D_MODEL, D_EXPERT), wmap),
            pl.BlockSpec((1, 1, D_MODEL, D_EXPERT), wmap),
            pl.BlockSpec((1, 1, D_EXPERT, D_MODEL), wmap),
        ],
        out_specs=pl.BlockSpec((TM_EXPERT * TOKEN_ROWS, LANES), lambda i, te, nt: (i, 0)),
        scratch_shapes=[
            pltpu.VMEM((2, TM_EXPERT * GATHER_PITCH, LANES), F32),
            pltpu.VMEM((D_MODEL, D_EXPERT), BF16),
            pltpu.VMEM((D_MODEL, D_EXPERT), BF16),
            pltpu.VMEM((D_EXPERT, D_MODEL), BF16),
            pltpu.SemaphoreType.DMA((2,)),
        ],
    )
    return pl.pallas_call(
        _expert_kernel,
        grid_spec=grid_spec,
        out_shape=jax.ShapeDtypeStruct((SORTED_ROWS * TOKEN_ROWS, LANES), F32),
        compiler_params=_cparams(("arbitrary",)),
        name="moe_experts",
    )(tile_expert, n_tiles, row_token, row_token, h1, w1, w3, w2)


def _combine_kernel(tm, n_steps, final, pos_cur_ref, pos_next_ref, h1_ref, wts_ref, g_ref, b_ref, y_hbm,
                    *rest):
    out_refs, (ybuf, sem) = rest[:-2], rest[-2:]
    i = pl.program_id(0)
    slot = i % 2

    def start_gather(pos_ref, s):
        def body(r, carry):
            for j in range(2):
                row0 = pl.multiple_of(pos_ref[0, 0, 2 * r + j] * TOKEN_ROWS, TOKEN_ROWS)
                dst = ybuf.at[s, j, pl.ds(pl.multiple_of(r * GATHER_PITCH, 8), TOKEN_ROWS)]
                pltpu.make_async_copy(y_hbm.at[pl.ds(row0, TOKEN_ROWS)], dst, sem.at[s]).start()
            return carry
        lax.fori_loop(0, tm, body, 0, unroll=GATHER_UNROLL // 2)

    @pl.when(i == 0)
    def _():
        start_gather(pos_cur_ref, 0)

    for j in range(2):
        n = tm * TOKEN_ROWS
        pltpu.make_async_copy(y_hbm.at[pl.ds(0, n)], ybuf.at[slot, j, pl.ds(0, n)], sem.at[slot]).wait()

    @pl.when(i + 1 < n_steps)
    def _():
        start_gather(pos_next_ref, 1 - slot)

    wts = wts_ref[...]
    y = wts[:, 0:1] * _load_gathered(ybuf.at[slot, 0], tm) + wts[:, 1:2] * _load_gathered(ybuf.at[slot, 1], tm)
    h2 = _layer_norm_rows(ALPHA * h1_ref[...] + y, g_ref[...], b_ref[...])
    if final:
        out_refs[0][...] = h2
    else:
        h2 = _zero_pad_rows(h2, i * tm)
        out_refs[0][...] = h2
        out_refs[1][...] = h2.astype(BF16)


def _combine(h1, y_sorted, pos, wts, g, b, final):
    tm = TM_FINAL if final else TM_LN
    rows = SEQ if final else LP
    n_steps = rows // tm
    pos3 = pos[:2 * rows].reshape(n_steps, 1, 2 * tm)
    row = lambda i: (i, 0)
    const2 = lambda i: (0, 0)
    if final:
        out_specs = [pl.BlockSpec((tm, D_MODEL), row)]
        out_shape = [jax.ShapeDtypeStruct((rows, D_MODEL), F32)]
    else:
        out_specs = [pl.BlockSpec((tm, D_MODEL), row), pl.BlockSpec((tm, D_MODEL), row)]
        out_shape = [jax.ShapeDtypeStruct((rows, D_MODEL), F32), jax.ShapeDtypeStruct((rows, D_MODEL), BF16)]
    return pl.pallas_call(
        functools.partial(_combine_kernel, tm, n_steps, final),
        grid=(n_steps,),
        in_specs=[
            pl.BlockSpec((1, 1, 2 * tm), lambda i: (i, 0, 0), memory_space=pltpu.SMEM),
            pl.BlockSpec((1, 1, 2 * tm), lambda i: (jnp.minimum(i + 1, n_steps - 1), 0, 0), memory_space=pltpu.SMEM),
            pl.BlockSpec((tm, D_MODEL), row),
            pl.BlockSpec((tm, LANES), row),
            pl.BlockSpec((1, D_MODEL), const2),
            pl.BlockSpec((1, D_MODEL), const2),
            pl.BlockSpec(memory_space=pl.ANY),
        ],
        out_specs=out_specs,
        out_shape=out_shape,
        scratch_shapes=[pltpu.VMEM((2, 2, tm * GATHER_PITCH, LANES), F32), pltpu.SemaphoreType.DMA((2,))],
        compiler_params=_cparams(("arbitrary",)),
        name="moe_combine_ln_final" if final else "moe_combine_ln",
    )(pos3, pos3, h1, wts, g, b, y_sorted)


def _pad_lanes(v):
    return jnp.pad(v, ((0, 0), (0, LANES - v.shape[1])))


def kernel(x, meta_tokens, emb_ln_g, emb_ln_b, hgrn_lb_logits, w_in, conv_w, conv_b, ig_b, fg_b,
           mlstm_norm_g, pool_w, pool_scale, hgrn_norm_g, w_out, ln1_g, ln1_b,
           w_router_group, b_router_group, w_router_expert, b_router_expert, w1, w3, w2,
           ln2_g, ln2_b):
    assert x.shape == (1, SEQ, D_MODEL) and x.dtype == F32
    row2 = lambda v: v.reshape(1, -1)
    meta_blk = jnp.pad(meta_tokens.astype(F32), ((META_PAD, 0), (0, 0)))
    h, hb = _embed(x.reshape(SEQ, D_MODEL), meta_blk, row2(emb_ln_g), row2(emb_ln_b))

    out = None
    for l in range(DEPTH):
        p = _in_proj(hb, w_in, l)
        y_pool = _pool(p, pool_w[l], row2(pool_scale[l]))
        gate_bias = _pad_lanes(jnp.concatenate([ig_b[l], fg_b[l]]).reshape(1, -1))
        y_m = _mlstm(p, conv_w[l], row2(conv_b[l]), gate_bias, row2(mlstm_norm_g[l]))
        y_h = _hgrn(p, hgrn_lb_logits, row2(hgrn_norm_g[l]), l)

        w_r = _pad_lanes(jnp.concatenate([w_router_group[l], w_router_expert[l]], axis=1))
        wr_hi, wr_lo = _split_bf16(w_r)
        b_r = _pad_lanes(jnp.concatenate([b_router_group[l], b_router_expert[l]]).reshape(1, -1))
        h1, h1_lin, eid, wts = _out_router(y_pool, y_m, y_h, h, w_out[l].astype(BF16), row2(ln1_g[l]),
                                           row2(ln1_b[l]), wr_hi, wr_lo, b_r)

        e_flat = jnp.pad(eid[:, 0:2].reshape(-1), (0, ROUTE_ROWS * LANES - N_ASSIGN), constant_values=-1)
        dest, tab = _route(e_flat.reshape(ROUTE_ROWS, LANES))
        row_token = _invert(dest).reshape(N_EXPERT_TILES, 1, TM_EXPERT)
        y_sorted = _experts(h1_lin, w1, w3, w2, l, tab[0, :N_EXPERT_TILES], tab[1, :1], row_token)
        pos = dest.reshape(-1)
        if l + 1 < DEPTH:
            h, hb = _combine(h1, y_sorted, pos, wts, row2(ln2_g[l]), row2(ln2_b[l]), final=False)
        else:
            (out,) = _combine(h1, y_sorted, pos, wts, row2(ln2_g[l]), row2(ln2_b[l]), final=True)
    return out.reshape(1, SEQ, D_MODEL)
```

```python
import functools

import jax
import jax.numpy as jnp
import numpy as np
from jax import lax
from jax.experimental import pallas as pl
from jax.experimental.pallas import tpu as pltpu

F32 = jnp.float32
BF16 = jnp.bfloat16

D_MODEL = 2048
SEQ = 8192
DEPTH = 2
N_META = 16
CHUNK = 64
D_POOL = D_MODEL // 4
POOL_WINDOWS = (2, 4, 8, 16)
POOL_GROUP = D_POOL // len(POOL_WINDOWS)
D_MLSTM = 3 * D_MODEL // 8
N_HEADS = 6
HEAD_DIM = D_MLSTM // N_HEADS
CONV_K = 4
D_HGRN = D_MODEL - D_POOL - D_MLSTM
N_GROUPS = 4
EXPERTS_PER_GROUP = 8
N_EXPERTS = N_GROUPS * EXPERTS_PER_GROUP
D_EXPERT = D_MODEL // 4
ALPHA = (2 * DEPTH) ** 0.25
LN_EPS = 1e-5
NEG_INF = float("-inf")

LANES = 128
LP = SEQ + CHUNK
META_ROW0 = SEQ
META_PAD = CHUNK - N_META
N_CHUNKS = LP // CHUNK
LP_EMBED = 17 * 512

COL_MQK = 0
COL_MV = 2 * D_MLSTM
COL_MO = COL_MV + D_MLSTM
COL_HQ = COL_MO + D_MLSTM
COL_HF = COL_HQ + D_HGRN
COL_HI = COL_HF + D_HGRN
COL_HG = COL_HI + D_HGRN
COL_POOL = COL_HG + D_HGRN
COL_GATE = COL_POOL + D_POOL
P_COLS = COL_GATE + 2 * LANES

TM_BIG = 2752
TN_IN = 256
TM_MID = 688
TM_LN = 192
TM_FINAL = 256
TM_EXPERT = 256
N_ASSIGN = 2 * LP
N_EXPERT_TILES = (N_ASSIGN + N_EXPERTS * (TM_EXPERT - 1)) // TM_EXPERT + 1
VMEM_LIMIT = 56 * 1024 * 1024


def _cparams(sem):
    return pltpu.CompilerParams(dimension_semantics=sem, vmem_limit_bytes=VMEM_LIMIT)


def _sigmoid(x):
    return 1.0 / (1.0 + jnp.exp(-x))


def _log_sigmoid(x):
    return jnp.minimum(x, 0.0) - jnp.log1p(jnp.exp(-jnp.abs(x)))


def _layer_norm_rows(x, g, b):
    mu = jnp.mean(x, axis=-1, keepdims=True)
    xc = x - mu
    var = jnp.mean(xc * xc, axis=-1, keepdims=True)
    return xc * lax.rsqrt(var + LN_EPS) * g + b


def _dot(a, b):
    return jnp.dot(a, b, preferred_element_type=F32)


def _dot_nt(a, b):
    return lax.dot_general(a, b, (((1,), (1,)), ((), ())), preferred_element_type=F32)


def _dot_tn(a, b):
    return lax.dot_general(a, b, (((0,), (0,)), ((), ())), preferred_element_type=F32)


def _split_bf16(x):
    hi = x.astype(BF16)
    lo = (x - hi.astype(F32)).astype(BF16)
    return hi, lo


def _chunk_cumsum(x):
    r = lax.broadcasted_iota(jnp.int32, (CHUNK, CHUNK), 0)
    c = lax.broadcasted_iota(jnp.int32, (CHUNK, CHUNK), 1)
    tri = jnp.where(r >= c, 1.0, 0.0).astype(BF16)
    hi, lo = _split_bf16(x)
    return _dot(tri, hi) + _dot(tri, lo)


def _embed_kernel(x_ref, meta_ref, g_ref, b_ref, h_ref, hb_ref):
    i = pl.program_id(0)
    g = g_ref[...]
    b = b_ref[...]

    @pl.when(i < SEQ // 512)
    def _():
        y = _layer_norm_rows(x_ref[...], g, b)
        h_ref[...] = y
        hb_ref[...] = y.astype(BF16)

    @pl.when(i == SEQ // 512)
    def _():
        y = _layer_norm_rows(meta_ref[...], g, b)
        row = lax.broadcasted_iota(jnp.int32, (CHUNK, D_MODEL), 0)
        y = jnp.where(row >= META_PAD, y, 0.0)
        h_ref[0:CHUNK, :] = y
        hb_ref[0:CHUNK, :] = y.astype(BF16)
        h_ref[CHUNK:512, :] = jnp.zeros((512 - CHUNK, D_MODEL), F32)
        hb_ref[CHUNK:512, :] = jnp.zeros((512 - CHUNK, D_MODEL), BF16)


def _embed(x2d, meta_blk, g, b):
    nx = SEQ // 512
    return pl.pallas_call(
        _embed_kernel,
        grid=(nx + 1,),
        in_specs=[
            pl.BlockSpec((512, D_MODEL), lambda i: (jnp.minimum(i, nx - 1), 0)),
            pl.BlockSpec((CHUNK, D_MODEL), lambda i: (0, 0)),
            pl.BlockSpec((1, D_MODEL), lambda i: (0, 0)),
            pl.BlockSpec((1, D_MODEL), lambda i: (0, 0)),
        ],
        out_specs=[
            pl.BlockSpec((512, D_MODEL), lambda i: (i, 0)),
            pl.BlockSpec((512, D_MODEL), lambda i: (i, 0)),
        ],
        out_shape=[
            jax.ShapeDtypeStruct((LP_EMBED, D_MODEL), F32),
            jax.ShapeDtypeStruct((LP_EMBED, D_MODEL), BF16),
        ],
        compiler_params=_cparams(("arbitrary",)),
        name="embed_ln",
    )(x2d, meta_blk, g, b)


GATE_COLS = 2 * N_HEADS
W_IN_GATE0 = D_POOL + 4 * D_MLSTM
N_TILES_A = W_IN_GATE0 // TN_IN
N_TILES_H = 4 * D_HGRN // TN_IN
N_TILES_POOL = D_POOL // TN_IN


def _in_proj_kernel(x_ref, wa_ref, wb_ref, o_ref):
    j = pl.program_id(1)
    shifted = jnp.logical_and(j >= N_TILES_A, j < N_TILES_A + N_TILES_H)

    @pl.when(jnp.logical_not(shifted))
    def _():
        o_ref[...] = _dot(x_ref[...], wa_ref[0].astype(BF16))

    @pl.when(shifted)
    def _():
        w = jnp.concatenate([wa_ref[0], wb_ref[0]], axis=1)[:, GATE_COLS:GATE_COLS + TN_IN]
        o_ref[...] = _dot(x_ref[...], w.astype(BF16))


def _in_proj(hb, w_in, layer):
    n_main = N_TILES_A + N_TILES_H

    def wa_map(i, j):
        return (layer, 0, jnp.where(j < n_main, j, N_TILES_A))

    def wb_map(i, j):
        shifted = jnp.logical_and(j >= N_TILES_A, j < n_main)
        return (layer, 0, jnp.where(shifted, (j + 1) * (TN_IN // LANES), 0))

    def out_map(i, j):
        return (i, jnp.where(j < N_TILES_POOL, j + n_main - N_TILES_POOL, jnp.where(j < n_main, j - N_TILES_POOL, j)))

    return pl.pallas_call(
        _in_proj_kernel,
        grid=(LP // TM_BIG, P_COLS // TN_IN),
        in_specs=[
            pl.BlockSpec((TM_BIG, D_MODEL), lambda i, j: (i, 0)),
            pl.BlockSpec((1, D_MODEL, TN_IN), wa_map),
            pl.BlockSpec((1, D_MODEL, LANES), wb_map),
        ],
        out_specs=pl.BlockSpec((TM_BIG, TN_IN), out_map),
        out_shape=jax.ShapeDtypeStruct((LP, P_COLS), F32),
        compiler_params=_cparams(("arbitrary", "arbitrary")),
        name="in_proj",
    )(hb, w_in, w_in)


POOL_HALO = 16


def _pool_kernel(u_ref, halo_ref, pw_ref, ps_ref, y_ref, ubuf):
    i = pl.program_id(0)
    u = u_ref[...]
    ubuf[0:POOL_HALO, :] = halo_ref[...]
    ubuf[POOL_HALO:POOL_HALO + TM_MID, :] = u
    row = i * TM_MID + lax.broadcasted_iota(jnp.int32, (TM_MID, POOL_GROUP), 0)
    pos = jnp.where(row >= META_ROW0 + META_PAD, row - (META_ROW0 + META_PAD) + 1, 2 * POOL_HALO)
    scale = ps_ref[...]
    for gi, w in enumerate(POOL_WINDOWS):
        cs = slice(gi * POOL_GROUP, (gi + 1) * POOL_GROUP)
        ug = u[:, cs]
        acc = ug
        for j in range(1, w):
            acc = acc + ubuf[POOL_HALO - j:POOL_HALO - j + TM_MID, cs]
        div = jnp.minimum(pos, w).astype(F32)
        d = acc / div - ug
        yg = _dot(d.astype(BF16), pw_ref[gi].astype(BF16))
        y_ref[:, cs] = (yg * scale[:, cs]).astype(BF16)


def _pool(p, pool_w, pool_scale):
    blocks_per_tile = TM_MID // POOL_HALO
    last_meta_block = (LP - POOL_HALO) // POOL_HALO

    def halo_map(i):
        return (jnp.where(i == 0, last_meta_block, i * blocks_per_tile - 1), COL_POOL // D_POOL)

    return pl.pallas_call(
        _pool_kernel,
        grid=(LP // TM_MID,),
        in_specs=[
            pl.BlockSpec((TM_MID, D_POOL), lambda i: (i, COL_POOL // D_POOL)),
            pl.BlockSpec((POOL_HALO, D_POOL), halo_map),
            pl.BlockSpec((len(POOL_WINDOWS), POOL_GROUP, POOL_GROUP), lambda i: (0, 0, 0)),
            pl.BlockSpec((1, D_POOL), lambda i: (0, 0)),
        ],
        out_specs=pl.BlockSpec((TM_MID, D_POOL), lambda i: (i, 0)),
        out_shape=jax.ShapeDtypeStruct((LP, D_POOL), BF16),
        scratch_shapes=[pltpu.VMEM((POOL_HALO + TM_MID, D_POOL), F32)],
        compiler_params=_cparams(("arbitrary",)),
        name="pool_mixer",
    )(p, p, pool_w, pool_scale)


MLSTM_CHUNKS_PER_STEP = 1
HGRN_CHUNKS_PER_STEP = 4


def _step_block(c, chunks_per_step):
    return jnp.where(c == 0, SEQ // (chunks_per_step * CHUNK), c - 1)


CONV_HALO = 8


def _mlstm_kernel(mqk_ref, mv_ref, mo_ref, gt_ref, cw_ref, cb_ref, gb_ref, ng_ref, y_ref,
                  c_sc, n_sc, m_sc, xbuf):
    c = pl.program_id(0)

    @pl.when(c == 0)
    def _():
        c_sc[...] = jnp.zeros(c_sc.shape, F32)
        n_sc[...] = jnp.zeros(n_sc.shape, F32)
        m_sc[...] = jnp.full(m_sc.shape, NEG_INF, F32)
        xbuf[0:CONV_HALO, :] = jnp.zeros((CONV_HALO, 2 * D_MLSTM), F32)
        _mlstm_chunk(0, True, mqk_ref, mv_ref, mo_ref, gt_ref, cw_ref, cb_ref, gb_ref, ng_ref, y_ref,
                     c_sc, n_sc, m_sc, xbuf)

    @pl.when(c > 0)
    def _():
        for g in range(MLSTM_CHUNKS_PER_STEP):
            _mlstm_chunk(g, False, mqk_ref, mv_ref, mo_ref, gt_ref, cw_ref, cb_ref, gb_ref, ng_ref, y_ref,
                         c_sc, n_sc, m_sc, xbuf)


def _mlstm_chunk(g, is_meta, mqk_ref, mv_ref, mo_ref, gt_ref, cw_ref, cb_ref, gb_ref, ng_ref, y_ref,
                 c_sc, n_sc, m_sc, xbuf):
    rows = pl.ds(g * CHUNK, CHUNK)
    x = mqk_ref[rows, :]
    xbuf[CONV_HALO:CONV_HALO + CHUNK, :] = x
    cw = cw_ref[...]
    conv = cb_ref[...] + cw[CONV_K - 1:CONV_K, :] * x
    for j in range(CONV_K - 1):
        off = CONV_HALO - (CONV_K - 1) + j
        conv = conv + cw[j:j + 1, :] * xbuf[off:off + CHUNK, :]
    xbuf[0:CONV_HALO, :] = x[CHUNK - CONV_HALO:CHUNK, :]
    qk = conv * _sigmoid(conv)
    v_all = mv_ref[rows, :]
    og_all = _sigmoid(mo_ref[rows, :])
    ng = ng_ref[...]

    z = gt_ref[rows, :] + gb_ref[...]
    lane = lax.broadcasted_iota(jnp.int32, (CHUNK, LANES), 1)
    if is_meta:
        valid = lax.broadcasted_iota(jnp.int32, (CHUNK, LANES), 0) >= META_PAD
        ig = jnp.where(valid, z, NEG_INF)
        lf = jnp.where(valid, _log_sigmoid(z), 0.0)
    else:
        ig = z
        lf = _log_sigmoid(z)
    g_cum = _chunk_cumsum(lf)
    gcol = jnp.where(lane < N_HEADS, ig, g_cum)
    grow = gcol.T

    r64 = lax.broadcasted_iota(jnp.int32, (CHUNK, CHUNK), 0)
    c64 = lax.broadcasted_iota(jnp.int32, (CHUNK, CHUNK), 1)
    causal = r64 >= c64
    k_scale = HEAD_DIM ** -0.5

    for h in range(N_HEADS):
        hs = slice(h * HEAD_DIM, (h + 1) * HEAD_DIM)
        q = qk[:, hs]
        k = qk[:, D_MLSTM + h * HEAD_DIM:D_MLSTM + (h + 1) * HEAD_DIM] * k_scale
        v = v_all[:, hs].astype(BF16)
        qb = q.astype(BF16)
        g_t = gcol[:, N_HEADS + h:N_HEADS + h + 1]
        ig_t = gcol[:, h:h + 1]
        g_s = grow[N_HEADS + h:N_HEADS + h + 1, :]
        ig_s = grow[h:h + 1, :]
        m_prev = m_sc[h:h + 1, 0:1]
        n_prev = n_sc[h:h + 1, :]
        c_prev = c_sc[h]

        dmat = jnp.where(causal, g_t - g_s + ig_s, NEG_INF)
        m_inter = g_t + m_prev
        m_t = jnp.maximum(jnp.max(dmat, axis=1, keepdims=True), m_inter)
        m_ts = jnp.where(m_t == NEG_INF, 0.0, m_t)
        s = _dot_nt(qb, k.astype(BF16)) * jnp.exp(dmat - m_ts)
        inter = jnp.exp(m_inter - m_ts)
        num = _dot(s.astype(BF16), v) + inter * _dot(qb, c_prev.astype(BF16))
        den = jnp.sum(s, axis=1, keepdims=True) + inter * jnp.sum(q * n_prev, axis=1, keepdims=True)
        hh = num / jnp.maximum(jnp.abs(den), jnp.exp(-m_ts))

        g_last = gcol[CHUNK - 1:CHUNK, N_HEADS + h:N_HEADS + h + 1]
        w_col = g_last - g_t + ig_t
        w_row = g_last - g_s + ig_s
        m_new = jnp.maximum(g_last + m_prev, jnp.max(w_row, axis=1, keepdims=True))
        decay = jnp.exp(g_last + m_prev - m_new)
        wk = k * jnp.exp(w_col - m_new)
        c_sc[h] = decay * c_prev + _dot_tn(wk.astype(BF16), v)
        n_sc[h:h + 1, :] = decay * n_prev + jnp.sum(wk, axis=0, keepdims=True)
        m_sc[h:h + 1, :] = jnp.broadcast_to(m_new, (1, LANES))

        mu = jnp.mean(hh, axis=1, keepdims=True)
        hc = hh - mu
        var = jnp.mean(hc * hc, axis=1, keepdims=True)
        y = hc * lax.rsqrt(var + LN_EPS) * ng[:, hs] * og_all[:, hs]
        y_ref[rows, hs] = y.astype(BF16)


def _mlstm(p, conv_w, conv_b, gate_bias, norm_g):
    n = MLSTM_CHUNKS_PER_STEP
    step_rows = n * CHUNK

    def col(block_w, start):
        return lambda c: (_step_block(c, n), start // block_w)

    const2 = lambda c: (0, 0)
    return pl.pallas_call(
        _mlstm_kernel,
        grid=(SEQ // step_rows + 1,),
        in_specs=[
            pl.BlockSpec((step_rows, 2 * D_MLSTM), col(2 * D_MLSTM, COL_MQK)),
            pl.BlockSpec((step_rows, D_MLSTM), col(D_MLSTM, COL_MV)),
            pl.BlockSpec((step_rows, D_MLSTM), col(D_MLSTM, COL_MO)),
            pl.BlockSpec((step_rows, LANES), col(LANES, COL_GATE)),
            pl.BlockSpec((CONV_K, 2 * D_MLSTM), const2),
            pl.BlockSpec((1, 2 * D_MLSTM), const2),
            pl.BlockSpec((1, LANES), const2),
            pl.BlockSpec((1, D_MLSTM), const2),
        ],
        out_specs=pl.BlockSpec((step_rows, D_MLSTM), lambda c: (_step_block(c, n), 0)),
        out_shape=jax.ShapeDtypeStruct((LP, D_MLSTM), BF16),
        scratch_shapes=[
            pltpu.VMEM((N_HEADS, HEAD_DIM, HEAD_DIM), F32),
            pltpu.VMEM((8, HEAD_DIM), F32),
            pltpu.VMEM((8, LANES), F32),
            pltpu.VMEM((CONV_HALO + CHUNK, 2 * D_MLSTM), F32),
        ],
        compiler_params=_cparams(("arbitrary",)),
        name="mlstm_mixer",
    )(p, p, p, p, conv_w, conv_b, gate_bias, norm_g)


N_LEVELS = 6


def _hgrn_tables():
    t = np.arange(CHUNK)
    sel = np.zeros((N_LEVELS * CHUNK, CHUNK), np.float32)
    mask = np.zeros((N_LEVELS + 1, CHUNK, CHUNK), np.float32)
    upper = np.zeros((CHUNK, LANES), np.float32)
    for l in range(N_LEVELS):
        half = 1 << l
        ref_row = (t // (2 * half)) * (2 * half) + half - 1
        sel[l * CHUNK + t, ref_row] = 1.0
        is_upper = (t // half) % 2 == 1
        upper[:, l] = is_upper
        same = (t[:, None] // (2 * half)) == (t[None, :] // (2 * half))
        mask[l] = same & is_upper[:, None] & ~is_upper[None, :]
    mask[N_LEVELS] = np.eye(CHUNK)
    return sel, mask, upper


def _hgrn_kernel(layer, hq_ref, hf_ref, hi_ref, hg_ref, lbl_ref, ng_ref, sel_ref, mask_ref, up_ref,
                 y_ref, st_sc):
    c = pl.program_id(0)

    lbl = lbl_ref[...]
    e = jnp.exp(lbl - jnp.max(lbl, axis=0, keepdims=True))
    sm = e / jnp.sum(e, axis=0, keepdims=True)
    lb = jnp.sum(sm[0:layer + 1, :], axis=0, keepdims=True) - sm[0:1, :]
    refs = (hq_ref, hf_ref, hi_ref, hg_ref, ng_ref, sel_ref, mask_ref, up_ref, y_ref, st_sc)

    @pl.when(c == 0)
    def _():
        st_sc[...] = jnp.zeros(st_sc.shape, F32)
        _hgrn_chunk(0, lb, *refs)

    @pl.when(c > 0)
    def _():
        for g in range(HGRN_CHUNKS_PER_STEP):
            _hgrn_chunk(g, lb, *refs)


def _hgrn_chunk(g, lb, hq_ref, hf_ref, hi_ref, hg_ref, ng_ref, sel_ref, mask_ref, up_ref, y_ref, st_sc):
    rows = pl.ds(g * CHUNK, CHUNK)
    z = hf_ref[rows, :]
    a = jnp.log(lb)
    bb = jnp.log1p(-lb) + _log_sigmoid(z)
    mx = jnp.maximum(a, bb)
    log_f = mx + jnp.log(jnp.exp(a - mx) + jnp.exp(bb - mx))
    kk = (1.0 - lb) * _sigmoid(-z)
    hq = hq_ref[rows, :]
    q = hq * _sigmoid(hq)
    v = hi_ref[rows, :].astype(BF16)
    hg = hg_ref[rows, :]
    gate = hg * _sigmoid(hg) * ng_ref[...]

    b = _chunk_cumsum(log_f)
    b_hi, b_lo = _split_bf16(b)
    sel = sel_ref[...]
    refs = _dot(sel, b_hi) + _dot(sel, b_lo)
    up = up_ref[...]

    amats = [None] * N_HEADS
    for l in range(N_LEVELS + 1):
        if l < N_LEVELS:
            ref_l = refs[l * CHUNK:(l + 1) * CHUNK, :]
            sign = 2.0 * up[:, l:l + 1] - 1.0
            zl = jnp.exp(sign * (b - ref_l))
            ql = (q * zl).astype(BF16)
            kl = (kk * zl).astype(BF16)
        else:
            ql = q.astype(BF16)
            kl = kk.astype(BF16)
        ml = mask_ref[l]
        for h in range(N_HEADS):
            hs = slice(h * HEAD_DIM, (h + 1) * HEAD_DIM)
            part = ml * _dot_nt(ql[:, hs], kl[:, hs])
            amats[h] = part if amats[h] is None else amats[h] + part

    b_last = b[CHUNK - 1:CHUNK, :]
    qe = (q * jnp.exp(b)).astype(BF16)
    kd = (kk * jnp.exp(b_last - b)).astype(BF16)
    e_last = jnp.exp(b_last)
    for h in range(N_HEADS):
        hs = slice(h * HEAD_DIM, (h + 1) * HEAD_DIM)
        st = st_sc[h]
        o = _dot(amats[h].astype(BF16), v[:, hs]) + _dot_nt(qe[:, hs], st.astype(BF16))
        st_sc[h] = e_last[:, hs] * st + _dot_tn(v[:, hs], kd[:, hs])
        o = o * lax.rsqrt(jnp.mean(o * o, axis=1, keepdims=True) + LN_EPS)
        y_ref[rows, hs] = (o * gate[:, hs]).astype(BF16)


def _hgrn(p, lb_logits, norm_g, layer):
    n = HGRN_CHUNKS_PER_STEP
    step_rows = n * CHUNK

    def col(start):
        return lambda c: (_step_block(c, n), start // D_HGRN)

    sel, mask, upper = _hgrn_tables()
    const2 = lambda c: (0, 0)
    return pl.pallas_call(
        functools.partial(_hgrn_kernel, layer),
        grid=(SEQ // step_rows + 1,),
        in_specs=[
            pl.BlockSpec((step_rows, D_HGRN), col(COL_HQ)),
            pl.BlockSpec((step_rows, D_HGRN), col(COL_HF)),
            pl.BlockSpec((step_rows, D_HGRN), col(COL_HI)),
            pl.BlockSpec((step_rows, D_HGRN), col(COL_HG)),
            pl.BlockSpec((DEPTH, D_HGRN), const2),
            pl.BlockSpec((1, D_HGRN), const2),
            pl.BlockSpec((N_LEVELS * CHUNK, CHUNK), const2),
            pl.BlockSpec((N_LEVELS + 1, CHUNK, CHUNK), lambda c: (0, 0, 0)),
            pl.BlockSpec((CHUNK, LANES), const2),
        ],
        out_specs=pl.BlockSpec((step_rows, D_HGRN), lambda c: (_step_block(c, n), 0)),
        out_shape=jax.ShapeDtypeStruct((LP, D_HGRN), BF16),
        scratch_shapes=[pltpu.VMEM((N_HEADS, HEAD_DIM, HEAD_DIM), F32)],
        compiler_params=_cparams(("arbitrary",)),
        name="hgrn_mixer",
    )(p, p, p, p, lb_logits, norm_g, jnp.asarray(sel, BF16), jnp.asarray(mask, F32), jnp.asarray(upper, F32))


def _zero_pad_rows(y, row0):
    row = row0 + lax.broadcasted_iota(jnp.int32, y.shape, 0)
    is_pad = jnp.logical_and(row >= META_ROW0, row < META_ROW0 + META_PAD)
    return jnp.where(is_pad, 0.0, y)


def _first_argmax(x, lane, valid):
    xm = jnp.where(valid, x, NEG_INF)
    mx = jnp.max(xm, axis=1, keepdims=True)
    idx = jnp.min(jnp.where(jnp.logical_and(valid, xm == mx), lane, float(LANES)), axis=1, keepdims=True)
    return mx, idx


TOKEN_ROWS = D_MODEL // LANES
GATHER_PITCH = 24


def _store_token_linear(ref, x):
    n = x.shape[0]
    for k in range(TOKEN_ROWS):
        ref[pl.ds(k, n, stride=TOKEN_ROWS), :] = x[:, k * LANES:(k + 1) * LANES]


def _load_gathered(ref, n):
    return jnp.concatenate([ref[pl.ds(k, n, stride=GATHER_PITCH), :] for k in range(TOKEN_ROWS)], axis=1)


def _out_router_kernel(yp_ref, ym_ref, yh_ref, h_ref, wo_ref, g_ref, b_ref, wrh_ref, wrl_ref, br_ref,
                       h1_ref, hlin_ref, eid_ref, wts_ref):
    i = pl.program_id(0)
    acc = _dot(yp_ref[...], wo_ref[0:D_POOL, :])
    acc = acc + _dot(ym_ref[...], wo_ref[D_POOL:D_POOL + D_MLSTM, :])
    acc = acc + _dot(yh_ref[...], wo_ref[D_POOL + D_MLSTM:D_MODEL, :])
    h1 = _layer_norm_rows(ALPHA * h_ref[...] + acc, g_ref[...], b_ref[...])
    h1 = _zero_pad_rows(h1, i * TM_MID)
    h1_ref[...] = h1
    _store_token_linear(hlin_ref, h1)

    x_hi, x_lo = _split_bf16(h1)
    wrh = wrh_ref[...]
    logits = _dot(x_hi, wrh) + _dot(x_lo, wrh) + _dot(x_hi, wrl_ref[...]) + br_ref[...]
    lane = lax.broadcasted_iota(jnp.int32, logits.shape, 1).astype(F32)

    is_grp = lane < N_GROUPS
    g_max, g_idx = _first_argmax(logits, lane, is_grp)
    g_exp = jnp.where(is_grp, jnp.exp(logits - g_max), 0.0)
    p_grp = 1.0 / jnp.sum(g_exp, axis=1, keepdims=True)

    e_lo = N_GROUPS + g_idx * EXPERTS_PER_GROUP
    in_grp = jnp.logical_and(lane >= e_lo, lane < e_lo + EXPERTS_PER_GROUP)
    e_max, e1 = _first_argmax(logits, lane, in_grp)
    e_exp = jnp.where(in_grp, jnp.exp(logits - e_max), 0.0)
    p_exp = e_exp / jnp.sum(e_exp, axis=1, keepdims=True)
    p1, _ = _first_argmax(p_exp, lane, in_grp)
    rest = jnp.logical_and(in_grp, lane != e1)
    p2, e2 = _first_argmax(p_exp, lane, rest)
    psum = p1 + p2
    w1 = p_grp * p1 / psum
    w2 = p_grp * p2 / psum
    eid = jnp.where(lane == 0.0, e1 - N_GROUPS, jnp.where(lane == 1.0, e2 - N_GROUPS, 0.0))
    eid_ref[...] = eid.astype(jnp.int32)
    wts_ref[...] = jnp.where(lane == 0.0, w1, jnp.where(lane == 1.0, w2, 0.0))


def _out_router(y_pool, y_m, y_h, h, w_out, g, b, wr_hi, wr_lo, br):
    row = lambda i: (i, 0)
    const2 = lambda i: (0, 0)
    return pl.pallas_call(
        _out_router_kernel,
        grid=(LP // TM_MID,),
        in_specs=[
            pl.BlockSpec((TM_MID, D_POOL), row),
            pl.BlockSpec((TM_MID, D_MLSTM), row),
            pl.BlockSpec((TM_MID, D_HGRN), row),
            pl.BlockSpec((TM_MID, D_MODEL), row),
            pl.BlockSpec((D_MODEL, D_MODEL), const2, pipeline_mode=pl.Buffered(1)),
            pl.BlockSpec((1, D_MODEL), const2),
            pl.BlockSpec((1, D_MODEL), const2),
            pl.BlockSpec((D_MODEL, LANES), const2),
            pl.BlockSpec((D_MODEL, LANES), const2),
            pl.BlockSpec((1, LANES), const2),
        ],
        out_specs=[
            pl.BlockSpec((TM_MID, D_MODEL), row),
            pl.BlockSpec((TM_MID * TOKEN_ROWS, LANES), row),
            pl.BlockSpec((TM_MID, LANES), row),
            pl.BlockSpec((TM_MID, LANES), row),
        ],
        out_shape=[
            jax.ShapeDtypeStruct((LP, D_MODEL), F32),
            jax.ShapeDtypeStruct((LP * TOKEN_ROWS, LANES), F32),
            jax.ShapeDtypeStruct((LP, LANES), jnp.int32),
            jax.ShapeDtypeStruct((LP, LANES), F32),
        ],
        compiler_params=_cparams(("arbitrary",)),
        name="out_proj_ln_router",
    )(y_pool, y_m, y_h, h, w_out, g, b, wr_hi, wr_lo, br)


ROUTE_ROWS = 256
SORTED_ROWS = N_EXPERT_TILES * TM_EXPERT
GATHER_UNROLL = 8


def _route_kernel(e_ref, dest_ref, tab_ref):
    e = e_ref[...]
    lane = lax.broadcasted_iota(jnp.int32, (ROUTE_ROWS, LANES), 1)
    kk = lax.broadcasted_iota(jnp.int32, (LANES, LANES), 0)
    ll = lax.broadcasted_iota(jnp.int32, (LANES, LANES), 1)
    before_lane = jnp.where(kk < ll, 1.0, 0.0).astype(BF16)
    all_lanes = jnp.ones((LANES, LANES), BF16)
    rr = lax.broadcasted_iota(jnp.int32, (ROUTE_ROWS, ROUTE_ROWS), 0)
    cc = lax.broadcasted_iota(jnp.int32, (ROUTE_ROWS, ROUTE_ROWS), 1)
    before_row = jnp.where(cc < rr, 1.0, 0.0).astype(BF16)

    lane1 = lane[0:1, :]
    rank = jnp.zeros((ROUTE_ROWS, LANES), F32)
    counts = jnp.zeros((1, LANES), F32)
    masks = []
    for x in range(N_EXPERTS):
        m = jnp.where(e == x, 1.0, 0.0)
        mb = m.astype(BF16)
        in_row = _dot(mb, before_lane)
        row_tot = _dot(mb, all_lanes)
        rows_before = _dot(before_row, row_tot.astype(BF16))
        rank = rank + m * (in_row + rows_before)
        total = rows_before[ROUTE_ROWS - 1:ROUTE_ROWS, :] + row_tot[ROUTE_ROWS - 1:ROUTE_ROWS, :]
        counts = counts + jnp.where(lane1 == x, total, 0.0)
        masks.append(m)

    padded = jnp.floor((counts + (TM_EXPERT - 1)) * (1.0 / TM_EXPERT)) * TM_EXPERT
    p_hi, p_lo = _split_bf16(padded)
    start = _dot(p_hi, before_lane) + _dot(p_lo, before_lane)
    end = start + padded

    dest = rank
    tile0 = (lane1 * TM_EXPERT).astype(F32)
    n_before = jnp.zeros((1, LANES), F32)
    for x in range(N_EXPERTS):
        dest = dest + masks[x] * start[:, x:x + 1]
        n_before = n_before + jnp.where(end[:, x:x + 1] <= tile0, 1.0, 0.0)
    dest_ref[...] = dest.astype(jnp.int32)

    tile_expert = jnp.minimum(n_before, float(N_EXPERTS - 1))
    n_tiles = end[:, N_EXPERTS - 1:N_EXPERTS] * (1.0 / TM_EXPERT)
    rows_left = jnp.zeros((1, LANES), F32)
    for x in range(N_EXPERTS):
        rows_left = rows_left + jnp.where(tile_expert == x, counts[:, x:x + 1] + start[:, x:x + 1] - tile0, 0.0)
    tile_rows = jnp.clip(rows_left, 0.0, float(TM_EXPERT))
    row = lax.broadcasted_iota(jnp.int32, (8, LANES), 0)
    tab = jnp.where(row == 0, tile_expert, jnp.where(row == 1, n_tiles, jnp.where(row == 2, tile_rows, 0.0)))
    tab_ref[...] = tab.astype(jnp.int32)


def _route(e_grid):
    return pl.pallas_call(
        _route_kernel,
        out_shape=[
            jax.ShapeDtypeStruct((ROUTE_ROWS, LANES), jnp.int32),
            jax.ShapeDtypeStruct((8, LANES), jnp.int32),
        ],
        compiler_params=pltpu.CompilerParams(vmem_limit_bytes=VMEM_LIMIT),
        name="moe_route",
    )(e_grid)


def _invert_kernel(dest_ref, tok_ref):
    def zero(p, carry):
        tok_ref[p] = 0
        return carry
    lax.fori_loop(0, SORTED_ROWS, zero, 0, unroll=GATHER_UNROLL)

    def place(a, carry):
        row = lax.shift_right_logical(a, LANES.bit_length() - 1)
        tok_ref[dest_ref[row, a & (LANES - 1)]] = lax.shift_right_logical(a, 1)
        return carry
    lax.fori_loop(0, N_ASSIGN, place, 0, unroll=GATHER_UNROLL)


def _invert(dest):
    used_rows = -(-N_ASSIGN // (8 * LANES)) * 8
    return pl.pallas_call(
        _invert_kernel,
        grid=(1,),
        in_specs=[pl.BlockSpec((used_rows, LANES), lambda i: (0, 0), memory_space=pltpu.SMEM)],
        out_specs=pl.BlockSpec((SORTED_ROWS,), lambda i: (0,), memory_space=pltpu.SMEM),
        out_shape=jax.ShapeDtypeStruct((SORTED_ROWS,), jnp.int32),
        name="moe_invert",
    )(dest)


GATHER_PRIORITY = 1


def _expert_kernel(tile_expert, tile_rows, n_tiles, tok_cur_ref, tok_next_ref, h_hbm, w1_ref, w3_ref, w2_ref,
                   y_ref, xbuf, w1b, w3b, w2b, gsem):
    i = pl.program_id(0)
    n_used = n_tiles[0]
    slot = i % 2

    def n_groups(tile):
        return lax.shift_right_logical(tile_rows[tile] + (GATHER_UNROLL - 1), GATHER_UNROLL.bit_length() - 1)

    def start_gather(tok_ref, s, groups):
        def body(g, carry):
            for u in range(GATHER_UNROLL):
                r = g * GATHER_UNROLL + u
                src = h_hbm.at[pl.ds(pl.multiple_of(tok_ref[0, 0, r] * TOKEN_ROWS, TOKEN_ROWS), TOKEN_ROWS)]
                dst = xbuf.at[s, pl.ds(pl.multiple_of(r * GATHER_PITCH, 8), TOKEN_ROWS)]
                pltpu.make_async_copy(src, dst, gsem.at[s]).start(priority=GATHER_PRIORITY)
            return carry
        lax.fori_loop(0, groups, body, 0)

    def wait_gather(s, groups):
        n = GATHER_UNROLL * TOKEN_ROWS

        def body(g, carry):
            pltpu.make_async_copy(h_hbm.at[pl.ds(0, n)], xbuf.at[s, pl.ds(0, n)], gsem.at[s]).wait()
            return carry
        lax.fori_loop(0, groups, body, 0)

    @pl.when(i == 0)
    def _():
        xbuf[...] = jnp.zeros(xbuf.shape, F32)
        start_gather(tok_cur_ref, 0, n_groups(0))

    @pl.when(i < n_used)
    def _():
        wait_gather(slot, n_groups(i))
        start_gather(tok_next_ref, 1 - slot, n_groups(jnp.minimum(i + 1, N_EXPERT_TILES - 1)))
        first_of_expert = jnp.logical_or(i == 0, tile_expert[i] != tile_expert[jnp.maximum(i - 1, 0)])

        @pl.when(first_of_expert)
        def _():
            w1b[...] = w1_ref[0, 0].astype(BF16)
            w3b[...] = w3_ref[0, 0].astype(BF16)
            w2b[...] = w2_ref[0, 0].astype(BF16)

        x = _load_gathered(xbuf.at[slot], TM_EXPERT).astype(BF16)
        a = _dot(x, w1b[...])
        g = _dot(x, w3b[...])
        he = (a * _sigmoid(a) * g).astype(BF16)
        _store_token_linear(y_ref, _dot(he, w2b[...]))

    @pl.when(i >= n_used)
    def _():
        y_ref[...] = jnp.zeros(y_ref.shape, F32)


def _experts(h1, w1, w3, w2, layer, tile_expert, tile_rows, n_tiles, row_token):
    def cur(i, te, tr, nt):
        return (i, 0, 0)

    def nxt(i, te, tr, nt):
        return (jnp.minimum(i + 1, N_EXPERT_TILES - 1), 0, 0)

    def wmap(i, te, tr, nt):
        return (layer, te[i], 0, 0)

    grid_spec = pltpu.PrefetchScalarGridSpec(
        num_scalar_prefetch=3,
        grid=(N_EXPERT_TILES,),
        in_specs=[
            pl.BlockSpec((1, 1, TM_EXPERT), cur, memory_space=pltpu.SMEM),
            pl.BlockSpec((1, 1, TM_EXPERT), nxt, memory_space=pltpu.SMEM),
            pl.BlockSpec(memory_space=pl.ANY),
            pl.BlockSpec((1, 1, D_MODEL, D_EXPERT), wmap),
            pl.BlockSpec((1, 1, D_MODEL, D_EXPERT), wmap),
            pl.BlockSpec((1, 1, D_EXPERT, D_MODEL), wmap),
        ],
        out_specs=pl.BlockSpec((TM_EXPERT * TOKEN_ROWS, LANES), lambda i, te, tr, nt: (i, 0)),
        scratch_shapes=[
            pltpu.VMEM((2, TM_EXPERT * GATHER_PITCH, LANES), F32),
            pltpu.VMEM((D_MODEL, D_EXPERT), BF16),
            pltpu.VMEM((D_MODEL, D_EXPERT), BF16),
            pltpu.VMEM((D_EXPERT, D_MODEL), BF16),
            pltpu.SemaphoreType.DMA((2,)),
        ],
    )
    return pl.pallas_call(
        _expert_kernel,
        grid_spec=grid_spec,
        out_shape=jax.ShapeDtypeStruct((SORTED_ROWS * TOKEN_ROWS, LANES), F32),
        compiler_params=_cparams(("arbitrary",)),
        name="moe_experts",
    )(tile_expert, tile_rows, n_tiles, row_token, row_token, h1, w1, w3, w2)


def _combine_kernel(tm, n_steps, final, pos_cur_ref, pos_next_ref, h1_ref, wts_ref, g_ref, b_ref, y_hbm,
                    *rest):
    out_refs, (ybuf, sem) = rest[:-2], rest[-2:]
    i = pl.program_id(0)
    slot = i % 2

    def start_gather(pos_ref, s):
        def body(r, carry):
            for j in range(2):
                row0 = pl.multiple_of(pos_ref[0, 0, 2 * r + j] * TOKEN_ROWS, TOKEN_ROWS)
                dst = ybuf.at[s, j, pl.ds(pl.multiple_of(r * GATHER_PITCH, 8), TOKEN_ROWS)]
                pltpu.make_async_copy(y_hbm.at[pl.ds(row0, TOKEN_ROWS)], dst, sem.at[s]).start(priority=j)
            return carry
        lax.fori_loop(0, tm, body, 0, unroll=GATHER_UNROLL // 2)

    @pl.when(i == 0)
    def _():
        start_gather(pos_cur_ref, 0)

    for j in range(2):
        n = tm * TOKEN_ROWS
        pltpu.make_async_copy(y_hbm.at[pl.ds(0, n)], ybuf.at[slot, j, pl.ds(0, n)], sem.at[slot]).wait()

    @pl.when(i + 1 < n_steps)
    def _():
        start_gather(pos_next_ref, 1 - slot)

    wts = wts_ref[...]
    y = wts[:, 0:1] * _load_gathered(ybuf.at[slot, 0], tm) + wts[:, 1:2] * _load_gathered(ybuf.at[slot, 1], tm)
    h2 = _layer_norm_rows(ALPHA * h1_ref[...] + y, g_ref[...], b_ref[...])
    if final:
        out_refs[0][...] = h2
    else:
        h2 = _zero_pad_rows(h2, i * tm)
        out_refs[0][...] = h2
        out_refs[1][...] = h2.astype(BF16)


def _combine(h1, y_sorted, pos, wts, g, b, final):
    tm = TM_FINAL if final else TM_LN
    rows = SEQ if final else LP
    n_steps = rows // tm
    pos3 = pos[:2 * rows].reshape(n_steps, 1, 2 * tm)
    row = lambda i: (i, 0)
    const2 = lambda i: (0, 0)
    if final:
        out_specs = [pl.BlockSpec((tm, D_MODEL), row)]
        out_shape = [jax.ShapeDtypeStruct((rows, D_MODEL), F32)]
    else:
        out_specs = [pl.BlockSpec((tm, D_MODEL), row), pl.BlockSpec((tm, D_MODEL), row)]
        out_shape = [jax.ShapeDtypeStruct((rows, D_MODEL), F32), jax.ShapeDtypeStruct((rows, D_MODEL), BF16)]
    return pl.pallas_call(
        functools.partial(_combine_kernel, tm, n_steps, final),
        grid=(n_steps,),
        in_specs=[
            pl.BlockSpec((1, 1, 2 * tm), lambda i: (i, 0, 0), memory_space=pltpu.SMEM),
            pl.BlockSpec((1, 1, 2 * tm), lambda i: (jnp.minimum(i + 1, n_steps - 1), 0, 0), memory_space=pltpu.SMEM),
            pl.BlockSpec((tm, D_MODEL), row),
            pl.BlockSpec((tm, LANES), row),
            pl.BlockSpec((1, D_MODEL), const2),
            pl.BlockSpec((1, D_MODEL), const2),
            pl.BlockSpec(memory_space=pl.ANY),
        ],
        out_specs=out_specs,
        out_shape=out_shape,
        scratch_shapes=[pltpu.VMEM((2, 2, tm * GATHER_PITCH, LANES), F32), pltpu.SemaphoreType.DMA((2,))],
        compiler_params=_cparams(("arbitrary",)),
        name="moe_combine_ln_final" if final else "moe_combine_ln",
    )(pos3, pos3, h1, wts, g, b, y_sorted)


def _pad_lanes(v):
    return jnp.pad(v, ((0, 0), (0, LANES - v.shape[1])))


def kernel(x, meta_tokens, emb_ln_g, emb_ln_b, hgrn_lb_logits, w_in, conv_w, conv_b, ig_b, fg_b,
           mlstm_norm_g, pool_w, pool_scale, hgrn_norm_g, w_out, ln1_g, ln1_b,
           w_router_group, b_router_group, w_router_expert, b_router_expert, w1, w3, w2,
           ln2_g, ln2_b):
    assert x.shape == (1, SEQ, D_MODEL) and x.dtype == F32
    row2 = lambda v: v.reshape(1, -1)
    meta_blk = jnp.pad(meta_tokens.astype(F32), ((META_PAD, 0), (0, 0)))
    h, hb = _embed(x.reshape(SEQ, D_MODEL), meta_blk, row2(emb_ln_g), row2(emb_ln_b))

    out = None
    for l in range(DEPTH):
        p = _in_proj(hb, w_in, l)
        y_pool = _pool(p, pool_w[l], row2(pool_scale[l]))
        gate_bias = _pad_lanes(jnp.concatenate([ig_b[l], fg_b[l]]).reshape(1, -1))
        y_m = _mlstm(p, conv_w[l], row2(conv_b[l]), gate_bias, row2(mlstm_norm_g[l]))
        y_h = _hgrn(p, hgrn_lb_logits, row2(hgrn_norm_g[l]), l)

        w_r = _pad_lanes(jnp.concatenate([w_router_group[l], w_router_expert[l]], axis=1))
        wr_hi, wr_lo = _split_bf16(w_r)
        b_r = _pad_lanes(jnp.concatenate([b_router_group[l], b_router_expert[l]]).reshape(1, -1))
        h1, h1_lin, eid, wts = _out_router(y_pool, y_m, y_h, h, w_out[l].astype(BF16), row2(ln1_g[l]),
                                           row2(ln1_b[l]), wr_hi, wr_lo, b_r)

        e_flat = jnp.pad(eid[:, 0:2].reshape(-1), (0, ROUTE_ROWS * LANES - N_ASSIGN), constant_values=-1)
        dest, tab = _route(e_flat.reshape(ROUTE_ROWS, LANES))
        row_token = _invert(dest).reshape(N_EXPERT_TILES, 1, TM_EXPERT)
        y_sorted = _experts(h1_lin, w1, w3, w2, l, tab[0, :N_EXPERT_TILES], tab[2, :N_EXPERT_TILES], tab[1, :1],
                            row_token)
        pos = dest.reshape(-1)
        if l + 1 < DEPTH:
            h, hb = _combine(h1, y_sorted, pos, wts, row2(ln2_g[l]), row2(ln2_b[l]), final=False)
        else:
            (out,) = _combine(h1, y_sorted, pos, wts, row2(ln2_g[l]), row2(ln2_b[l]), final=True)
    return out.reshape(1, SEQ, D_MODEL)
```

```python
import functools

import jax
import jax.numpy as jnp
import numpy as np
from jax import lax
from jax.experimental import pallas as pl
from jax.experimental.pallas import tpu as pltpu

F32 = jnp.float32
BF16 = jnp.bfloat16

D_MODEL = 2048
SEQ = 8192
DEPTH = 2
N_META = 16
CHUNK = 64
D_POOL = D_MODEL // 4
POOL_WINDOWS = (2, 4, 8, 16)
POOL_GROUP = D_POOL // len(POOL_WINDOWS)
D_MLSTM = 3 * D_MODEL // 8
N_HEADS = 6
HEAD_DIM = D_MLSTM // N_HEADS
CONV_K = 4
D_HGRN = D_MODEL - D_POOL - D_MLSTM
N_GROUPS = 4
EXPERTS_PER_GROUP = 8
N_EXPERTS = N_GROUPS * EXPERTS_PER_GROUP
D_EXPERT = D_MODEL // 4
ALPHA = (2 * DEPTH) ** 0.25
LN_EPS = 1e-5
NEG_INF = float("-inf")

LANES = 128
LP = SEQ + CHUNK
META_ROW0 = SEQ
META_PAD = CHUNK - N_META
N_CHUNKS = LP // CHUNK
LP_EMBED = 17 * 512

COL_MQK = 0
COL_MV = 2 * D_MLSTM
COL_MO = COL_MV + D_MLSTM
COL_HQ = COL_MO + D_MLSTM
COL_HF = COL_HQ + D_HGRN
COL_HI = COL_HF + D_HGRN
COL_HG = COL_HI + D_HGRN
COL_POOL = COL_HG + D_HGRN
COL_GATE = COL_POOL + D_POOL
P_COLS = COL_GATE + 2 * LANES

TM_BIG = 2752
TN_IN = 256
TM_MID = 688
TM_LN = 192
TM_FINAL = 256
TM_EXPERT = 256
N_ASSIGN = 2 * LP
N_EXPERT_TILES = (N_ASSIGN + N_EXPERTS * (TM_EXPERT - 1)) // TM_EXPERT + 1
VMEM_LIMIT = 56 * 1024 * 1024


def _cparams(sem):
    return pltpu.CompilerParams(dimension_semantics=sem, vmem_limit_bytes=VMEM_LIMIT)


def _sigmoid(x):
    return 1.0 / (1.0 + jnp.exp(-x))


def _log_sigmoid(x):
    return jnp.minimum(x, 0.0) - jnp.log1p(jnp.exp(-jnp.abs(x)))


def _layer_norm_rows(x, g, b):
    mu = jnp.mean(x, axis=-1, keepdims=True)
    xc = x - mu
    var = jnp.mean(xc * xc, axis=-1, keepdims=True)
    return xc * lax.rsqrt(var + LN_EPS) * g + b


def _dot(a, b):
    return jnp.dot(a, b, preferred_element_type=F32)


def _dot_nt(a, b):
    return lax.dot_general(a, b, (((1,), (1,)), ((), ())), preferred_element_type=F32)


def _dot_tn(a, b):
    return lax.dot_general(a, b, (((0,), (0,)), ((), ())), preferred_element_type=F32)


def _split_bf16(x):
    hi = x.astype(BF16)
    lo = (x - hi.astype(F32)).astype(BF16)
    return hi, lo


def _chunk_cumsum(x):
    r = lax.broadcasted_iota(jnp.int32, (CHUNK, CHUNK), 0)
    c = lax.broadcasted_iota(jnp.int32, (CHUNK, CHUNK), 1)
    tri = jnp.where(r >= c, 1.0, 0.0).astype(BF16)
    hi, lo = _split_bf16(x)
    return _dot(tri, hi) + _dot(tri, lo)


def _embed_kernel(x_ref, meta_ref, g_ref, b_ref, h_ref, hb_ref):
    i = pl.program_id(0)
    g = g_ref[...]
    b = b_ref[...]

    @pl.when(i < SEQ // 512)
    def _():
        y = _layer_norm_rows(x_ref[...], g, b)
        h_ref[...] = y
        hb_ref[...] = y.astype(BF16)

    @pl.when(i == SEQ // 512)
    def _():
        y = _layer_norm_rows(meta_ref[...], g, b)
        row = lax.broadcasted_iota(jnp.int32, (CHUNK, D_MODEL), 0)
        y = jnp.where(row >= META_PAD, y, 0.0)
        h_ref[0:CHUNK, :] = y
        hb_ref[0:CHUNK, :] = y.astype(BF16)
        h_ref[CHUNK:512, :] = jnp.zeros((512 - CHUNK, D_MODEL), F32)
        hb_ref[CHUNK:512, :] = jnp.zeros((512 - CHUNK, D_MODEL), BF16)


def _embed(x2d, meta_blk, g, b):
    nx = SEQ // 512
    return pl.pallas_call(
        _embed_kernel,
        grid=(nx + 1,),
        in_specs=[
            pl.BlockSpec((512, D_MODEL), lambda i: (jnp.minimum(i, nx - 1), 0)),
            pl.BlockSpec((CHUNK, D_MODEL), lambda i: (0, 0)),
            pl.BlockSpec((1, D_MODEL), lambda i: (0, 0)),
            pl.BlockSpec((1, D_MODEL), lambda i: (0, 0)),
        ],
        out_specs=[
            pl.BlockSpec((512, D_MODEL), lambda i: (i, 0)),
            pl.BlockSpec((512, D_MODEL), lambda i: (i, 0)),
        ],
        out_shape=[
            jax.ShapeDtypeStruct((LP_EMBED, D_MODEL), F32),
            jax.ShapeDtypeStruct((LP_EMBED, D_MODEL), BF16),
        ],
        compiler_params=_cparams(("arbitrary",)),
        name="embed_ln",
    )(x2d, meta_blk, g, b)


GATE_COLS = 2 * N_HEADS
W_IN_GATE0 = D_POOL + 4 * D_MLSTM
N_TILES_A = W_IN_GATE0 // TN_IN
N_TILES_H = 4 * D_HGRN // TN_IN
N_TILES_POOL = D_POOL // TN_IN


def _in_proj_kernel(x_ref, wa_ref, wb_ref, o_ref):
    j = pl.program_id(1)
    shifted = jnp.logical_and(j >= N_TILES_A, j < N_TILES_A + N_TILES_H)

    @pl.when(jnp.logical_not(shifted))
    def _():
        o_ref[...] = _dot(x_ref[...], wa_ref[0].astype(BF16))

    @pl.when(shifted)
    def _():
        w = jnp.concatenate([wa_ref[0], wb_ref[0]], axis=1)[:, GATE_COLS:GATE_COLS + TN_IN]
        o_ref[...] = _dot(x_ref[...], w.astype(BF16))


def _in_proj(hb, w_in, layer):
    n_main = N_TILES_A + N_TILES_H

    def wa_map(i, j):
        return (layer, 0, jnp.where(j < n_main, j, N_TILES_A))

    def wb_map(i, j):
        shifted = jnp.logical_and(j >= N_TILES_A, j < n_main)
        return (layer, 0, jnp.where(shifted, (j + 1) * (TN_IN // LANES), 0))

    def out_map(i, j):
        return (i, jnp.where(j < N_TILES_POOL, j + n_main - N_TILES_POOL, jnp.where(j < n_main, j - N_TILES_POOL, j)))

    return pl.pallas_call(
        _in_proj_kernel,
        grid=(LP // TM_BIG, P_COLS // TN_IN),
        in_specs=[
            pl.BlockSpec((TM_BIG, D_MODEL), lambda i, j: (i, 0)),
            pl.BlockSpec((1, D_MODEL, TN_IN), wa_map),
            pl.BlockSpec((1, D_MODEL, LANES), wb_map),
        ],
        out_specs=pl.BlockSpec((TM_BIG, TN_IN), out_map),
        out_shape=jax.ShapeDtypeStruct((LP, P_COLS), F32),
        compiler_params=_cparams(("arbitrary", "arbitrary")),
        name="in_proj",
    )(hb, w_in, w_in)


POOL_HALO = 16


def _pool_kernel(u_ref, halo_ref, pw_ref, ps_ref, y_ref, ubuf):
    i = pl.program_id(0)
    u = u_ref[...]
    ubuf[0:POOL_HALO, :] = halo_ref[...]
    ubuf[POOL_HALO:POOL_HALO + TM_MID, :] = u
    row = i * TM_MID + lax.broadcasted_iota(jnp.int32, (TM_MID, POOL_GROUP), 0)
    pos = jnp.where(row >= META_ROW0 + META_PAD, row - (META_ROW0 + META_PAD) + 1, 2 * POOL_HALO)
    scale = ps_ref[...]
    for gi, w in enumerate(POOL_WINDOWS):
        cs = slice(gi * POOL_GROUP, (gi + 1) * POOL_GROUP)
        ug = u[:, cs]
        acc = ug
        for j in range(1, w):
            acc = acc + ubuf[POOL_HALO - j:POOL_HALO - j + TM_MID, cs]
        div = jnp.minimum(pos, w).astype(F32)
        d = acc / div - ug
        yg = _dot(d.astype(BF16), pw_ref[gi].astype(BF16))
        y_ref[:, cs] = (yg * scale[:, cs]).astype(BF16)


def _pool(p, pool_w, pool_scale):
    blocks_per_tile = TM_MID // POOL_HALO
    last_meta_block = (LP - POOL_HALO) // POOL_HALO

    def halo_map(i):
        return (jnp.where(i == 0, last_meta_block, i * blocks_per_tile - 1), COL_POOL // D_POOL)

    return pl.pallas_call(
        _pool_kernel,
        grid=(LP // TM_MID,),
        in_specs=[
            pl.BlockSpec((TM_MID, D_POOL), lambda i: (i, COL_POOL // D_POOL)),
            pl.BlockSpec((POOL_HALO, D_POOL), halo_map),
            pl.BlockSpec((len(POOL_WINDOWS), POOL_GROUP, POOL_GROUP), lambda i: (0, 0, 0)),
            pl.BlockSpec((1, D_POOL), lambda i: (0, 0)),
        ],
        out_specs=pl.BlockSpec((TM_MID, D_POOL), lambda i: (i, 0)),
        out_shape=jax.ShapeDtypeStruct((LP, D_POOL), BF16),
        scratch_shapes=[pltpu.VMEM((POOL_HALO + TM_MID, D_POOL), F32)],
        compiler_params=_cparams(("arbitrary",)),
        name="pool_mixer",
    )(p, p, pool_w, pool_scale)


MLSTM_CHUNKS_PER_STEP = 4
HGRN_CHUNKS_PER_STEP = 4


def _step_block(c, chunks_per_step):
    return jnp.where(c == 0, SEQ // (chunks_per_step * CHUNK), c - 1)


CONV_HALO = 8


def _mlstm_kernel(mqk_ref, mv_ref, mo_ref, gt_ref, cw_ref, cb_ref, gb_ref, ng_ref, y_ref,
                  s_sc, m_sc, xbuf):
    c = pl.program_id(0)

    @pl.when(c == 0)
    def _():
        s_sc[...] = jnp.zeros(s_sc.shape, F32)
        m_sc[...] = jnp.full(m_sc.shape, NEG_INF, F32)
        xbuf[0:CONV_HALO, :] = jnp.zeros((CONV_HALO, 2 * D_MLSTM), F32)
        _mlstm_chunk(0, True, mqk_ref, mv_ref, mo_ref, gt_ref, cw_ref, cb_ref, gb_ref, ng_ref, y_ref,
                     s_sc, m_sc, xbuf)

    @pl.when(c > 0)
    def _():
        for g in range(MLSTM_CHUNKS_PER_STEP):
            _mlstm_chunk(g, False, mqk_ref, mv_ref, mo_ref, gt_ref, cw_ref, cb_ref, gb_ref, ng_ref, y_ref,
                         s_sc, m_sc, xbuf)


def _mlstm_chunk(g, is_meta, mqk_ref, mv_ref, mo_ref, gt_ref, cw_ref, cb_ref, gb_ref, ng_ref, y_ref,
                 s_sc, m_sc, xbuf):
    rows = pl.ds(g * CHUNK, CHUNK)
    x = mqk_ref[rows, :]
    xbuf[CONV_HALO:CONV_HALO + CHUNK, :] = x
    cw = cw_ref[...]
    conv = cb_ref[...] + cw[CONV_K - 1:CONV_K, :] * x
    for j in range(CONV_K - 1):
        off = CONV_HALO - (CONV_K - 1) + j
        conv = conv + cw[j:j + 1, :] * xbuf[off:off + CHUNK, :]
    xbuf[0:CONV_HALO, :] = x[CHUNK - CONV_HALO:CHUNK, :]
    qk = conv * _sigmoid(conv)
    v_all = mv_ref[rows, :]
    og_all = _sigmoid(mo_ref[rows, :])
    ng = ng_ref[...]

    z = gt_ref[rows, :] + gb_ref[...]
    if is_meta:
        valid = lax.broadcasted_iota(jnp.int32, (CHUNK, LANES), 0) >= META_PAD
        ig = jnp.where(valid, z, NEG_INF)
        lf = jnp.where(valid, _log_sigmoid(z), 0.0)
    else:
        ig = z
        lf = _log_sigmoid(z)
    g_t = pltpu.roll(_chunk_cumsum(lf), LANES - N_HEADS, axis=1)
    a = ig - g_t
    row = lax.broadcasted_iota(jnp.int32, (CHUNK, LANES), 0)
    a_max = a
    shift = 1
    while shift < CHUNK:
        a_max = jnp.maximum(a_max, jnp.where(row >= shift, pltpu.roll(a_max, shift, axis=0), NEG_INF))
        shift *= 2
    m_prev = m_sc[0:1, :]
    m_t = g_t + jnp.maximum(a_max, m_prev)
    m_ts = jnp.where(m_t == NEG_INF, 0.0, m_t)
    c_t = g_t - m_ts
    inter_all = jnp.exp(g_t + m_prev - m_ts)
    floor_all = jnp.exp(-m_ts)
    g_last = g_t[CHUNK - 1:CHUNK, :]
    m_new = g_last + jnp.maximum(a_max[CHUNK - 1:CHUNK, :], m_prev)
    decay_all = jnp.exp(g_last + m_prev - m_new)
    wexp_all = jnp.exp(g_last + a - m_new)
    m_sc[0:1, :] = m_new
    a_rows = a.T

    r64 = lax.broadcasted_iota(jnp.int32, (CHUNK, CHUNK), 0)
    c64 = lax.broadcasted_iota(jnp.int32, (CHUNK, CHUNK), 1)
    causal = r64 >= c64
    k_scale = HEAD_DIM ** -0.5
    ones_cols = jnp.ones((CHUNK, HEAD_DIM), BF16)
    mean_cols = jnp.full((HEAD_DIM, HEAD_DIM), 1.0 / HEAD_DIM, BF16)

    def row_mean(x):
        hi, lo = _split_bf16(x)
        return _dot(hi, mean_cols) + _dot(lo, mean_cols)

    heads = range(N_HEADS)
    hsl = [slice(h * HEAD_DIM, (h + 1) * HEAD_DIM) for h in heads]
    qb = [qk[:, hsl[h]].astype(BF16) for h in heads]
    k = [qk[:, D_MLSTM + h * HEAD_DIM:D_MLSTM + (h + 1) * HEAD_DIM] * k_scale for h in heads]
    v_aug = [jnp.concatenate([v_all[:, hsl[h]].astype(BF16), ones_cols], axis=1) for h in heads]
    state = [s_sc[h] for h in heads]

    s = []
    for h in heads:
        dexp = jnp.exp(jnp.where(causal, c_t[:, h:h + 1] + a_rows[h:h + 1, :], NEG_INF))
        s.append((_dot_nt(qb[h], k[h].astype(BF16)) * dexp).astype(BF16))
    hh = []
    for h in heads:
        nd = _dot(s[h], v_aug[h]) + inter_all[:, h:h + 1] * _dot(qb[h], state[h].astype(BF16))
        den = nd[:, HEAD_DIM:2 * HEAD_DIM]
        hh.append(nd[:, 0:HEAD_DIM] / jnp.maximum(jnp.abs(den), floor_all[:, h:h + 1]))
    for h in heads:
        wk = (k[h] * wexp_all[:, h:h + 1]).astype(BF16)
        s_sc[h] = decay_all[:, h:h + 1] * state[h] + _dot_tn(wk, v_aug[h])
    hc = [hh[h] - row_mean(hh[h]) for h in heads]
    var = [row_mean(hc[h] * hc[h]) for h in heads]
    for h in heads:
        y = hc[h] * lax.rsqrt(var[h] + LN_EPS) * ng[:, hsl[h]] * og_all[:, hsl[h]]
        y_ref[rows, hsl[h]] = y.astype(BF16)


def _mlstm(p, conv_w, conv_b, gate_bias, norm_g):
    n = MLSTM_CHUNKS_PER_STEP
    step_rows = n * CHUNK

    def col(block_w, start):
        return lambda c: (_step_block(c, n), start // block_w)

    const2 = lambda c: (0, 0)
    return pl.pallas_call(
        _mlstm_kernel,
        grid=(SEQ // step_rows + 1,),
        in_specs=[
            pl.BlockSpec((step_rows, 2 * D_MLSTM), col(2 * D_MLSTM, COL_MQK)),
            pl.BlockSpec((step_rows, D_MLSTM), col(D_MLSTM, COL_MV)),
            pl.BlockSpec((step_rows, D_MLSTM), col(D_MLSTM, COL_MO)),
            pl.BlockSpec((step_rows, LANES), col(LANES, COL_GATE)),
            pl.BlockSpec((CONV_K, 2 * D_MLSTM), const2),
            pl.BlockSpec((1, 2 * D_MLSTM), const2),
            pl.BlockSpec((1, LANES), const2),
            pl.BlockSpec((1, D_MLSTM), const2),
        ],
        out_specs=pl.BlockSpec((step_rows, D_MLSTM), lambda c: (_step_block(c, n), 0)),
        out_shape=jax.ShapeDtypeStruct((LP, D_MLSTM), BF16),
        scratch_shapes=[
            pltpu.VMEM((N_HEADS, HEAD_DIM, 2 * HEAD_DIM), F32),
            pltpu.VMEM((8, LANES), F32),
            pltpu.VMEM((CONV_HALO + CHUNK, 2 * D_MLSTM), F32),
        ],
        compiler_params=_cparams(("arbitrary",)),
        name="mlstm_mixer",
    )(p, p, p, p, conv_w, conv_b, gate_bias, norm_g)


N_LEVELS = 6


def _hgrn_tables():
    t = np.arange(CHUNK)
    sel = np.zeros((N_LEVELS * CHUNK, CHUNK), np.float32)
    mask = np.zeros((N_LEVELS + 1, CHUNK, CHUNK), np.float32)
    upper = np.zeros((CHUNK, LANES), np.float32)
    for l in range(N_LEVELS):
        half = 1 << l
        ref_row = (t // (2 * half)) * (2 * half) + half - 1
        sel[l * CHUNK + t, ref_row] = 1.0
        is_upper = (t // half) % 2 == 1
        upper[:, l] = is_upper
        same = (t[:, None] // (2 * half)) == (t[None, :] // (2 * half))
        mask[l] = same & is_upper[:, None] & ~is_upper[None, :]
    mask[N_LEVELS] = np.eye(CHUNK)
    return sel, mask, upper


def _hgrn_kernel(layer, hq_ref, hf_ref, hi_ref, hg_ref, lbl_ref, ng_ref, sel_ref, mask_ref, up_ref,
                 y_ref, st_sc):
    c = pl.program_id(0)

    lbl = lbl_ref[...]
    e = jnp.exp(lbl - jnp.max(lbl, axis=0, keepdims=True))
    sm = e / jnp.sum(e, axis=0, keepdims=True)
    lb = jnp.sum(sm[0:layer + 1, :], axis=0, keepdims=True) - sm[0:1, :]
    refs = (hq_ref, hf_ref, hi_ref, hg_ref, ng_ref, sel_ref, mask_ref, up_ref, y_ref, st_sc)

    @pl.when(c == 0)
    def _():
        st_sc[...] = jnp.zeros(st_sc.shape, F32)
        _hgrn_chunk(0, lb, *refs)

    @pl.when(c > 0)
    def _():
        for g in range(HGRN_CHUNKS_PER_STEP):
            _hgrn_chunk(g, lb, *refs)


def _hgrn_chunk(g, lb, hq_ref, hf_ref, hi_ref, hg_ref, ng_ref, sel_ref, mask_ref, up_ref, y_ref, st_sc):
    rows = pl.ds(g * CHUNK, CHUNK)
    z = hf_ref[rows, :]
    a = jnp.log(lb)
    bb = jnp.log1p(-lb) + _log_sigmoid(z)
    mx = jnp.maximum(a, bb)
    log_f = mx + jnp.log(jnp.exp(a - mx) + jnp.exp(bb - mx))
    kk = (1.0 - lb) * _sigmoid(-z)
    hq = hq_ref[rows, :]
    q = hq * _sigmoid(hq)
    v = hi_ref[rows, :].astype(BF16)
    hg = hg_ref[rows, :]
    gate = hg * _sigmoid(hg) * ng_ref[...]

    b = _chunk_cumsum(log_f)
    b_hi, b_lo = _split_bf16(b)
    sel = sel_ref[...]
    refs = _dot(sel, b_hi) + _dot(sel, b_lo)
    up = up_ref[...]

    amats = [None] * N_HEADS
    for l in range(N_LEVELS + 1):
        if l < N_LEVELS:
            ref_l = refs[l * CHUNK:(l + 1) * CHUNK, :]
            sign = 2.0 * up[:, l:l + 1] - 1.0
            zl = jnp.exp(sign * (b - ref_l))
            ql = (q * zl).astype(BF16)
            kl = (kk * zl).astype(BF16)
        else:
            ql = q.astype(BF16)
            kl = kk.astype(BF16)
        ml = mask_ref[l]
        for h in range(N_HEADS):
            hs = slice(h * HEAD_DIM, (h + 1) * HEAD_DIM)
            part = ml * _dot_nt(ql[:, hs], kl[:, hs])
            amats[h] = part if amats[h] is None else amats[h] + part

    b_last = b[CHUNK - 1:CHUNK, :]
    qe = (q * jnp.exp(b)).astype(BF16)
    kd = (kk * jnp.exp(b_last - b)).astype(BF16)
    e_last = jnp.exp(b_last)
    for h in range(N_HEADS):
        hs = slice(h * HEAD_DIM, (h + 1) * HEAD_DIM)
        st = st_sc[h]
        o = _dot(amats[h].astype(BF16), v[:, hs]) + _dot_nt(qe[:, hs], st.astype(BF16))
        st_sc[h] = e_last[:, hs] * st + _dot_tn(v[:, hs], kd[:, hs])
        o = o * lax.rsqrt(jnp.mean(o * o, axis=1, keepdims=True) + LN_EPS)
        y_ref[rows, hs] = (o * gate[:, hs]).astype(BF16)


def _hgrn(p, lb_logits, norm_g, layer):
    n = HGRN_CHUNKS_PER_STEP
    step_rows = n * CHUNK

    def col(start):
        return lambda c: (_step_block(c, n), start // D_HGRN)

    sel, mask, upper = _hgrn_tables()
    const2 = lambda c: (0, 0)
    return pl.pallas_call(
        functools.partial(_hgrn_kernel, layer),
        grid=(SEQ // step_rows + 1,),
        in_specs=[
            pl.BlockSpec((step_rows, D_HGRN), col(COL_HQ)),
            pl.BlockSpec((step_rows, D_HGRN), col(COL_HF)),
            pl.BlockSpec((step_rows, D_HGRN), col(COL_HI)),
            pl.BlockSpec((step_rows, D_HGRN), col(COL_HG)),
            pl.BlockSpec((DEPTH, D_HGRN), const2),
            pl.BlockSpec((1, D_HGRN), const2),
            pl.BlockSpec((N_LEVELS * CHUNK, CHUNK), const2),
            pl.BlockSpec((N_LEVELS + 1, CHUNK, CHUNK), lambda c: (0, 0, 0)),
            pl.BlockSpec((CHUNK, LANES), const2),
        ],
        out_specs=pl.BlockSpec((step_rows, D_HGRN), lambda c: (_step_block(c, n), 0)),
        out_shape=jax.ShapeDtypeStruct((LP, D_HGRN), BF16),
        scratch_shapes=[pltpu.VMEM((N_HEADS, HEAD_DIM, HEAD_DIM), F32)],
        compiler_params=_cparams(("arbitrary",)),
        name="hgrn_mixer",
    )(p, p, p, p, lb_logits, norm_g, jnp.asarray(sel, BF16), jnp.asarray(mask, F32), jnp.asarray(upper, F32))


def _zero_pad_rows(y, row0):
    row = row0 + lax.broadcasted_iota(jnp.int32, y.shape, 0)
    is_pad = jnp.logical_and(row >= META_ROW0, row < META_ROW0 + META_PAD)
    return jnp.where(is_pad, 0.0, y)


def _first_argmax(x, lane, valid):
    xm = jnp.where(valid, x, NEG_INF)
    mx = jnp.max(xm, axis=1, keepdims=True)
    idx = jnp.min(jnp.where(jnp.logical_and(valid, xm == mx), lane, float(LANES)), axis=1, keepdims=True)
    return mx, idx


TOKEN_ROWS = D_MODEL // LANES
GATHER_PITCH = 24


def _store_token_linear(ref, x):
    n = x.shape[0]
    for k in range(TOKEN_ROWS):
        ref[pl.ds(k, n, stride=TOKEN_ROWS), :] = x[:, k * LANES:(k + 1) * LANES]


def _load_gathered(ref, n):
    return jnp.concatenate([ref[pl.ds(k, n, stride=GATHER_PITCH), :] for k in range(TOKEN_ROWS)], axis=1)


def _out_router_kernel(yp_ref, ym_ref, yh_ref, h_ref, wo_ref, g_ref, b_ref, wrh_ref, wrl_ref, br_ref,
                       h1_ref, hlin_ref, eid_ref, wts_ref):
    i = pl.program_id(0)
    acc = _dot(yp_ref[...], wo_ref[0:D_POOL, :])
    acc = acc + _dot(ym_ref[...], wo_ref[D_POOL:D_POOL + D_MLSTM, :])
    acc = acc + _dot(yh_ref[...], wo_ref[D_POOL + D_MLSTM:D_MODEL, :])
    h1 = _layer_norm_rows(ALPHA * h_ref[...] + acc, g_ref[...], b_ref[...])
    h1 = _zero_pad_rows(h1, i * TM_MID)
    h1_ref[...] = h1
    _store_token_linear(hlin_ref, h1)

    x_hi, x_lo = _split_bf16(h1)
    wrh = wrh_ref[...]
    logits = _dot(x_hi, wrh) + _dot(x_lo, wrh) + _dot(x_hi, wrl_ref[...]) + br_ref[...]
    lane = lax.broadcasted_iota(jnp.int32, logits.shape, 1).astype(F32)

    is_grp = lane < N_GROUPS
    g_max, g_idx = _first_argmax(logits, lane, is_grp)
    g_exp = jnp.where(is_grp, jnp.exp(logits - g_max), 0.0)
    p_grp = 1.0 / jnp.sum(g_exp, axis=1, keepdims=True)

    e_lo = N_GROUPS + g_idx * EXPERTS_PER_GROUP
    in_grp = jnp.logical_and(lane >= e_lo, lane < e_lo + EXPERTS_PER_GROUP)
    e_max, e1 = _first_argmax(logits, lane, in_grp)
    e_exp = jnp.where(in_grp, jnp.exp(logits - e_max), 0.0)
    p_exp = e_exp / jnp.sum(e_exp, axis=1, keepdims=True)
    p1, _ = _first_argmax(p_exp, lane, in_grp)
    rest = jnp.logical_and(in_grp, lane != e1)
    p2, e2 = _first_argmax(p_exp, lane, rest)
    psum = p1 + p2
    w1 = p_grp * p1 / psum
    w2 = p_grp * p2 / psum
    eid = jnp.where(lane == 0.0, e1 - N_GROUPS, jnp.where(lane == 1.0, e2 - N_GROUPS, 0.0))
    eid_ref[...] = eid.astype(jnp.int32)
    wts_ref[...] = jnp.where(lane == 0.0, w1, jnp.where(lane == 1.0, w2, 0.0))


def _out_router(y_pool, y_m, y_h, h, w_out, g, b, wr_hi, wr_lo, br):
    row = lambda i: (i, 0)
    const2 = lambda i: (0, 0)
    return pl.pallas_call(
        _out_router_kernel,
        grid=(LP // TM_MID,),
        in_specs=[
            pl.BlockSpec((TM_MID, D_POOL), row),
            pl.BlockSpec((TM_MID, D_MLSTM), row),
            pl.BlockSpec((TM_MID, D_HGRN), row),
            pl.BlockSpec((TM_MID, D_MODEL), row),
            pl.BlockSpec((D_MODEL, D_MODEL), const2, pipeline_mode=pl.Buffered(1)),
            pl.BlockSpec((1, D_MODEL), const2),
            pl.BlockSpec((1, D_MODEL), const2),
            pl.BlockSpec((D_MODEL, LANES), const2),
            pl.BlockSpec((D_MODEL, LANES), const2),
            pl.BlockSpec((1, LANES), const2),
        ],
        out_specs=[
            pl.BlockSpec((TM_MID, D_MODEL), row),
            pl.BlockSpec((TM_MID * TOKEN_ROWS, LANES), row),
            pl.BlockSpec((TM_MID, LANES), row),
            pl.BlockSpec((TM_MID, LANES), row),
        ],
        out_shape=[
            jax.ShapeDtypeStruct((LP, D_MODEL), F32),
            jax.ShapeDtypeStruct((LP * TOKEN_ROWS, LANES), F32),
            jax.ShapeDtypeStruct((LP, LANES), jnp.int32),
            jax.ShapeDtypeStruct((LP, LANES), F32),
        ],
        compiler_params=_cparams(("arbitrary",)),
        name="out_proj_ln_router",
    )(y_pool, y_m, y_h, h, w_out, g, b, wr_hi, wr_lo, br)


ROUTE_ROWS = 256
SORTED_ROWS = N_EXPERT_TILES * TM_EXPERT
GATHER_UNROLL = 8


def _route_kernel(e_ref, dest_ref, tab_ref):
    e = e_ref[...]
    lane = lax.broadcasted_iota(jnp.int32, (ROUTE_ROWS, LANES), 1)
    kk = lax.broadcasted_iota(jnp.int32, (LANES, LANES), 0)
    ll = lax.broadcasted_iota(jnp.int32, (LANES, LANES), 1)
    before_lane = jnp.where(kk < ll, 1.0, 0.0).astype(BF16)
    all_lanes = jnp.ones((LANES, LANES), BF16)
    rr = lax.broadcasted_iota(jnp.int32, (ROUTE_ROWS, ROUTE_ROWS), 0)
    cc = lax.broadcasted_iota(jnp.int32, (ROUTE_ROWS, ROUTE_ROWS), 1)
    before_row = jnp.where(cc < rr, 1.0, 0.0).astype(BF16)

    lane1 = lane[0:1, :]
    rank = jnp.zeros((ROUTE_ROWS, LANES), F32)
    counts = jnp.zeros((1, LANES), F32)
    masks = []
    for x in range(N_EXPERTS):
        m = jnp.where(e == x, 1.0, 0.0)
        mb = m.astype(BF16)
        in_row = _dot(mb, before_lane)
        row_tot = _dot(mb, all_lanes)
        rows_before = _dot(before_row, row_tot.astype(BF16))
        rank = rank + m * (in_row + rows_before)
        total = rows_before[ROUTE_ROWS - 1:ROUTE_ROWS, :] + row_tot[ROUTE_ROWS - 1:ROUTE_ROWS, :]
        counts = counts + jnp.where(lane1 == x, total, 0.0)
        masks.append(m)

    padded = jnp.floor((counts + (TM_EXPERT - 1)) * (1.0 / TM_EXPERT)) * TM_EXPERT
    p_hi, p_lo = _split_bf16(padded)
    start = _dot(p_hi, before_lane) + _dot(p_lo, before_lane)
    end = start + padded

    dest = rank
    tile0 = (lane1 * TM_EXPERT).astype(F32)
    n_before = jnp.zeros((1, LANES), F32)
    for x in range(N_EXPERTS):
        dest = dest + masks[x] * start[:, x:x + 1]
        n_before = n_before + jnp.where(end[:, x:x + 1] <= tile0, 1.0, 0.0)
    dest_ref[...] = dest.astype(jnp.int32)

    tile_expert = jnp.minimum(n_before, float(N_EXPERTS - 1))
    n_tiles = end[:, N_EXPERTS - 1:N_EXPERTS] * (1.0 / TM_EXPERT)
    rows_left = jnp.zeros((1, LANES), F32)
    for x in range(N_EXPERTS):
        rows_left = rows_left + jnp.where(tile_expert == x, counts[:, x:x + 1] + start[:, x:x + 1] - tile0, 0.0)
    tile_rows = jnp.clip(rows_left, 0.0, float(TM_EXPERT))
    row = lax.broadcasted_iota(jnp.int32, (8, LANES), 0)
    tab = jnp.where(row == 0, tile_expert, jnp.where(row == 1, n_tiles, jnp.where(row == 2, tile_rows, 0.0)))
    tab_ref[...] = tab.astype(jnp.int32)


def _route(e_grid):
    return pl.pallas_call(
        _route_kernel,
        out_shape=[
            jax.ShapeDtypeStruct((ROUTE_ROWS, LANES), jnp.int32),
            jax.ShapeDtypeStruct((8, LANES), jnp.int32),
        ],
        compiler_params=pltpu.CompilerParams(vmem_limit_bytes=VMEM_LIMIT),
        name="moe_route",
    )(e_grid)


def _invert_kernel(dest_ref, tok_ref):
    def zero(p, carry):
        tok_ref[p] = 0
        return carry
    lax.fori_loop(0, SORTED_ROWS, zero, 0, unroll=GATHER_UNROLL)

    def place(a, carry):
        row = lax.shift_right_logical(a, LANES.bit_length() - 1)
        tok_ref[dest_ref[row, a & (LANES - 1)]] = lax.shift_right_logical(a, 1)
        return carry
    lax.fori_loop(0, N_ASSIGN, place, 0, unroll=GATHER_UNROLL)


def _invert(dest):
    used_rows = -(-N_ASSIGN // (8 * LANES)) * 8
    return pl.pallas_call(
        _invert_kernel,
        grid=(1,),
        in_specs=[pl.BlockSpec((used_rows, LANES), lambda i: (0, 0), memory_space=pltpu.SMEM)],
        out_specs=pl.BlockSpec((SORTED_ROWS,), lambda i: (0,), memory_space=pltpu.SMEM),
        out_shape=jax.ShapeDtypeStruct((SORTED_ROWS,), jnp.int32),
        name="moe_invert",
    )(dest)


GATHER_PRIORITY = 1


def _expert_kernel(tile_expert, tile_rows, n_tiles, tok_cur_ref, tok_next_ref, h_hbm, w1_ref, w3_ref, w2_ref,
                   y_ref, xbuf, w1b, w3b, w2b, gsem):
    i = pl.program_id(0)
    n_used = n_tiles[0]
    slot = i % 2

    def n_groups(tile):
        return lax.shift_right_logical(tile_rows[tile] + (GATHER_UNROLL - 1), GATHER_UNROLL.bit_length() - 1)

    def start_gather(tok_ref, s, groups):
        def body(g, carry):
            for u in range(GATHER_UNROLL):
                r = g * GATHER_UNROLL + u
                src = h_hbm.at[pl.ds(pl.multiple_of(tok_ref[0, 0, r] * TOKEN_ROWS, TOKEN_ROWS), TOKEN_ROWS)]
                dst = xbuf.at[s, pl.ds(pl.multiple_of(r * GATHER_PITCH, 8), TOKEN_ROWS)]
                pltpu.make_async_copy(src, dst, gsem.at[s]).start(priority=GATHER_PRIORITY)
            return carry
        lax.fori_loop(0, groups, body, 0)

    def wait_gather(s, groups):
        n = GATHER_UNROLL * TOKEN_ROWS

        def body(g, carry):
            pltpu.make_async_copy(h_hbm.at[pl.ds(0, n)], xbuf.at[s, pl.ds(0, n)], gsem.at[s]).wait()
            return carry
        lax.fori_loop(0, groups, body, 0)

    @pl.when(i == 0)
    def _():
        xbuf[...] = jnp.zeros(xbuf.shape, F32)
        start_gather(tok_cur_ref, 0, n_groups(0))

    @pl.when(i < n_used)
    def _():
        wait_gather(slot, n_groups(i))
        start_gather(tok_next_ref, 1 - slot, n_groups(jnp.minimum(i + 1, N_EXPERT_TILES - 1)))
        first_of_expert = jnp.logical_or(i == 0, tile_expert[i] != tile_expert[jnp.maximum(i - 1, 0)])

        @pl.when(first_of_expert)
        def _():
            w1b[...] = w1_ref[0, 0].astype(BF16)
            w3b[...] = w3_ref[0, 0].astype(BF16)
            w2b[...] = w2_ref[0, 0].astype(BF16)

        x = _load_gathered(xbuf.at[slot], TM_EXPERT).astype(BF16)
        a = _dot(x, w1b[...])
        g = _dot(x, w3b[...])
        he = (a * _sigmoid(a) * g).astype(BF16)
        _store_token_linear(y_ref, _dot(he, w2b[...]))

    @pl.when(i >= n_used)
    def _():
        y_ref[...] = jnp.zeros(y_ref.shape, F32)


def _experts(h1, w1, w3, w2, layer, tile_expert, tile_rows, n_tiles, row_token):
    def cur(i, te, tr, nt):
        return (i, 0, 0)

    def nxt(i, te, tr, nt):
        return (jnp.minimum(i + 1, N_EXPERT_TILES - 1), 0, 0)

    def wmap(i, te, tr, nt):
        return (layer, te[i], 0, 0)

    grid_spec = pltpu.PrefetchScalarGridSpec(
        num_scalar_prefetch=3,
        grid=(N_EXPERT_TILES,),
        in_specs=[
            pl.BlockSpec((1, 1, TM_EXPERT), cur, memory_space=pltpu.SMEM),
            pl.BlockSpec((1, 1, TM_EXPERT), nxt, memory_space=pltpu.SMEM),
            pl.BlockSpec(memory_space=pl.ANY),
            pl.BlockSpec((1, 1, D_MODEL, D_EXPERT), wmap),
            pl.BlockSpec((1, 1, D_MODEL, D_EXPERT), wmap),
            pl.BlockSpec((1, 1, D_EXPERT, D_MODEL), wmap),
        ],
        out_specs=pl.BlockSpec((TM_EXPERT * TOKEN_ROWS, LANES), lambda i, te, tr, nt: (i, 0)),
        scratch_shapes=[
            pltpu.VMEM((2, TM_EXPERT * GATHER_PITCH, LANES), F32),
            pltpu.VMEM((D_MODEL, D_EXPERT), BF16),
            pltpu.VMEM((D_MODEL, D_EXPERT), BF16),
            pltpu.VMEM((D_EXPERT, D_MODEL), BF16),
            pltpu.SemaphoreType.DMA((2,)),
        ],
    )
    return pl.pallas_call(
        _expert_kernel,
        grid_spec=grid_spec,
        out_shape=jax.ShapeDtypeStruct((SORTED_ROWS * TOKEN_ROWS, LANES), F32),
        compiler_params=_cparams(("arbitrary",)),
        name="moe_experts",
    )(tile_expert, tile_rows, n_tiles, row_token, row_token, h1, w1, w3, w2)


def _combine_kernel(tm, n_steps, final, pos_cur_ref, pos_next_ref, h1_ref, wts_ref, g_ref, b_ref, y_hbm,
                    *rest):
    out_refs, (ybuf, sem) = rest[:-2], rest[-2:]
    i = pl.program_id(0)
    slot = i % 2

    def start_gather(pos_ref, s):
        def body(r, carry):
            for j in range(2):
                row0 = pl.multiple_of(pos_ref[0, 0, 2 * r + j] * TOKEN_ROWS, TOKEN_ROWS)
                dst = ybuf.at[s, j, pl.ds(pl.multiple_of(r * GATHER_PITCH, 8), TOKEN_ROWS)]
                pltpu.make_async_copy(y_hbm.at[pl.ds(row0, TOKEN_ROWS)], dst, sem.at[s]).start(priority=j)
            return carry
        lax.fori_loop(0, tm, body, 0, unroll=GATHER_UNROLL // 2)

    @pl.when(i == 0)
    def _():
        start_gather(pos_cur_ref, 0)

    for j in range(2):
        n = tm * TOKEN_ROWS
        pltpu.make_async_copy(y_hbm.at[pl.ds(0, n)], ybuf.at[slot, j, pl.ds(0, n)], sem.at[slot]).wait()

    @pl.when(i + 1 < n_steps)
    def _():
        start_gather(pos_next_ref, 1 - slot)

    wts = wts_ref[...]
    y = wts[:, 0:1] * _load_gathered(ybuf.at[slot, 0], tm) + wts[:, 1:2] * _load_gathered(ybuf.at[slot, 1], tm)
    h2 = _layer_norm_rows(ALPHA * h1_ref[...] + y, g_ref[...], b_ref[...])
    if final:
        out_refs[0][...] = h2
    else:
        h2 = _zero_pad_rows(h2, i * tm)
        out_refs[0][...] = h2
        out_refs[1][...] = h2.astype(BF16)


def _combine(h1, y_sorted, pos, wts, g, b, final):
    tm = TM_FINAL if final else TM_LN
    rows = SEQ if final else LP
    n_steps = rows // tm
    pos3 = pos[:2 * rows].reshape(n_steps, 1, 2 * tm)
    row = lambda i: (i, 0)
    const2 = lambda i: (0, 0)
    if final:
        out_specs = [pl.BlockSpec((tm, D_MODEL), row)]
        out_shape = [jax.ShapeDtypeStruct((rows, D_MODEL), F32)]
    else:
        out_specs = [pl.BlockSpec((tm, D_MODEL), row), pl.BlockSpec((tm, D_MODEL), row)]
        out_shape = [jax.ShapeDtypeStruct((rows, D_MODEL), F32), jax.ShapeDtypeStruct((rows, D_MODEL), BF16)]
    return pl.pallas_call(
        functools.partial(_combine_kernel, tm, n_steps, final),
        grid=(n_steps,),
        in_specs=[
            pl.BlockSpec((1, 1, 2 * tm), lambda i: (i, 0, 0), memory_space=pltpu.SMEM),
            pl.BlockSpec((1, 1, 2 * tm), lambda i: (jnp.minimum(i + 1, n_steps - 1), 0, 0), memory_space=pltpu.SMEM),
            pl.BlockSpec((tm, D_MODEL), row),
            pl.BlockSpec((tm, LANES), row),
            pl.BlockSpec((1, D_MODEL), const2),
            pl.BlockSpec((1, D_MODEL), const2),
            pl.BlockSpec(memory_space=pl.ANY),
        ],
        out_specs=out_specs,
        out_shape=out_shape,
        scratch_shapes=[pltpu.VMEM((2, 2, tm * GATHER_PITCH, LANES), F32), pltpu.SemaphoreType.DMA((2,))],
        compiler_params=_cparams(("arbitrary",)),
        name="moe_combine_ln_final" if final else "moe_combine_ln",
    )(pos3, pos3, h1, wts, g, b, y_sorted)


def _pad_lanes(v):
    return jnp.pad(v, ((0, 0), (0, LANES - v.shape[1])))


def kernel(x, meta_tokens, emb_ln_g, emb_ln_b, hgrn_lb_logits, w_in, conv_w, conv_b, ig_b, fg_b,
           mlstm_norm_g, pool_w, pool_scale, hgrn_norm_g, w_out, ln1_g, ln1_b,
           w_router_group, b_router_group, w_router_expert, b_router_expert, w1, w3, w2,
           ln2_g, ln2_b):
    assert x.shape == (1, SEQ, D_MODEL) and x.dtype == F32
    row2 = lambda v: v.reshape(1, -1)
    meta_blk = jnp.pad(meta_tokens.astype(F32), ((META_PAD, 0), (0, 0)))
    h, hb = _embed(x.reshape(SEQ, D_MODEL), meta_blk, row2(emb_ln_g), row2(emb_ln_b))

    out = None
    for l in range(DEPTH):
        p = _in_proj(hb, w_in, l)
        y_pool = _pool(p, pool_w[l], row2(pool_scale[l]))
        gate_bias = _pad_lanes(jnp.concatenate([ig_b[l], fg_b[l]]).reshape(1, -1))
        y_m = _mlstm(p, conv_w[l], row2(conv_b[l]), gate_bias, row2(mlstm_norm_g[l]))
        y_h = _hgrn(p, hgrn_lb_logits, row2(hgrn_norm_g[l]), l)

        w_r = _pad_lanes(jnp.concatenate([w_router_group[l], w_router_expert[l]], axis=1))
        wr_hi, wr_lo = _split_bf16(w_r)
        b_r = _pad_lanes(jnp.concatenate([b_router_group[l], b_router_expert[l]]).reshape(1, -1))
        h1, h1_lin, eid, wts = _out_router(y_pool, y_m, y_h, h, w_out[l].astype(BF16), row2(ln1_g[l]),
                                           row2(ln1_b[l]), wr_hi, wr_lo, b_r)

        e_flat = jnp.pad(eid[:, 0:2].reshape(-1), (0, ROUTE_ROWS * LANES - N_ASSIGN), constant_values=-1)
        dest, tab = _route(e_flat.reshape(ROUTE_ROWS, LANES))
        row_token = _invert(dest).reshape(N_EXPERT_TILES, 1, TM_EXPERT)
        y_sorted = _experts(h1_lin, w1, w3, w2, l, tab[0, :N_EXPERT_TILES], tab[2, :N_EXPERT_TILES], tab[1, :1],
                            row_token)
        pos = dest.reshape(-1)
        if l + 1 < DEPTH:
            h, hb = _combine(h1, y_sorted, pos, wts, row2(ln2_g[l]), row2(ln2_b[l]), final=False)
        else:
            (out,) = _combine(h1, y_sorted, pos, wts, row2(ln2_g[l]), row2(ln2_b[l]), final=True)
    return out.reshape(1, SEQ, D_MODEL)
```

```python
import functools

import jax
import jax.numpy as jnp
import numpy as np
from jax import lax
from jax.experimental import pallas as pl
from jax.experimental.pallas import tpu as pltpu

F32 = jnp.float32
BF16 = jnp.bfloat16

D_MODEL = 2048
SEQ = 8192
DEPTH = 2
N_META = 16
CHUNK = 64
D_POOL = D_MODEL // 4
POOL_WINDOWS = (2, 4, 8, 16)
POOL_GROUP = D_POOL // len(POOL_WINDOWS)
D_MLSTM = 3 * D_MODEL // 8
N_HEADS = 6
HEAD_DIM = D_MLSTM // N_HEADS
CONV_K = 4
D_HGRN = D_MODEL - D_POOL - D_MLSTM
N_GROUPS = 4
EXPERTS_PER_GROUP = 8
N_EXPERTS = N_GROUPS * EXPERTS_PER_GROUP
D_EXPERT = D_MODEL // 4
ALPHA = (2 * DEPTH) ** 0.25
LN_EPS = 1e-5
NEG_INF = float("-inf")

LANES = 128
LP = SEQ + CHUNK
META_ROW0 = SEQ
META_PAD = CHUNK - N_META
N_CHUNKS = LP // CHUNK
LP_EMBED = 17 * 512

COL_MQK = 0
COL_MV = 2 * D_MLSTM
COL_MO = COL_MV + D_MLSTM
COL_HQ = COL_MO + D_MLSTM
COL_HF = COL_HQ + D_HGRN
COL_HI = COL_HF + D_HGRN
COL_HG = COL_HI + D_HGRN
COL_POOL = COL_HG + D_HGRN
COL_GATE = COL_POOL + D_POOL
P_COLS = COL_GATE + 2 * LANES

TM_BIG = 2752
TN_IN = 256
TM_MID = 688
TM_LN = 192
TM_FINAL = 256
TM_EXPERT = 256
N_ASSIGN = 2 * LP
N_EXPERT_TILES = (N_ASSIGN + N_EXPERTS * (TM_EXPERT - 1)) // TM_EXPERT + 1
VMEM_LIMIT = 56 * 1024 * 1024


def _cparams(sem):
    return pltpu.CompilerParams(dimension_semantics=sem, vmem_limit_bytes=VMEM_LIMIT)


def _sigmoid(x):
    return 1.0 / (1.0 + jnp.exp(-x))


def _log_sigmoid(x):
    return jnp.minimum(x, 0.0) - jnp.log1p(jnp.exp(-jnp.abs(x)))


def _layer_norm_rows(x, g, b):
    mu = jnp.mean(x, axis=-1, keepdims=True)
    xc = x - mu
    var = jnp.mean(xc * xc, axis=-1, keepdims=True)
    return xc * lax.rsqrt(var + LN_EPS) * g + b


def _dot(a, b):
    return jnp.dot(a, b, preferred_element_type=F32)


def _dot_nt(a, b):
    return lax.dot_general(a, b, (((1,), (1,)), ((), ())), preferred_element_type=F32)


def _dot_tn(a, b):
    return lax.dot_general(a, b, (((0,), (0,)), ((), ())), preferred_element_type=F32)


def _split_bf16(x):
    hi = x.astype(BF16)
    lo = (x - hi.astype(F32)).astype(BF16)
    return hi, lo


def _chunk_cumsum(x):
    r = lax.broadcasted_iota(jnp.int32, (CHUNK, CHUNK), 0)
    c = lax.broadcasted_iota(jnp.int32, (CHUNK, CHUNK), 1)
    tri = jnp.where(r >= c, 1.0, 0.0).astype(BF16)
    hi, lo = _split_bf16(x)
    return _dot(tri, hi) + _dot(tri, lo)


def _embed_kernel(x_ref, meta_ref, g_ref, b_ref, h_ref, hb_ref):
    i = pl.program_id(0)
    g = g_ref[...]
    b = b_ref[...]

    @pl.when(i < SEQ // 512)
    def _():
        y = _layer_norm_rows(x_ref[...], g, b)
        h_ref[...] = y
        hb_ref[...] = y.astype(BF16)

    @pl.when(i == SEQ // 512)
    def _():
        y = _layer_norm_rows(meta_ref[...], g, b)
        row = lax.broadcasted_iota(jnp.int32, (CHUNK, D_MODEL), 0)
        y = jnp.where(row >= META_PAD, y, 0.0)
        h_ref[0:CHUNK, :] = y
        hb_ref[0:CHUNK, :] = y.astype(BF16)
        h_ref[CHUNK:512, :] = jnp.zeros((512 - CHUNK, D_MODEL), F32)
        hb_ref[CHUNK:512, :] = jnp.zeros((512 - CHUNK, D_MODEL), BF16)


def _embed(x2d, meta_blk, g, b):
    nx = SEQ // 512
    return pl.pallas_call(
        _embed_kernel,
        grid=(nx + 1,),
        in_specs=[
            pl.BlockSpec((512, D_MODEL), lambda i: (jnp.minimum(i, nx - 1), 0)),
            pl.BlockSpec((CHUNK, D_MODEL), lambda i: (0, 0)),
            pl.BlockSpec((1, D_MODEL), lambda i: (0, 0)),
            pl.BlockSpec((1, D_MODEL), lambda i: (0, 0)),
        ],
        out_specs=[
            pl.BlockSpec((512, D_MODEL), lambda i: (i, 0)),
            pl.BlockSpec((512, D_MODEL), lambda i: (i, 0)),
        ],
        out_shape=[
            jax.ShapeDtypeStruct((LP_EMBED, D_MODEL), F32),
            jax.ShapeDtypeStruct((LP_EMBED, D_MODEL), BF16),
        ],
        compiler_params=_cparams(("arbitrary",)),
        name="embed_ln",
    )(x2d, meta_blk, g, b)


GATE_COLS = 2 * N_HEADS
W_IN_GATE0 = D_POOL + 4 * D_MLSTM
N_TILES_A = W_IN_GATE0 // TN_IN
N_TILES_H = 4 * D_HGRN // TN_IN
N_TILES_POOL = D_POOL // TN_IN


def _in_proj_kernel(x_ref, wa_ref, wb_ref, o_ref):
    j = pl.program_id(1)
    shifted = jnp.logical_and(j >= N_TILES_A, j < N_TILES_A + N_TILES_H)

    @pl.when(jnp.logical_not(shifted))
    def _():
        o_ref[...] = _dot(x_ref[...], wa_ref[0].astype(BF16))

    @pl.when(shifted)
    def _():
        w = jnp.concatenate([wa_ref[0], wb_ref[0]], axis=1)[:, GATE_COLS:GATE_COLS + TN_IN]
        o_ref[...] = _dot(x_ref[...], w.astype(BF16))


def _in_proj(hb, w_in, layer):
    n_main = N_TILES_A + N_TILES_H

    def wa_map(i, j):
        return (layer, 0, jnp.where(j < n_main, j, N_TILES_A))

    def wb_map(i, j):
        shifted = jnp.logical_and(j >= N_TILES_A, j < n_main)
        return (layer, 0, jnp.where(shifted, (j + 1) * (TN_IN // LANES), 0))

    def out_map(i, j):
        return (i, jnp.where(j < N_TILES_POOL, j + n_main - N_TILES_POOL, jnp.where(j < n_main, j - N_TILES_POOL, j)))

    return pl.pallas_call(
        _in_proj_kernel,
        grid=(LP // TM_BIG, P_COLS // TN_IN),
        in_specs=[
            pl.BlockSpec((TM_BIG, D_MODEL), lambda i, j: (i, 0)),
            pl.BlockSpec((1, D_MODEL, TN_IN), wa_map),
            pl.BlockSpec((1, D_MODEL, LANES), wb_map),
        ],
        out_specs=pl.BlockSpec((TM_BIG, TN_IN), out_map),
        out_shape=jax.ShapeDtypeStruct((LP, P_COLS), F32),
        compiler_params=_cparams(("arbitrary", "arbitrary")),
        name="in_proj",
    )(hb, w_in, w_in)


POOL_HALO = 16


def _pool_kernel(u_ref, halo_ref, pw_ref, ps_ref, y_ref, ubuf):
    i = pl.program_id(0)
    u = u_ref[...]
    ubuf[0:POOL_HALO, :] = halo_ref[...]
    ubuf[POOL_HALO:POOL_HALO + TM_MID, :] = u
    row = i * TM_MID + lax.broadcasted_iota(jnp.int32, (TM_MID, POOL_GROUP), 0)
    pos = jnp.where(row >= META_ROW0 + META_PAD, row - (META_ROW0 + META_PAD) + 1, 2 * POOL_HALO)
    scale = ps_ref[...]
    for gi, w in enumerate(POOL_WINDOWS):
        cs = slice(gi * POOL_GROUP, (gi + 1) * POOL_GROUP)
        ug = u[:, cs]
        acc = ug
        for j in range(1, w):
            acc = acc + ubuf[POOL_HALO - j:POOL_HALO - j + TM_MID, cs]
        div = jnp.minimum(pos, w).astype(F32)
        d = acc / div - ug
        yg = _dot(d.astype(BF16), pw_ref[gi].astype(BF16))
        y_ref[:, cs] = (yg * scale[:, cs]).astype(BF16)


def _pool(p, pool_w, pool_scale):
    blocks_per_tile = TM_MID // POOL_HALO
    last_meta_block = (LP - POOL_HALO) // POOL_HALO

    def halo_map(i):
        return (jnp.where(i == 0, last_meta_block, i * blocks_per_tile - 1), COL_POOL // D_POOL)

    return pl.pallas_call(
        _pool_kernel,
        grid=(LP // TM_MID,),
        in_specs=[
            pl.BlockSpec((TM_MID, D_POOL), lambda i: (i, COL_POOL // D_POOL)),
            pl.BlockSpec((POOL_HALO, D_POOL), halo_map),
            pl.BlockSpec((len(POOL_WINDOWS), POOL_GROUP, POOL_GROUP), lambda i: (0, 0, 0)),
            pl.BlockSpec((1, D_POOL), lambda i: (0, 0)),
        ],
        out_specs=pl.BlockSpec((TM_MID, D_POOL), lambda i: (i, 0)),
        out_shape=jax.ShapeDtypeStruct((LP, D_POOL), BF16),
        scratch_shapes=[pltpu.VMEM((POOL_HALO + TM_MID, D_POOL), F32)],
        compiler_params=_cparams(("arbitrary",)),
        name="pool_mixer",
    )(p, p, pool_w, pool_scale)


MLSTM_CHUNKS_PER_STEP = 4
HGRN_CHUNKS_PER_STEP = 4


def _step_block(c, chunks_per_step):
    return jnp.where(c == 0, SEQ // (chunks_per_step * CHUNK), c - 1)


CONV_HALO = 8


def _mlstm_kernel(mqk_ref, mv_ref, mo_ref, gt_ref, cw_ref, cb_ref, gb_ref, ng_ref, y_ref,
                  s_sc, m_sc, xbuf):
    c = pl.program_id(0)

    @pl.when(c == 0)
    def _():
        s_sc[...] = jnp.zeros(s_sc.shape, F32)
        m_sc[...] = jnp.full(m_sc.shape, NEG_INF, F32)
        xbuf[0:CONV_HALO, :] = jnp.zeros((CONV_HALO, 2 * D_MLSTM), F32)
        _mlstm_chunk(0, True, mqk_ref, mv_ref, mo_ref, gt_ref, cw_ref, cb_ref, gb_ref, ng_ref, y_ref,
                     s_sc, m_sc, xbuf)

    @pl.when(c > 0)
    def _():
        for g in range(MLSTM_CHUNKS_PER_STEP):
            _mlstm_chunk(g, False, mqk_ref, mv_ref, mo_ref, gt_ref, cw_ref, cb_ref, gb_ref, ng_ref, y_ref,
                         s_sc, m_sc, xbuf)


def _mlstm_chunk(g, is_meta, mqk_ref, mv_ref, mo_ref, gt_ref, cw_ref, cb_ref, gb_ref, ng_ref, y_ref,
                 s_sc, m_sc, xbuf):
    rows = pl.ds(g * CHUNK, CHUNK)
    x = mqk_ref[rows, :]
    xbuf[CONV_HALO:CONV_HALO + CHUNK, :] = x
    cw = cw_ref[...]
    conv = cb_ref[...] + cw[CONV_K - 1:CONV_K, :] * x
    for j in range(CONV_K - 1):
        off = CONV_HALO - (CONV_K - 1) + j
        conv = conv + cw[j:j + 1, :] * xbuf[off:off + CHUNK, :]
    xbuf[0:CONV_HALO, :] = x[CHUNK - CONV_HALO:CHUNK, :]
    qk = conv * _sigmoid(conv)
    v_all = mv_ref[rows, :]
    og_all = _sigmoid(mo_ref[rows, :])
    ng = ng_ref[...]

    z = gt_ref[rows, :] + gb_ref[...]
    if is_meta:
        valid = lax.broadcasted_iota(jnp.int32, (CHUNK, LANES), 0) >= META_PAD
        ig = jnp.where(valid, z, NEG_INF)
        lf = jnp.where(valid, _log_sigmoid(z), 0.0)
    else:
        ig = z
        lf = _log_sigmoid(z)
    g_t = pltpu.roll(_chunk_cumsum(lf), LANES - N_HEADS, axis=1)
    a = ig - g_t
    row = lax.broadcasted_iota(jnp.int32, (CHUNK, LANES), 0)
    a_max = a
    shift = 1
    while shift < CHUNK:
        a_max = jnp.maximum(a_max, jnp.where(row >= shift, pltpu.roll(a_max, shift, axis=0), NEG_INF))
        shift *= 2
    m_prev = m_sc[0:1, :]
    m_t = g_t + jnp.maximum(a_max, m_prev)
    m_ts = jnp.where(m_t == NEG_INF, 0.0, m_t)
    c_t = g_t - m_ts
    inter_all = jnp.exp(g_t + m_prev - m_ts)
    floor_all = jnp.exp(-m_ts)
    g_last = g_t[CHUNK - 1:CHUNK, :]
    m_new = g_last + jnp.maximum(a_max[CHUNK - 1:CHUNK, :], m_prev)
    decay_all = jnp.exp(g_last + m_prev - m_new)
    wexp_all = jnp.exp(g_last + a - m_new)
    m_sc[0:1, :] = m_new
    a_rows = a.T

    r64 = lax.broadcasted_iota(jnp.int32, (CHUNK, CHUNK), 0)
    c64 = lax.broadcasted_iota(jnp.int32, (CHUNK, CHUNK), 1)
    causal = r64 >= c64
    k_scale = HEAD_DIM ** -0.5
    ones_cols = jnp.ones((CHUNK, HEAD_DIM), BF16)
    mean_cols = jnp.full((HEAD_DIM, HEAD_DIM), 1.0 / HEAD_DIM, BF16)

    def row_mean(x):
        hi, lo = _split_bf16(x)
        return _dot(hi, mean_cols) + _dot(lo, mean_cols)

    heads = range(N_HEADS)
    hsl = [slice(h * HEAD_DIM, (h + 1) * HEAD_DIM) for h in heads]
    qb = [qk[:, hsl[h]].astype(BF16) for h in heads]
    k = [qk[:, D_MLSTM + h * HEAD_DIM:D_MLSTM + (h + 1) * HEAD_DIM] * k_scale for h in heads]
    v_aug = [jnp.concatenate([v_all[:, hsl[h]].astype(BF16), ones_cols], axis=1) for h in heads]
    state = [s_sc[h] for h in heads]

    s = []
    for h in heads:
        dexp = jnp.exp(jnp.where(causal, c_t[:, h:h + 1] + a_rows[h:h + 1, :], NEG_INF))
        s.append((_dot_nt(qb[h], k[h].astype(BF16)) * dexp).astype(BF16))
    hh = []
    for h in heads:
        nd = _dot(s[h], v_aug[h]) + inter_all[:, h:h + 1] * _dot(qb[h], state[h].astype(BF16))
        den = nd[:, HEAD_DIM:2 * HEAD_DIM]
        hh.append(nd[:, 0:HEAD_DIM] / jnp.maximum(jnp.abs(den), floor_all[:, h:h + 1]))
    for h in heads:
        wk = (k[h] * wexp_all[:, h:h + 1]).astype(BF16)
        s_sc[h] = decay_all[:, h:h + 1] * state[h] + _dot_tn(wk, v_aug[h])
    hc = [hh[h] - row_mean(hh[h]) for h in heads]
    var = [row_mean(hc[h] * hc[h]) for h in heads]
    for h in heads:
        y = hc[h] * lax.rsqrt(var[h] + LN_EPS) * ng[:, hsl[h]] * og_all[:, hsl[h]]
        y_ref[rows, hsl[h]] = y.astype(BF16)


def _mlstm(p, conv_w, conv_b, gate_bias, norm_g):
    n = MLSTM_CHUNKS_PER_STEP
    step_rows = n * CHUNK

    def col(block_w, start):
        return lambda c: (_step_block(c, n), start // block_w)

    const2 = lambda c: (0, 0)
    return pl.pallas_call(
        _mlstm_kernel,
        grid=(SEQ // step_rows + 1,),
        in_specs=[
            pl.BlockSpec((step_rows, 2 * D_MLSTM), col(2 * D_MLSTM, COL_MQK)),
            pl.BlockSpec((step_rows, D_MLSTM), col(D_MLSTM, COL_MV)),
            pl.BlockSpec((step_rows, D_MLSTM), col(D_MLSTM, COL_MO)),
            pl.BlockSpec((step_rows, LANES), col(LANES, COL_GATE)),
            pl.BlockSpec((CONV_K, 2 * D_MLSTM), const2),
            pl.BlockSpec((1, 2 * D_MLSTM), const2),
            pl.BlockSpec((1, LANES), const2),
            pl.BlockSpec((1, D_MLSTM), const2),
        ],
        out_specs=pl.BlockSpec((step_rows, D_MLSTM), lambda c: (_step_block(c, n), 0)),
        out_shape=jax.ShapeDtypeStruct((LP, D_MLSTM), BF16),
        scratch_shapes=[
            pltpu.VMEM((N_HEADS, HEAD_DIM, 2 * HEAD_DIM), F32),
            pltpu.VMEM((8, LANES), F32),
            pltpu.VMEM((CONV_HALO + CHUNK, 2 * D_MLSTM), F32),
        ],
        compiler_params=_cparams(("arbitrary",)),
        name="mlstm_mixer",
    )(p, p, p, p, conv_w, conv_b, gate_bias, norm_g)


N_LEVELS = 6


def _hgrn_tables():
    t = np.arange(CHUNK)
    sel = np.zeros((N_LEVELS * CHUNK, CHUNK), np.float32)
    mask = np.zeros((N_LEVELS + 1, CHUNK, CHUNK), np.float32)
    upper = np.zeros((CHUNK, LANES), np.float32)
    for l in range(N_LEVELS):
        half = 1 << l
        ref_row = (t // (2 * half)) * (2 * half) + half - 1
        sel[l * CHUNK + t, ref_row] = 1.0
        is_upper = (t // half) % 2 == 1
        upper[:, l] = is_upper
        same = (t[:, None] // (2 * half)) == (t[None, :] // (2 * half))
        mask[l] = same & is_upper[:, None] & ~is_upper[None, :]
    mask[N_LEVELS] = np.eye(CHUNK)
    return sel, mask, upper


def _hgrn_kernel(layer, hq_ref, hf_ref, hi_ref, hg_ref, lbl_ref, ng_ref, sel_ref, mask_ref, up_ref,
                 y_ref, st_sc):
    c = pl.program_id(0)

    lbl = lbl_ref[...]
    e = jnp.exp(lbl - jnp.max(lbl, axis=0, keepdims=True))
    sm = e / jnp.sum(e, axis=0, keepdims=True)
    lb = jnp.sum(sm[0:layer + 1, :], axis=0, keepdims=True) - sm[0:1, :]
    refs = (hq_ref, hf_ref, hi_ref, hg_ref, ng_ref, sel_ref, mask_ref, up_ref, y_ref, st_sc)

    @pl.when(c == 0)
    def _():
        st_sc[...] = jnp.zeros(st_sc.shape, F32)
        _hgrn_chunk(0, lb, *refs)

    @pl.when(c > 0)
    def _():
        for g in range(HGRN_CHUNKS_PER_STEP):
            _hgrn_chunk(g, lb, *refs)


def _hgrn_chunk(g, lb, hq_ref, hf_ref, hi_ref, hg_ref, ng_ref, sel_ref, mask_ref, up_ref, y_ref, st_sc):
    rows = pl.ds(g * CHUNK, CHUNK)
    z = hf_ref[rows, :]
    a = jnp.log(lb)
    bb = jnp.log1p(-lb) + _log_sigmoid(z)
    mx = jnp.maximum(a, bb)
    log_f = mx + jnp.log(jnp.exp(a - mx) + jnp.exp(bb - mx))
    kk = (1.0 - lb) * _sigmoid(-z)
    hq = hq_ref[rows, :]
    q = hq * _sigmoid(hq)
    v = hi_ref[rows, :].astype(BF16)
    hg = hg_ref[rows, :]
    gate = hg * _sigmoid(hg) * ng_ref[...]

    b = _chunk_cumsum(log_f)
    b_hi, b_lo = _split_bf16(b)
    sel = sel_ref[...]
    refs = _dot(sel, b_hi) + _dot(sel, b_lo)
    up = up_ref[...]

    amats = [None] * N_HEADS
    for l in range(N_LEVELS + 1):
        if l < N_LEVELS:
            ref_l = refs[l * CHUNK:(l + 1) * CHUNK, :]
            sign = 2.0 * up[:, l:l + 1] - 1.0
            zl = jnp.exp(sign * (b - ref_l))
            ql = (q * zl).astype(BF16)
            kl = (kk * zl).astype(BF16)
        else:
            ql = q.astype(BF16)
            kl = kk.astype(BF16)
        ml = mask_ref[l]
        for h in range(N_HEADS):
            hs = slice(h * HEAD_DIM, (h + 1) * HEAD_DIM)
            part = ml * _dot_nt(ql[:, hs], kl[:, hs])
            amats[h] = part if amats[h] is None else amats[h] + part

    b_last = b[CHUNK - 1:CHUNK, :]
    qe = (q * jnp.exp(b)).astype(BF16)
    kd = (kk * jnp.exp(b_last - b)).astype(BF16)
    e_last = jnp.exp(b_last)
    for h in range(N_HEADS):
        hs = slice(h * HEAD_DIM, (h + 1) * HEAD_DIM)
        st = st_sc[h]
        o = _dot(amats[h].astype(BF16), v[:, hs]) + _dot_nt(qe[:, hs], st.astype(BF16))
        st_sc[h] = e_last[:, hs] * st + _dot_tn(v[:, hs], kd[:, hs])
        o = o * lax.rsqrt(jnp.mean(o * o, axis=1, keepdims=True) + LN_EPS)
        y_ref[rows, hs] = (o * gate[:, hs]).astype(BF16)


def _hgrn(p, lb_logits, norm_g, layer):
    n = HGRN_CHUNKS_PER_STEP
    step_rows = n * CHUNK

    def col(start):
        return lambda c: (_step_block(c, n), start // D_HGRN)

    sel, mask, upper = _hgrn_tables()
    const2 = lambda c: (0, 0)
    return pl.pallas_call(
        functools.partial(_hgrn_kernel, layer),
        grid=(SEQ // step_rows + 1,),
        in_specs=[
            pl.BlockSpec((step_rows, D_HGRN), col(COL_HQ)),
            pl.BlockSpec((step_rows, D_HGRN), col(COL_HF)),
            pl.BlockSpec((step_rows, D_HGRN), col(COL_HI)),
            pl.BlockSpec((step_rows, D_HGRN), col(COL_HG)),
            pl.BlockSpec((DEPTH, D_HGRN), const2),
            pl.BlockSpec((1, D_HGRN), const2),
            pl.BlockSpec((N_LEVELS * CHUNK, CHUNK), const2),
            pl.BlockSpec((N_LEVELS + 1, CHUNK, CHUNK), lambda c: (0, 0, 0)),
            pl.BlockSpec((CHUNK, LANES), const2),
        ],
        out_specs=pl.BlockSpec((step_rows, D_HGRN), lambda c: (_step_block(c, n), 0)),
        out_shape=jax.ShapeDtypeStruct((LP, D_HGRN), BF16),
        scratch_shapes=[pltpu.VMEM((N_HEADS, HEAD_DIM, HEAD_DIM), F32)],
        compiler_params=_cparams(("arbitrary",)),
        name="hgrn_mixer",
    )(p, p, p, p, lb_logits, norm_g, jnp.asarray(sel, BF16), jnp.asarray(mask, F32), jnp.asarray(upper, F32))


def _zero_pad_rows(y, row0):
    row = row0 + lax.broadcasted_iota(jnp.int32, y.shape, 0)
    is_pad = jnp.logical_and(row >= META_ROW0, row < META_ROW0 + META_PAD)
    return jnp.where(is_pad, 0.0, y)


def _first_argmax(x, lane, valid):
    xm = jnp.where(valid, x, NEG_INF)
    mx = jnp.max(xm, axis=1, keepdims=True)
    idx = jnp.min(jnp.where(jnp.logical_and(valid, xm == mx), lane, float(LANES)), axis=1, keepdims=True)
    return mx, idx


TOKEN_ROWS = D_MODEL // LANES
GATHER_PITCH = 24


def _store_token_linear(ref, x, first_token=0):
    n = x.shape[0]
    for k in range(TOKEN_ROWS):
        ref[pl.ds(first_token * TOKEN_ROWS + k, n, stride=TOKEN_ROWS), :] = x[:, k * LANES:(k + 1) * LANES]


def _load_gathered(ref, n):
    return jnp.concatenate([ref[pl.ds(k, n, stride=GATHER_PITCH), :] for k in range(TOKEN_ROWS)], axis=1)


OUT_SUB_BLOCKS = ((0, TM_MID),)


def _out_router_kernel(yp_ref, ym_ref, yh_ref, h_ref, wo_ref, g_ref, b_ref, wrh_ref, wrl_ref, br_ref,
                       h1_ref, hlin_ref, eid_ref, wts_ref):
    for r0, r1 in OUT_SUB_BLOCKS:
        _out_router_rows(r0, r1, yp_ref, ym_ref, yh_ref, h_ref, wo_ref, g_ref, b_ref, wrh_ref, wrl_ref, br_ref,
                         h1_ref, hlin_ref, eid_ref, wts_ref)


def _out_router_rows(r0, r1, yp_ref, ym_ref, yh_ref, h_ref, wo_ref, g_ref, b_ref, wrh_ref, wrl_ref, br_ref,
                     h1_ref, hlin_ref, eid_ref, wts_ref):
    i = pl.program_id(0)
    y = jnp.concatenate([yp_ref[r0:r1, :], ym_ref[r0:r1, :], yh_ref[r0:r1, :]], axis=1)
    h1 = _layer_norm_rows(ALPHA * h_ref[r0:r1, :] + _dot(y, wo_ref[...]), g_ref[...], b_ref[...])
    h1 = _zero_pad_rows(h1, i * TM_MID + r0)
    h1_ref[r0:r1, :] = h1
    _store_token_linear(hlin_ref, h1, r0)

    x_hi, x_lo = _split_bf16(h1)
    hi_both = _dot(x_hi, jnp.concatenate([wrh_ref[...], wrl_ref[...]], axis=1))
    logits = hi_both[:, 0:LANES] + hi_both[:, LANES:2 * LANES] + _dot(x_lo, wrh_ref[...]) + br_ref[...]
    lane = lax.broadcasted_iota(jnp.int32, logits.shape, 1).astype(F32)

    is_grp = lane < N_GROUPS
    g_max, g_idx = _first_argmax(logits, lane, is_grp)
    g_exp = jnp.where(is_grp, jnp.exp(logits - g_max), 0.0)
    p_grp = 1.0 / jnp.sum(g_exp, axis=1, keepdims=True)

    e_lo = N_GROUPS + g_idx * EXPERTS_PER_GROUP
    in_grp = jnp.logical_and(lane >= e_lo, lane < e_lo + EXPERTS_PER_GROUP)
    e_max, e1 = _first_argmax(logits, lane, in_grp)
    e_exp = jnp.where(in_grp, jnp.exp(logits - e_max), 0.0)
    p_exp = e_exp / jnp.sum(e_exp, axis=1, keepdims=True)
    p1, _ = _first_argmax(p_exp, lane, in_grp)
    rest = jnp.logical_and(in_grp, lane != e1)
    p2, e2 = _first_argmax(p_exp, lane, rest)
    psum = p1 + p2
    w1 = p_grp * p1 / psum
    w2 = p_grp * p2 / psum
    eid = jnp.where(lane == 0.0, e1 - N_GROUPS, jnp.where(lane == 1.0, e2 - N_GROUPS, 0.0))
    eid_ref[r0:r1, :] = eid.astype(jnp.int32)
    wts_ref[r0:r1, :] = jnp.where(lane == 0.0, w1, jnp.where(lane == 1.0, w2, 0.0))


def _out_router(y_pool, y_m, y_h, h, w_out, g, b, wr_hi, wr_lo, br):
    row = lambda i: (i, 0)
    const2 = lambda i: (0, 0)
    return pl.pallas_call(
        _out_router_kernel,
        grid=(LP // TM_MID,),
        in_specs=[
            pl.BlockSpec((TM_MID, D_POOL), row),
            pl.BlockSpec((TM_MID, D_MLSTM), row),
            pl.BlockSpec((TM_MID, D_HGRN), row),
            pl.BlockSpec((TM_MID, D_MODEL), row),
            pl.BlockSpec((D_MODEL, D_MODEL), const2, pipeline_mode=pl.Buffered(1)),
            pl.BlockSpec((1, D_MODEL), const2),
            pl.BlockSpec((1, D_MODEL), const2),
            pl.BlockSpec((D_MODEL, LANES), const2),
            pl.BlockSpec((D_MODEL, LANES), const2),
            pl.BlockSpec((1, LANES), const2),
        ],
        out_specs=[
            pl.BlockSpec((TM_MID, D_MODEL), row),
            pl.BlockSpec((TM_MID * TOKEN_ROWS, LANES), row),
            pl.BlockSpec((TM_MID, LANES), row),
            pl.BlockSpec((TM_MID, LANES), row),
        ],
        out_shape=[
            jax.ShapeDtypeStruct((LP, D_MODEL), F32),
            jax.ShapeDtypeStruct((LP * TOKEN_ROWS, LANES), F32),
            jax.ShapeDtypeStruct((LP, LANES), jnp.int32),
            jax.ShapeDtypeStruct((LP, LANES), F32),
        ],
        compiler_params=_cparams(("arbitrary",)),
        name="out_proj_ln_router",
    )(y_pool, y_m, y_h, h, w_out, g, b, wr_hi, wr_lo, br)


ROUTE_ROWS = 256
SORTED_ROWS = N_EXPERT_TILES * TM_EXPERT
GATHER_UNROLL = 8


def _route_kernel(e_ref, dest_ref, tab_ref):
    e = e_ref[...]
    lane = lax.broadcasted_iota(jnp.int32, (ROUTE_ROWS, LANES), 1)
    kk = lax.broadcasted_iota(jnp.int32, (LANES, LANES), 0)
    ll = lax.broadcasted_iota(jnp.int32, (LANES, LANES), 1)
    before_lane = jnp.where(kk < ll, 1.0, 0.0).astype(BF16)
    all_lanes = jnp.ones((LANES, LANES), BF16)
    rr = lax.broadcasted_iota(jnp.int32, (ROUTE_ROWS, ROUTE_ROWS), 0)
    cc = lax.broadcasted_iota(jnp.int32, (ROUTE_ROWS, ROUTE_ROWS), 1)
    before_row = jnp.where(cc < rr, 1.0, 0.0).astype(BF16)

    lane1 = lane[0:1, :]
    rank = jnp.zeros((ROUTE_ROWS, LANES), F32)
    counts = jnp.zeros((1, LANES), F32)
    masks = []
    for x in range(N_EXPERTS):
        m = jnp.where(e == x, 1.0, 0.0)
        mb = m.astype(BF16)
        in_row = _dot(mb, before_lane)
        row_tot = _dot(mb, all_lanes)
        rows_before = _dot(before_row, row_tot.astype(BF16))
        rank = rank + m * (in_row + rows_before)
        total = rows_before[ROUTE_ROWS - 1:ROUTE_ROWS, :] + row_tot[ROUTE_ROWS - 1:ROUTE_ROWS, :]
        counts = counts + jnp.where(lane1 == x, total, 0.0)
        masks.append(m)

    padded = jnp.floor((counts + (TM_EXPERT - 1)) * (1.0 / TM_EXPERT)) * TM_EXPERT
    p_hi, p_lo = _split_bf16(padded)
    start = _dot(p_hi, before_lane) + _dot(p_lo, before_lane)
    end = start + padded

    dest = rank
    tile0 = (lane1 * TM_EXPERT).astype(F32)
    n_before = jnp.zeros((1, LANES), F32)
    for x in range(N_EXPERTS):
        dest = dest + masks[x] * start[:, x:x + 1]
        n_before = n_before + jnp.where(end[:, x:x + 1] <= tile0, 1.0, 0.0)
    dest_ref[...] = dest.astype(jnp.int32)

    tile_expert = jnp.minimum(n_before, float(N_EXPERTS - 1))
    n_tiles = end[:, N_EXPERTS - 1:N_EXPERTS] * (1.0 / TM_EXPERT)
    rows_left = jnp.zeros((1, LANES), F32)
    for x in range(N_EXPERTS):
        rows_left = rows_left + jnp.where(tile_expert == x, counts[:, x:x + 1] + start[:, x:x + 1] - tile0, 0.0)
    tile_rows = jnp.clip(rows_left, 0.0, float(TM_EXPERT))
    row = lax.broadcasted_iota(jnp.int32, (8, LANES), 0)
    tab = jnp.where(row == 0, tile_expert, jnp.where(row == 1, n_tiles, jnp.where(row == 2, tile_rows, 0.0)))
    tab_ref[...] = tab.astype(jnp.int32)


def _route(e_grid):
    return pl.pallas_call(
        _route_kernel,
        out_shape=[
            jax.ShapeDtypeStruct((ROUTE_ROWS, LANES), jnp.int32),
            jax.ShapeDtypeStruct((8, LANES), jnp.int32),
        ],
        compiler_params=pltpu.CompilerParams(vmem_limit_bytes=VMEM_LIMIT),
        name="moe_route",
    )(e_grid)


def _invert_kernel(dest_ref, tok_ref):
    def zero(p, carry):
        tok_ref[p] = 0
        return carry
    lax.fori_loop(0, SORTED_ROWS, zero, 0, unroll=GATHER_UNROLL)

    def place_row(r, carry):
        first_token = r * (LANES // 2)
        for c in range(LANES):
            tok_ref[dest_ref[r, c]] = first_token + c // 2
        return carry
    lax.fori_loop(0, N_ASSIGN // LANES, place_row, 0)


def _invert(dest):
    used_rows = -(-N_ASSIGN // (8 * LANES)) * 8
    return pl.pallas_call(
        _invert_kernel,
        grid=(1,),
        in_specs=[pl.BlockSpec((used_rows, LANES), lambda i: (0, 0), memory_space=pltpu.SMEM)],
        out_specs=pl.BlockSpec((SORTED_ROWS,), lambda i: (0,), memory_space=pltpu.SMEM),
        out_shape=jax.ShapeDtypeStruct((SORTED_ROWS,), jnp.int32),
        name="moe_invert",
    )(dest)


GATHER_PRIORITY = 1


def _expert_kernel(layer, tile_expert, tile_rows, n_tiles, tok_cur_ref, tok_next_ref, h_hbm, w1_hbm, w3_hbm, w2_hbm,
                   y_ref, xbuf, w1f, w3f, w2f, w1b, w3b, w2b, gsem, wsem, wslot_ref):
    i = pl.program_id(0)
    n_used = n_tiles[0]
    slot = i % 2

    def weight_copies(expert, ws):
        return [pltpu.make_async_copy(src.at[layer, expert], dst.at[ws], wsem.at[ws])
                for src, dst in ((w1_hbm, w1f), (w3_hbm, w3f), (w2_hbm, w2f))]

    def n_groups(tile):
        return lax.shift_right_logical(tile_rows[tile] + (GATHER_UNROLL - 1), GATHER_UNROLL.bit_length() - 1)

    def start_gather(tok_ref, s, groups):
        def body(g, carry):
            for u in range(GATHER_UNROLL):
                r = g * GATHER_UNROLL + u
                src = h_hbm.at[pl.ds(pl.multiple_of(tok_ref[0, 0, r] * TOKEN_ROWS, TOKEN_ROWS), TOKEN_ROWS)]
                dst = xbuf.at[s, pl.ds(pl.multiple_of(r * GATHER_PITCH, 8), TOKEN_ROWS)]
                pltpu.make_async_copy(src, dst, gsem.at[s]).start(priority=GATHER_PRIORITY)
            return carry
        lax.fori_loop(0, groups, body, 0)

    def wait_gather(s, groups):
        n = GATHER_UNROLL * TOKEN_ROWS

        def body(g, carry):
            pltpu.make_async_copy(h_hbm.at[pl.ds(0, n)], xbuf.at[s, pl.ds(0, n)], gsem.at[s]).wait()
            return carry
        lax.fori_loop(0, groups, body, 0)

    last_tile = N_EXPERT_TILES - 1

    @pl.when(i == 0)
    def _():
        xbuf[...] = jnp.zeros(xbuf.shape, F32)
        start_gather(tok_cur_ref, 0, n_groups(0))
        wslot_ref[0] = 0
        for cp in weight_copies(tile_expert[0], 0):
            cp.start()

    @pl.when(i < n_used)
    def _():
        wait_gather(slot, n_groups(i))
        start_gather(tok_next_ref, 1 - slot, n_groups(jnp.minimum(i + 1, last_tile)))
        expert = tile_expert[i]
        first_of_expert = jnp.logical_or(i == 0, expert != tile_expert[jnp.maximum(i - 1, 0)])

        @pl.when(first_of_expert)
        def _():
            ws = wslot_ref[0]
            for cp in weight_copies(expert, ws):
                cp.wait()
            nxt = lax.while_loop(
                lambda j: jnp.logical_and(j < n_used, tile_expert[jnp.minimum(j, last_tile)] == expert),
                lambda j: j + 1, i + 1)

            @pl.when(nxt < n_used)
            def _():
                for cp in weight_copies(tile_expert[jnp.minimum(nxt, last_tile)], 1 - ws):
                    cp.start()

            w1b[...] = w1f[ws].astype(BF16)
            w3b[...] = w3f[ws].astype(BF16)
            w2b[...] = w2f[ws].astype(BF16)
            wslot_ref[0] = 1 - ws

        x = _load_gathered(xbuf.at[slot], TM_EXPERT).astype(BF16)
        a = _dot(x, w1b[...])
        g = _dot(x, w3b[...])
        he = (a * _sigmoid(a) * g).astype(BF16)
        _store_token_linear(y_ref, _dot(he, w2b[...]))

    @pl.when(i >= n_used)
    def _():
        y_ref[...] = jnp.zeros(y_ref.shape, F32)


def _experts(h1, w1, w3, w2, layer, tile_expert, tile_rows, n_tiles, row_token):
    def cur(i, te, tr, nt):
        return (i, 0, 0)

    def nxt(i, te, tr, nt):
        return (jnp.minimum(i + 1, N_EXPERT_TILES - 1), 0, 0)

    grid_spec = pltpu.PrefetchScalarGridSpec(
        num_scalar_prefetch=3,
        grid=(N_EXPERT_TILES,),
        in_specs=[
            pl.BlockSpec((1, 1, TM_EXPERT), cur, memory_space=pltpu.SMEM),
            pl.BlockSpec((1, 1, TM_EXPERT), nxt, memory_space=pltpu.SMEM),
            pl.BlockSpec(memory_space=pl.ANY),
            pl.BlockSpec(memory_space=pl.ANY),
            pl.BlockSpec(memory_space=pl.ANY),
            pl.BlockSpec(memory_space=pl.ANY),
        ],
        out_specs=pl.BlockSpec((TM_EXPERT * TOKEN_ROWS, LANES), lambda i, te, tr, nt: (i, 0)),
        scratch_shapes=[
            pltpu.VMEM((2, TM_EXPERT * GATHER_PITCH, LANES), F32),
            pltpu.VMEM((2, D_MODEL, D_EXPERT), F32),
            pltpu.VMEM((2, D_MODEL, D_EXPERT), F32),
            pltpu.VMEM((2, D_EXPERT, D_MODEL), F32),
            pltpu.VMEM((D_MODEL, D_EXPERT), BF16),
            pltpu.VMEM((D_MODEL, D_EXPERT), BF16),
            pltpu.VMEM((D_EXPERT, D_MODEL), BF16),
            pltpu.SemaphoreType.DMA((2,)),
            pltpu.SemaphoreType.DMA((2,)),
            pltpu.SMEM((1,), jnp.int32),
        ],
    )
    return pl.pallas_call(
        functools.partial(_expert_kernel, layer),
        grid_spec=grid_spec,
        out_shape=jax.ShapeDtypeStruct((SORTED_ROWS * TOKEN_ROWS, LANES), F32),
        compiler_params=_cparams(("arbitrary",)),
        name="moe_experts",
    )(tile_expert, tile_rows, n_tiles, row_token, row_token, h1, w1, w3, w2)


def _combine_kernel(tm, n_steps, final, pos_cur_ref, pos_next_ref, h1_ref, wts_ref, g_ref, b_ref, y_hbm,
                    *rest):
    out_refs, (ybuf, sem) = rest[:-2], rest[-2:]
    i = pl.program_id(0)
    slot = i % 2

    def start_gather(pos_ref, s):
        def body(r, carry):
            for j in range(2):
                row0 = pl.multiple_of(pos_ref[0, 0, 2 * r + j] * TOKEN_ROWS, TOKEN_ROWS)
                dst = ybuf.at[s, j, pl.ds(pl.multiple_of(r * GATHER_PITCH, 8), TOKEN_ROWS)]
                pltpu.make_async_copy(y_hbm.at[pl.ds(row0, TOKEN_ROWS)], dst, sem.at[s]).start(priority=j)
            return carry
        lax.fori_loop(0, tm, body, 0, unroll=GATHER_UNROLL // 2)

    @pl.when(i == 0)
    def _():
        start_gather(pos_cur_ref, 0)

    for j in range(2):
        n = tm * TOKEN_ROWS
        pltpu.make_async_copy(y_hbm.at[pl.ds(0, n)], ybuf.at[slot, j, pl.ds(0, n)], sem.at[slot]).wait()

    @pl.when(i + 1 < n_steps)
    def _():
        start_gather(pos_next_ref, 1 - slot)

    wts = wts_ref[...]
    y = wts[:, 0:1] * _load_gathered(ybuf.at[slot, 0], tm) + wts[:, 1:2] * _load_gathered(ybuf.at[slot, 1], tm)
    h2 = _layer_norm_rows(ALPHA * h1_ref[...] + y, g_ref[...], b_ref[...])
    if final:
        out_refs[0][...] = h2
    else:
        h2 = _zero_pad_rows(h2, i * tm)
        out_refs[0][...] = h2
        out_refs[1][...] = h2.astype(BF16)


def _combine(h1, y_sorted, pos, wts, g, b, final):
    tm = TM_FINAL if final else TM_LN
    rows = SEQ if final else LP
    n_steps = rows // tm
    pos3 = pos[:2 * rows].reshape(n_steps, 1, 2 * tm)
    row = lambda i: (i, 0)
    const2 = lambda i: (0, 0)
    if final:
        out_specs = [pl.BlockSpec((tm, D_MODEL), row)]
        out_shape = [jax.ShapeDtypeStruct((rows, D_MODEL), F32)]
    else:
        out_specs = [pl.BlockSpec((tm, D_MODEL), row), pl.BlockSpec((tm, D_MODEL), row)]
        out_shape = [jax.ShapeDtypeStruct((rows, D_MODEL), F32), jax.ShapeDtypeStruct((rows, D_MODEL), BF16)]
    return pl.pallas_call(
        functools.partial(_combine_kernel, tm, n_steps, final),
        grid=(n_steps,),
        in_specs=[
            pl.BlockSpec((1, 1, 2 * tm), lambda i: (i, 0, 0), memory_space=pltpu.SMEM),
            pl.BlockSpec((1, 1, 2 * tm), lambda i: (jnp.minimum(i + 1, n_steps - 1), 0, 0), memory_space=pltpu.SMEM),
            pl.BlockSpec((tm, D_MODEL), row),
            pl.BlockSpec((tm, LANES), row),
            pl.BlockSpec((1, D_MODEL), const2),
            pl.BlockSpec((1, D_MODEL), const2),
            pl.BlockSpec(memory_space=pl.ANY),
        ],
        out_specs=out_specs,
        out_shape=out_shape,
        scratch_shapes=[pltpu.VMEM((2, 2, tm * GATHER_PITCH, LANES), F32), pltpu.SemaphoreType.DMA((2,))],
        compiler_params=_cparams(("arbitrary",)),
        name="moe_combine_ln_final" if final else "moe_combine_ln",
    )(pos3, pos3, h1, wts, g, b, y_sorted)


def _pad_lanes(v):
    return jnp.pad(v, ((0, 0), (0, LANES - v.shape[1])))


def kernel(x, meta_tokens, emb_ln_g, emb_ln_b, hgrn_lb_logits, w_in, conv_w, conv_b, ig_b, fg_b,
           mlstm_norm_g, pool_w, pool_scale, hgrn_norm_g, w_out, ln1_g, ln1_b,
           w_router_group, b_router_group, w_router_expert, b_router_expert, w1, w3, w2,
           ln2_g, ln2_b):
    assert x.shape == (1, SEQ, D_MODEL) and x.dtype == F32
    row2 = lambda v: v.reshape(1, -1)
    meta_blk = jnp.pad(meta_tokens.astype(F32), ((META_PAD, 0), (0, 0)))
    h, hb = _embed(x.reshape(SEQ, D_MODEL), meta_blk, row2(emb_ln_g), row2(emb_ln_b))

    out = None
    for l in range(DEPTH):
        p = _in_proj(hb, w_in, l)
        y_pool = _pool(p, pool_w[l], row2(pool_scale[l]))
        gate_bias = _pad_lanes(jnp.concatenate([ig_b[l], fg_b[l]]).reshape(1, -1))
        y_m = _mlstm(p, conv_w[l], row2(conv_b[l]), gate_bias, row2(mlstm_norm_g[l]))
        y_h = _hgrn(p, hgrn_lb_logits, row2(hgrn_norm_g[l]), l)

        w_r = _pad_lanes(jnp.concatenate([w_router_group[l], w_router_expert[l]], axis=1))
        wr_hi, wr_lo = _split_bf16(w_r)
        b_r = _pad_lanes(jnp.concatenate([b_router_group[l], b_router_expert[l]]).reshape(1, -1))
        h1, h1_lin, eid, wts = _out_router(y_pool, y_m, y_h, h, w_out[l].astype(BF16), row2(ln1_g[l]),
                                           row2(ln1_b[l]), wr_hi, wr_lo, b_r)

        e_flat = jnp.pad(eid[:, 0:2].reshape(-1), (0, ROUTE_ROWS * LANES - N_ASSIGN), constant_values=-1)
        dest, tab = _route(e_flat.reshape(ROUTE_ROWS, LANES))
        row_token = _invert(dest).reshape(N_EXPERT_TILES, 1, TM_EXPERT)
        y_sorted = _experts(h1_lin, w1, w3, w2, l, tab[0, :N_EXPERT_TILES], tab[2, :N_EXPERT_TILES], tab[1, :1],
                            row_token)
        pos = dest.reshape(-1)
        if l + 1 < DEPTH:
            h, hb = _combine(h1, y_sorted, pos, wts, row2(ln2_g[l]), row2(ln2_b[l]), final=False)
        else:
            (out,) = _combine(h1, y_sorted, pos, wts, row2(ln2_g[l]), row2(ln2_b[l]), final=True)
    return out.reshape(1, SEQ, D_MODEL)
```

```python
import functools

import jax
import jax.numpy as jnp
import numpy as np
from jax import lax
from jax.experimental import pallas as pl
from jax.experimental.pallas import tpu as pltpu

F32 = jnp.float32
BF16 = jnp.bfloat16

D_MODEL = 2048
SEQ = 8192
DEPTH = 2
N_META = 16
CHUNK = 64
D_POOL = D_MODEL // 4
POOL_WINDOWS = (2, 4, 8, 16)
POOL_GROUP = D_POOL // len(POOL_WINDOWS)
D_MLSTM = 3 * D_MODEL // 8
N_HEADS = 6
HEAD_DIM = D_MLSTM // N_HEADS
CONV_K = 4
D_HGRN = D_MODEL - D_POOL - D_MLSTM
N_GROUPS = 4
EXPERTS_PER_GROUP = 8
N_EXPERTS = N_GROUPS * EXPERTS_PER_GROUP
D_EXPERT = D_MODEL // 4
ALPHA = (2 * DEPTH) ** 0.25
LN_EPS = 1e-5
NEG_INF = float("-inf")

LANES = 128
LP = SEQ + CHUNK
META_ROW0 = SEQ
META_PAD = CHUNK - N_META
N_CHUNKS = LP // CHUNK
LP_EMBED = 17 * 512

COL_MQK = 0
COL_MV = 2 * D_MLSTM
COL_MO = COL_MV + D_MLSTM
COL_HQ = COL_MO + D_MLSTM
COL_HF = COL_HQ + D_HGRN
COL_HI = COL_HF + D_HGRN
COL_HG = COL_HI + D_HGRN
COL_POOL = COL_HG + D_HGRN
COL_GATE = COL_POOL + D_POOL
P_COLS = COL_GATE + 2 * LANES

TM_BIG = 2752
TN_IN = 256
TM_MID = 688
TM_LN = 192
TM_FINAL = 256
TM_EXPERT = 256
N_ASSIGN = 2 * LP
N_EXPERT_TILES = (N_ASSIGN + N_EXPERTS * (TM_EXPERT - 1)) // TM_EXPERT + 1
VMEM_LIMIT = 56 * 1024 * 1024
VMEM_LIMIT_OUT_PROJ = 62 * 1024 * 1024


def _cparams(sem, vmem_limit=VMEM_LIMIT):
    return pltpu.CompilerParams(dimension_semantics=sem, vmem_limit_bytes=vmem_limit)


def _sigmoid(x):
    return 1.0 / (1.0 + jnp.exp(-x))


def _log_sigmoid(x):
    return jnp.minimum(x, 0.0) - jnp.log1p(jnp.exp(-jnp.abs(x)))


def _layer_norm_rows(x, g, b):
    mu = jnp.mean(x, axis=-1, keepdims=True)
    xc = x - mu
    var = jnp.mean(xc * xc, axis=-1, keepdims=True)
    return xc * lax.rsqrt(var + LN_EPS) * g + b


def _dot(a, b):
    return jnp.dot(a, b, preferred_element_type=F32)


def _dot_nt(a, b):
    return lax.dot_general(a, b, (((1,), (1,)), ((), ())), preferred_element_type=F32)


def _dot_tn(a, b):
    return lax.dot_general(a, b, (((0,), (0,)), ((), ())), preferred_element_type=F32)


def _split_bf16(x):
    hi = x.astype(BF16)
    lo = (x - hi.astype(F32)).astype(BF16)
    return hi, lo


def _chunk_cumsum(x):
    r = lax.broadcasted_iota(jnp.int32, (CHUNK, CHUNK), 0)
    c = lax.broadcasted_iota(jnp.int32, (CHUNK, CHUNK), 1)
    tri = jnp.where(r >= c, 1.0, 0.0).astype(BF16)
    hi, lo = _split_bf16(x)
    return _dot(tri, hi) + _dot(tri, lo)


def _embed_kernel(x_ref, meta_ref, g_ref, b_ref, h_ref, hb_ref):
    i = pl.program_id(0)
    g = g_ref[...]
    b = b_ref[...]

    @pl.when(i < SEQ // 512)
    def _():
        y = _layer_norm_rows(x_ref[...], g, b)
        h_ref[...] = y
        hb_ref[...] = y.astype(BF16)

    @pl.when(i == SEQ // 512)
    def _():
        y = _layer_norm_rows(meta_ref[...], g, b)
        row = lax.broadcasted_iota(jnp.int32, (CHUNK, D_MODEL), 0)
        y = jnp.where(row >= META_PAD, y, 0.0)
        h_ref[0:CHUNK, :] = y
        hb_ref[0:CHUNK, :] = y.astype(BF16)
        h_ref[CHUNK:512, :] = jnp.zeros((512 - CHUNK, D_MODEL), F32)
        hb_ref[CHUNK:512, :] = jnp.zeros((512 - CHUNK, D_MODEL), BF16)


def _embed(x2d, meta_blk, g, b):
    nx = SEQ // 512
    return pl.pallas_call(
        _embed_kernel,
        grid=(nx + 1,),
        in_specs=[
            pl.BlockSpec((512, D_MODEL), lambda i: (jnp.minimum(i, nx - 1), 0)),
            pl.BlockSpec((CHUNK, D_MODEL), lambda i: (0, 0)),
            pl.BlockSpec((1, D_MODEL), lambda i: (0, 0)),
            pl.BlockSpec((1, D_MODEL), lambda i: (0, 0)),
        ],
        out_specs=[
            pl.BlockSpec((512, D_MODEL), lambda i: (i, 0)),
            pl.BlockSpec((512, D_MODEL), lambda i: (i, 0)),
        ],
        out_shape=[
            jax.ShapeDtypeStruct((LP_EMBED, D_MODEL), F32),
            jax.ShapeDtypeStruct((LP_EMBED, D_MODEL), BF16),
        ],
        compiler_params=_cparams(("arbitrary",)),
        name="embed_ln",
    )(x2d, meta_blk, g, b)


GATE_COLS = 2 * N_HEADS
W_IN_GATE0 = D_POOL + 4 * D_MLSTM
N_TILES_A = W_IN_GATE0 // TN_IN
N_TILES_H = 4 * D_HGRN // TN_IN
N_TILES_POOL = D_POOL // TN_IN


def _in_proj_kernel(x_ref, wa_ref, wb_ref, o_ref):
    j = pl.program_id(1)
    shifted = jnp.logical_and(j >= N_TILES_A, j < N_TILES_A + N_TILES_H)

    @pl.when(jnp.logical_not(shifted))
    def _():
        o_ref[...] = _dot(x_ref[...], wa_ref[0].astype(BF16))

    @pl.when(shifted)
    def _():
        w = jnp.concatenate([wa_ref[0], wb_ref[0]], axis=1)[:, GATE_COLS:GATE_COLS + TN_IN]
        o_ref[...] = _dot(x_ref[...], w.astype(BF16))


def _in_proj(hb, w_in, layer):
    n_main = N_TILES_A + N_TILES_H

    def wa_map(i, j):
        return (layer, 0, jnp.where(j < n_main, j, N_TILES_A))

    def wb_map(i, j):
        shifted = jnp.logical_and(j >= N_TILES_A, j < n_main)
        return (layer, 0, jnp.where(shifted, (j + 1) * (TN_IN // LANES), 0))

    def out_map(i, j):
        return (i, jnp.where(j < N_TILES_POOL, j + n_main - N_TILES_POOL, jnp.where(j < n_main, j - N_TILES_POOL, j)))

    return pl.pallas_call(
        _in_proj_kernel,
        grid=(LP // TM_BIG, P_COLS // TN_IN),
        in_specs=[
            pl.BlockSpec((TM_BIG, D_MODEL), lambda i, j: (i, 0)),
            pl.BlockSpec((1, D_MODEL, TN_IN), wa_map),
            pl.BlockSpec((1, D_MODEL, LANES), wb_map),
        ],
        out_specs=pl.BlockSpec((TM_BIG, TN_IN), out_map),
        out_shape=jax.ShapeDtypeStruct((LP, P_COLS), F32),
        compiler_params=_cparams(("arbitrary", "arbitrary")),
        name="in_proj",
    )(hb, w_in, w_in)


POOL_HALO = 16


def _pool_kernel(u_ref, halo_ref, pw_ref, ps_ref, y_ref, ubuf):
    i = pl.program_id(0)
    u = u_ref[...]
    ubuf[0:POOL_HALO, :] = halo_ref[...]
    ubuf[POOL_HALO:POOL_HALO + TM_MID, :] = u
    row = i * TM_MID + lax.broadcasted_iota(jnp.int32, (TM_MID, POOL_GROUP), 0)
    pos = jnp.where(row >= META_ROW0 + META_PAD, row - (META_ROW0 + META_PAD) + 1, 2 * POOL_HALO)
    scale = ps_ref[...]
    for gi, w in enumerate(POOL_WINDOWS):
        cs = slice(gi * POOL_GROUP, (gi + 1) * POOL_GROUP)
        ug = u[:, cs]
        acc = ug
        for j in range(1, w):
            acc = acc + ubuf[POOL_HALO - j:POOL_HALO - j + TM_MID, cs]
        div = jnp.minimum(pos, w).astype(F32)
        d = acc / div - ug
        yg = _dot(d.astype(BF16), pw_ref[gi].astype(BF16))
        y_ref[:, cs] = (yg * scale[:, cs]).astype(BF16)


def _pool(p, pool_w, pool_scale):
    blocks_per_tile = TM_MID // POOL_HALO
    last_meta_block = (LP - POOL_HALO) // POOL_HALO

    def halo_map(i):
        return (jnp.where(i == 0, last_meta_block, i * blocks_per_tile - 1), COL_POOL // D_POOL)

    return pl.pallas_call(
        _pool_kernel,
        grid=(LP // TM_MID,),
        in_specs=[
            pl.BlockSpec((TM_MID, D_POOL), lambda i: (i, COL_POOL // D_POOL)),
            pl.BlockSpec((POOL_HALO, D_POOL), halo_map),
            pl.BlockSpec((len(POOL_WINDOWS), POOL_GROUP, POOL_GROUP), lambda i: (0, 0, 0)),
            pl.BlockSpec((1, D_POOL), lambda i: (0, 0)),
        ],
        out_specs=pl.BlockSpec((TM_MID, D_POOL), lambda i: (i, 0)),
        out_shape=jax.ShapeDtypeStruct((LP, D_POOL), BF16),
        scratch_shapes=[pltpu.VMEM((POOL_HALO + TM_MID, D_POOL), F32)],
        compiler_params=_cparams(("arbitrary",)),
        name="pool_mixer",
    )(p, p, pool_w, pool_scale)


MLSTM_CHUNKS_PER_STEP = 4
HGRN_CHUNKS_PER_STEP = 4


def _step_block(c, chunks_per_step):
    return jnp.where(c == 0, SEQ // (chunks_per_step * CHUNK), c - 1)


CONV_HALO = 8


def _mlstm_kernel(mqk_ref, mv_ref, mo_ref, gt_ref, cw_ref, cb_ref, gb_ref, ng_ref, y_ref,
                  s_sc, m_sc, xbuf):
    c = pl.program_id(0)

    @pl.when(c == 0)
    def _():
        s_sc[...] = jnp.zeros(s_sc.shape, F32)
        m_sc[...] = jnp.full(m_sc.shape, NEG_INF, F32)
        xbuf[0:CONV_HALO, :] = jnp.zeros((CONV_HALO, 2 * D_MLSTM), F32)
        _mlstm_chunk(0, True, mqk_ref, mv_ref, mo_ref, gt_ref, cw_ref, cb_ref, gb_ref, ng_ref, y_ref,
                     s_sc, m_sc, xbuf)

    @pl.when(c > 0)
    def _():
        for g in range(MLSTM_CHUNKS_PER_STEP):
            _mlstm_chunk(g, False, mqk_ref, mv_ref, mo_ref, gt_ref, cw_ref, cb_ref, gb_ref, ng_ref, y_ref,
                         s_sc, m_sc, xbuf)


def _mlstm_chunk(g, is_meta, mqk_ref, mv_ref, mo_ref, gt_ref, cw_ref, cb_ref, gb_ref, ng_ref, y_ref,
                 s_sc, m_sc, xbuf):
    rows = pl.ds(g * CHUNK, CHUNK)
    x = mqk_ref[rows, :]
    xbuf[CONV_HALO:CONV_HALO + CHUNK, :] = x
    cw = cw_ref[...]
    conv = cb_ref[...] + cw[CONV_K - 1:CONV_K, :] * x
    for j in range(CONV_K - 1):
        off = CONV_HALO - (CONV_K - 1) + j
        conv = conv + cw[j:j + 1, :] * xbuf[off:off + CHUNK, :]
    xbuf[0:CONV_HALO, :] = x[CHUNK - CONV_HALO:CHUNK, :]
    qk = conv * _sigmoid(conv)
    v_all = mv_ref[rows, :]
    og_all = _sigmoid(mo_ref[rows, :])
    ng = ng_ref[...]

    z = gt_ref[rows, :] + gb_ref[...]
    if is_meta:
        valid = lax.broadcasted_iota(jnp.int32, (CHUNK, LANES), 0) >= META_PAD
        ig = jnp.where(valid, z, NEG_INF)
        lf = jnp.where(valid, _log_sigmoid(z), 0.0)
    else:
        ig = z
        lf = _log_sigmoid(z)
    g_t = pltpu.roll(_chunk_cumsum(lf), LANES - N_HEADS, axis=1)
    a = ig - g_t
    row = lax.broadcasted_iota(jnp.int32, (CHUNK, LANES), 0)
    a_max = a
    shift = 1
    while shift < CHUNK:
        a_max = jnp.maximum(a_max, jnp.where(row >= shift, pltpu.roll(a_max, shift, axis=0), NEG_INF))
        shift *= 2
    m_prev = m_sc[0:1, :]
    m_t = g_t + jnp.maximum(a_max, m_prev)
    m_ts = jnp.where(m_t == NEG_INF, 0.0, m_t)
    c_t = g_t - m_ts
    inter_all = jnp.exp(g_t + m_prev - m_ts)
    floor_all = jnp.exp(-m_ts)
    g_last = g_t[CHUNK - 1:CHUNK, :]
    m_new = g_last + jnp.maximum(a_max[CHUNK - 1:CHUNK, :], m_prev)
    decay_all = jnp.exp(g_last + m_prev - m_new)
    wexp_all = jnp.exp(g_last + a - m_new)
    m_sc[0:1, :] = m_new
    a_rows = a.T

    r64 = lax.broadcasted_iota(jnp.int32, (CHUNK, CHUNK), 0)
    c64 = lax.broadcasted_iota(jnp.int32, (CHUNK, CHUNK), 1)
    causal = r64 >= c64
    k_scale = HEAD_DIM ** -0.5
    ones_cols = jnp.ones((CHUNK, HEAD_DIM), BF16)
    mean_cols = jnp.full((HEAD_DIM, HEAD_DIM), 1.0 / HEAD_DIM, BF16)

    def row_mean(x):
        hi, lo = _split_bf16(x)
        return _dot(hi, mean_cols) + _dot(lo, mean_cols)

    heads = range(N_HEADS)
    hsl = [slice(h * HEAD_DIM, (h + 1) * HEAD_DIM) for h in heads]
    qb = [qk[:, hsl[h]].astype(BF16) for h in heads]
    k = [qk[:, D_MLSTM + h * HEAD_DIM:D_MLSTM + (h + 1) * HEAD_DIM] * k_scale for h in heads]
    v_aug = [jnp.concatenate([v_all[:, hsl[h]].astype(BF16), ones_cols], axis=1) for h in heads]
    state = [s_sc[h] for h in heads]

    s = []
    for h in heads:
        dexp = jnp.exp(jnp.where(causal, c_t[:, h:h + 1] + a_rows[h:h + 1, :], NEG_INF))
        s.append((_dot_nt(qb[h], k[h].astype(BF16)) * dexp).astype(BF16))
    hh = []
    for h in heads:
        nd = _dot(s[h], v_aug[h]) + inter_all[:, h:h + 1] * _dot(qb[h], state[h].astype(BF16))
        den = nd[:, HEAD_DIM:2 * HEAD_DIM]
        hh.append(nd[:, 0:HEAD_DIM] / jnp.maximum(jnp.abs(den), floor_all[:, h:h + 1]))
    for h in heads:
        wk = (k[h] * wexp_all[:, h:h + 1]).astype(BF16)
        s_sc[h] = decay_all[:, h:h + 1] * state[h] + _dot_tn(wk, v_aug[h])
    hc = [hh[h] - row_mean(hh[h]) for h in heads]
    var = [row_mean(hc[h] * hc[h]) for h in heads]
    for h in heads:
        y = hc[h] * lax.rsqrt(var[h] + LN_EPS) * ng[:, hsl[h]] * og_all[:, hsl[h]]
        y_ref[rows, hsl[h]] = y.astype(BF16)


def _mlstm(p, conv_w, conv_b, gate_bias, norm_g):
    n = MLSTM_CHUNKS_PER_STEP
    step_rows = n * CHUNK

    def col(block_w, start):
        return lambda c: (_step_block(c, n), start // block_w)

    const2 = lambda c: (0, 0)
    return pl.pallas_call(
        _mlstm_kernel,
        grid=(SEQ // step_rows + 1,),
        in_specs=[
            pl.BlockSpec((step_rows, 2 * D_MLSTM), col(2 * D_MLSTM, COL_MQK)),
            pl.BlockSpec((step_rows, D_MLSTM), col(D_MLSTM, COL_MV)),
            pl.BlockSpec((step_rows, D_MLSTM), col(D_MLSTM, COL_MO)),
            pl.BlockSpec((step_rows, LANES), col(LANES, COL_GATE)),
            pl.BlockSpec((CONV_K, 2 * D_MLSTM), const2),
            pl.BlockSpec((1, 2 * D_MLSTM), const2),
            pl.BlockSpec((1, LANES), const2),
            pl.BlockSpec((1, D_MLSTM), const2),
        ],
        out_specs=pl.BlockSpec((step_rows, D_MLSTM), lambda c: (_step_block(c, n), 0)),
        out_shape=jax.ShapeDtypeStruct((LP, D_MLSTM), BF16),
        scratch_shapes=[
            pltpu.VMEM((N_HEADS, HEAD_DIM, 2 * HEAD_DIM), F32),
            pltpu.VMEM((8, LANES), F32),
            pltpu.VMEM((CONV_HALO + CHUNK, 2 * D_MLSTM), F32),
        ],
        compiler_params=_cparams(("arbitrary",)),
        name="mlstm_mixer",
    )(p, p, p, p, conv_w, conv_b, gate_bias, norm_g)


N_LEVELS = 6


def _hgrn_tables():
    t = np.arange(CHUNK)
    sel = np.zeros((N_LEVELS * CHUNK, CHUNK), np.float32)
    mask = np.zeros((N_LEVELS + 1, CHUNK, CHUNK), np.float32)
    upper = np.zeros((CHUNK, LANES), np.float32)
    for l in range(N_LEVELS):
        half = 1 << l
        ref_row = (t // (2 * half)) * (2 * half) + half - 1
        sel[l * CHUNK + t, ref_row] = 1.0
        is_upper = (t // half) % 2 == 1
        upper[:, l] = is_upper
        same = (t[:, None] // (2 * half)) == (t[None, :] // (2 * half))
        mask[l] = same & is_upper[:, None] & ~is_upper[None, :]
    mask[N_LEVELS] = np.eye(CHUNK)
    return sel, mask, upper


def _hgrn_kernel(layer, hq_ref, hf_ref, hi_ref, hg_ref, lbl_ref, ng_ref, sel_ref, mask_ref, up_ref,
                 y_ref, st_sc):
    c = pl.program_id(0)

    lbl = lbl_ref[...]
    e = jnp.exp(lbl - jnp.max(lbl, axis=0, keepdims=True))
    sm = e / jnp.sum(e, axis=0, keepdims=True)
    lb = jnp.sum(sm[0:layer + 1, :], axis=0, keepdims=True) - sm[0:1, :]
    refs = (hq_ref, hf_ref, hi_ref, hg_ref, ng_ref, sel_ref, mask_ref, up_ref, y_ref, st_sc)

    @pl.when(c == 0)
    def _():
        st_sc[...] = jnp.zeros(st_sc.shape, F32)
        _hgrn_chunk(0, lb, *refs)

    @pl.when(c > 0)
    def _():
        for g in range(HGRN_CHUNKS_PER_STEP):
            _hgrn_chunk(g, lb, *refs)


def _hgrn_chunk(g, lb, hq_ref, hf_ref, hi_ref, hg_ref, ng_ref, sel_ref, mask_ref, up_ref, y_ref, st_sc):
    rows = pl.ds(g * CHUNK, CHUNK)
    z = hf_ref[rows, :]
    a = jnp.log(lb)
    bb = jnp.log1p(-lb) + _log_sigmoid(z)
    mx = jnp.maximum(a, bb)
    log_f = mx + jnp.log(jnp.exp(a - mx) + jnp.exp(bb - mx))
    kk = (1.0 - lb) * _sigmoid(-z)
    hq = hq_ref[rows, :]
    q = hq * _sigmoid(hq)
    v = hi_ref[rows, :].astype(BF16)
    hg = hg_ref[rows, :]
    gate = hg * _sigmoid(hg) * ng_ref[...]

    b = _chunk_cumsum(log_f)
    b_hi, b_lo = _split_bf16(b)
    sel = sel_ref[...]
    refs = _dot(sel, b_hi) + _dot(sel, b_lo)
    up = up_ref[...]

    amats = [None] * N_HEADS
    for l in range(N_LEVELS + 1):
        if l < N_LEVELS:
            ref_l = refs[l * CHUNK:(l + 1) * CHUNK, :]
            sign = 2.0 * up[:, l:l + 1] - 1.0
            zl = jnp.exp(sign * (b - ref_l))
            ql = (q * zl).astype(BF16)
            kl = (kk * zl).astype(BF16)
        else:
            ql = q.astype(BF16)
            kl = kk.astype(BF16)
        ml = mask_ref[l]
        for h in range(N_HEADS):
            hs = slice(h * HEAD_DIM, (h + 1) * HEAD_DIM)
            part = ml * _dot_nt(ql[:, hs], kl[:, hs])
            amats[h] = part if amats[h] is None else amats[h] + part

    b_last = b[CHUNK - 1:CHUNK, :]
    qe = (q * jnp.exp(b)).astype(BF16)
    kd = (kk * jnp.exp(b_last - b)).astype(BF16)
    e_last = jnp.exp(b_last)
    for h in range(N_HEADS):
        hs = slice(h * HEAD_DIM, (h + 1) * HEAD_DIM)
        st = st_sc[h]
        o = _dot(amats[h].astype(BF16), v[:, hs]) + _dot_nt(qe[:, hs], st.astype(BF16))
        st_sc[h] = e_last[:, hs] * st + _dot_tn(v[:, hs], kd[:, hs])
        o = o * lax.rsqrt(jnp.mean(o * o, axis=1, keepdims=True) + LN_EPS)
        y_ref[rows, hs] = (o * gate[:, hs]).astype(BF16)


def _hgrn(p, lb_logits, norm_g, layer):
    n = HGRN_CHUNKS_PER_STEP
    step_rows = n * CHUNK

    def col(start):
        return lambda c: (_step_block(c, n), start // D_HGRN)

    sel, mask, upper = _hgrn_tables()
    const2 = lambda c: (0, 0)
    return pl.pallas_call(
        functools.partial(_hgrn_kernel, layer),
        grid=(SEQ // step_rows + 1,),
        in_specs=[
            pl.BlockSpec((step_rows, D_HGRN), col(COL_HQ)),
            pl.BlockSpec((step_rows, D_HGRN), col(COL_HF)),
            pl.BlockSpec((step_rows, D_HGRN), col(COL_HI)),
            pl.BlockSpec((step_rows, D_HGRN), col(COL_HG)),
            pl.BlockSpec((DEPTH, D_HGRN), const2),
            pl.BlockSpec((1, D_HGRN), const2),
            pl.BlockSpec((N_LEVELS * CHUNK, CHUNK), const2),
            pl.BlockSpec((N_LEVELS + 1, CHUNK, CHUNK), lambda c: (0, 0, 0)),
            pl.BlockSpec((CHUNK, LANES), const2),
        ],
        out_specs=pl.BlockSpec((step_rows, D_HGRN), lambda c: (_step_block(c, n), 0)),
        out_shape=jax.ShapeDtypeStruct((LP, D_HGRN), BF16),
        scratch_shapes=[pltpu.VMEM((N_HEADS, HEAD_DIM, HEAD_DIM), F32)],
        compiler_params=_cparams(("arbitrary",)),
        name="hgrn_mixer",
    )(p, p, p, p, lb_logits, norm_g, jnp.asarray(sel, BF16), jnp.asarray(mask, F32), jnp.asarray(upper, F32))


def _zero_pad_rows(y, row0):
    row = row0 + lax.broadcasted_iota(jnp.int32, y.shape, 0)
    is_pad = jnp.logical_and(row >= META_ROW0, row < META_ROW0 + META_PAD)
    return jnp.where(is_pad, 0.0, y)


def _first_argmax(x, lane, valid):
    xm = jnp.where(valid, x, NEG_INF)
    mx = jnp.max(xm, axis=1, keepdims=True)
    idx = jnp.min(jnp.where(jnp.logical_and(valid, xm == mx), lane, float(LANES)), axis=1, keepdims=True)
    return mx, idx


TOKEN_ROWS = D_MODEL // LANES
GATHER_PITCH = 24


def _store_token_linear(ref, x, first_token=0):
    n = x.shape[0]
    for k in range(TOKEN_ROWS):
        ref[pl.ds(first_token * TOKEN_ROWS + k, n, stride=TOKEN_ROWS), :] = x[:, k * LANES:(k + 1) * LANES]


def _load_gathered(ref, n):
    return jnp.concatenate([ref[pl.ds(k, n, stride=GATHER_PITCH), :] for k in range(TOKEN_ROWS)], axis=1)


OUT_SUB_BLOCKS = ((0, 352), (352, TM_MID))


def _out_router_kernel(yp_ref, ym_ref, yh_ref, h_ref, wo_ref, g_ref, b_ref, wrh_ref, wrl_ref, br_ref,
                       h1_ref, hlin_ref, eid_ref, wts_ref):
    projected = []
    for r0, r1 in OUT_SUB_BLOCKS:
        y = jnp.concatenate([yp_ref[r0:r1, :], ym_ref[r0:r1, :], yh_ref[r0:r1, :]], axis=1)
        projected.append(_dot(y, wo_ref[...]))
    for (r0, r1), acc in zip(OUT_SUB_BLOCKS, projected):
        _out_router_rows(r0, r1, acc, h_ref, g_ref, b_ref, wrh_ref, wrl_ref, br_ref,
                         h1_ref, hlin_ref, eid_ref, wts_ref)


def _out_router_rows(r0, r1, acc, h_ref, g_ref, b_ref, wrh_ref, wrl_ref, br_ref, h1_ref, hlin_ref, eid_ref, wts_ref):
    i = pl.program_id(0)
    h1 = _layer_norm_rows(ALPHA * h_ref[r0:r1, :] + acc, g_ref[...], b_ref[...])
    h1 = _zero_pad_rows(h1, i * TM_MID + r0)
    h1_ref[r0:r1, :] = h1
    _store_token_linear(hlin_ref, h1, r0)

    x_hi, x_lo = _split_bf16(h1)
    hi_both = _dot(x_hi, jnp.concatenate([wrh_ref[...], wrl_ref[...]], axis=1))
    logits = hi_both[:, 0:LANES] + hi_both[:, LANES:2 * LANES] + _dot(x_lo, wrh_ref[...]) + br_ref[...]
    lane = lax.broadcasted_iota(jnp.int32, logits.shape, 1).astype(F32)

    is_grp = lane < N_GROUPS
    g_max, g_idx = _first_argmax(logits, lane, is_grp)
    g_exp = jnp.where(is_grp, jnp.exp(logits - g_max), 0.0)
    p_grp = 1.0 / jnp.sum(g_exp, axis=1, keepdims=True)

    e_lo = N_GROUPS + g_idx * EXPERTS_PER_GROUP
    in_grp = jnp.logical_and(lane >= e_lo, lane < e_lo + EXPERTS_PER_GROUP)
    e_max, e1 = _first_argmax(logits, lane, in_grp)
    e_exp = jnp.where(in_grp, jnp.exp(logits - e_max), 0.0)
    p_exp = e_exp / jnp.sum(e_exp, axis=1, keepdims=True)
    p1, _ = _first_argmax(p_exp, lane, in_grp)
    rest = jnp.logical_and(in_grp, lane != e1)
    p2, e2 = _first_argmax(p_exp, lane, rest)
    psum = p1 + p2
    w1 = p_grp * p1 / psum
    w2 = p_grp * p2 / psum
    eid = jnp.where(lane == 0.0, e1 - N_GROUPS, jnp.where(lane == 1.0, e2 - N_GROUPS, 0.0))
    eid_ref[r0:r1, :] = eid.astype(jnp.int32)
    wts_ref[r0:r1, :] = jnp.where(lane == 0.0, w1, jnp.where(lane == 1.0, w2, 0.0))


def _out_router(y_pool, y_m, y_h, h, w_out, g, b, wr_hi, wr_lo, br):
    row = lambda i: (i, 0)
    const2 = lambda i: (0, 0)
    return pl.pallas_call(
        _out_router_kernel,
        grid=(LP // TM_MID,),
        in_specs=[
            pl.BlockSpec((TM_MID, D_POOL), row),
            pl.BlockSpec((TM_MID, D_MLSTM), row),
            pl.BlockSpec((TM_MID, D_HGRN), row),
            pl.BlockSpec((TM_MID, D_MODEL), row),
            pl.BlockSpec((D_MODEL, D_MODEL), const2, pipeline_mode=pl.Buffered(1)),
            pl.BlockSpec((1, D_MODEL), const2),
            pl.BlockSpec((1, D_MODEL), const2),
            pl.BlockSpec((D_MODEL, LANES), const2),
            pl.BlockSpec((D_MODEL, LANES), const2),
            pl.BlockSpec((1, LANES), const2),
        ],
        out_specs=[
            pl.BlockSpec((TM_MID, D_MODEL), row),
            pl.BlockSpec((TM_MID * TOKEN_ROWS, LANES), row),
            pl.BlockSpec((TM_MID, LANES), row),
            pl.BlockSpec((TM_MID, LANES), row),
        ],
        out_shape=[
            jax.ShapeDtypeStruct((LP, D_MODEL), F32),
            jax.ShapeDtypeStruct((LP * TOKEN_ROWS, LANES), F32),
            jax.ShapeDtypeStruct((LP, LANES), jnp.int32),
            jax.ShapeDtypeStruct((LP, LANES), F32),
        ],
        compiler_params=_cparams(("arbitrary",), VMEM_LIMIT_OUT_PROJ),
        name="out_proj_ln_router",
    )(y_pool, y_m, y_h, h, w_out, g, b, wr_hi, wr_lo, br)


ROUTE_ROWS = 256
SORTED_ROWS = N_EXPERT_TILES * TM_EXPERT
GATHER_UNROLL = 8


def _route_kernel(e_ref, dest_ref, tab_ref):
    e = e_ref[...]
    lane = lax.broadcasted_iota(jnp.int32, (ROUTE_ROWS, LANES), 1)
    kk = lax.broadcasted_iota(jnp.int32, (LANES, LANES), 0)
    ll = lax.broadcasted_iota(jnp.int32, (LANES, LANES), 1)
    before_lane = jnp.where(kk < ll, 1.0, 0.0).astype(BF16)
    all_lanes = jnp.ones((LANES, LANES), BF16)
    rr = lax.broadcasted_iota(jnp.int32, (ROUTE_ROWS, ROUTE_ROWS), 0)
    cc = lax.broadcasted_iota(jnp.int32, (ROUTE_ROWS, ROUTE_ROWS), 1)
    before_row = jnp.where(cc < rr, 1.0, 0.0).astype(BF16)

    lane1 = lane[0:1, :]
    rank = jnp.zeros((ROUTE_ROWS, LANES), F32)
    counts = jnp.zeros((1, LANES), F32)
    masks = []
    for x in range(N_EXPERTS):
        m = jnp.where(e == x, 1.0, 0.0)
        mb = m.astype(BF16)
        in_row = _dot(mb, before_lane)
        row_tot = _dot(mb, all_lanes)
        rows_before = _dot(before_row, row_tot.astype(BF16))
        rank = rank + m * (in_row + rows_before)
        total = rows_before[ROUTE_ROWS - 1:ROUTE_ROWS, :] + row_tot[ROUTE_ROWS - 1:ROUTE_ROWS, :]
        counts = counts + jnp.where(lane1 == x, total, 0.0)
        masks.append(m)

    padded = jnp.floor((counts + (TM_EXPERT - 1)) * (1.0 / TM_EXPERT)) * TM_EXPERT
    p_hi, p_lo = _split_bf16(padded)
    start = _dot(p_hi, before_lane) + _dot(p_lo, before_lane)
    end = start + padded

    dest = rank
    tile0 = (lane1 * TM_EXPERT).astype(F32)
    n_before = jnp.zeros((1, LANES), F32)
    for x in range(N_EXPERTS):
        dest = dest + masks[x] * start[:, x:x + 1]
        n_before = n_before + jnp.where(end[:, x:x + 1] <= tile0, 1.0, 0.0)
    dest_ref[...] = dest.astype(jnp.int32)

    tile_expert = jnp.minimum(n_before, float(N_EXPERTS - 1))
    n_tiles = end[:, N_EXPERTS - 1:N_EXPERTS] * (1.0 / TM_EXPERT)
    rows_left = jnp.zeros((1, LANES), F32)
    for x in range(N_EXPERTS):
        rows_left = rows_left + jnp.where(tile_expert == x, counts[:, x:x + 1] + start[:, x:x + 1] - tile0, 0.0)
    tile_rows = jnp.clip(rows_left, 0.0, float(TM_EXPERT))
    row = lax.broadcasted_iota(jnp.int32, (8, LANES), 0)
    tab = jnp.where(row == 0, tile_expert, jnp.where(row == 1, n_tiles, jnp.where(row == 2, tile_rows, 0.0)))
    tab_ref[...] = tab.astype(jnp.int32)


def _route(e_grid):
    return pl.pallas_call(
        _route_kernel,
        out_shape=[
            jax.ShapeDtypeStruct((ROUTE_ROWS, LANES), jnp.int32),
            jax.ShapeDtypeStruct((8, LANES), jnp.int32),
        ],
        compiler_params=pltpu.CompilerParams(vmem_limit_bytes=VMEM_LIMIT),
        name="moe_route",
    )(e_grid)


def _invert_kernel(dest_ref, tok_ref):
    def zero(p, carry):
        tok_ref[p] = 0
        return carry
    lax.fori_loop(0, SORTED_ROWS, zero, 0, unroll=GATHER_UNROLL)

    def place_row(r, carry):
        first_token = r * (LANES // 2)
        for c in range(LANES):
            tok_ref[dest_ref[r, c]] = first_token + c // 2
        return carry
    lax.fori_loop(0, N_ASSIGN // LANES, place_row, 0)


def _invert(dest):
    used_rows = -(-N_ASSIGN // (8 * LANES)) * 8
    return pl.pallas_call(
        _invert_kernel,
        grid=(1,),
        in_specs=[pl.BlockSpec((used_rows, LANES), lambda i: (0, 0), memory_space=pltpu.SMEM)],
        out_specs=pl.BlockSpec((SORTED_ROWS,), lambda i: (0,), memory_space=pltpu.SMEM),
        out_shape=jax.ShapeDtypeStruct((SORTED_ROWS,), jnp.int32),
        name="moe_invert",
    )(dest)


GATHER_PRIORITY = 0
WEIGHT_PRIORITY = 1


def _expert_kernel(layer, tile_expert, tile_rows, n_tiles, tok_cur_ref, tok_next_ref, h_hbm, w1_hbm, w3_hbm, w2_hbm,
                   y_ref, xbuf, w1f, w3f, w2f, w1b, w3b, w2b, gsem, wsem, wslot_ref):
    i = pl.program_id(0)
    n_used = n_tiles[0]
    slot = i % 2

    def weight_copies(expert, ws):
        return [pltpu.make_async_copy(src.at[layer, expert], dst.at[ws], wsem.at[ws])
                for src, dst in ((w1_hbm, w1f), (w3_hbm, w3f), (w2_hbm, w2f))]

    def n_groups(tile):
        return lax.shift_right_logical(tile_rows[tile] + (GATHER_UNROLL - 1), GATHER_UNROLL.bit_length() - 1)

    def start_gather(tok_ref, s, groups):
        def body(g, carry):
            for u in range(GATHER_UNROLL):
                r = g * GATHER_UNROLL + u
                src = h_hbm.at[pl.ds(pl.multiple_of(tok_ref[0, 0, r] * TOKEN_ROWS, TOKEN_ROWS), TOKEN_ROWS)]
                dst = xbuf.at[s, pl.ds(pl.multiple_of(r * GATHER_PITCH, 8), TOKEN_ROWS)]
                pltpu.make_async_copy(src, dst, gsem.at[s]).start(priority=GATHER_PRIORITY)
            return carry
        lax.fori_loop(0, groups, body, 0)

    def wait_gather(s, groups):
        n = GATHER_UNROLL * TOKEN_ROWS

        def body(g, carry):
            pltpu.make_async_copy(h_hbm.at[pl.ds(0, n)], xbuf.at[s, pl.ds(0, n)], gsem.at[s]).wait()
            return carry
        lax.fori_loop(0, groups, body, 0)

    last_tile = N_EXPERT_TILES - 1

    @pl.when(i == 0)
    def _():
        xbuf[...] = jnp.zeros(xbuf.shape, F32)
        start_gather(tok_cur_ref, 0, n_groups(0))
        wslot_ref[0] = 0
        for cp in weight_copies(tile_expert[0], 0):
            cp.start(priority=WEIGHT_PRIORITY)

    @pl.when(i < n_used)
    def _():
        wait_gather(slot, n_groups(i))
        start_gather(tok_next_ref, 1 - slot, n_groups(jnp.minimum(i + 1, last_tile)))
        expert = tile_expert[i]
        first_of_expert = jnp.logical_or(i == 0, expert != tile_expert[jnp.maximum(i - 1, 0)])

        @pl.when(first_of_expert)
        def _():
            ws = wslot_ref[0]
            for cp in weight_copies(expert, ws):
                cp.wait()
            nxt = lax.while_loop(
                lambda j: jnp.logical_and(j < n_used, tile_expert[jnp.minimum(j, last_tile)] == expert),
                lambda j: j + 1, i + 1)

            @pl.when(nxt < n_used)
            def _():
                for cp in weight_copies(tile_expert[jnp.minimum(nxt, last_tile)], 1 - ws):
                    cp.start(priority=WEIGHT_PRIORITY)

            w1b[...] = w1f[ws].astype(BF16)
            w3b[...] = w3f[ws].astype(BF16)
            w2b[...] = w2f[ws].astype(BF16)
            wslot_ref[0] = 1 - ws

        x = _load_gathered(xbuf.at[slot], TM_EXPERT).astype(BF16)
        a = _dot(x, w1b[...])
        g = _dot(x, w3b[...])
        he = (a * _sigmoid(a) * g).astype(BF16)
        _store_token_linear(y_ref, _dot(he, w2b[...]))

    @pl.when(i >= n_used)
    def _():
        y_ref[...] = jnp.zeros(y_ref.shape, F32)


def _experts(h1, w1, w3, w2, layer, tile_expert, tile_rows, n_tiles, row_token):
    def cur(i, te, tr, nt):
        return (i, 0, 0)

    def nxt(i, te, tr, nt):
        return (jnp.minimum(i + 1, N_EXPERT_TILES - 1), 0, 0)

    grid_spec = pltpu.PrefetchScalarGridSpec(
        num_scalar_prefetch=3,
        grid=(N_EXPERT_TILES,),
        in_specs=[
            pl.BlockSpec((1, 1, TM_EXPERT), cur, memory_space=pltpu.SMEM),
            pl.BlockSpec((1, 1, TM_EXPERT), nxt, memory_space=pltpu.SMEM),
            pl.BlockSpec(memory_space=pl.ANY),
            pl.BlockSpec(memory_space=pl.ANY),
            pl.BlockSpec(memory_space=pl.ANY),
            pl.BlockSpec(memory_space=pl.ANY),
        ],
        out_specs=pl.BlockSpec((TM_EXPERT * TOKEN_ROWS, LANES), lambda i, te, tr, nt: (i, 0)),
        scratch_shapes=[
            pltpu.VMEM((2, TM_EXPERT * GATHER_PITCH, LANES), F32),
            pltpu.VMEM((2, D_MODEL, D_EXPERT), F32),
            pltpu.VMEM((2, D_MODEL, D_EXPERT), F32),
            pltpu.VMEM((2, D_EXPERT, D_MODEL), F32),
            pltpu.VMEM((D_MODEL, D_EXPERT), BF16),
            pltpu.VMEM((D_MODEL, D_EXPERT), BF16),
            pltpu.VMEM((D_EXPERT, D_MODEL), BF16),
            pltpu.SemaphoreType.DMA((2,)),
            pltpu.SemaphoreType.DMA((2,)),
            pltpu.SMEM((1,), jnp.int32),
        ],
    )
    return pl.pallas_call(
        functools.partial(_expert_kernel, layer),
        grid_spec=grid_spec,
        out_shape=jax.ShapeDtypeStruct((SORTED_ROWS * TOKEN_ROWS, LANES), F32),
        compiler_params=_cparams(("arbitrary",)),
        name="moe_experts",
    )(tile_expert, tile_rows, n_tiles, row_token, row_token, h1, w1, w3, w2)


def _combine_kernel(tm, n_steps, final, pos_cur_ref, pos_next_ref, h1_ref, wts_ref, g_ref, b_ref, y_hbm,
                    *rest):
    out_refs, (ybuf, sem) = rest[:-2], rest[-2:]
    i = pl.program_id(0)
    slot = i % 2

    def start_gather(pos_ref, s):
        def body(r, carry):
            for j in range(2):
                row0 = pl.multiple_of(pos_ref[0, 0, 2 * r + j] * TOKEN_ROWS, TOKEN_ROWS)
                dst = ybuf.at[s, j, pl.ds(pl.multiple_of(r * GATHER_PITCH, 8), TOKEN_ROWS)]
                pltpu.make_async_copy(y_hbm.at[pl.ds(row0, TOKEN_ROWS)], dst, sem.at[s]).start()
            return carry
        lax.fori_loop(0, tm, body, 0, unroll=GATHER_UNROLL // 2)

    @pl.when(i == 0)
    def _():
        start_gather(pos_cur_ref, 0)

    for j in range(2):
        n = tm * TOKEN_ROWS
        pltpu.make_async_copy(y_hbm.at[pl.ds(0, n)], ybuf.at[slot, j, pl.ds(0, n)], sem.at[slot]).wait()

    @pl.when(i + 1 < n_steps)
    def _():
        start_gather(pos_next_ref, 1 - slot)

    wts = wts_ref[...]
    y = wts[:, 0:1] * _load_gathered(ybuf.at[slot, 0], tm) + wts[:, 1:2] * _load_gathered(ybuf.at[slot, 1], tm)
    h2 = _layer_norm_rows(ALPHA * h1_ref[...] + y, g_ref[...], b_ref[...])
    if final:
        out_refs[0][...] = h2
    else:
        h2 = _zero_pad_rows(h2, i * tm)
        out_refs[0][...] = h2
        out_refs[1][...] = h2.astype(BF16)


def _combine(h1, y_sorted, pos, wts, g, b, final):
    tm = TM_FINAL if final else TM_LN
    rows = SEQ if final else LP
    n_steps = rows // tm
    pos3 = pos[:2 * rows].reshape(n_steps, 1, 2 * tm)
    row = lambda i: (i, 0)
    const2 = lambda i: (0, 0)
    if final:
        out_specs = [pl.BlockSpec((tm, D_MODEL), row)]
        out_shape = [jax.ShapeDtypeStruct((rows, D_MODEL), F32)]
    else:
        out_specs = [pl.BlockSpec((tm, D_MODEL), row), pl.BlockSpec((tm, D_MODEL), row)]
        out_shape = [jax.ShapeDtypeStruct((rows, D_MODEL), F32), jax.ShapeDtypeStruct((rows, D_MODEL), BF16)]
    return pl.pallas_call(
        functools.partial(_combine_kernel, tm, n_steps, final),
        grid=(n_steps,),
        in_specs=[
            pl.BlockSpec((1, 1, 2 * tm), lambda i: (i, 0, 0), memory_space=pltpu.SMEM),
            pl.BlockSpec((1, 1, 2 * tm), lambda i: (jnp.minimum(i + 1, n_steps - 1), 0, 0), memory_space=pltpu.SMEM),
            pl.BlockSpec((tm, D_MODEL), row),
            pl.BlockSpec((tm, LANES), row),
            pl.BlockSpec((1, D_MODEL), const2),
            pl.BlockSpec((1, D_MODEL), const2),
            pl.BlockSpec(memory_space=pl.ANY),
        ],
        out_specs=out_specs,
        out_shape=out_shape,
        scratch_shapes=[pltpu.VMEM((2, 2, tm * GATHER_PITCH, LANES), F32), pltpu.SemaphoreType.DMA((2,))],
        compiler_params=_cparams(("arbitrary",)),
        name="moe_combine_ln_final" if final else "moe_combine_ln",
    )(pos3, pos3, h1, wts, g, b, y_sorted)


def _pad_lanes(v):
    return jnp.pad(v, ((0, 0), (0, LANES - v.shape[1])))


def kernel(x, meta_tokens, emb_ln_g, emb_ln_b, hgrn_lb_logits, w_in, conv_w, conv_b, ig_b, fg_b,
           mlstm_norm_g, pool_w, pool_scale, hgrn_norm_g, w_out, ln1_g, ln1_b,
           w_router_group, b_router_group, w_router_expert, b_router_expert, w1, w3, w2,
           ln2_g, ln2_b):
    assert x.shape == (1, SEQ, D_MODEL) and x.dtype == F32
    row2 = lambda v: v.reshape(1, -1)
    meta_blk = jnp.pad(meta_tokens.astype(F32), ((META_PAD, 0), (0, 0)))
    h, hb = _embed(x.reshape(SEQ, D_MODEL), meta_blk, row2(emb_ln_g), row2(emb_ln_b))

    out = None
    for l in range(DEPTH):
        p = _in_proj(hb, w_in, l)
        y_pool = _pool(p, pool_w[l], row2(pool_scale[l]))
        gate_bias = _pad_lanes(jnp.concatenate([ig_b[l], fg_b[l]]).reshape(1, -1))
        y_m = _mlstm(p, conv_w[l], row2(conv_b[l]), gate_bias, row2(mlstm_norm_g[l]))
        y_h = _hgrn(p, hgrn_lb_logits, row2(hgrn_norm_g[l]), l)

        w_r = _pad_lanes(jnp.concatenate([w_router_group[l], w_router_expert[l]], axis=1))
        wr_hi, wr_lo = _split_bf16(w_r)
        b_r = _pad_lanes(jnp.concatenate([b_router_group[l], b_router_expert[l]]).reshape(1, -1))
        h1, h1_lin, eid, wts = _out_router(y_pool, y_m, y_h, h, w_out[l].astype(BF16), row2(ln1_g[l]),
                                           row2(ln1_b[l]), wr_hi, wr_lo, b_r)

        e_flat = jnp.pad(eid[:, 0:2].reshape(-1), (0, ROUTE_ROWS * LANES - N_ASSIGN), constant_values=-1)
        dest, tab = _route(e_flat.reshape(ROUTE_ROWS, LANES))
        row_token = _invert(dest).reshape(N_EXPERT_TILES, 1, TM_EXPERT)
        y_sorted = _experts(h1_lin, w1, w3, w2, l, tab[0, :N_EXPERT_TILES], tab[2, :N_EXPERT_TILES], tab[1, :1],
                            row_token)
        pos = dest.reshape(-1)
        if l + 1 < DEPTH:
            h, hb = _combine(h1, y_sorted, pos, wts, row2(ln2_g[l]), row2(ln2_b[l]), final=False)
        else:
            (out,) = _combine(h1, y_sorted, pos, wts, row2(ln2_g[l]), row2(ln2_b[l]), final=True)
    return out.reshape(1, SEQ, D_MODEL)
```

```python
import functools

import jax
import jax.numpy as jnp
import numpy as np
from jax import lax
from jax.experimental import pallas as pl
from jax.experimental.pallas import tpu as pltpu

F32 = jnp.float32
BF16 = jnp.bfloat16

D_MODEL = 2048
SEQ = 8192
DEPTH = 2
N_META = 16
CHUNK = 64
D_POOL = D_MODEL // 4
POOL_WINDOWS = (2, 4, 8, 16)
POOL_GROUP = D_POOL // len(POOL_WINDOWS)
D_MLSTM = 3 * D_MODEL // 8
N_HEADS = 6
HEAD_DIM = D_MLSTM // N_HEADS
CONV_K = 4
D_HGRN = D_MODEL - D_POOL - D_MLSTM
N_GROUPS = 4
EXPERTS_PER_GROUP = 8
N_EXPERTS = N_GROUPS * EXPERTS_PER_GROUP
D_EXPERT = D_MODEL // 4
ALPHA = (2 * DEPTH) ** 0.25
LN_EPS = 1e-5
NEG_INF = float("-inf")

LANES = 128
LP = SEQ + CHUNK
META_ROW0 = SEQ
META_PAD = CHUNK - N_META
N_CHUNKS = LP // CHUNK
LP_EMBED = 17 * 512

COL_MQK = 0
COL_MV = 2 * D_MLSTM
COL_MO = COL_MV + D_MLSTM
COL_HQ = COL_MO + D_MLSTM
COL_HF = COL_HQ + D_HGRN
COL_HI = COL_HF + D_HGRN
COL_HG = COL_HI + D_HGRN
COL_POOL = COL_HG + D_HGRN
COL_GATE = COL_POOL + D_POOL
P_COLS = COL_GATE + 2 * LANES

TM_BIG = 2752
TN_IN = 256
TM_MID = 688
TM_LN = 192
TM_FINAL = 256
TM_EXPERT = 256
N_ASSIGN = 2 * LP
N_EXPERT_TILES = (N_ASSIGN + N_EXPERTS * (TM_EXPERT - 1)) // TM_EXPERT + 1
VMEM_LIMIT = 56 * 1024 * 1024
VMEM_LIMIT_OUT_PROJ = 62 * 1024 * 1024


def _cparams(sem, vmem_limit=VMEM_LIMIT):
    return pltpu.CompilerParams(dimension_semantics=sem, vmem_limit_bytes=vmem_limit)


def _sigmoid(x):
    return 1.0 / (1.0 + jnp.exp(-x))


def _log_sigmoid(x):
    return jnp.minimum(x, 0.0) - jnp.log1p(jnp.exp(-jnp.abs(x)))


def _layer_norm_rows(x, g, b):
    mu = jnp.mean(x, axis=-1, keepdims=True)
    xc = x - mu
    var = jnp.mean(xc * xc, axis=-1, keepdims=True)
    return xc * lax.rsqrt(var + LN_EPS) * g + b


def _dot(a, b):
    return jnp.dot(a, b, preferred_element_type=F32)


def _dot_nt(a, b):
    return lax.dot_general(a, b, (((1,), (1,)), ((), ())), preferred_element_type=F32)


def _dot_tn(a, b):
    return lax.dot_general(a, b, (((0,), (0,)), ((), ())), preferred_element_type=F32)


def _split_bf16(x):
    hi = x.astype(BF16)
    lo = (x - hi.astype(F32)).astype(BF16)
    return hi, lo


def _chunk_cumsum(x):
    r = lax.broadcasted_iota(jnp.int32, (CHUNK, CHUNK), 0)
    c = lax.broadcasted_iota(jnp.int32, (CHUNK, CHUNK), 1)
    tri = jnp.where(r >= c, 1.0, 0.0).astype(BF16)
    hi, lo = _split_bf16(x)
    return _dot(tri, hi) + _dot(tri, lo)


def _embed_kernel(x_ref, meta_ref, g_ref, b_ref, h_ref, hb_ref):
    i = pl.program_id(0)
    g = g_ref[...]
    b = b_ref[...]

    @pl.when(i < SEQ // 512)
    def _():
        y = _layer_norm_rows(x_ref[...], g, b)
        h_ref[...] = y
        hb_ref[...] = y.astype(BF16)

    @pl.when(i == SEQ // 512)
    def _():
        y = _layer_norm_rows(meta_ref[...], g, b)
        row = lax.broadcasted_iota(jnp.int32, (CHUNK, D_MODEL), 0)
        y = jnp.where(row >= META_PAD, y, 0.0)
        h_ref[0:CHUNK, :] = y
        hb_ref[0:CHUNK, :] = y.astype(BF16)
        h_ref[CHUNK:512, :] = jnp.zeros((512 - CHUNK, D_MODEL), F32)
        hb_ref[CHUNK:512, :] = jnp.zeros((512 - CHUNK, D_MODEL), BF16)


def _embed(x2d, meta_blk, g, b):
    nx = SEQ // 512
    return pl.pallas_call(
        _embed_kernel,
        grid=(nx + 1,),
        in_specs=[
            pl.BlockSpec((512, D_MODEL), lambda i: (jnp.minimum(i, nx - 1), 0)),
            pl.BlockSpec((CHUNK, D_MODEL), lambda i: (0, 0)),
            pl.BlockSpec((1, D_MODEL), lambda i: (0, 0)),
            pl.BlockSpec((1, D_MODEL), lambda i: (0, 0)),
        ],
        out_specs=[
            pl.BlockSpec((512, D_MODEL), lambda i: (i, 0)),
            pl.BlockSpec((512, D_MODEL), lambda i: (i, 0)),
        ],
        out_shape=[
            jax.ShapeDtypeStruct((LP_EMBED, D_MODEL), F32),
            jax.ShapeDtypeStruct((LP_EMBED, D_MODEL), BF16),
        ],
        compiler_params=_cparams(("arbitrary",)),
        name="embed_ln",
    )(x2d, meta_blk, g, b)


K_TILES = D_MODEL // LANES
W_IN_MLSTM0 = D_POOL
W_IN_GATE0 = D_POOL + 4 * D_MLSTM
W_IN_HGRN0 = W_IN_GATE0 + 2 * N_HEADS
D_IN = W_IN_HGRN0 + 4 * D_HGRN


def _in_proj_kernel(layer, x_ref, w_ref, o_ref):
    w_t = jnp.concatenate([w_ref[:, DEPTH * kt + layer, :] for kt in range(K_TILES)], axis=1)
    o_ref[...] = _dot_nt(x_ref[...], w_t.astype(BF16))


def _in_proj(hb, w_in, layer):
    w_view = w_in.reshape(DEPTH, K_TILES, LANES, D_IN).transpose(3, 1, 0, 2).reshape(D_IN, K_TILES * DEPTH, LANES)

    def first_col(j):
        c = j * TN_IN
        return jnp.where(c < COL_HQ, W_IN_MLSTM0 + c,
                         jnp.where(c < COL_POOL, W_IN_HGRN0 + (c - COL_HQ),
                                   jnp.where(c < COL_GATE, c - COL_POOL, W_IN_GATE0)))

    w_block = (pl.Element(TN_IN), pl.Element(K_TILES * DEPTH), pl.Element(LANES))
    return pl.pallas_call(
        functools.partial(_in_proj_kernel, layer),
        grid=(LP // TM_BIG, P_COLS // TN_IN),
        in_specs=[
            pl.BlockSpec((TM_BIG, D_MODEL), lambda i, j: (i, 0)),
            pl.BlockSpec(w_block, lambda i, j: (first_col(j), 0, 0)),
        ],
        out_specs=pl.BlockSpec((TM_BIG, TN_IN), lambda i, j: (i, j)),
        out_shape=jax.ShapeDtypeStruct((LP, P_COLS), F32),
        compiler_params=_cparams(("arbitrary", "arbitrary")),
        name="in_proj",
    )(hb, w_view)


POOL_HALO = 16


def _pool_kernel(u_ref, halo_ref, pw_ref, ps_ref, y_ref, ubuf):
    i = pl.program_id(0)
    u = u_ref[...]
    ubuf[0:POOL_HALO, :] = halo_ref[...]
    ubuf[POOL_HALO:POOL_HALO + TM_MID, :] = u
    row = i * TM_MID + lax.broadcasted_iota(jnp.int32, (TM_MID, POOL_GROUP), 0)
    pos = jnp.where(row >= META_ROW0 + META_PAD, row - (META_ROW0 + META_PAD) + 1, 2 * POOL_HALO)
    scale = ps_ref[...]
    for gi, w in enumerate(POOL_WINDOWS):
        cs = slice(gi * POOL_GROUP, (gi + 1) * POOL_GROUP)
        ug = u[:, cs]
        acc = ug
        for j in range(1, w):
            acc = acc + ubuf[POOL_HALO - j:POOL_HALO - j + TM_MID, cs]
        div = jnp.minimum(pos, w).astype(F32)
        d = acc / div - ug
        yg = _dot(d.astype(BF16), pw_ref[gi].astype(BF16))
        y_ref[:, cs] = (yg * scale[:, cs]).astype(BF16)


def _pool(p, pool_w, pool_scale):
    blocks_per_tile = TM_MID // POOL_HALO
    last_meta_block = (LP - POOL_HALO) // POOL_HALO

    def halo_map(i):
        return (jnp.where(i == 0, last_meta_block, i * blocks_per_tile - 1), COL_POOL // D_POOL)

    return pl.pallas_call(
        _pool_kernel,
        grid=(LP // TM_MID,),
        in_specs=[
            pl.BlockSpec((TM_MID, D_POOL), lambda i: (i, COL_POOL // D_POOL)),
            pl.BlockSpec((POOL_HALO, D_POOL), halo_map),
            pl.BlockSpec((len(POOL_WINDOWS), POOL_GROUP, POOL_GROUP), lambda i: (0, 0, 0)),
            pl.BlockSpec((1, D_POOL), lambda i: (0, 0)),
        ],
        out_specs=pl.BlockSpec((TM_MID, D_POOL), lambda i: (i, 0)),
        out_shape=jax.ShapeDtypeStruct((LP, D_POOL), BF16),
        scratch_shapes=[pltpu.VMEM((POOL_HALO + TM_MID, D_POOL), F32)],
        compiler_params=_cparams(("arbitrary",)),
        name="pool_mixer",
    )(p, p, pool_w, pool_scale)


MLSTM_CHUNKS_PER_STEP = 4
HGRN_CHUNKS_PER_STEP = 4


def _step_block(c, chunks_per_step):
    return jnp.where(c == 0, SEQ // (chunks_per_step * CHUNK), c - 1)


CONV_HALO = 8


def _mlstm_kernel(mqk_ref, mv_ref, mo_ref, gt_ref, cw_ref, cb_ref, gb_ref, ng_ref, y_ref,
                  s_sc, m_sc, xbuf):
    c = pl.program_id(0)

    @pl.when(c == 0)
    def _():
        s_sc[...] = jnp.zeros(s_sc.shape, F32)
        m_sc[...] = jnp.full(m_sc.shape, NEG_INF, F32)
        xbuf[0:CONV_HALO, :] = jnp.zeros((CONV_HALO, 2 * D_MLSTM), F32)
        _mlstm_chunk(0, True, mqk_ref, mv_ref, mo_ref, gt_ref, cw_ref, cb_ref, gb_ref, ng_ref, y_ref,
                     s_sc, m_sc, xbuf)

    @pl.when(c > 0)
    def _():
        for g in range(MLSTM_CHUNKS_PER_STEP):
            _mlstm_chunk(g, False, mqk_ref, mv_ref, mo_ref, gt_ref, cw_ref, cb_ref, gb_ref, ng_ref, y_ref,
                         s_sc, m_sc, xbuf)


def _mlstm_chunk(g, is_meta, mqk_ref, mv_ref, mo_ref, gt_ref, cw_ref, cb_ref, gb_ref, ng_ref, y_ref,
                 s_sc, m_sc, xbuf):
    rows = pl.ds(g * CHUNK, CHUNK)
    x = mqk_ref[rows, :]
    xbuf[CONV_HALO:CONV_HALO + CHUNK, :] = x
    cw = cw_ref[...]
    conv = cb_ref[...] + cw[CONV_K - 1:CONV_K, :] * x
    for j in range(CONV_K - 1):
        off = CONV_HALO - (CONV_K - 1) + j
        conv = conv + cw[j:j + 1, :] * xbuf[off:off + CHUNK, :]
    xbuf[0:CONV_HALO, :] = x[CHUNK - CONV_HALO:CHUNK, :]
    qk = conv * _sigmoid(conv)
    v_all = mv_ref[rows, :]
    og_all = _sigmoid(mo_ref[rows, :])
    ng = ng_ref[...]

    z = gt_ref[rows, :] + gb_ref[...]
    if is_meta:
        valid = lax.broadcasted_iota(jnp.int32, (CHUNK, LANES), 0) >= META_PAD
        ig = jnp.where(valid, z, NEG_INF)
        lf = jnp.where(valid, _log_sigmoid(z), 0.0)
    else:
        ig = z
        lf = _log_sigmoid(z)
    g_t = pltpu.roll(_chunk_cumsum(lf), LANES - N_HEADS, axis=1)
    a = ig - g_t
    row = lax.broadcasted_iota(jnp.int32, (CHUNK, LANES), 0)
    a_max = a
    shift = 1
    while shift < CHUNK:
        a_max = jnp.maximum(a_max, jnp.where(row >= shift, pltpu.roll(a_max, shift, axis=0), NEG_INF))
        shift *= 2
    m_prev = m_sc[0:1, :]
    m_t = g_t + jnp.maximum(a_max, m_prev)
    m_ts = jnp.where(m_t == NEG_INF, 0.0, m_t)
    c_t = g_t - m_ts
    inter_all = jnp.exp(g_t + m_prev - m_ts)
    floor_all = jnp.exp(-m_ts)
    g_last = g_t[CHUNK - 1:CHUNK, :]
    m_new = g_last + jnp.maximum(a_max[CHUNK - 1:CHUNK, :], m_prev)
    decay_all = jnp.exp(g_last + m_prev - m_new)
    wexp_all = jnp.exp(g_last + a - m_new)
    m_sc[0:1, :] = m_new
    a_rows = a.T

    r64 = lax.broadcasted_iota(jnp.int32, (CHUNK, CHUNK), 0)
    c64 = lax.broadcasted_iota(jnp.int32, (CHUNK, CHUNK), 1)
    causal = r64 >= c64
    k_scale = HEAD_DIM ** -0.5
    ones_cols = jnp.ones((CHUNK, HEAD_DIM), BF16)
    mean_cols = jnp.full((HEAD_DIM, HEAD_DIM), 1.0 / HEAD_DIM, BF16)

    def row_mean(x):
        hi, lo = _split_bf16(x)
        return _dot(hi, mean_cols) + _dot(lo, mean_cols)

    heads = range(N_HEADS)
    hsl = [slice(h * HEAD_DIM, (h + 1) * HEAD_DIM) for h in heads]
    qb = [qk[:, hsl[h]].astype(BF16) for h in heads]
    k = [qk[:, D_MLSTM + h * HEAD_DIM:D_MLSTM + (h + 1) * HEAD_DIM] * k_scale for h in heads]
    v_aug = [jnp.concatenate([v_all[:, hsl[h]].astype(BF16), ones_cols], axis=1) for h in heads]
    state = [s_sc[h] for h in heads]

    s = []
    for h in heads:
        dexp = jnp.exp(jnp.where(causal, c_t[:, h:h + 1] + a_rows[h:h + 1, :], NEG_INF))
        s.append((_dot_nt(qb[h], k[h].astype(BF16)) * dexp).astype(BF16))
    hh = []
    for h in heads:
        nd = _dot(s[h], v_aug[h]) + inter_all[:, h:h + 1] * _dot(qb[h], state[h].astype(BF16))
        den = nd[:, HEAD_DIM:2 * HEAD_DIM]
        hh.append(nd[:, 0:HEAD_DIM] / jnp.maximum(jnp.abs(den), floor_all[:, h:h + 1]))
    for h in heads:
        wk = (k[h] * wexp_all[:, h:h + 1]).astype(BF16)
        s_sc[h] = decay_all[:, h:h + 1] * state[h] + _dot_tn(wk, v_aug[h])
    hc = [hh[h] - row_mean(hh[h]) for h in heads]
    var = [row_mean(hc[h] * hc[h]) for h in heads]
    for h in heads:
        y = hc[h] * lax.rsqrt(var[h] + LN_EPS) * ng[:, hsl[h]] * og_all[:, hsl[h]]
        y_ref[rows, hsl[h]] = y.astype(BF16)


def _mlstm(p, conv_w, conv_b, gate_bias, norm_g):
    n = MLSTM_CHUNKS_PER_STEP
    step_rows = n * CHUNK

    def col(block_w, start):
        return lambda c: (_step_block(c, n), start // block_w)

    const2 = lambda c: (0, 0)
    return pl.pallas_call(
        _mlstm_kernel,
        grid=(SEQ // step_rows + 1,),
        in_specs=[
            pl.BlockSpec((step_rows, 2 * D_MLSTM), col(2 * D_MLSTM, COL_MQK)),
            pl.BlockSpec((step_rows, D_MLSTM), col(D_MLSTM, COL_MV)),
            pl.BlockSpec((step_rows, D_MLSTM), col(D_MLSTM, COL_MO)),
            pl.BlockSpec((step_rows, LANES), col(LANES, COL_GATE)),
            pl.BlockSpec((CONV_K, 2 * D_MLSTM), const2),
            pl.BlockSpec((1, 2 * D_MLSTM), const2),
            pl.BlockSpec((1, LANES), const2),
            pl.BlockSpec((1, D_MLSTM), const2),
        ],
        out_specs=pl.BlockSpec((step_rows, D_MLSTM), lambda c: (_step_block(c, n), 0)),
        out_shape=jax.ShapeDtypeStruct((LP, D_MLSTM), BF16),
        scratch_shapes=[
            pltpu.VMEM((N_HEADS, HEAD_DIM, 2 * HEAD_DIM), F32),
            pltpu.VMEM((8, LANES), F32),
            pltpu.VMEM((CONV_HALO + CHUNK, 2 * D_MLSTM), F32),
        ],
        compiler_params=_cparams(("arbitrary",)),
        name="mlstm_mixer",
    )(p, p, p, p, conv_w, conv_b, gate_bias, norm_g)


N_LEVELS = 6


def _hgrn_tables():
    t = np.arange(CHUNK)
    sel = np.zeros((N_LEVELS * CHUNK, CHUNK), np.float32)
    mask = np.zeros((N_LEVELS + 1, CHUNK, CHUNK), np.float32)
    upper = np.zeros((CHUNK, LANES), np.float32)
    for l in range(N_LEVELS):
        half = 1 << l
        ref_row = (t // (2 * half)) * (2 * half) + half - 1
        sel[l * CHUNK + t, ref_row] = 1.0
        is_upper = (t // half) % 2 == 1
        upper[:, l] = is_upper
        same = (t[:, None] // (2 * half)) == (t[None, :] // (2 * half))
        mask[l] = same & is_upper[:, None] & ~is_upper[None, :]
    mask[N_LEVELS] = np.eye(CHUNK)
    return sel, mask, upper


def _hgrn_kernel(layer, hq_ref, hf_ref, hi_ref, hg_ref, lbl_ref, ng_ref, sel_ref, mask_ref, up_ref,
                 y_ref, st_sc):
    c = pl.program_id(0)

    lbl = lbl_ref[...]
    e = jnp.exp(lbl - jnp.max(lbl, axis=0, keepdims=True))
    sm = e / jnp.sum(e, axis=0, keepdims=True)
    lb = jnp.sum(sm[0:layer + 1, :], axis=0, keepdims=True) - sm[0:1, :]
    refs = (hq_ref, hf_ref, hi_ref, hg_ref, ng_ref, sel_ref, mask_ref, up_ref, y_ref, st_sc)

    @pl.when(c == 0)
    def _():
        st_sc[...] = jnp.zeros(st_sc.shape, F32)
        _hgrn_chunk(0, lb, *refs)

    @pl.when(c > 0)
    def _():
        for g in range(HGRN_CHUNKS_PER_STEP):
            _hgrn_chunk(g, lb, *refs)


def _hgrn_chunk(g, lb, hq_ref, hf_ref, hi_ref, hg_ref, ng_ref, sel_ref, mask_ref, up_ref, y_ref, st_sc):
    rows = pl.ds(g * CHUNK, CHUNK)
    z = hf_ref[rows, :]
    a = jnp.log(lb)
    bb = jnp.log1p(-lb) + _log_sigmoid(z)
    mx = jnp.maximum(a, bb)
    log_f = mx + jnp.log(jnp.exp(a - mx) + jnp.exp(bb - mx))
    kk = (1.0 - lb) * _sigmoid(-z)
    hq = hq_ref[rows, :]
    q = hq * _sigmoid(hq)
    v = hi_ref[rows, :].astype(BF16)
    hg = hg_ref[rows, :]
    gate = hg * _sigmoid(hg) * ng_ref[...]

    b = _chunk_cumsum(log_f)
    b_hi, b_lo = _split_bf16(b)
    sel = sel_ref[...]
    refs = _dot(sel, b_hi) + _dot(sel, b_lo)
    up = up_ref[...]

    amats = [None] * N_HEADS
    for l in range(N_LEVELS + 1):
        if l < N_LEVELS:
            ref_l = refs[l * CHUNK:(l + 1) * CHUNK, :]
            sign = 2.0 * up[:, l:l + 1] - 1.0
            zl = jnp.exp(sign * (b - ref_l))
            ql = (q * zl).astype(BF16)
            kl = (kk * zl).astype(BF16)
        else:
            ql = q.astype(BF16)
            kl = kk.astype(BF16)
        ml = mask_ref[l]
        for h in range(N_HEADS):
            hs = slice(h * HEAD_DIM, (h + 1) * HEAD_DIM)
            part = ml * _dot_nt(ql[:, hs], kl[:, hs])
            amats[h] = part if amats[h] is None else amats[h] + part

    b_last = b[CHUNK - 1:CHUNK, :]
    qe = (q * jnp.exp(b)).astype(BF16)
    kd = (kk * jnp.exp(b_last - b)).astype(BF16)
    e_last = jnp.exp(b_last)
    for h in range(N_HEADS):
        hs = slice(h * HEAD_DIM, (h + 1) * HEAD_DIM)
        st = st_sc[h]
        o = _dot(amats[h].astype(BF16), v[:, hs]) + _dot_nt(qe[:, hs], st.astype(BF16))
        st_sc[h] = e_last[:, hs] * st + _dot_tn(v[:, hs], kd[:, hs])
        o = o * lax.rsqrt(jnp.mean(o * o, axis=1, keepdims=True) + LN_EPS)
        y_ref[rows, hs] = (o * gate[:, hs]).astype(BF16)


def _hgrn(p, lb_logits, norm_g, layer):
    n = HGRN_CHUNKS_PER_STEP
    step_rows = n * CHUNK

    def col(start):
        return lambda c: (_step_block(c, n), start // D_HGRN)

    sel, mask, upper = _hgrn_tables()
    const2 = lambda c: (0, 0)
    return pl.pallas_call(
        functools.partial(_hgrn_kernel, layer),
        grid=(SEQ // step_rows + 1,),
        in_specs=[
            pl.BlockSpec((step_rows, D_HGRN), col(COL_HQ)),
            pl.BlockSpec((step_rows, D_HGRN), col(COL_HF)),
            pl.BlockSpec((step_rows, D_HGRN), col(COL_HI)),
            pl.BlockSpec((step_rows, D_HGRN), col(COL_HG)),
            pl.BlockSpec((DEPTH, D_HGRN), const2),
            pl.BlockSpec((1, D_HGRN), const2),
            pl.BlockSpec((N_LEVELS * CHUNK, CHUNK), const2),
            pl.BlockSpec((N_LEVELS + 1, CHUNK, CHUNK), lambda c: (0, 0, 0)),
            pl.BlockSpec((CHUNK, LANES), const2),
        ],
        out_specs=pl.BlockSpec((step_rows, D_HGRN), lambda c: (_step_block(c, n), 0)),
        out_shape=jax.ShapeDtypeStruct((LP, D_HGRN), BF16),
        scratch_shapes=[pltpu.VMEM((N_HEADS, HEAD_DIM, HEAD_DIM), F32)],
        compiler_params=_cparams(("arbitrary",)),
        name="hgrn_mixer",
    )(p, p, p, p, lb_logits, norm_g, jnp.asarray(sel, BF16), jnp.asarray(mask, F32), jnp.asarray(upper, F32))


def _zero_pad_rows(y, row0):
    row = row0 + lax.broadcasted_iota(jnp.int32, y.shape, 0)
    is_pad = jnp.logical_and(row >= META_ROW0, row < META_ROW0 + META_PAD)
    return jnp.where(is_pad, 0.0, y)


def _first_argmax(x, lane, valid):
    xm = jnp.where(valid, x, NEG_INF)
    mx = jnp.max(xm, axis=1, keepdims=True)
    idx = jnp.min(jnp.where(jnp.logical_and(valid, xm == mx), lane, float(LANES)), axis=1, keepdims=True)
    return mx, idx


TOKEN_ROWS = D_MODEL // LANES
GATHER_PITCH = 24


def _store_token_linear(ref, x, first_token=0):
    n = x.shape[0]
    for k in range(TOKEN_ROWS):
        ref[pl.ds(first_token * TOKEN_ROWS + k, n, stride=TOKEN_ROWS), :] = x[:, k * LANES:(k + 1) * LANES]


def _load_gathered(ref, n):
    return jnp.concatenate([ref[pl.ds(k, n, stride=GATHER_PITCH), :] for k in range(TOKEN_ROWS)], axis=1)


OUT_SUB_BLOCKS = ((0, 352), (352, TM_MID))


def _out_router_kernel(yp_ref, ym_ref, yh_ref, h_ref, wo_ref, g_ref, b_ref, wrh_ref, wrl_ref, br_ref,
                       h1_ref, hlin_ref, eid_ref, wts_ref):
    projected = []
    for r0, r1 in OUT_SUB_BLOCKS:
        y = jnp.concatenate([yp_ref[r0:r1, :], ym_ref[r0:r1, :], yh_ref[r0:r1, :]], axis=1)
        projected.append(_dot(y, wo_ref[...]))
    for (r0, r1), acc in zip(OUT_SUB_BLOCKS, projected):
        _out_router_rows(r0, r1, acc, h_ref, g_ref, b_ref, wrh_ref, wrl_ref, br_ref,
                         h1_ref, hlin_ref, eid_ref, wts_ref)


def _out_router_rows(r0, r1, acc, h_ref, g_ref, b_ref, wrh_ref, wrl_ref, br_ref, h1_ref, hlin_ref, eid_ref, wts_ref):
    i = pl.program_id(0)
    h1 = _layer_norm_rows(ALPHA * h_ref[r0:r1, :] + acc, g_ref[...], b_ref[...])
    h1 = _zero_pad_rows(h1, i * TM_MID + r0)
    h1_ref[r0:r1, :] = h1
    _store_token_linear(hlin_ref, h1, r0)

    x_hi, x_lo = _split_bf16(h1)
    hi_both = _dot(x_hi, jnp.concatenate([wrh_ref[...], wrl_ref[...]], axis=1))
    logits = hi_both[:, 0:LANES] + hi_both[:, LANES:2 * LANES] + _dot(x_lo, wrh_ref[...]) + br_ref[...]
    lane = lax.broadcasted_iota(jnp.int32, logits.shape, 1).astype(F32)

    is_grp = lane < N_GROUPS
    g_max, g_idx = _first_argmax(logits, lane, is_grp)
    g_exp = jnp.where(is_grp, jnp.exp(logits - g_max), 0.0)
    p_grp = 1.0 / jnp.sum(g_exp, axis=1, keepdims=True)

    e_lo = N_GROUPS + g_idx * EXPERTS_PER_GROUP
    in_grp = jnp.logical_and(lane >= e_lo, lane < e_lo + EXPERTS_PER_GROUP)
    e_max, e1 = _first_argmax(logits, lane, in_grp)
    e_exp = jnp.where(in_grp, jnp.exp(logits - e_max), 0.0)
    p_exp = e_exp / jnp.sum(e_exp, axis=1, keepdims=True)
    p1, _ = _first_argmax(p_exp, lane, in_grp)
    rest = jnp.logical_and(in_grp, lane != e1)
    p2, e2 = _first_argmax(p_exp, lane, rest)
    psum = p1 + p2
    w1 = p_grp * p1 / psum
    w2 = p_grp * p2 / psum
    eid = jnp.where(lane == 0.0, e1 - N_GROUPS, jnp.where(lane == 1.0, e2 - N_GROUPS, 0.0))
    eid_ref[r0:r1, :] = eid.astype(jnp.int32)
    wts_ref[r0:r1, :] = jnp.where(lane == 0.0, w1, jnp.where(lane == 1.0, w2, 0.0))


def _out_router(y_pool, y_m, y_h, h, w_out, g, b, wr_hi, wr_lo, br):
    row = lambda i: (i, 0)
    const2 = lambda i: (0, 0)
    return pl.pallas_call(
        _out_router_kernel,
        grid=(LP // TM_MID,),
        in_specs=[
            pl.BlockSpec((TM_MID, D_POOL), row),
            pl.BlockSpec((TM_MID, D_MLSTM), row),
            pl.BlockSpec((TM_MID, D_HGRN), row),
            pl.BlockSpec((TM_MID, D_MODEL), row),
            pl.BlockSpec((D_MODEL, D_MODEL), const2, pipeline_mode=pl.Buffered(1)),
            pl.BlockSpec((1, D_MODEL), const2),
            pl.BlockSpec((1, D_MODEL), const2),
            pl.BlockSpec((D_MODEL, LANES), const2),
            pl.BlockSpec((D_MODEL, LANES), const2),
            pl.BlockSpec((1, LANES), const2),
        ],
        out_specs=[
            pl.BlockSpec((TM_MID, D_MODEL), row),
            pl.BlockSpec((TM_MID * TOKEN_ROWS, LANES), row),
            pl.BlockSpec((TM_MID, LANES), row),
            pl.BlockSpec((TM_MID, LANES), row),
        ],
        out_shape=[
            jax.ShapeDtypeStruct((LP, D_MODEL), F32),
            jax.ShapeDtypeStruct((LP * TOKEN_ROWS, LANES), F32),
            jax.ShapeDtypeStruct((LP, LANES), jnp.int32),
            jax.ShapeDtypeStruct((LP, LANES), F32),
        ],
        compiler_params=_cparams(("arbitrary",), VMEM_LIMIT_OUT_PROJ),
        name="out_proj_ln_router",
    )(y_pool, y_m, y_h, h, w_out, g, b, wr_hi, wr_lo, br)


ROUTE_ROWS = 256
SORTED_ROWS = N_EXPERT_TILES * TM_EXPERT
GATHER_UNROLL = 8


def _route_kernel(e_ref, dest_ref, tab_ref):
    e = e_ref[...]
    lane = lax.broadcasted_iota(jnp.int32, (ROUTE_ROWS, LANES), 1)
    kk = lax.broadcasted_iota(jnp.int32, (LANES, LANES), 0)
    ll = lax.broadcasted_iota(jnp.int32, (LANES, LANES), 1)
    before_lane = jnp.where(kk < ll, 1.0, 0.0).astype(BF16)
    all_lanes = jnp.ones((LANES, LANES), BF16)
    rr = lax.broadcasted_iota(jnp.int32, (ROUTE_ROWS, ROUTE_ROWS), 0)
    cc = lax.broadcasted_iota(jnp.int32, (ROUTE_ROWS, ROUTE_ROWS), 1)
    before_row = jnp.where(cc < rr, 1.0, 0.0).astype(BF16)

    lane1 = lane[0:1, :]
    rank = jnp.zeros((ROUTE_ROWS, LANES), F32)
    counts = jnp.zeros((1, LANES), F32)
    masks = []
    for x in range(N_EXPERTS):
        m = jnp.where(e == x, 1.0, 0.0)
        mb = m.astype(BF16)
        in_row = _dot(mb, before_lane)
        row_tot = _dot(mb, all_lanes)
        rows_before = _dot(before_row, row_tot.astype(BF16))
        rank = rank + m * (in_row + rows_before)
        total = rows_before[ROUTE_ROWS - 1:ROUTE_ROWS, :] + row_tot[ROUTE_ROWS - 1:ROUTE_ROWS, :]
        counts = counts + jnp.where(lane1 == x, total, 0.0)
        masks.append(m)

    padded = jnp.floor((counts + (TM_EXPERT - 1)) * (1.0 / TM_EXPERT)) * TM_EXPERT
    p_hi, p_lo = _split_bf16(padded)
    start = _dot(p_hi, before_lane) + _dot(p_lo, before_lane)
    end = start + padded

    dest = rank
    tile0 = (lane1 * TM_EXPERT).astype(F32)
    n_before = jnp.zeros((1, LANES), F32)
    for x in range(N_EXPERTS):
        dest = dest + masks[x] * start[:, x:x + 1]
        n_before = n_before + jnp.where(end[:, x:x + 1] <= tile0, 1.0, 0.0)
    dest_ref[...] = dest.astype(jnp.int32)

    tile_expert = jnp.minimum(n_before, float(N_EXPERTS - 1))
    n_tiles = end[:, N_EXPERTS - 1:N_EXPERTS] * (1.0 / TM_EXPERT)
    rows_left = jnp.zeros((1, LANES), F32)
    for x in range(N_EXPERTS):
        rows_left = rows_left + jnp.where(tile_expert == x, counts[:, x:x + 1] + start[:, x:x + 1] - tile0, 0.0)
    tile_rows = jnp.clip(rows_left, 0.0, float(TM_EXPERT))
    row = lax.broadcasted_iota(jnp.int32, (8, LANES), 0)
    tab = jnp.where(row == 0, tile_expert, jnp.where(row == 1, n_tiles, jnp.where(row == 2, tile_rows, 0.0)))
    tab_ref[...] = tab.astype(jnp.int32)


def _route(e_grid):
    return pl.pallas_call(
        _route_kernel,
        out_shape=[
            jax.ShapeDtypeStruct((ROUTE_ROWS, LANES), jnp.int32),
            jax.ShapeDtypeStruct((8, LANES), jnp.int32),
        ],
        compiler_params=pltpu.CompilerParams(vmem_limit_bytes=VMEM_LIMIT),
        name="moe_route",
    )(e_grid)


def _invert_kernel(dest_ref, tok_ref):
    def zero(p, carry):
        tok_ref[p] = 0
        return carry
    lax.fori_loop(0, SORTED_ROWS, zero, 0, unroll=GATHER_UNROLL)

    def place_row(r, carry):
        first_token = r * (LANES // 2)
        for c in range(LANES):
            tok_ref[dest_ref[r, c]] = first_token + c // 2
        return carry
    lax.fori_loop(0, N_ASSIGN // LANES, place_row, 0)


def _invert(dest):
    used_rows = -(-N_ASSIGN // (8 * LANES)) * 8
    return pl.pallas_call(
        _invert_kernel,
        grid=(1,),
        in_specs=[pl.BlockSpec((used_rows, LANES), lambda i: (0, 0), memory_space=pltpu.SMEM)],
        out_specs=pl.BlockSpec((SORTED_ROWS,), lambda i: (0,), memory_space=pltpu.SMEM),
        out_shape=jax.ShapeDtypeStruct((SORTED_ROWS,), jnp.int32),
        name="moe_invert",
    )(dest)


GATHER_PRIORITY = 0
WEIGHT_PRIORITY = 1


def _expert_kernel(layer, tile_expert, tile_rows, n_tiles, tok_cur_ref, tok_next_ref, h_hbm, w1_hbm, w3_hbm, w2_hbm,
                   y_ref, xbuf, w1f, w3f, w2f, w1b, w3b, w2b, gsem, wsem, wslot_ref):
    i = pl.program_id(0)
    n_used = n_tiles[0]
    slot = i % 2

    def weight_copies(expert, ws):
        return [pltpu.make_async_copy(src.at[layer, expert], dst.at[ws], wsem.at[ws])
                for src, dst in ((w1_hbm, w1f), (w3_hbm, w3f), (w2_hbm, w2f))]

    def n_groups(tile):
        return lax.shift_right_logical(tile_rows[tile] + (GATHER_UNROLL - 1), GATHER_UNROLL.bit_length() - 1)

    def start_gather(tok_ref, s, groups):
        def body(g, carry):
            for u in range(GATHER_UNROLL):
                r = g * GATHER_UNROLL + u
                src = h_hbm.at[pl.ds(pl.multiple_of(tok_ref[0, 0, r] * TOKEN_ROWS, TOKEN_ROWS), TOKEN_ROWS)]
                dst = xbuf.at[s, pl.ds(pl.multiple_of(r * GATHER_PITCH, 8), TOKEN_ROWS)]
                pltpu.make_async_copy(src, dst, gsem.at[s]).start(priority=GATHER_PRIORITY)
            return carry
        lax.fori_loop(0, groups, body, 0)

    def wait_gather(s, groups):
        n = GATHER_UNROLL * TOKEN_ROWS

        def body(g, carry):
            pltpu.make_async_copy(h_hbm.at[pl.ds(0, n)], xbuf.at[s, pl.ds(0, n)], gsem.at[s]).wait()
            return carry
        lax.fori_loop(0, groups, body, 0)

    last_tile = N_EXPERT_TILES - 1

    @pl.when(i == 0)
    def _():
        xbuf[...] = jnp.zeros(xbuf.shape, F32)
        start_gather(tok_cur_ref, 0, n_groups(0))
        wslot_ref[0] = 0
        for cp in weight_copies(tile_expert[0], 0):
            cp.start(priority=WEIGHT_PRIORITY)

    @pl.when(i < n_used)
    def _():
        wait_gather(slot, n_groups(i))
        start_gather(tok_next_ref, 1 - slot, n_groups(jnp.minimum(i + 1, last_tile)))
        expert = tile_expert[i]
        first_of_expert = jnp.logical_or(i == 0, expert != tile_expert[jnp.maximum(i - 1, 0)])

        @pl.when(first_of_expert)
        def _():
            ws = wslot_ref[0]
            for cp in weight_copies(expert, ws):
                cp.wait()
            nxt = lax.while_loop(
                lambda j: jnp.logical_and(j < n_used, tile_expert[jnp.minimum(j, last_tile)] == expert),
                lambda j: j + 1, i + 1)

            @pl.when(nxt < n_used)
            def _():
                for cp in weight_copies(tile_expert[jnp.minimum(nxt, last_tile)], 1 - ws):
                    cp.start(priority=WEIGHT_PRIORITY)

            w1b[...] = w1f[ws].astype(BF16)
            w3b[...] = w3f[ws].astype(BF16)
            w2b[...] = w2f[ws].astype(BF16)
            wslot_ref[0] = 1 - ws

        x = _load_gathered(xbuf.at[slot], TM_EXPERT).astype(BF16)
        a = _dot(x, w1b[...])
        g = _dot(x, w3b[...])
        he = (a * _sigmoid(a) * g).astype(BF16)
        _store_token_linear(y_ref, _dot(he, w2b[...]))

    @pl.when(i >= n_used)
    def _():
        y_ref[...] = jnp.zeros(y_ref.shape, F32)


def _experts(h1, w1, w3, w2, layer, tile_expert, tile_rows, n_tiles, row_token):
    def cur(i, te, tr, nt):
        return (i, 0, 0)

    def nxt(i, te, tr, nt):
        return (jnp.minimum(i + 1, N_EXPERT_TILES - 1), 0, 0)

    grid_spec = pltpu.PrefetchScalarGridSpec(
        num_scalar_prefetch=3,
        grid=(N_EXPERT_TILES,),
        in_specs=[
            pl.BlockSpec((1, 1, TM_EXPERT), cur, memory_space=pltpu.SMEM),
            pl.BlockSpec((1, 1, TM_EXPERT), nxt, memory_space=pltpu.SMEM),
            pl.BlockSpec(memory_space=pl.ANY),
            pl.BlockSpec(memory_space=pl.ANY),
            pl.BlockSpec(memory_space=pl.ANY),
            pl.BlockSpec(memory_space=pl.ANY),
        ],
        out_specs=pl.BlockSpec((TM_EXPERT * TOKEN_ROWS, LANES), lambda i, te, tr, nt: (i, 0)),
        scratch_shapes=[
            pltpu.VMEM((2, TM_EXPERT * GATHER_PITCH, LANES), F32),
            pltpu.VMEM((2, D_MODEL, D_EXPERT), F32),
            pltpu.VMEM((2, D_MODEL, D_EXPERT), F32),
            pltpu.VMEM((2, D_EXPERT, D_MODEL), F32),
            pltpu.VMEM((D_MODEL, D_EXPERT), BF16),
            pltpu.VMEM((D_MODEL, D_EXPERT), BF16),
            pltpu.VMEM((D_EXPERT, D_MODEL), BF16),
            pltpu.SemaphoreType.DMA((2,)),
            pltpu.SemaphoreType.DMA((2,)),
            pltpu.SMEM((1,), jnp.int32),
        ],
    )
    return pl.pallas_call(
        functools.partial(_expert_kernel, layer),
        grid_spec=grid_spec,
        out_shape=jax.ShapeDtypeStruct((SORTED_ROWS * TOKEN_ROWS, LANES), F32),
        compiler_params=_cparams(("arbitrary",)),
        name="moe_experts",
    )(tile_expert, tile_rows, n_tiles, row_token, row_token, h1, w1, w3, w2)


def _combine_kernel(tm, n_steps, final, pos_cur_ref, pos_next_ref, h1_ref, wts_ref, g_ref, b_ref, y_hbm,
                    *rest):
    out_refs, (ybuf, sem) = rest[:-2], rest[-2:]
    i = pl.program_id(0)
    slot = i % 2

    def start_gather(pos_ref, s):
        def body(r, carry):
            for j in range(2):
                row0 = pl.multiple_of(pos_ref[0, 0, 2 * r + j] * TOKEN_ROWS, TOKEN_ROWS)
                dst = ybuf.at[s, j, pl.ds(pl.multiple_of(r * GATHER_PITCH, 8), TOKEN_ROWS)]
                pltpu.make_async_copy(y_hbm.at[pl.ds(row0, TOKEN_ROWS)], dst, sem.at[s]).start()
            return carry
        lax.fori_loop(0, tm, body, 0, unroll=GATHER_UNROLL // 2)

    @pl.when(i == 0)
    def _():
        start_gather(pos_cur_ref, 0)

    for j in range(2):
        n = tm * TOKEN_ROWS
        pltpu.make_async_copy(y_hbm.at[pl.ds(0, n)], ybuf.at[slot, j, pl.ds(0, n)], sem.at[slot]).wait()

    @pl.when(i + 1 < n_steps)
    def _():
        start_gather(pos_next_ref, 1 - slot)

    wts = wts_ref[...]
    y = wts[:, 0:1] * _load_gathered(ybuf.at[slot, 0], tm) + wts[:, 1:2] * _load_gathered(ybuf.at[slot, 1], tm)
    h2 = _layer_norm_rows(ALPHA * h1_ref[...] + y, g_ref[...], b_ref[...])
    if final:
        out_refs[0][...] = h2
    else:
        h2 = _zero_pad_rows(h2, i * tm)
        out_refs[0][...] = h2
        out_refs[1][...] = h2.astype(BF16)


def _combine(h1, y_sorted, pos, wts, g, b, final):
    tm = TM_FINAL if final else TM_LN
    rows = SEQ if final else LP
    n_steps = rows // tm
    pos3 = pos[:2 * rows].reshape(n_steps, 1, 2 * tm)
    row = lambda i: (i, 0)
    const2 = lambda i: (0, 0)
    if final:
        out_specs = [pl.BlockSpec((tm, D_MODEL), row)]
        out_shape = [jax.ShapeDtypeStruct((rows, D_MODEL), F32)]
    else:
        out_specs = [pl.BlockSpec((tm, D_MODEL), row), pl.BlockSpec((tm, D_MODEL), row)]
        out_shape = [jax.ShapeDtypeStruct((rows, D_MODEL), F32), jax.ShapeDtypeStruct((rows, D_MODEL), BF16)]
    return pl.pallas_call(
        functools.partial(_combine_kernel, tm, n_steps, final),
        grid=(n_steps,),
        in_specs=[
            pl.BlockSpec((1, 1, 2 * tm), lambda i: (i, 0, 0), memory_space=pltpu.SMEM),
            pl.BlockSpec((1, 1, 2 * tm), lambda i: (jnp.minimum(i + 1, n_steps - 1), 0, 0), memory_space=pltpu.SMEM),
            pl.BlockSpec((tm, D_MODEL), row),
            pl.BlockSpec((tm, LANES), row),
            pl.BlockSpec((1, D_MODEL), const2),
            pl.BlockSpec((1, D_MODEL), const2),
            pl.BlockSpec(memory_space=pl.ANY),
        ],
        out_specs=out_specs,
        out_shape=out_shape,
        scratch_shapes=[pltpu.VMEM((2, 2, tm * GATHER_PITCH, LANES), F32), pltpu.SemaphoreType.DMA((2,))],
        compiler_params=_cparams(("arbitrary",)),
        name="moe_combine_ln_final" if final else "moe_combine_ln",
    )(pos3, pos3, h1, wts, g, b, y_sorted)


def _pad_lanes(v):
    return jnp.pad(v, ((0, 0), (0, LANES - v.shape[1])))


def kernel(x, meta_tokens, emb_ln_g, emb_ln_b, hgrn_lb_logits, w_in, conv_w, conv_b, ig_b, fg_b,
           mlstm_norm_g, pool_w, pool_scale, hgrn_norm_g, w_out, ln1_g, ln1_b,
           w_router_group, b_router_group, w_router_expert, b_router_expert, w1, w3, w2,
           ln2_g, ln2_b):
    assert x.shape == (1, SEQ, D_MODEL) and x.dtype == F32
    row2 = lambda v: v.reshape(1, -1)
    meta_blk = jnp.pad(meta_tokens.astype(F32), ((META_PAD, 0), (0, 0)))
    h, hb = _embed(x.reshape(SEQ, D_MODEL), meta_blk, row2(emb_ln_g), row2(emb_ln_b))

    out = None
    for l in range(DEPTH):
        p = _in_proj(hb, w_in, l)
        y_pool = _pool(p, pool_w[l], row2(pool_scale[l]))
        gate_bias = _pad_lanes(jnp.concatenate([ig_b[l], fg_b[l]]).reshape(1, -1))
        y_m = _mlstm(p, conv_w[l], row2(conv_b[l]), gate_bias, row2(mlstm_norm_g[l]))
        y_h = _hgrn(p, hgrn_lb_logits, row2(hgrn_norm_g[l]), l)

        w_r = _pad_lanes(jnp.concatenate([w_router_group[l], w_router_expert[l]], axis=1))
        wr_hi, wr_lo = _split_bf16(w_r)
        b_r = _pad_lanes(jnp.concatenate([b_router_group[l], b_router_expert[l]]).reshape(1, -1))
        h1, h1_lin, eid, wts = _out_router(y_pool, y_m, y_h, h, w_out[l].astype(BF16), row2(ln1_g[l]),
                                           row2(ln1_b[l]), wr_hi, wr_lo, b_r)

        e_flat = jnp.pad(eid[:, 0:2].reshape(-1), (0, ROUTE_ROWS * LANES - N_ASSIGN), constant_values=-1)
        dest, tab = _route(e_flat.reshape(ROUTE_ROWS, LANES))
        row_token = _invert(dest).reshape(N_EXPERT_TILES, 1, TM_EXPERT)
        y_sorted = _experts(h1_lin, w1, w3, w2, l, tab[0, :N_EXPERT_TILES], tab[2, :N_EXPERT_TILES], tab[1, :1],
                            row_token)
        pos = dest.reshape(-1)
        if l + 1 < DEPTH:
            h, hb = _combine(h1, y_sorted, pos, wts, row2(ln2_g[l]), row2(ln2_b[l]), final=False)
        else:
            (out,) = _combine(h1, y_sorted, pos, wts, row2(ln2_g[l]), row2(ln2_b[l]), final=True)
    return out.reshape(1, SEQ, D_MODEL)
```

```python
import functools

import jax
import jax.numpy as jnp
import numpy as np
from jax import lax
from jax.experimental import pallas as pl
from jax.experimental.pallas import tpu as pltpu

F32 = jnp.float32
BF16 = jnp.bfloat16

D_MODEL = 2048
SEQ = 8192
DEPTH = 2
N_META = 16
CHUNK = 64
D_POOL = D_MODEL // 4
POOL_WINDOWS = (2, 4, 8, 16)
POOL_GROUP = D_POOL // len(POOL_WINDOWS)
D_MLSTM = 3 * D_MODEL // 8
N_HEADS = 6
HEAD_DIM = D_MLSTM // N_HEADS
CONV_K = 4
D_HGRN = D_MODEL - D_POOL - D_MLSTM
N_GROUPS = 4
EXPERTS_PER_GROUP = 8
N_EXPERTS = N_GROUPS * EXPERTS_PER_GROUP
D_EXPERT = D_MODEL // 4
ALPHA = (2 * DEPTH) ** 0.25
LN_EPS = 1e-5
NEG_INF = float("-inf")

LANES = 128
LP = SEQ + CHUNK
META_ROW0 = SEQ
META_PAD = CHUNK - N_META
N_CHUNKS = LP // CHUNK
LP_EMBED = 17 * 512

COL_MQK = 0
COL_MV = 2 * D_MLSTM
COL_MO = COL_MV + D_MLSTM
COL_HQ = COL_MO + D_MLSTM
COL_HF = COL_HQ + D_HGRN
COL_HI = COL_HF + D_HGRN
COL_HG = COL_HI + D_HGRN
COL_POOL = COL_HG + D_HGRN
COL_GATE = COL_POOL + D_POOL
P_COLS = COL_GATE + 2 * LANES

TM_BIG = 2752
TN_IN = 256
TM_MID = 688
TM_LN = 192
TM_FINAL = 256
TM_EXPERT = 256
N_ASSIGN = 2 * LP
N_EXPERT_TILES = (N_ASSIGN + N_EXPERTS * (TM_EXPERT - 1)) // TM_EXPERT + 1
VMEM_LIMIT = 56 * 1024 * 1024
VMEM_LIMIT_OUT_PROJ = 62 * 1024 * 1024


def _cparams(sem, vmem_limit=VMEM_LIMIT):
    return pltpu.CompilerParams(dimension_semantics=sem, vmem_limit_bytes=vmem_limit)


def _sigmoid(x):
    return 1.0 / (1.0 + jnp.exp(-x))


def _log_sigmoid(x):
    return jnp.minimum(x, 0.0) - jnp.log1p(jnp.exp(-jnp.abs(x)))


def _layer_norm_rows(x, g, b):
    mu = jnp.mean(x, axis=-1, keepdims=True)
    xc = x - mu
    var = jnp.mean(xc * xc, axis=-1, keepdims=True)
    return xc * lax.rsqrt(var + LN_EPS) * g + b


def _dot(a, b):
    return jnp.dot(a, b, preferred_element_type=F32)


def _dot_nt(a, b):
    return lax.dot_general(a, b, (((1,), (1,)), ((), ())), preferred_element_type=F32)


def _dot_tn(a, b):
    return lax.dot_general(a, b, (((0,), (0,)), ((), ())), preferred_element_type=F32)


def _split_bf16(x):
    hi = x.astype(BF16)
    lo = (x - hi.astype(F32)).astype(BF16)
    return hi, lo


def _chunk_cumsum(x):
    r = lax.broadcasted_iota(jnp.int32, (CHUNK, CHUNK), 0)
    c = lax.broadcasted_iota(jnp.int32, (CHUNK, CHUNK), 1)
    tri = jnp.where(r >= c, 1.0, 0.0).astype(BF16)
    hi, lo = _split_bf16(x)
    return _dot(tri, hi) + _dot(tri, lo)


def _embed_kernel(x_ref, meta_ref, g_ref, b_ref, h_ref, hb_ref):
    i = pl.program_id(0)
    g = g_ref[...]
    b = b_ref[...]

    @pl.when(i < SEQ // 512)
    def _():
        y = _layer_norm_rows(x_ref[...], g, b)
        h_ref[...] = y
        hb_ref[...] = y.astype(BF16)

    @pl.when(i == SEQ // 512)
    def _():
        y = _layer_norm_rows(meta_ref[...], g, b)
        row = lax.broadcasted_iota(jnp.int32, (CHUNK, D_MODEL), 0)
        y = jnp.where(row >= META_PAD, y, 0.0)
        h_ref[0:CHUNK, :] = y
        hb_ref[0:CHUNK, :] = y.astype(BF16)
        h_ref[CHUNK:512, :] = jnp.zeros((512 - CHUNK, D_MODEL), F32)
        hb_ref[CHUNK:512, :] = jnp.zeros((512 - CHUNK, D_MODEL), BF16)


def _embed(x2d, meta_blk, g, b):
    nx = SEQ // 512
    return pl.pallas_call(
        _embed_kernel,
        grid=(nx + 1,),
        in_specs=[
            pl.BlockSpec((512, D_MODEL), lambda i: (jnp.minimum(i, nx - 1), 0)),
            pl.BlockSpec((CHUNK, D_MODEL), lambda i: (0, 0)),
            pl.BlockSpec((1, D_MODEL), lambda i: (0, 0)),
            pl.BlockSpec((1, D_MODEL), lambda i: (0, 0)),
        ],
        out_specs=[
            pl.BlockSpec((512, D_MODEL), lambda i: (i, 0)),
            pl.BlockSpec((512, D_MODEL), lambda i: (i, 0)),
        ],
        out_shape=[
            jax.ShapeDtypeStruct((LP_EMBED, D_MODEL), F32),
            jax.ShapeDtypeStruct((LP_EMBED, D_MODEL), BF16),
        ],
        compiler_params=_cparams(("arbitrary",)),
        name="embed_ln",
    )(x2d, meta_blk, g, b)


K_TILES = D_MODEL // LANES
W_IN_MLSTM0 = D_POOL
W_IN_GATE0 = D_POOL + 4 * D_MLSTM
W_IN_HGRN0 = W_IN_GATE0 + 2 * N_HEADS
D_IN = W_IN_HGRN0 + 4 * D_HGRN


def _in_proj_kernel(layer, x_ref, w_ref, o_ref):
    w_t = jnp.concatenate([w_ref[:, DEPTH * kt + layer, :] for kt in range(K_TILES)], axis=1)
    o_ref[...] = _dot_nt(x_ref[...], w_t.astype(BF16))


def _in_proj(hb, w_in, layer):
    w_view = w_in.reshape(DEPTH, K_TILES, LANES, D_IN).transpose(3, 1, 0, 2).reshape(D_IN, K_TILES * DEPTH, LANES)

    def first_col(j):
        c = j * TN_IN
        return jnp.where(c < COL_HQ, W_IN_MLSTM0 + c,
                         jnp.where(c < COL_POOL, W_IN_HGRN0 + (c - COL_HQ),
                                   jnp.where(c < COL_GATE, c - COL_POOL, W_IN_GATE0)))

    w_block = (pl.Element(TN_IN), pl.Element(K_TILES * DEPTH), pl.Element(LANES))
    return pl.pallas_call(
        functools.partial(_in_proj_kernel, layer),
        grid=(LP // TM_BIG, P_COLS // TN_IN),
        in_specs=[
            pl.BlockSpec((TM_BIG, D_MODEL), lambda i, j: (i, 0)),
            pl.BlockSpec(w_block, lambda i, j: (first_col(j), 0, 0)),
        ],
        out_specs=pl.BlockSpec((TM_BIG, TN_IN), lambda i, j: (i, j)),
        out_shape=jax.ShapeDtypeStruct((LP, P_COLS), F32),
        compiler_params=_cparams(("arbitrary", "arbitrary")),
        name="in_proj",
    )(hb, w_view)


POOL_HALO = 16


def _pool_kernel(u_ref, halo_ref, pw_ref, ps_ref, y_ref, ubuf):
    i = pl.program_id(0)
    u = u_ref[...]
    ubuf[0:POOL_HALO, :] = halo_ref[...]
    ubuf[POOL_HALO:POOL_HALO + TM_MID, :] = u
    row = i * TM_MID + lax.broadcasted_iota(jnp.int32, (TM_MID, POOL_GROUP), 0)
    pos = jnp.where(row >= META_ROW0 + META_PAD, row - (META_ROW0 + META_PAD) + 1, 2 * POOL_HALO)
    scale = ps_ref[...]
    for gi, w in enumerate(POOL_WINDOWS):
        cs = slice(gi * POOL_GROUP, (gi + 1) * POOL_GROUP)
        ug = u[:, cs]
        acc = ug
        for j in range(1, w):
            acc = acc + ubuf[POOL_HALO - j:POOL_HALO - j + TM_MID, cs]
        div = jnp.minimum(pos, w).astype(F32)
        d = acc / div - ug
        yg = _dot(d.astype(BF16), pw_ref[gi].astype(BF16))
        y_ref[:, cs] = (yg * scale[:, cs]).astype(BF16)


def _pool(p, pool_w, pool_scale):
    blocks_per_tile = TM_MID // POOL_HALO
    last_meta_block = (LP - POOL_HALO) // POOL_HALO

    def halo_map(i):
        return (jnp.where(i == 0, last_meta_block, i * blocks_per_tile - 1), COL_POOL // D_POOL)

    return pl.pallas_call(
        _pool_kernel,
        grid=(LP // TM_MID,),
        in_specs=[
            pl.BlockSpec((TM_MID, D_POOL), lambda i: (i, COL_POOL // D_POOL)),
            pl.BlockSpec((POOL_HALO, D_POOL), halo_map),
            pl.BlockSpec((len(POOL_WINDOWS), POOL_GROUP, POOL_GROUP), lambda i: (0, 0, 0)),
            pl.BlockSpec((1, D_POOL), lambda i: (0, 0)),
        ],
        out_specs=pl.BlockSpec((TM_MID, D_POOL), lambda i: (i, 0)),
        out_shape=jax.ShapeDtypeStruct((LP, D_POOL), BF16),
        scratch_shapes=[pltpu.VMEM((POOL_HALO + TM_MID, D_POOL), F32)],
        compiler_params=_cparams(("arbitrary",)),
        name="pool_mixer",
    )(p, p, pool_w, pool_scale)


MLSTM_CHUNKS_PER_STEP = 8
HGRN_CHUNKS_PER_STEP = 8


def _step_block(c, chunks_per_step):
    return jnp.where(c == 0, SEQ // (chunks_per_step * CHUNK), c - 1)


CONV_HALO = 8


def _mlstm_kernel(mqk_ref, mv_ref, mo_ref, gt_ref, cw_ref, cb_ref, gb_ref, ng_ref, y_ref,
                  s_sc, m_sc, xbuf):
    c = pl.program_id(0)

    @pl.when(c == 0)
    def _():
        s_sc[...] = jnp.zeros(s_sc.shape, F32)
        m_sc[...] = jnp.full(m_sc.shape, NEG_INF, F32)
        xbuf[0:CONV_HALO, :] = jnp.zeros((CONV_HALO, 2 * D_MLSTM), F32)
        _mlstm_chunk(0, True, mqk_ref, mv_ref, mo_ref, gt_ref, cw_ref, cb_ref, gb_ref, ng_ref, y_ref,
                     s_sc, m_sc, xbuf)

    @pl.when(c > 0)
    def _():
        for g in range(MLSTM_CHUNKS_PER_STEP):
            _mlstm_chunk(g, False, mqk_ref, mv_ref, mo_ref, gt_ref, cw_ref, cb_ref, gb_ref, ng_ref, y_ref,
                         s_sc, m_sc, xbuf)


def _mlstm_chunk(g, is_meta, mqk_ref, mv_ref, mo_ref, gt_ref, cw_ref, cb_ref, gb_ref, ng_ref, y_ref,
                 s_sc, m_sc, xbuf):
    rows = pl.ds(g * CHUNK, CHUNK)
    x = mqk_ref[rows, :]
    xbuf[CONV_HALO:CONV_HALO + CHUNK, :] = x
    cw = cw_ref[...]
    conv = cb_ref[...] + cw[CONV_K - 1:CONV_K, :] * x
    for j in range(CONV_K - 1):
        off = CONV_HALO - (CONV_K - 1) + j
        conv = conv + cw[j:j + 1, :] * xbuf[off:off + CHUNK, :]
    xbuf[0:CONV_HALO, :] = x[CHUNK - CONV_HALO:CHUNK, :]
    qk = conv * _sigmoid(conv)
    v_all = mv_ref[rows, :]
    og_all = _sigmoid(mo_ref[rows, :])
    ng = ng_ref[...]

    z = gt_ref[rows, :] + gb_ref[...]
    if is_meta:
        valid = lax.broadcasted_iota(jnp.int32, (CHUNK, LANES), 0) >= META_PAD
        ig = jnp.where(valid, z, NEG_INF)
        lf = jnp.where(valid, _log_sigmoid(z), 0.0)
    else:
        ig = z
        lf = _log_sigmoid(z)
    g_t = pltpu.roll(_chunk_cumsum(lf), LANES - N_HEADS, axis=1)
    a = ig - g_t
    row = lax.broadcasted_iota(jnp.int32, (CHUNK, LANES), 0)
    a_max = a
    shift = 1
    while shift < CHUNK:
        a_max = jnp.maximum(a_max, jnp.where(row >= shift, pltpu.roll(a_max, shift, axis=0), NEG_INF))
        shift *= 2
    m_prev = m_sc[0:1, :]
    m_t = g_t + jnp.maximum(a_max, m_prev)
    m_ts = jnp.where(m_t == NEG_INF, 0.0, m_t)
    c_t = g_t - m_ts
    inter_all = jnp.exp(g_t + m_prev - m_ts)
    floor_all = jnp.exp(-m_ts)
    g_last = g_t[CHUNK - 1:CHUNK, :]
    m_new = g_last + jnp.maximum(a_max[CHUNK - 1:CHUNK, :], m_prev)
    decay_all = jnp.exp(g_last + m_prev - m_new)
    wexp_all = jnp.exp(g_last + a - m_new)
    m_sc[0:1, :] = m_new
    a_rows = a.T

    r64 = lax.broadcasted_iota(jnp.int32, (CHUNK, CHUNK), 0)
    c64 = lax.broadcasted_iota(jnp.int32, (CHUNK, CHUNK), 1)
    causal = r64 >= c64
    k_scale = HEAD_DIM ** -0.5
    ones_cols = jnp.ones((CHUNK, HEAD_DIM), BF16)
    mean_cols = jnp.full((HEAD_DIM, HEAD_DIM), 1.0 / HEAD_DIM, BF16)

    def row_mean(x):
        hi, lo = _split_bf16(x)
        return _dot(hi, mean_cols) + _dot(lo, mean_cols)

    heads = range(N_HEADS)
    hsl = [slice(h * HEAD_DIM, (h + 1) * HEAD_DIM) for h in heads]
    qb = [qk[:, hsl[h]].astype(BF16) for h in heads]
    k = [qk[:, D_MLSTM + h * HEAD_DIM:D_MLSTM + (h + 1) * HEAD_DIM] * k_scale for h in heads]
    v_aug = [jnp.concatenate([v_all[:, hsl[h]].astype(BF16), ones_cols], axis=1) for h in heads]
    state = [s_sc[h] for h in heads]

    s = []
    for h in heads:
        dexp = jnp.exp(jnp.where(causal, c_t[:, h:h + 1] + a_rows[h:h + 1, :], NEG_INF))
        s.append((_dot_nt(qb[h], k[h].astype(BF16)) * dexp).astype(BF16))
    hh = []
    for h in heads:
        nd = _dot(s[h], v_aug[h]) + inter_all[:, h:h + 1] * _dot(qb[h], state[h].astype(BF16))
        den = nd[:, HEAD_DIM:2 * HEAD_DIM]
        hh.append(nd[:, 0:HEAD_DIM] / jnp.maximum(jnp.abs(den), floor_all[:, h:h + 1]))
    for h in heads:
        wk = (k[h] * wexp_all[:, h:h + 1]).astype(BF16)
        s_sc[h] = decay_all[:, h:h + 1] * state[h] + _dot_tn(wk, v_aug[h])
    hc = [hh[h] - row_mean(hh[h]) for h in heads]
    var = [row_mean(hc[h] * hc[h]) for h in heads]
    for h in heads:
        y = hc[h] * lax.rsqrt(var[h] + LN_EPS) * ng[:, hsl[h]] * og_all[:, hsl[h]]
        y_ref[rows, hsl[h]] = y.astype(BF16)


def _mlstm(p, conv_w, conv_b, gate_bias, norm_g):
    n = MLSTM_CHUNKS_PER_STEP
    step_rows = n * CHUNK

    def col(block_w, start):
        return lambda c: (_step_block(c, n), start // block_w)

    const2 = lambda c: (0, 0)
    return pl.pallas_call(
        _mlstm_kernel,
        grid=(SEQ // step_rows + 1,),
        in_specs=[
            pl.BlockSpec((step_rows, 2 * D_MLSTM), col(2 * D_MLSTM, COL_MQK)),
            pl.BlockSpec((step_rows, D_MLSTM), col(D_MLSTM, COL_MV)),
            pl.BlockSpec((step_rows, D_MLSTM), col(D_MLSTM, COL_MO)),
            pl.BlockSpec((step_rows, LANES), col(LANES, COL_GATE)),
            pl.BlockSpec((CONV_K, 2 * D_MLSTM), const2),
            pl.BlockSpec((1, 2 * D_MLSTM), const2),
            pl.BlockSpec((1, LANES), const2),
            pl.BlockSpec((1, D_MLSTM), const2),
        ],
        out_specs=pl.BlockSpec((step_rows, D_MLSTM), lambda c: (_step_block(c, n), 0)),
        out_shape=jax.ShapeDtypeStruct((LP, D_MLSTM), BF16),
        scratch_shapes=[
            pltpu.VMEM((N_HEADS, HEAD_DIM, 2 * HEAD_DIM), F32),
            pltpu.VMEM((8, LANES), F32),
            pltpu.VMEM((CONV_HALO + CHUNK, 2 * D_MLSTM), F32),
        ],
        compiler_params=_cparams(("arbitrary",)),
        name="mlstm_mixer",
    )(p, p, p, p, conv_w, conv_b, gate_bias, norm_g)


N_LEVELS = 6


def _hgrn_tables():
    t = np.arange(CHUNK)
    sel = np.zeros((N_LEVELS * CHUNK, CHUNK), np.float32)
    mask = np.zeros((N_LEVELS + 1, CHUNK, CHUNK), np.float32)
    upper = np.zeros((CHUNK, LANES), np.float32)
    for l in range(N_LEVELS):
        half = 1 << l
        ref_row = (t // (2 * half)) * (2 * half) + half - 1
        sel[l * CHUNK + t, ref_row] = 1.0
        is_upper = (t // half) % 2 == 1
        upper[:, l] = is_upper
        same = (t[:, None] // (2 * half)) == (t[None, :] // (2 * half))
        mask[l] = same & is_upper[:, None] & ~is_upper[None, :]
    mask[N_LEVELS] = np.eye(CHUNK)
    return sel, mask, upper


def _hgrn_kernel(layer, hq_ref, hf_ref, hi_ref, hg_ref, lbl_ref, ng_ref, sel_ref, mask_ref, up_ref,
                 y_ref, st_sc):
    c = pl.program_id(0)

    lbl = lbl_ref[...]
    e = jnp.exp(lbl - jnp.max(lbl, axis=0, keepdims=True))
    sm = e / jnp.sum(e, axis=0, keepdims=True)
    lb = jnp.sum(sm[0:layer + 1, :], axis=0, keepdims=True) - sm[0:1, :]
    refs = (hq_ref, hf_ref, hi_ref, hg_ref, ng_ref, sel_ref, mask_ref, up_ref)

    @pl.when(c == 0)
    def _():
        st_sc[...] = jnp.zeros(st_sc.shape, F32)
        _hgrn_state(0, _hgrn_intra(0, lb, *refs), y_ref, st_sc)

    @pl.when(c > 0)
    def _():
        for g in range(HGRN_CHUNKS_PER_STEP):
            _hgrn_state(g, _hgrn_intra(g, lb, *refs), y_ref, st_sc)


def _hgrn_intra(g, lb, hq_ref, hf_ref, hi_ref, hg_ref, ng_ref, sel_ref, mask_ref, up_ref):
    rows = pl.ds(g * CHUNK, CHUNK)
    z = hf_ref[rows, :]
    a = jnp.log(lb)
    bb = jnp.log1p(-lb) + _log_sigmoid(z)
    mx = jnp.maximum(a, bb)
    log_f = mx + jnp.log(jnp.exp(a - mx) + jnp.exp(bb - mx))
    kk = (1.0 - lb) * _sigmoid(-z)
    hq = hq_ref[rows, :]
    q = hq * _sigmoid(hq)
    v = hi_ref[rows, :].astype(BF16)
    hg = hg_ref[rows, :]
    gate = hg * _sigmoid(hg) * ng_ref[...]

    b = _chunk_cumsum(log_f)
    b_hi, b_lo = _split_bf16(b)
    sel = sel_ref[...]
    refs = _dot(sel, b_hi) + _dot(sel, b_lo)
    up = up_ref[...]

    amats = [None] * N_HEADS
    for l in range(N_LEVELS + 1):
        if l < N_LEVELS:
            ref_l = refs[l * CHUNK:(l + 1) * CHUNK, :]
            sign = 2.0 * up[:, l:l + 1] - 1.0
            zl = jnp.exp(sign * (b - ref_l))
            ql = (q * zl).astype(BF16)
            kl = (kk * zl).astype(BF16)
        else:
            ql = q.astype(BF16)
            kl = kk.astype(BF16)
        ml = mask_ref[l]
        for h in range(N_HEADS):
            hs = slice(h * HEAD_DIM, (h + 1) * HEAD_DIM)
            part = ml * _dot_nt(ql[:, hs], kl[:, hs])
            amats[h] = part if amats[h] is None else amats[h] + part

    b_last = b[CHUNK - 1:CHUNK, :]
    qe = (q * jnp.exp(b)).astype(BF16)
    kd = (kk * jnp.exp(b_last - b)).astype(BF16)
    e_last = jnp.exp(b_last)
    return [a_h.astype(BF16) for a_h in amats], v, qe, kd, e_last, gate


def _hgrn_state(g, intra, y_ref, st_sc):
    amats, v, qe, kd, e_last, gate = intra
    rows = pl.ds(g * CHUNK, CHUNK)
    for h in range(N_HEADS):
        hs = slice(h * HEAD_DIM, (h + 1) * HEAD_DIM)
        st = st_sc[h]
        o = _dot(amats[h], v[:, hs]) + _dot_nt(qe[:, hs], st.astype(BF16))
        st_sc[h] = e_last[:, hs] * st + _dot_tn(v[:, hs], kd[:, hs])
        o = o * lax.rsqrt(jnp.mean(o * o, axis=1, keepdims=True) + LN_EPS)
        y_ref[rows, hs] = (o * gate[:, hs]).astype(BF16)


def _hgrn(p, lb_logits, norm_g, layer):
    n = HGRN_CHUNKS_PER_STEP
    step_rows = n * CHUNK

    def col(start):
        return lambda c: (_step_block(c, n), start // D_HGRN)

    sel, mask, upper = _hgrn_tables()
    const2 = lambda c: (0, 0)
    return pl.pallas_call(
        functools.partial(_hgrn_kernel, layer),
        grid=(SEQ // step_rows + 1,),
        in_specs=[
            pl.BlockSpec((step_rows, D_HGRN), col(COL_HQ)),
            pl.BlockSpec((step_rows, D_HGRN), col(COL_HF)),
            pl.BlockSpec((step_rows, D_HGRN), col(COL_HI)),
            pl.BlockSpec((step_rows, D_HGRN), col(COL_HG)),
            pl.BlockSpec((DEPTH, D_HGRN), const2),
            pl.BlockSpec((1, D_HGRN), const2),
            pl.BlockSpec((N_LEVELS * CHUNK, CHUNK), const2),
            pl.BlockSpec((N_LEVELS + 1, CHUNK, CHUNK), lambda c: (0, 0, 0)),
            pl.BlockSpec((CHUNK, LANES), const2),
        ],
        out_specs=pl.BlockSpec((step_rows, D_HGRN), lambda c: (_step_block(c, n), 0)),
        out_shape=jax.ShapeDtypeStruct((LP, D_HGRN), BF16),
        scratch_shapes=[pltpu.VMEM((N_HEADS, HEAD_DIM, HEAD_DIM), F32)],
        compiler_params=_cparams(("arbitrary",)),
        name="hgrn_mixer",
    )(p, p, p, p, lb_logits, norm_g, jnp.asarray(sel, BF16), jnp.asarray(mask, F32), jnp.asarray(upper, F32))


def _zero_pad_rows(y, row0):
    row = row0 + lax.broadcasted_iota(jnp.int32, y.shape, 0)
    is_pad = jnp.logical_and(row >= META_ROW0, row < META_ROW0 + META_PAD)
    return jnp.where(is_pad, 0.0, y)


def _first_argmax(x, lane, valid):
    xm = jnp.where(valid, x, NEG_INF)
    mx = jnp.max(xm, axis=1, keepdims=True)
    idx = jnp.min(jnp.where(jnp.logical_and(valid, xm == mx), lane, float(LANES)), axis=1, keepdims=True)
    return mx, idx


TOKEN_ROWS = D_MODEL // LANES
GATHER_PITCH = 24


def _store_token_linear(ref, x, first_token=0):
    n = x.shape[0]
    for k in range(TOKEN_ROWS):
        ref[pl.ds(first_token * TOKEN_ROWS + k, n, stride=TOKEN_ROWS), :] = x[:, k * LANES:(k + 1) * LANES]


def _load_gathered(ref, n):
    return jnp.concatenate([ref[pl.ds(k, n, stride=GATHER_PITCH), :] for k in range(TOKEN_ROWS)], axis=1)


OUT_SUB_BLOCKS = ((0, 352), (352, TM_MID))


def _out_router_kernel(yp_ref, ym_ref, yh_ref, h_ref, wo_ref, g_ref, b_ref, wrh_ref, wrl_ref, br_ref,
                       h1_ref, hlin_ref, eid_ref, wts_ref):
    def project(r0, r1):
        y = jnp.concatenate([yp_ref[r0:r1, :], ym_ref[r0:r1, :], yh_ref[r0:r1, :]], axis=1)
        return _dot(y, wo_ref[...])

    acc = project(*OUT_SUB_BLOCKS[0])
    for n, (r0, r1) in enumerate(OUT_SUB_BLOCKS):
        acc_next = project(*OUT_SUB_BLOCKS[n + 1]) if n + 1 < len(OUT_SUB_BLOCKS) else None
        _out_router_rows(r0, r1, acc, h_ref, g_ref, b_ref, wrh_ref, wrl_ref, br_ref,
                         h1_ref, hlin_ref, eid_ref, wts_ref)
        acc = acc_next


def _out_router_rows(r0, r1, acc, h_ref, g_ref, b_ref, wrh_ref, wrl_ref, br_ref, h1_ref, hlin_ref, eid_ref, wts_ref):
    i = pl.program_id(0)
    h1 = _layer_norm_rows(ALPHA * h_ref[r0:r1, :] + acc, g_ref[...], b_ref[...])
    h1 = _zero_pad_rows(h1, i * TM_MID + r0)
    h1_ref[r0:r1, :] = h1
    _store_token_linear(hlin_ref, h1, r0)

    x_hi, x_lo = _split_bf16(h1)
    hi_both = _dot(x_hi, jnp.concatenate([wrh_ref[...], wrl_ref[...]], axis=1))
    logits = hi_both[:, 0:LANES] + hi_both[:, LANES:2 * LANES] + _dot(x_lo, wrh_ref[...]) + br_ref[...]
    lane = lax.broadcasted_iota(jnp.int32, logits.shape, 1).astype(F32)

    is_grp = lane < N_GROUPS
    g_max, g_idx = _first_argmax(logits, lane, is_grp)
    g_exp = jnp.where(is_grp, jnp.exp(logits - g_max), 0.0)
    p_grp = 1.0 / jnp.sum(g_exp, axis=1, keepdims=True)

    e_lo = N_GROUPS + g_idx * EXPERTS_PER_GROUP
    in_grp = jnp.logical_and(lane >= e_lo, lane < e_lo + EXPERTS_PER_GROUP)
    e_max, e1 = _first_argmax(logits, lane, in_grp)
    e_exp = jnp.where(in_grp, jnp.exp(logits - e_max), 0.0)
    p_exp = e_exp / jnp.sum(e_exp, axis=1, keepdims=True)
    p1, _ = _first_argmax(p_exp, lane, in_grp)
    rest = jnp.logical_and(in_grp, lane != e1)
    p2, e2 = _first_argmax(p_exp, lane, rest)
    psum = p1 + p2
    w1 = p_grp * p1 / psum
    w2 = p_grp * p2 / psum
    eid = jnp.where(lane == 0.0, e1 - N_GROUPS, jnp.where(lane == 1.0, e2 - N_GROUPS, 0.0))
    eid_ref[r0:r1, :] = eid.astype(jnp.int32)
    wts_ref[r0:r1, :] = jnp.where(lane == 0.0, w1, jnp.where(lane == 1.0, w2, 0.0))


def _out_router(y_pool, y_m, y_h, h, w_out, g, b, wr_hi, wr_lo, br):
    row = lambda i: (i, 0)
    const2 = lambda i: (0, 0)
    return pl.pallas_call(
        _out_router_kernel,
        grid=(LP // TM_MID,),
        in_specs=[
            pl.BlockSpec((TM_MID, D_POOL), row),
            pl.BlockSpec((TM_MID, D_MLSTM), row),
            pl.BlockSpec((TM_MID, D_HGRN), row),
            pl.BlockSpec((TM_MID, D_MODEL), row),
            pl.BlockSpec((D_MODEL, D_MODEL), const2, pipeline_mode=pl.Buffered(1)),
            pl.BlockSpec((1, D_MODEL), const2),
            pl.BlockSpec((1, D_MODEL), const2),
            pl.BlockSpec((D_MODEL, LANES), const2),
            pl.BlockSpec((D_MODEL, LANES), const2),
            pl.BlockSpec((1, LANES), const2),
        ],
        out_specs=[
            pl.BlockSpec((TM_MID, D_MODEL), row),
            pl.BlockSpec((TM_MID * TOKEN_ROWS, LANES), row),
            pl.BlockSpec((TM_MID, LANES), row),
            pl.BlockSpec((TM_MID, LANES), row),
        ],
        out_shape=[
            jax.ShapeDtypeStruct((LP, D_MODEL), F32),
            jax.ShapeDtypeStruct((LP * TOKEN_ROWS, LANES), F32),
            jax.ShapeDtypeStruct((LP, LANES), jnp.int32),
            jax.ShapeDtypeStruct((LP, LANES), F32),
        ],
        compiler_params=_cparams(("arbitrary",), VMEM_LIMIT_OUT_PROJ),
        name="out_proj_ln_router",
    )(y_pool, y_m, y_h, h, w_out, g, b, wr_hi, wr_lo, br)


ROUTE_ROWS = 256
SORTED_ROWS = N_EXPERT_TILES * TM_EXPERT
GATHER_UNROLL = 8


def _route_kernel(e_ref, dest_ref, tab_ref):
    e = e_ref[...]
    lane = lax.broadcasted_iota(jnp.int32, (ROUTE_ROWS, LANES), 1)
    kk = lax.broadcasted_iota(jnp.int32, (LANES, LANES), 0)
    ll = lax.broadcasted_iota(jnp.int32, (LANES, LANES), 1)
    before_lane = jnp.where(kk < ll, 1.0, 0.0).astype(BF16)
    all_lanes = jnp.ones((LANES, LANES), BF16)
    rr = lax.broadcasted_iota(jnp.int32, (ROUTE_ROWS, ROUTE_ROWS), 0)
    cc = lax.broadcasted_iota(jnp.int32, (ROUTE_ROWS, ROUTE_ROWS), 1)
    before_row = jnp.where(cc < rr, 1.0, 0.0).astype(BF16)

    lane1 = lane[0:1, :]
    rank = jnp.zeros((ROUTE_ROWS, LANES), F32)
    counts = jnp.zeros((1, LANES), F32)
    masks = []
    for x in range(N_EXPERTS):
        m = jnp.where(e == x, 1.0, 0.0)
        mb = m.astype(BF16)
        in_row = _dot(mb, before_lane)
        row_tot = _dot(mb, all_lanes)
        rows_before = _dot(before_row, row_tot.astype(BF16))
        rank = rank + m * (in_row + rows_before)
        total = rows_before[ROUTE_ROWS - 1:ROUTE_ROWS, :] + row_tot[ROUTE_ROWS - 1:ROUTE_ROWS, :]
        counts = counts + jnp.where(lane1 == x, total, 0.0)
        masks.append(m)

    padded = jnp.floor((counts + (TM_EXPERT - 1)) * (1.0 / TM_EXPERT)) * TM_EXPERT
    p_hi, p_lo = _split_bf16(padded)
    start = _dot(p_hi, before_lane) + _dot(p_lo, before_lane)
    end = start + padded

    dest = rank
    tile0 = (lane1 * TM_EXPERT).astype(F32)
    n_before = jnp.zeros((1, LANES), F32)
    for x in range(N_EXPERTS):
        dest = dest + masks[x] * start[:, x:x + 1]
        n_before = n_before + jnp.where(end[:, x:x + 1] <= tile0, 1.0, 0.0)
    dest_ref[...] = dest.astype(jnp.int32)

    tile_expert = jnp.minimum(n_before, float(N_EXPERTS - 1))
    n_tiles = end[:, N_EXPERTS - 1:N_EXPERTS] * (1.0 / TM_EXPERT)
    rows_left = jnp.zeros((1, LANES), F32)
    for x in range(N_EXPERTS):
        rows_left = rows_left + jnp.where(tile_expert == x, counts[:, x:x + 1] + start[:, x:x + 1] - tile0, 0.0)
    tile_rows = jnp.clip(rows_left, 0.0, float(TM_EXPERT))
    row = lax.broadcasted_iota(jnp.int32, (8, LANES), 0)
    tab = jnp.where(row == 0, tile_expert, jnp.where(row == 1, n_tiles, jnp.where(row == 2, tile_rows, 0.0)))
    tab_ref[...] = tab.astype(jnp.int32)


def _route(e_grid):
    return pl.pallas_call(
        _route_kernel,
        out_shape=[
            jax.ShapeDtypeStruct((ROUTE_ROWS, LANES), jnp.int32),
            jax.ShapeDtypeStruct((8, LANES), jnp.int32),
        ],
        compiler_params=pltpu.CompilerParams(vmem_limit_bytes=VMEM_LIMIT),
        name="moe_route",
    )(e_grid)


def _invert_kernel(dest_ref, tok_ref):
    def zero(p, carry):
        tok_ref[p] = 0
        return carry
    lax.fori_loop(0, SORTED_ROWS, zero, 0, unroll=GATHER_UNROLL)

    def place_row(r, carry):
        first_token = r * (LANES // 2)
        for c in range(LANES):
            tok_ref[dest_ref[r, c]] = first_token + c // 2
        return carry
    lax.fori_loop(0, N_ASSIGN // LANES, place_row, 0)


def _invert(dest):
    used_rows = -(-N_ASSIGN // (8 * LANES)) * 8
    return pl.pallas_call(
        _invert_kernel,
        grid=(1,),
        in_specs=[pl.BlockSpec((used_rows, LANES), lambda i: (0, 0), memory_space=pltpu.SMEM)],
        out_specs=pl.BlockSpec((SORTED_ROWS,), lambda i: (0,), memory_space=pltpu.SMEM),
        out_shape=jax.ShapeDtypeStruct((SORTED_ROWS,), jnp.int32),
        name="moe_invert",
    )(dest)


GATHER_PRIORITY = 0
WEIGHT_PRIORITY = 1


def _expert_kernel(layer, tile_expert, tile_rows, n_tiles, tok_cur_ref, tok_next_ref, h_hbm, w1_hbm, w3_hbm, w2_hbm,
                   y_ref, xbuf, w1f, w3f, w2f, w1b, w3b, w2b, gsem, wsem, wslot_ref):
    i = pl.program_id(0)
    n_used = n_tiles[0]
    slot = i % 2

    def weight_copies(expert, ws):
        return [pltpu.make_async_copy(src.at[layer, expert], dst.at[ws], wsem.at[ws])
                for src, dst in ((w1_hbm, w1f), (w3_hbm, w3f), (w2_hbm, w2f))]

    def n_groups(tile):
        return lax.shift_right_logical(tile_rows[tile] + (GATHER_UNROLL - 1), GATHER_UNROLL.bit_length() - 1)

    def start_gather(tok_ref, s, groups):
        def body(g, carry):
            for u in range(GATHER_UNROLL):
                r = g * GATHER_UNROLL + u
                src = h_hbm.at[pl.ds(pl.multiple_of(tok_ref[0, 0, r] * TOKEN_ROWS, TOKEN_ROWS), TOKEN_ROWS)]
                dst = xbuf.at[s, pl.ds(pl.multiple_of(r * GATHER_PITCH, 8), TOKEN_ROWS)]
                pltpu.make_async_copy(src, dst, gsem.at[s]).start(priority=GATHER_PRIORITY)
            return carry
        lax.fori_loop(0, groups, body, 0)

    def wait_gather(s, groups):
        n = GATHER_UNROLL * TOKEN_ROWS

        def body(g, carry):
            pltpu.make_async_copy(h_hbm.at[pl.ds(0, n)], xbuf.at[s, pl.ds(0, n)], gsem.at[s]).wait()
            return carry
        lax.fori_loop(0, groups, body, 0)

    last_tile = N_EXPERT_TILES - 1

    @pl.when(i == 0)
    def _():
        xbuf[...] = jnp.zeros(xbuf.shape, F32)
        start_gather(tok_cur_ref, 0, n_groups(0))
        wslot_ref[0] = 0
        for cp in weight_copies(tile_expert[0], 0):
            cp.start(priority=WEIGHT_PRIORITY)

    @pl.when(i < n_used)
    def _():
        wait_gather(slot, n_groups(i))
        start_gather(tok_next_ref, 1 - slot, n_groups(jnp.minimum(i + 1, last_tile)))
        expert = tile_expert[i]
        first_of_expert = jnp.logical_or(i == 0, expert != tile_expert[jnp.maximum(i - 1, 0)])

        @pl.when(first_of_expert)
        def _():
            ws = wslot_ref[0]
            for cp in weight_copies(expert, ws):
                cp.wait()
            nxt = lax.while_loop(
                lambda j: jnp.logical_and(j < n_used, tile_expert[jnp.minimum(j, last_tile)] == expert),
                lambda j: j + 1, i + 1)

            @pl.when(nxt < n_used)
            def _():
                for cp in weight_copies(tile_expert[jnp.minimum(nxt, last_tile)], 1 - ws):
                    cp.start(priority=WEIGHT_PRIORITY)

            w1b[...] = w1f[ws].astype(BF16)
            w3b[...] = w3f[ws].astype(BF16)
            w2b[...] = w2f[ws].astype(BF16)
            wslot_ref[0] = 1 - ws

        x = _load_gathered(xbuf.at[slot], TM_EXPERT).astype(BF16)
        a = _dot(x, w1b[...])
        g = _dot(x, w3b[...])
        he = (a * _sigmoid(a) * g).astype(BF16)
        _store_token_linear(y_ref, _dot(he, w2b[...]))

    @pl.when(i >= n_used)
    def _():
        y_ref[...] = jnp.zeros(y_ref.shape, F32)


def _experts(h1, w1, w3, w2, layer, tile_expert, tile_rows, n_tiles, row_token):
    def cur(i, te, tr, nt):
        return (i, 0, 0)

    def nxt(i, te, tr, nt):
        return (jnp.minimum(i + 1, N_EXPERT_TILES - 1), 0, 0)

    grid_spec = pltpu.PrefetchScalarGridSpec(
        num_scalar_prefetch=3,
        grid=(N_EXPERT_TILES,),
        in_specs=[
            pl.BlockSpec((1, 1, TM_EXPERT), cur, memory_space=pltpu.SMEM),
            pl.BlockSpec((1, 1, TM_EXPERT), nxt, memory_space=pltpu.SMEM),
            pl.BlockSpec(memory_space=pl.ANY),
            pl.BlockSpec(memory_space=pl.ANY),
            pl.BlockSpec(memory_space=pl.ANY),
            pl.BlockSpec(memory_space=pl.ANY),
        ],
        out_specs=pl.BlockSpec((TM_EXPERT * TOKEN_ROWS, LANES), lambda i, te, tr, nt: (i, 0)),
        scratch_shapes=[
            pltpu.VMEM((2, TM_EXPERT * GATHER_PITCH, LANES), F32),
            pltpu.VMEM((2, D_MODEL, D_EXPERT), F32),
            pltpu.VMEM((2, D_MODEL, D_EXPERT), F32),
            pltpu.VMEM((2, D_EXPERT, D_MODEL), F32),
            pltpu.VMEM((D_MODEL, D_EXPERT), BF16),
            pltpu.VMEM((D_MODEL, D_EXPERT), BF16),
            pltpu.VMEM((D_EXPERT, D_MODEL), BF16),
            pltpu.SemaphoreType.DMA((2,)),
            pltpu.SemaphoreType.DMA((2,)),
            pltpu.SMEM((1,), jnp.int32),
        ],
    )
    return pl.pallas_call(
        functools.partial(_expert_kernel, layer),
        grid_spec=grid_spec,
        out_shape=jax.ShapeDtypeStruct((SORTED_ROWS * TOKEN_ROWS, LANES), F32),
        compiler_params=_cparams(("arbitrary",)),
        name="moe_experts",
    )(tile_expert, tile_rows, n_tiles, row_token, row_token, h1, w1, w3, w2)


def _combine_kernel(tm, n_steps, final, pos_cur_ref, pos_next_ref, h1_ref, wts_ref, g_ref, b_ref, y_hbm,
                    *rest):
    out_refs, (ybuf, sem) = rest[:-2], rest[-2:]
    i = pl.program_id(0)
    slot = i % 2

    def start_gather(pos_ref, s):
        def body(r, carry):
            for j in range(2):
                row0 = pl.multiple_of(pos_ref[0, 0, 2 * r + j] * TOKEN_ROWS, TOKEN_ROWS)
                dst = ybuf.at[s, j, pl.ds(pl.multiple_of(r * GATHER_PITCH, 8), TOKEN_ROWS)]
                pltpu.make_async_copy(y_hbm.at[pl.ds(row0, TOKEN_ROWS)], dst, sem.at[s]).start()
            return carry
        lax.fori_loop(0, tm, body, 0, unroll=GATHER_UNROLL // 2)

    @pl.when(i == 0)
    def _():
        start_gather(pos_cur_ref, 0)

    for j in range(2):
        n = tm * TOKEN_ROWS
        pltpu.make_async_copy(y_hbm.at[pl.ds(0, n)], ybuf.at[slot, j, pl.ds(0, n)], sem.at[slot]).wait()

    @pl.when(i + 1 < n_steps)
    def _():
        start_gather(pos_next_ref, 1 - slot)

    wts = wts_ref[...]
    y = wts[:, 0:1] * _load_gathered(ybuf.at[slot, 0], tm) + wts[:, 1:2] * _load_gathered(ybuf.at[slot, 1], tm)
    h2 = _layer_norm_rows(ALPHA * h1_ref[...] + y, g_ref[...], b_ref[...])
    if final:
        out_refs[0][...] = h2
    else:
        h2 = _zero_pad_rows(h2, i * tm)
        out_refs[0][...] = h2
        out_refs[1][...] = h2.astype(BF16)


def _combine(h1, y_sorted, pos, wts, g, b, final):
    tm = TM_FINAL if final else TM_LN
    rows = SEQ if final else LP
    n_steps = rows // tm
    pos3 = pos[:2 * rows].reshape(n_steps, 1, 2 * tm)
    row = lambda i: (i, 0)
    const2 = lambda i: (0, 0)
    if final:
        out_specs = [pl.BlockSpec((tm, D_MODEL), row)]
        out_shape = [jax.ShapeDtypeStruct((rows, D_MODEL), F32)]
    else:
        out_specs = [pl.BlockSpec((tm, D_MODEL), row), pl.BlockSpec((tm, D_MODEL), row)]
        out_shape = [jax.ShapeDtypeStruct((rows, D_MODEL), F32), jax.ShapeDtypeStruct((rows, D_MODEL), BF16)]
    return pl.pallas_call(
        functools.partial(_combine_kernel, tm, n_steps, final),
        grid=(n_steps,),
        in_specs=[
            pl.BlockSpec((1, 1, 2 * tm), lambda i: (i, 0, 0), memory_space=pltpu.SMEM),
            pl.BlockSpec((1, 1, 2 * tm), lambda i: (jnp.minimum(i + 1, n_steps - 1), 0, 0), memory_space=pltpu.SMEM),
            pl.BlockSpec((tm, D_MODEL), row),
            pl.BlockSpec((tm, LANES), row),
            pl.BlockSpec((1, D_MODEL), const2),
            pl.BlockSpec((1, D_MODEL), const2),
            pl.BlockSpec(memory_space=pl.ANY),
        ],
        out_specs=out_specs,
        out_shape=out_shape,
        scratch_shapes=[pltpu.VMEM((2, 2, tm * GATHER_PITCH, LANES), F32), pltpu.SemaphoreType.DMA((2,))],
        compiler_params=_cparams(("arbitrary",)),
        name="moe_combine_ln_final" if final else "moe_combine_ln",
    )(pos3, pos3, h1, wts, g, b, y_sorted)


def _pad_lanes(v):
    return jnp.pad(v, ((0, 0), (0, LANES - v.shape[1])))


def kernel(x, meta_tokens, emb_ln_g, emb_ln_b, hgrn_lb_logits, w_in, conv_w, conv_b, ig_b, fg_b,
           mlstm_norm_g, pool_w, pool_scale, hgrn_norm_g, w_out, ln1_g, ln1_b,
           w_router_group, b_router_group, w_router_expert, b_router_expert, w1, w3, w2,
           ln2_g, ln2_b):
    assert x.shape == (1, SEQ, D_MODEL) and x.dtype == F32
    row2 = lambda v: v.reshape(1, -1)
    meta_blk = jnp.pad(meta_tokens.astype(F32), ((META_PAD, 0), (0, 0)))
    h, hb = _embed(x.reshape(SEQ, D_MODEL), meta_blk, row2(emb_ln_g), row2(emb_ln_b))

    out = None
    for l in range(DEPTH):
        p = _in_proj(hb, w_in, l)
        y_pool = _pool(p, pool_w[l], row2(pool_scale[l]))
        gate_bias = _pad_lanes(jnp.concatenate([ig_b[l], fg_b[l]]).reshape(1, -1))
        y_m = _mlstm(p, conv_w[l], row2(conv_b[l]), gate_bias, row2(mlstm_norm_g[l]))
        y_h = _hgrn(p, hgrn_lb_logits, row2(hgrn_norm_g[l]), l)

        w_r = _pad_lanes(jnp.concatenate([w_router_group[l], w_router_expert[l]], axis=1))
        wr_hi, wr_lo = _split_bf16(w_r)
        b_r = _pad_lanes(jnp.concatenate([b_router_group[l], b_router_expert[l]]).reshape(1, -1))
        h1, h1_lin, eid, wts = _out_router(y_pool, y_m, y_h, h, w_out[l].astype(BF16), row2(ln1_g[l]),
                                           row2(ln1_b[l]), wr_hi, wr_lo, b_r)

        e_flat = jnp.pad(eid[:, 0:2].reshape(-1), (0, ROUTE_ROWS * LANES - N_ASSIGN), constant_values=-1)
        dest, tab = _route(e_flat.reshape(ROUTE_ROWS, LANES))
        row_token = _invert(dest).reshape(N_EXPERT_TILES, 1, TM_EXPERT)
        y_sorted = _experts(h1_lin, w1, w3, w2, l, tab[0, :N_EXPERT_TILES], tab[2, :N_EXPERT_TILES], tab[1, :1],
                            row_token)
        pos = dest.reshape(-1)
        if l + 1 < DEPTH:
            h, hb = _combine(h1, y_sorted, pos, wts, row2(ln2_g[l]), row2(ln2_b[l]), final=False)
        else:
            (out,) = _combine(h1, y_sorted, pos, wts, row2(ln2_g[l]), row2(ln2_b[l]), final=True)
    return out.reshape(1, SEQ, D_MODEL)
```

```python
import functools

import jax
import jax.numpy as jnp
import numpy as np
from jax import lax
from jax.experimental import pallas as pl
from jax.experimental.pallas import tpu as pltpu

F32 = jnp.float32
BF16 = jnp.bfloat16

D_MODEL = 2048
SEQ = 8192
DEPTH = 2
N_META = 16
CHUNK = 64
D_POOL = D_MODEL // 4
POOL_WINDOWS = (2, 4, 8, 16)
POOL_GROUP = D_POOL // len(POOL_WINDOWS)
D_MLSTM = 3 * D_MODEL // 8
N_HEADS = 6
HEAD_DIM = D_MLSTM // N_HEADS
CONV_K = 4
D_HGRN = D_MODEL - D_POOL - D_MLSTM
N_GROUPS = 4
EXPERTS_PER_GROUP = 8
N_EXPERTS = N_GROUPS * EXPERTS_PER_GROUP
D_EXPERT = D_MODEL // 4
ALPHA = (2 * DEPTH) ** 0.25
LN_EPS = 1e-5
NEG_INF = float("-inf")

LANES = 128
V7X_VMEM_BYTES = 64 * 1024 * 1024
LP = SEQ + CHUNK
META_ROW0 = SEQ
META_PAD = CHUNK - N_META
TM_EMBED = 512
LP_EMBED = (SEQ // TM_EMBED + 1) * TM_EMBED

COL_MQK = 0
COL_MV = 2 * D_MLSTM
COL_MO = COL_MV + D_MLSTM
COL_HQ = COL_MO + D_MLSTM
COL_HF = COL_HQ + D_HGRN
COL_HI = COL_HF + D_HGRN
COL_HG = COL_HI + D_HGRN
COL_POOL = COL_HG + D_HGRN
COL_GATE = COL_POOL + D_POOL
P_COLS = COL_GATE + 2 * LANES

TM_BIG = 2752
TN_IN = 256
TM_MID = 688
TM_LN = 192
TM_FINAL = 256
TM_EXPERT = 256
N_ASSIGN = 2 * LP
N_EXPERT_TILES = (N_ASSIGN + N_EXPERTS * (TM_EXPERT - 1)) // TM_EXPERT + 1
VMEM_LIMIT = V7X_VMEM_BYTES * 7 // 8
VMEM_LIMIT_OUT_PROJ = V7X_VMEM_BYTES * 31 // 32


def _cparams(sem, vmem_limit=VMEM_LIMIT):
    return pltpu.CompilerParams(dimension_semantics=sem, vmem_limit_bytes=vmem_limit)


def _sigmoid(x):
    return 1.0 / (1.0 + jnp.exp(-x))


def _log_sigmoid(x):
    return jnp.minimum(x, 0.0) - jnp.log1p(jnp.exp(-jnp.abs(x)))


def _layer_norm_rows(x, g, b):
    mu = jnp.mean(x, axis=-1, keepdims=True)
    xc = x - mu
    var = jnp.mean(xc * xc, axis=-1, keepdims=True)
    return xc * lax.rsqrt(var + LN_EPS) * g + b


def _dot(a, b):
    return jnp.dot(a, b, preferred_element_type=F32)


def _dot_nt(a, b):
    return lax.dot_general(a, b, (((1,), (1,)), ((), ())), preferred_element_type=F32)


def _dot_tn(a, b):
    return lax.dot_general(a, b, (((0,), (0,)), ((), ())), preferred_element_type=F32)


def _split_bf16(x):
    hi = x.astype(BF16)
    lo = (x - hi.astype(F32)).astype(BF16)
    return hi, lo


def _chunk_cumsum(x):
    r = lax.broadcasted_iota(jnp.int32, (CHUNK, CHUNK), 0)
    c = lax.broadcasted_iota(jnp.int32, (CHUNK, CHUNK), 1)
    tri = jnp.where(r >= c, 1.0, 0.0).astype(BF16)
    hi, lo = _split_bf16(x)
    return _dot(tri, hi) + _dot(tri, lo)


def _embed_kernel(x_ref, meta_ref, g_ref, b_ref, h_ref, hb_ref):
    i = pl.program_id(0)
    g = g_ref[...]
    b = b_ref[...]

    @pl.when(i < SEQ // TM_EMBED)
    def _():
        y = _layer_norm_rows(x_ref[...], g, b)
        h_ref[...] = y
        hb_ref[...] = y.astype(BF16)

    @pl.when(i == SEQ // TM_EMBED)
    def _():
        y = _layer_norm_rows(meta_ref[...], g, b)
        row = lax.broadcasted_iota(jnp.int32, (CHUNK, D_MODEL), 0)
        y = jnp.where(row >= META_PAD, y, 0.0)
        h_ref[0:CHUNK, :] = y
        hb_ref[0:CHUNK, :] = y.astype(BF16)
        h_ref[CHUNK:TM_EMBED, :] = jnp.zeros((TM_EMBED - CHUNK, D_MODEL), F32)
        hb_ref[CHUNK:TM_EMBED, :] = jnp.zeros((TM_EMBED - CHUNK, D_MODEL), BF16)


def _embed(x2d, meta_blk, g, b):
    nx = SEQ // TM_EMBED
    return pl.pallas_call(
        _embed_kernel,
        grid=(nx + 1,),
        in_specs=[
            pl.BlockSpec((TM_EMBED, D_MODEL), lambda i: (jnp.minimum(i, nx - 1), 0)),
            pl.BlockSpec((CHUNK, D_MODEL), lambda i: (0, 0)),
            pl.BlockSpec((1, D_MODEL), lambda i: (0, 0)),
            pl.BlockSpec((1, D_MODEL), lambda i: (0, 0)),
        ],
        out_specs=[
            pl.BlockSpec((TM_EMBED, D_MODEL), lambda i: (i, 0)),
            pl.BlockSpec((TM_EMBED, D_MODEL), lambda i: (i, 0)),
        ],
        out_shape=[
            jax.ShapeDtypeStruct((LP_EMBED, D_MODEL), F32),
            jax.ShapeDtypeStruct((LP_EMBED, D_MODEL), BF16),
        ],
        compiler_params=_cparams(("arbitrary",)),
        name="embed_ln",
    )(x2d, meta_blk, g, b)


K_TILES = D_MODEL // LANES
W_IN_MLSTM0 = D_POOL
W_IN_GATE0 = D_POOL + 4 * D_MLSTM
W_IN_HGRN0 = W_IN_GATE0 + 2 * N_HEADS
D_IN = W_IN_HGRN0 + 4 * D_HGRN


def _in_proj_kernel(layer, x_ref, w_ref, o_ref):
    w_t = jnp.concatenate([w_ref[:, DEPTH * kt + layer, :] for kt in range(K_TILES)], axis=1)
    o_ref[...] = _dot_nt(x_ref[...], w_t.astype(BF16))


def _in_proj(hb, w_in, layer):
    w_view = w_in.reshape(DEPTH, K_TILES, LANES, D_IN).transpose(3, 1, 0, 2).reshape(D_IN, K_TILES * DEPTH, LANES)

    def first_col(j):
        c = j * TN_IN
        return jnp.where(c < COL_HQ, W_IN_MLSTM0 + c,
                         jnp.where(c < COL_POOL, W_IN_HGRN0 + (c - COL_HQ),
                                   jnp.where(c < COL_GATE, c - COL_POOL, W_IN_GATE0)))

    w_block = (pl.Element(TN_IN), pl.Element(K_TILES * DEPTH), pl.Element(LANES))
    return pl.pallas_call(
        functools.partial(_in_proj_kernel, layer),
        grid=(LP // TM_BIG, P_COLS // TN_IN),
        in_specs=[
            pl.BlockSpec((TM_BIG, D_MODEL), lambda i, j: (i, 0)),
            pl.BlockSpec(w_block, lambda i, j: (first_col(j), 0, 0)),
        ],
        out_specs=pl.BlockSpec((TM_BIG, TN_IN), lambda i, j: (i, j)),
        out_shape=jax.ShapeDtypeStruct((LP, P_COLS), F32),
        compiler_params=_cparams(("arbitrary", "arbitrary")),
        name="in_proj",
    )(hb, w_view)


POOL_HALO = 16


def _pool_kernel(u_ref, halo_ref, pw_ref, ps_ref, y_ref, ubuf):
    i = pl.program_id(0)
    u = u_ref[...]
    ubuf[0:POOL_HALO, :] = halo_ref[...]
    ubuf[POOL_HALO:POOL_HALO + TM_MID, :] = u
    row = i * TM_MID + lax.broadcasted_iota(jnp.int32, (TM_MID, POOL_GROUP), 0)
    pos = jnp.where(row >= META_ROW0 + META_PAD, row - (META_ROW0 + META_PAD) + 1, 2 * POOL_HALO)
    scale = ps_ref[...]
    for gi, w in enumerate(POOL_WINDOWS):
        cs = slice(gi * POOL_GROUP, (gi + 1) * POOL_GROUP)
        ug = u[:, cs]
        acc = ug
        for j in range(1, w):
            acc = acc + ubuf[POOL_HALO - j:POOL_HALO - j + TM_MID, cs]
        div = jnp.minimum(pos, w).astype(F32)
        d = acc / div - ug
        yg = _dot(d.astype(BF16), pw_ref[gi].astype(BF16))
        y_ref[:, cs] = (yg * scale[:, cs]).astype(BF16)


def _pool(p, pool_w, pool_scale):
    blocks_per_tile = TM_MID // POOL_HALO
    last_meta_block = (LP - POOL_HALO) // POOL_HALO

    def halo_map(i):
        return (jnp.where(i == 0, last_meta_block, i * blocks_per_tile - 1), COL_POOL // D_POOL)

    return pl.pallas_call(
        _pool_kernel,
        grid=(LP // TM_MID,),
        in_specs=[
            pl.BlockSpec((TM_MID, D_POOL), lambda i: (i, COL_POOL // D_POOL)),
            pl.BlockSpec((POOL_HALO, D_POOL), halo_map),
            pl.BlockSpec((len(POOL_WINDOWS), POOL_GROUP, POOL_GROUP), lambda i: (0, 0, 0)),
            pl.BlockSpec((1, D_POOL), lambda i: (0, 0)),
        ],
        out_specs=pl.BlockSpec((TM_MID, D_POOL), lambda i: (i, 0)),
        out_shape=jax.ShapeDtypeStruct((LP, D_POOL), BF16),
        scratch_shapes=[pltpu.VMEM((POOL_HALO + TM_MID, D_POOL), F32)],
        compiler_params=_cparams(("arbitrary",)),
        name="pool_mixer",
    )(p, p, pool_w, pool_scale)


MLSTM_CHUNKS_PER_STEP = 8
HGRN_CHUNKS_PER_STEP = 8


def _step_block(c, chunks_per_step):
    return jnp.where(c == 0, SEQ // (chunks_per_step * CHUNK), c - 1)


CONV_HALO = 8


def _mlstm_kernel(mqk_ref, mv_ref, mo_ref, gt_ref, cw_ref, cb_ref, gb_ref, ng_ref, y_ref,
                  s_sc, m_sc, xbuf):
    c = pl.program_id(0)

    @pl.when(c == 0)
    def _():
        s_sc[...] = jnp.zeros(s_sc.shape, F32)
        m_sc[...] = jnp.full(m_sc.shape, NEG_INF, F32)
        xbuf[0:CONV_HALO, :] = jnp.zeros((CONV_HALO, 2 * D_MLSTM), F32)
        _mlstm_chunk(0, True, mqk_ref, mv_ref, mo_ref, gt_ref, cw_ref, cb_ref, gb_ref, ng_ref, y_ref,
                     s_sc, m_sc, xbuf)

    @pl.when(c > 0)
    def _():
        for g in range(MLSTM_CHUNKS_PER_STEP):
            _mlstm_chunk(g, False, mqk_ref, mv_ref, mo_ref, gt_ref, cw_ref, cb_ref, gb_ref, ng_ref, y_ref,
                         s_sc, m_sc, xbuf)


def _mlstm_chunk(g, is_meta, mqk_ref, mv_ref, mo_ref, gt_ref, cw_ref, cb_ref, gb_ref, ng_ref, y_ref,
                 s_sc, m_sc, xbuf):
    rows = pl.ds(g * CHUNK, CHUNK)
    x = mqk_ref[rows, :]
    xbuf[CONV_HALO:CONV_HALO + CHUNK, :] = x
    cw = cw_ref[...]
    conv = cb_ref[...] + cw[CONV_K - 1:CONV_K, :] * x
    for j in range(CONV_K - 1):
        off = CONV_HALO - (CONV_K - 1) + j
        conv = conv + cw[j:j + 1, :] * xbuf[off:off + CHUNK, :]
    xbuf[0:CONV_HALO, :] = x[CHUNK - CONV_HALO:CHUNK, :]
    qk = conv * _sigmoid(conv)
    v_all = mv_ref[rows, :]
    og_all = _sigmoid(mo_ref[rows, :])
    ng = ng_ref[...]

    z = gt_ref[rows, :] + gb_ref[...]
    if is_meta:
        valid = lax.broadcasted_iota(jnp.int32, (CHUNK, LANES), 0) >= META_PAD
        ig = jnp.where(valid, z, NEG_INF)
        lf = jnp.where(valid, _log_sigmoid(z), 0.0)
    else:
        ig = z
        lf = _log_sigmoid(z)
    g_t = pltpu.roll(_chunk_cumsum(lf), LANES - N_HEADS, axis=1)
    a = ig - g_t
    row = lax.broadcasted_iota(jnp.int32, (CHUNK, LANES), 0)
    a_max = a
    shift = 1
    while shift < CHUNK:
        a_max = jnp.maximum(a_max, jnp.where(row >= shift, pltpu.roll(a_max, shift, axis=0), NEG_INF))
        shift *= 2
    m_prev = m_sc[0:1, :]
    m_t = g_t + jnp.maximum(a_max, m_prev)
    m_ts = jnp.where(m_t == NEG_INF, 0.0, m_t)
    c_t = g_t - m_ts
    inter_all = jnp.exp(g_t + m_prev - m_ts)
    floor_all = jnp.exp(-m_ts)
    g_last = g_t[CHUNK - 1:CHUNK, :]
    m_new = g_last + jnp.maximum(a_max[CHUNK - 1:CHUNK, :], m_prev)
    decay_all = jnp.exp(g_last + m_prev - m_new)
    wexp_all = jnp.exp(g_last + a - m_new)
    m_sc[0:1, :] = m_new
    a_rows = a.T

    r64 = lax.broadcasted_iota(jnp.int32, (CHUNK, CHUNK), 0)
    c64 = lax.broadcasted_iota(jnp.int32, (CHUNK, CHUNK), 1)
    causal = r64 >= c64
    k_scale = HEAD_DIM ** -0.5
    ones_cols = jnp.ones((CHUNK, HEAD_DIM), BF16)
    mean_cols = jnp.full((HEAD_DIM, HEAD_DIM), 1.0 / HEAD_DIM, BF16)

    def row_mean(x):
        hi, lo = _split_bf16(x)
        return _dot(hi, mean_cols) + _dot(lo, mean_cols)

    heads = range(N_HEADS)
    hsl = [slice(h * HEAD_DIM, (h + 1) * HEAD_DIM) for h in heads]
    qb = [qk[:, hsl[h]].astype(BF16) for h in heads]
    k = [qk[:, D_MLSTM + h * HEAD_DIM:D_MLSTM + (h + 1) * HEAD_DIM] * k_scale for h in heads]
    v_aug = [jnp.concatenate([v_all[:, hsl[h]].astype(BF16), ones_cols], axis=1) for h in heads]
    state = [s_sc[h] for h in heads]

    s = []
    for h in heads:
        dexp = jnp.exp(jnp.where(causal, c_t[:, h:h + 1] + a_rows[h:h + 1, :], NEG_INF))
        s.append((_dot_nt(qb[h], k[h].astype(BF16)) * dexp).astype(BF16))
    hh = []
    for h in heads:
        nd = _dot(s[h], v_aug[h]) + inter_all[:, h:h + 1] * _dot(qb[h], state[h].astype(BF16))
        den = nd[:, HEAD_DIM:2 * HEAD_DIM]
        hh.append(nd[:, 0:HEAD_DIM] / jnp.maximum(jnp.abs(den), floor_all[:, h:h + 1]))
    for h in heads:
        wk = (k[h] * wexp_all[:, h:h + 1]).astype(BF16)
        s_sc[h] = decay_all[:, h:h + 1] * state[h] + _dot_tn(wk, v_aug[h])
    hc = [hh[h] - row_mean(hh[h]) for h in heads]
    var = [row_mean(hc[h] * hc[h]) for h in heads]
    for h in heads:
        y = hc[h] * lax.rsqrt(var[h] + LN_EPS) * ng[:, hsl[h]] * og_all[:, hsl[h]]
        y_ref[rows, hsl[h]] = y.astype(BF16)


def _mlstm(p, conv_w, conv_b, gate_bias, norm_g):
    n = MLSTM_CHUNKS_PER_STEP
    step_rows = n * CHUNK

    def col(block_w, start):
        return lambda c: (_step_block(c, n), start // block_w)

    const2 = lambda c: (0, 0)
    return pl.pallas_call(
        _mlstm_kernel,
        grid=(SEQ // step_rows + 1,),
        in_specs=[
            pl.BlockSpec((step_rows, 2 * D_MLSTM), col(2 * D_MLSTM, COL_MQK)),
            pl.BlockSpec((step_rows, D_MLSTM), col(D_MLSTM, COL_MV)),
            pl.BlockSpec((step_rows, D_MLSTM), col(D_MLSTM, COL_MO)),
            pl.BlockSpec((step_rows, LANES), col(LANES, COL_GATE)),
            pl.BlockSpec((CONV_K, 2 * D_MLSTM), const2),
            pl.BlockSpec((1, 2 * D_MLSTM), const2),
            pl.BlockSpec((1, LANES), const2),
            pl.BlockSpec((1, D_MLSTM), const2),
        ],
        out_specs=pl.BlockSpec((step_rows, D_MLSTM), lambda c: (_step_block(c, n), 0)),
        out_shape=jax.ShapeDtypeStruct((LP, D_MLSTM), BF16),
        scratch_shapes=[
            pltpu.VMEM((N_HEADS, HEAD_DIM, 2 * HEAD_DIM), F32),
            pltpu.VMEM((8, LANES), F32),
            pltpu.VMEM((CONV_HALO + CHUNK, 2 * D_MLSTM), F32),
        ],
        compiler_params=_cparams(("arbitrary",)),
        name="mlstm_mixer",
    )(p, p, p, p, conv_w, conv_b, gate_bias, norm_g)


N_LEVELS = 6


def _hgrn_tables():
    t = np.arange(CHUNK)
    sel = np.zeros((N_LEVELS * CHUNK, CHUNK), np.float32)
    mask = np.zeros((N_LEVELS + 1, CHUNK, CHUNK), np.float32)
    upper = np.zeros((CHUNK, LANES), np.float32)
    for l in range(N_LEVELS):
        half = 1 << l
        ref_row = (t // (2 * half)) * (2 * half) + half - 1
        sel[l * CHUNK + t, ref_row] = 1.0
        is_upper = (t // half) % 2 == 1
        upper[:, l] = is_upper
        same = (t[:, None] // (2 * half)) == (t[None, :] // (2 * half))
        mask[l] = same & is_upper[:, None] & ~is_upper[None, :]
    mask[N_LEVELS] = np.eye(CHUNK)
    return sel, mask, upper


def _hgrn_kernel(layer, hq_ref, hf_ref, hi_ref, hg_ref, lbl_ref, ng_ref, sel_ref, mask_ref, up_ref,
                 y_ref, st_sc):
    c = pl.program_id(0)

    lbl = lbl_ref[...]
    e = jnp.exp(lbl - jnp.max(lbl, axis=0, keepdims=True))
    sm = e / jnp.sum(e, axis=0, keepdims=True)
    lb = jnp.sum(sm[0:layer + 1, :], axis=0, keepdims=True) - sm[0:1, :]
    refs = (hq_ref, hf_ref, hi_ref, hg_ref, ng_ref, sel_ref, mask_ref, up_ref)

    @pl.when(c == 0)
    def _():
        st_sc[...] = jnp.zeros(st_sc.shape, F32)
        _hgrn_state(0, _hgrn_intra(0, lb, *refs), y_ref, st_sc)

    @pl.when(c > 0)
    def _():
        for g in range(HGRN_CHUNKS_PER_STEP):
            _hgrn_state(g, _hgrn_intra(g, lb, *refs), y_ref, st_sc)


def _hgrn_intra(g, lb, hq_ref, hf_ref, hi_ref, hg_ref, ng_ref, sel_ref, mask_ref, up_ref):
    rows = pl.ds(g * CHUNK, CHUNK)
    z = hf_ref[rows, :]
    a = jnp.log(lb)
    bb = jnp.log1p(-lb) + _log_sigmoid(z)
    mx = jnp.maximum(a, bb)
    log_f = mx + jnp.log(jnp.exp(a - mx) + jnp.exp(bb - mx))
    kk = (1.0 - lb) * _sigmoid(-z)
    hq = hq_ref[rows, :]
    q = hq * _sigmoid(hq)
    v = hi_ref[rows, :].astype(BF16)
    hg = hg_ref[rows, :]
    gate = hg * _sigmoid(hg) * ng_ref[...]

    b = _chunk_cumsum(log_f)
    b_hi, b_lo = _split_bf16(b)
    sel = sel_ref[...]
    refs = _dot(sel, b_hi) + _dot(sel, b_lo)
    up = up_ref[...]

    amats = [None] * N_HEADS
    for l in range(N_LEVELS + 1):
        if l < N_LEVELS:
            ref_l = refs[l * CHUNK:(l + 1) * CHUNK, :]
            sign = 2.0 * up[:, l:l + 1] - 1.0
            zl = jnp.exp(sign * (b - ref_l))
            ql = (q * zl).astype(BF16)
            kl = (kk * zl).astype(BF16)
        else:
            ql = q.astype(BF16)
            kl = kk.astype(BF16)
        ml = mask_ref[l]
        for h in range(N_HEADS):
            hs = slice(h * HEAD_DIM, (h + 1) * HEAD_DIM)
            part = ml * _dot_nt(ql[:, hs], kl[:, hs])
            amats[h] = part if amats[h] is None else amats[h] + part

    b_last = b[CHUNK - 1:CHUNK, :]
    qe = (q * jnp.exp(b)).astype(BF16)
    kd = (kk * jnp.exp(b_last - b)).astype(BF16)
    e_last = jnp.exp(b_last)
    return [a_h.astype(BF16) for a_h in amats], v, qe, kd, e_last, gate


def _hgrn_state(g, intra, y_ref, st_sc):
    amats, v, qe, kd, e_last, gate = intra
    rows = pl.ds(g * CHUNK, CHUNK)
    for h in range(N_HEADS):
        hs = slice(h * HEAD_DIM, (h + 1) * HEAD_DIM)
        st = st_sc[h]
        o = _dot(amats[h], v[:, hs]) + _dot_nt(qe[:, hs], st.astype(BF16))
        st_sc[h] = e_last[:, hs] * st + _dot_tn(v[:, hs], kd[:, hs])
        o = o * lax.rsqrt(jnp.mean(o * o, axis=1, keepdims=True) + LN_EPS)
        y_ref[rows, hs] = (o * gate[:, hs]).astype(BF16)


def _hgrn(p, lb_logits, norm_g, layer):
    n = HGRN_CHUNKS_PER_STEP
    step_rows = n * CHUNK

    def col(start):
        return lambda c: (_step_block(c, n), start // D_HGRN)

    sel, mask, upper = _hgrn_tables()
    const2 = lambda c: (0, 0)
    return pl.pallas_call(
        functools.partial(_hgrn_kernel, layer),
        grid=(SEQ // step_rows + 1,),
        in_specs=[
            pl.BlockSpec((step_rows, D_HGRN), col(COL_HQ)),
            pl.BlockSpec((step_rows, D_HGRN), col(COL_HF)),
            pl.BlockSpec((step_rows, D_HGRN), col(COL_HI)),
            pl.BlockSpec((step_rows, D_HGRN), col(COL_HG)),
            pl.BlockSpec((DEPTH, D_HGRN), const2),
            pl.BlockSpec((1, D_HGRN), const2),
            pl.BlockSpec((N_LEVELS * CHUNK, CHUNK), const2),
            pl.BlockSpec((N_LEVELS + 1, CHUNK, CHUNK), lambda c: (0, 0, 0)),
            pl.BlockSpec((CHUNK, LANES), const2),
        ],
        out_specs=pl.BlockSpec((step_rows, D_HGRN), lambda c: (_step_block(c, n), 0)),
        out_shape=jax.ShapeDtypeStruct((LP, D_HGRN), BF16),
        scratch_shapes=[pltpu.VMEM((N_HEADS, HEAD_DIM, HEAD_DIM), F32)],
        compiler_params=_cparams(("arbitrary",)),
        name="hgrn_mixer",
    )(p, p, p, p, lb_logits, norm_g, jnp.asarray(sel, BF16), jnp.asarray(mask, F32), jnp.asarray(upper, F32))


def _zero_pad_rows(y, row0):
    row = row0 + lax.broadcasted_iota(jnp.int32, y.shape, 0)
    is_pad = jnp.logical_and(row >= META_ROW0, row < META_ROW0 + META_PAD)
    return jnp.where(is_pad, 0.0, y)


def _first_argmax(x, lane, valid):
    xm = jnp.where(valid, x, NEG_INF)
    mx = jnp.max(xm, axis=1, keepdims=True)
    idx = jnp.min(jnp.where(jnp.logical_and(valid, xm == mx), lane, float(LANES)), axis=1, keepdims=True)
    return mx, idx


TOKEN_ROWS = D_MODEL // LANES
GATHER_PITCH = 24


def _store_token_linear(ref, x, first_token=0):
    n = x.shape[0]
    for k in range(TOKEN_ROWS):
        ref[pl.ds(first_token * TOKEN_ROWS + k, n, stride=TOKEN_ROWS), :] = x[:, k * LANES:(k + 1) * LANES]


def _load_gathered(ref, n):
    return jnp.concatenate([ref[pl.ds(k, n, stride=GATHER_PITCH), :] for k in range(TOKEN_ROWS)], axis=1)


OUT_HALF = -(-TM_MID // 32) * 16
OUT_SUB_BLOCKS = ((0, OUT_HALF), (OUT_HALF, TM_MID))


def _out_router_kernel(yp_ref, ym_ref, yh_ref, h_ref, wo_ref, g_ref, b_ref, wrh_ref, wrl_ref, br_ref,
                       h1_ref, hlin_ref, eid_ref, wts_ref):
    def project(r0, r1):
        y = jnp.concatenate([yp_ref[r0:r1, :], ym_ref[r0:r1, :], yh_ref[r0:r1, :]], axis=1)
        return _dot(y, wo_ref[...])

    acc = project(*OUT_SUB_BLOCKS[0])
    for n, (r0, r1) in enumerate(OUT_SUB_BLOCKS):
        acc_next = project(*OUT_SUB_BLOCKS[n + 1]) if n + 1 < len(OUT_SUB_BLOCKS) else None
        _out_router_rows(r0, r1, acc, h_ref, g_ref, b_ref, wrh_ref, wrl_ref, br_ref,
                         h1_ref, hlin_ref, eid_ref, wts_ref)
        acc = acc_next


def _out_router_rows(r0, r1, acc, h_ref, g_ref, b_ref, wrh_ref, wrl_ref, br_ref, h1_ref, hlin_ref, eid_ref, wts_ref):
    i = pl.program_id(0)
    h1 = _layer_norm_rows(ALPHA * h_ref[r0:r1, :] + acc, g_ref[...], b_ref[...])
    h1 = _zero_pad_rows(h1, i * TM_MID + r0)
    h1_ref[r0:r1, :] = h1
    _store_token_linear(hlin_ref, h1, r0)

    x_hi, x_lo = _split_bf16(h1)
    hi_both = _dot(x_hi, jnp.concatenate([wrh_ref[...], wrl_ref[...]], axis=1))
    logits = hi_both[:, 0:LANES] + hi_both[:, LANES:2 * LANES] + _dot(x_lo, wrh_ref[...]) + br_ref[...]
    lane = lax.broadcasted_iota(jnp.int32, logits.shape, 1).astype(F32)

    is_grp = lane < N_GROUPS
    g_max, g_idx = _first_argmax(logits, lane, is_grp)
    g_exp = jnp.where(is_grp, jnp.exp(logits - g_max), 0.0)
    p_grp = 1.0 / jnp.sum(g_exp, axis=1, keepdims=True)

    e_lo = N_GROUPS + g_idx * EXPERTS_PER_GROUP
    in_grp = jnp.logical_and(lane >= e_lo, lane < e_lo + EXPERTS_PER_GROUP)
    e_max, e1 = _first_argmax(logits, lane, in_grp)
    e_exp = jnp.where(in_grp, jnp.exp(logits - e_max), 0.0)
    p_exp = e_exp / jnp.sum(e_exp, axis=1, keepdims=True)
    p1, _ = _first_argmax(p_exp, lane, in_grp)
    rest = jnp.logical_and(in_grp, lane != e1)
    p2, e2 = _first_argmax(p_exp, lane, rest)
    psum = p1 + p2
    w1 = p_grp * p1 / psum
    w2 = p_grp * p2 / psum
    eid = jnp.where(lane == 0.0, e1 - N_GROUPS, jnp.where(lane == 1.0, e2 - N_GROUPS, 0.0))
    eid_ref[r0:r1, :] = eid.astype(jnp.int32)
    wts_ref[r0:r1, :] = jnp.where(lane == 0.0, w1, jnp.where(lane == 1.0, w2, 0.0))


def _out_router(y_pool, y_m, y_h, h, w_out, g, b, wr_hi, wr_lo, br):
    row = lambda i: (i, 0)
    const2 = lambda i: (0, 0)
    return pl.pallas_call(
        _out_router_kernel,
        grid=(LP // TM_MID,),
        in_specs=[
            pl.BlockSpec((TM_MID, D_POOL), row),
            pl.BlockSpec((TM_MID, D_MLSTM), row),
            pl.BlockSpec((TM_MID, D_HGRN), row),
            pl.BlockSpec((TM_MID, D_MODEL), row),
            pl.BlockSpec((D_MODEL, D_MODEL), const2, pipeline_mode=pl.Buffered(1)),
            pl.BlockSpec((1, D_MODEL), const2),
            pl.BlockSpec((1, D_MODEL), const2),
            pl.BlockSpec((D_MODEL, LANES), const2),
            pl.BlockSpec((D_MODEL, LANES), const2),
            pl.BlockSpec((1, LANES), const2),
        ],
        out_specs=[
            pl.BlockSpec((TM_MID, D_MODEL), row),
            pl.BlockSpec((TM_MID * TOKEN_ROWS, LANES), row),
            pl.BlockSpec((TM_MID, LANES), row),
            pl.BlockSpec((TM_MID, LANES), row),
        ],
        out_shape=[
            jax.ShapeDtypeStruct((LP, D_MODEL), F32),
            jax.ShapeDtypeStruct((LP * TOKEN_ROWS, LANES), F32),
            jax.ShapeDtypeStruct((LP, LANES), jnp.int32),
            jax.ShapeDtypeStruct((LP, LANES), F32),
        ],
        compiler_params=_cparams(("arbitrary",), VMEM_LIMIT_OUT_PROJ),
        name="out_proj_ln_router",
    )(y_pool, y_m, y_h, h, w_out, g, b, wr_hi, wr_lo, br)


ROUTE_ROWS = 256
SORTED_ROWS = N_EXPERT_TILES * TM_EXPERT
GATHER_UNROLL = 8


def _route_kernel(e_ref, dest_ref, tab_ref):
    e = e_ref[...]
    lane = lax.broadcasted_iota(jnp.int32, (ROUTE_ROWS, LANES), 1)
    kk = lax.broadcasted_iota(jnp.int32, (LANES, LANES), 0)
    ll = lax.broadcasted_iota(jnp.int32, (LANES, LANES), 1)
    before_lane = jnp.where(kk < ll, 1.0, 0.0).astype(BF16)
    all_lanes = jnp.ones((LANES, LANES), BF16)
    rr = lax.broadcasted_iota(jnp.int32, (ROUTE_ROWS, ROUTE_ROWS), 0)
    cc = lax.broadcasted_iota(jnp.int32, (ROUTE_ROWS, ROUTE_ROWS), 1)
    before_row = jnp.where(cc < rr, 1.0, 0.0).astype(BF16)

    lane1 = lane[0:1, :]
    rank = jnp.zeros((ROUTE_ROWS, LANES), F32)
    counts = jnp.zeros((1, LANES), F32)
    masks = []
    for x in range(N_EXPERTS):
        m = jnp.where(e == x, 1.0, 0.0)
        mb = m.astype(BF16)
        in_row = _dot(mb, before_lane)
        row_tot = _dot(mb, all_lanes)
        rows_before = _dot(before_row, row_tot.astype(BF16))
        rank = rank + m * (in_row + rows_before)
        total = rows_before[ROUTE_ROWS - 1:ROUTE_ROWS, :] + row_tot[ROUTE_ROWS - 1:ROUTE_ROWS, :]
        counts = counts + jnp.where(lane1 == x, total, 0.0)
        masks.append(m)

    padded = jnp.floor((counts + (TM_EXPERT - 1)) * (1.0 / TM_EXPERT)) * TM_EXPERT
    p_hi, p_lo = _split_bf16(padded)
    start = _dot(p_hi, before_lane) + _dot(p_lo, before_lane)
    end = start + padded

    dest = rank
    tile0 = (lane1 * TM_EXPERT).astype(F32)
    n_before = jnp.zeros((1, LANES), F32)
    for x in range(N_EXPERTS):
        dest = dest + masks[x] * start[:, x:x + 1]
        n_before = n_before + jnp.where(end[:, x:x + 1] <= tile0, 1.0, 0.0)
    dest_ref[...] = dest.astype(jnp.int32)

    tile_expert = jnp.minimum(n_before, float(N_EXPERTS - 1))
    n_tiles = end[:, N_EXPERTS - 1:N_EXPERTS] * (1.0 / TM_EXPERT)
    rows_left = jnp.zeros((1, LANES), F32)
    for x in range(N_EXPERTS):
        rows_left = rows_left + jnp.where(tile_expert == x, counts[:, x:x + 1] + start[:, x:x + 1] - tile0, 0.0)
    tile_rows = jnp.clip(rows_left, 0.0, float(TM_EXPERT))
    row = lax.broadcasted_iota(jnp.int32, (8, LANES), 0)
    tab = jnp.where(row == 0, tile_expert, jnp.where(row == 1, n_tiles, jnp.where(row == 2, tile_rows, 0.0)))
    tab_ref[...] = tab.astype(jnp.int32)


def _route(e_grid):
    return pl.pallas_call(
        _route_kernel,
        out_shape=[
            jax.ShapeDtypeStruct((ROUTE_ROWS, LANES), jnp.int32),
            jax.ShapeDtypeStruct((8, LANES), jnp.int32),
        ],
        compiler_params=pltpu.CompilerParams(vmem_limit_bytes=VMEM_LIMIT),
        name="moe_route",
    )(e_grid)


def _invert_kernel(dest_ref, tok_ref):
    def zero(p, carry):
        tok_ref[p] = 0
        return carry
    lax.fori_loop(0, SORTED_ROWS, zero, 0, unroll=4 * GATHER_UNROLL)

    def place_row(r, carry):
        first_token = r * (LANES // 2)
        for c in range(LANES):
            tok_ref[dest_ref[r, c]] = first_token + c // 2
        return carry
    lax.fori_loop(0, N_ASSIGN // LANES, place_row, 0)


def _invert(dest):
    used_rows = -(-N_ASSIGN // (8 * LANES)) * 8
    return pl.pallas_call(
        _invert_kernel,
        grid=(1,),
        in_specs=[pl.BlockSpec((used_rows, LANES), lambda i: (0, 0), memory_space=pltpu.SMEM)],
        out_specs=pl.BlockSpec((SORTED_ROWS,), lambda i: (0,), memory_space=pltpu.SMEM),
        out_shape=jax.ShapeDtypeStruct((SORTED_ROWS,), jnp.int32),
        name="moe_invert",
    )(dest)


GATHER_PRIORITY = 0
WEIGHT_PRIORITY = 1


def _expert_kernel(layer, tile_expert, tile_rows, n_tiles, tok_cur_ref, tok_next_ref, h_hbm, w1_hbm, w3_hbm, w2_hbm,
                   y_ref, xbuf, w1f, w3f, w2f, w1b, w3b, w2b, gsem, wsem, wslot_ref):
    i = pl.program_id(0)
    n_used = n_tiles[0]
    slot = i % 2

    def weight_copies(expert, ws):
        return [pltpu.make_async_copy(src.at[layer, expert], dst.at[ws], wsem.at[ws])
                for src, dst in ((w1_hbm, w1f), (w3_hbm, w3f), (w2_hbm, w2f))]

    def n_groups(tile):
        return lax.shift_right_logical(tile_rows[tile] + (GATHER_UNROLL - 1), GATHER_UNROLL.bit_length() - 1)

    def start_gather(tok_ref, s, groups):
        def body(g, carry):
            for u in range(GATHER_UNROLL):
                r = g * GATHER_UNROLL + u
                src = h_hbm.at[pl.ds(pl.multiple_of(tok_ref[0, 0, r] * TOKEN_ROWS, TOKEN_ROWS), TOKEN_ROWS)]
                dst = xbuf.at[s, pl.ds(pl.multiple_of(r * GATHER_PITCH, 8), TOKEN_ROWS)]
                pltpu.make_async_copy(src, dst, gsem.at[s]).start(priority=GATHER_PRIORITY)
            return carry
        lax.fori_loop(0, groups, body, 0)

    def wait_gather(s, groups):
        n = GATHER_UNROLL * TOKEN_ROWS

        def body(g, carry):
            pltpu.make_async_copy(h_hbm.at[pl.ds(0, n)], xbuf.at[s, pl.ds(0, n)], gsem.at[s]).wait()
            return carry
        lax.fori_loop(0, groups, body, 0)

    last_tile = N_EXPERT_TILES - 1

    @pl.when(i == 0)
    def _():
        xbuf[...] = jnp.zeros(xbuf.shape, F32)
        start_gather(tok_cur_ref, 0, n_groups(0))
        wslot_ref[0] = 0
        for cp in weight_copies(tile_expert[0], 0):
            cp.start(priority=WEIGHT_PRIORITY)

    @pl.when(i < n_used)
    def _():
        wait_gather(slot, n_groups(i))
        start_gather(tok_next_ref, 1 - slot, n_groups(jnp.minimum(i + 1, last_tile)))
        expert = tile_expert[i]
        first_of_expert = jnp.logical_or(i == 0, expert != tile_expert[jnp.maximum(i - 1, 0)])

        @pl.when(first_of_expert)
        def _():
            ws = wslot_ref[0]
            for cp in weight_copies(expert, ws):
                cp.wait()
            nxt = lax.while_loop(
                lambda j: jnp.logical_and(j < n_used, tile_expert[jnp.minimum(j, last_tile)] == expert),
                lambda j: j + 1, i + 1)

            @pl.when(nxt < n_used)
            def _():
                for cp in weight_copies(tile_expert[jnp.minimum(nxt, last_tile)], 1 - ws):
                    cp.start(priority=WEIGHT_PRIORITY)

            w1b[...] = w1f[ws].astype(BF16)
            w3b[...] = w3f[ws].astype(BF16)
            w2b[...] = w2f[ws].astype(BF16)
            wslot_ref[0] = 1 - ws

        x = _load_gathered(xbuf.at[slot], TM_EXPERT).astype(BF16)
        a = _dot(x, w1b[...])
        g = _dot(x, w3b[...])
        he = (a * _sigmoid(a) * g).astype(BF16)
        _store_token_linear(y_ref, _dot(he, w2b[...]))

    @pl.when(i >= n_used)
    def _():
        y_ref[...] = jnp.zeros(y_ref.shape, F32)


def _experts(h1, w1, w3, w2, layer, tile_expert, tile_rows, n_tiles, row_token):
    def cur(i, te, tr, nt):
        return (i, 0, 0)

    def nxt(i, te, tr, nt):
        return (jnp.minimum(i + 1, N_EXPERT_TILES - 1), 0, 0)

    grid_spec = pltpu.PrefetchScalarGridSpec(
        num_scalar_prefetch=3,
        grid=(N_EXPERT_TILES,),
        in_specs=[
            pl.BlockSpec((1, 1, TM_EXPERT), cur, memory_space=pltpu.SMEM),
            pl.BlockSpec((1, 1, TM_EXPERT), nxt, memory_space=pltpu.SMEM),
            pl.BlockSpec(memory_space=pl.ANY),
            pl.BlockSpec(memory_space=pl.ANY),
            pl.BlockSpec(memory_space=pl.ANY),
            pl.BlockSpec(memory_space=pl.ANY),
        ],
        out_specs=pl.BlockSpec((TM_EXPERT * TOKEN_ROWS, LANES), lambda i, te, tr, nt: (i, 0)),
        scratch_shapes=[
            pltpu.VMEM((2, TM_EXPERT * GATHER_PITCH, LANES), F32),
            pltpu.VMEM((2, D_MODEL, D_EXPERT), F32),
            pltpu.VMEM((2, D_MODEL, D_EXPERT), F32),
            pltpu.VMEM((2, D_EXPERT, D_MODEL), F32),
            pltpu.VMEM((D_MODEL, D_EXPERT), BF16),
            pltpu.VMEM((D_MODEL, D_EXPERT), BF16),
            pltpu.VMEM((D_EXPERT, D_MODEL), BF16),
            pltpu.SemaphoreType.DMA((2,)),
            pltpu.SemaphoreType.DMA((2,)),
            pltpu.SMEM((1,), jnp.int32),
        ],
    )
    return pl.pallas_call(
        functools.partial(_expert_kernel, layer),
        grid_spec=grid_spec,
        out_shape=jax.ShapeDtypeStruct((SORTED_ROWS * TOKEN_ROWS, LANES), F32),
        compiler_params=_cparams(("arbitrary",)),
        name="moe_experts",
    )(tile_expert, tile_rows, n_tiles, row_token, row_token, h1, w1, w3, w2)


def _combine_kernel(tm, n_steps, final, pos_cur_ref, pos_next_ref, h1_ref, wts_ref, g_ref, b_ref, y_hbm,
                    *rest):
    out_refs, (ybuf, sem) = rest[:-2], rest[-2:]
    i = pl.program_id(0)
    slot = i % 2

    def start_gather(pos_ref, s):
        def body(r, carry):
            for j in range(2):
                row0 = pl.multiple_of(pos_ref[0, 0, 2 * r + j] * TOKEN_ROWS, TOKEN_ROWS)
                dst = ybuf.at[s, j, pl.ds(pl.multiple_of(r * GATHER_PITCH, 8), TOKEN_ROWS)]
                pltpu.make_async_copy(y_hbm.at[pl.ds(row0, TOKEN_ROWS)], dst, sem.at[s]).start()
            return carry
        lax.fori_loop(0, tm, body, 0, unroll=GATHER_UNROLL)

    @pl.when(i == 0)
    def _():
        start_gather(pos_cur_ref, 0)

    for j in range(2):
        n = tm * TOKEN_ROWS
        pltpu.make_async_copy(y_hbm.at[pl.ds(0, n)], ybuf.at[slot, j, pl.ds(0, n)], sem.at[slot]).wait()

    @pl.when(i + 1 < n_steps)
    def _():
        start_gather(pos_next_ref, 1 - slot)

    wts = wts_ref[...]
    y = wts[:, 0:1] * _load_gathered(ybuf.at[slot, 0], tm) + wts[:, 1:2] * _load_gathered(ybuf.at[slot, 1], tm)
    h2 = _layer_norm_rows(ALPHA * h1_ref[...] + y, g_ref[...], b_ref[...])
    if final:
        out_refs[0][...] = h2
    else:
        h2 = _zero_pad_rows(h2, i * tm)
        out_refs[0][...] = h2
        out_refs[1][...] = h2.astype(BF16)


def _combine(h1, y_sorted, pos, wts, g, b, final):
    tm = TM_FINAL if final else TM_LN
    rows = SEQ if final else LP
    n_steps = rows // tm
    pos3 = pos[:2 * rows].reshape(n_steps, 1, 2 * tm)
    row = lambda i: (i, 0)
    const2 = lambda i: (0, 0)
    if final:
        out_specs = [pl.BlockSpec((tm, D_MODEL), row)]
        out_shape = [jax.ShapeDtypeStruct((rows, D_MODEL), F32)]
    else:
        out_specs = [pl.BlockSpec((tm, D_MODEL), row), pl.BlockSpec((tm, D_MODEL), row)]
        out_shape = [jax.ShapeDtypeStruct((rows, D_MODEL), F32), jax.ShapeDtypeStruct((rows, D_MODEL), BF16)]
    return pl.pallas_call(
        functools.partial(_combine_kernel, tm, n_steps, final),
        grid=(n_steps,),
        in_specs=[
            pl.BlockSpec((1, 1, 2 * tm), lambda i: (i, 0, 0), memory_space=pltpu.SMEM),
            pl.BlockSpec((1, 1, 2 * tm), lambda i: (jnp.minimum(i + 1, n_steps - 1), 0, 0), memory_space=pltpu.SMEM),
            pl.BlockSpec((tm, D_MODEL), row),
            pl.BlockSpec((tm, LANES), row),
            pl.BlockSpec((1, D_MODEL), const2),
            pl.BlockSpec((1, D_MODEL), const2),
            pl.BlockSpec(memory_space=pl.ANY),
        ],
        out_specs=out_specs,
        out_shape=out_shape,
        scratch_shapes=[pltpu.VMEM((2, 2, tm * GATHER_PITCH, LANES), F32), pltpu.SemaphoreType.DMA((2,))],
        compiler_params=_cparams(("arbitrary",)),
        name="moe_combine_ln_final" if final else "moe_combine_ln",
    )(pos3, pos3, h1, wts, g, b, y_sorted)


def _pad_lanes(v):
    return jnp.pad(v, ((0, 0), (0, LANES - v.shape[1])))


def kernel(x, meta_tokens, emb_ln_g, emb_ln_b, hgrn_lb_logits, w_in, conv_w, conv_b, ig_b, fg_b,
           mlstm_norm_g, pool_w, pool_scale, hgrn_norm_g, w_out, ln1_g, ln1_b,
           w_router_group, b_router_group, w_router_expert, b_router_expert, w1, w3, w2,
           ln2_g, ln2_b):
    assert x.shape == (1, SEQ, D_MODEL) and x.dtype == F32
    row2 = lambda v: v.reshape(1, -1)
    meta_blk = jnp.pad(meta_tokens.astype(F32), ((META_PAD, 0), (0, 0)))
    h, hb = _embed(x.reshape(SEQ, D_MODEL), meta_blk, row2(emb_ln_g), row2(emb_ln_b))

    out = None
    for l in range(DEPTH):
        p = _in_proj(hb, w_in, l)
        y_pool = _pool(p, pool_w[l], row2(pool_scale[l]))
        gate_bias = _pad_lanes(jnp.concatenate([ig_b[l], fg_b[l]]).reshape(1, -1))
        y_m = _mlstm(p, conv_w[l], row2(conv_b[l]), gate_bias, row2(mlstm_norm_g[l]))
        y_h = _hgrn(p, hgrn_lb_logits, row2(hgrn_norm_g[l]), l)

        w_r = _pad_lanes(jnp.concatenate([w_router_group[l], w_router_expert[l]], axis=1))
        wr_hi, wr_lo = _split_bf16(w_r)
        b_r = _pad_lanes(jnp.concatenate([b_router_group[l], b_router_expert[l]]).reshape(1, -1))
        h1, h1_lin, eid, wts = _out_router(y_pool, y_m, y_h, h, w_out[l].astype(BF16), row2(ln1_g[l]),
                                           row2(ln1_b[l]), wr_hi, wr_lo, b_r)

        e_flat = jnp.pad(eid[:, 0:2].reshape(-1), (0, ROUTE_ROWS * LANES - N_ASSIGN), constant_values=-1)
        dest, tab = _route(e_flat.reshape(ROUTE_ROWS, LANES))
        row_token = _invert(dest).reshape(N_EXPERT_TILES, 1, TM_EXPERT)
        y_sorted = _experts(h1_lin, w1, w3, w2, l, tab[0, :N_EXPERT_TILES], tab[2, :N_EXPERT_TILES], tab[1, :1],
                            row_token)
        pos = dest.reshape(-1)
        if l + 1 < DEPTH:
            h, hb = _combine(h1, y_sorted, pos, wts, row2(ln2_g[l]), row2(ln2_b[l]), final=False)
        else:
            (out,) = _combine(h1, y_sorted, pos, wts, row2(ln2_g[l]), row2(ln2_b[l]), final=True)
    return out.reshape(1, SEQ, D_MODEL)
```

```python
import functools

import jax
import jax.numpy as jnp
import numpy as np
from jax import lax
from jax.experimental import pallas as pl
from jax.experimental.pallas import tpu as pltpu

F32 = jnp.float32
BF16 = jnp.bfloat16

D_MODEL = 2048
SEQ = 8192
DEPTH = 2
N_META = 16
CHUNK = 64
D_POOL = D_MODEL // 4
POOL_WINDOWS = (2, 4, 8, 16)
POOL_GROUP = D_POOL // len(POOL_WINDOWS)
D_MLSTM = 3 * D_MODEL // 8
N_HEADS = 6
HEAD_DIM = D_MLSTM // N_HEADS
CONV_K = 4
D_HGRN = D_MODEL - D_POOL - D_MLSTM
N_GROUPS = 4
EXPERTS_PER_GROUP = 8
N_EXPERTS = N_GROUPS * EXPERTS_PER_GROUP
D_EXPERT = D_MODEL // 4
ALPHA = (2 * DEPTH) ** 0.25
LN_EPS = 1e-5
NEG_INF = float("-inf")

LANES = 128
V7X_VMEM_BYTES = 64 * 1024 * 1024
LP = SEQ + CHUNK
META_ROW0 = SEQ
META_PAD = CHUNK - N_META
TM_EMBED = 512
LP_EMBED = (SEQ // TM_EMBED + 1) * TM_EMBED

COL_MQK = 0
COL_MV = 2 * D_MLSTM
COL_MO = COL_MV + D_MLSTM
COL_HQ = COL_MO + D_MLSTM
COL_HF = COL_HQ + D_HGRN
COL_HI = COL_HF + D_HGRN
COL_HG = COL_HI + D_HGRN
COL_POOL = COL_HG + D_HGRN
COL_GATE = COL_POOL + D_POOL
P_COLS = COL_GATE + 2 * LANES

TM_BIG = 2752
TN_IN = 256
TM_MID = 688
TM_LN = 192
TM_FINAL = 256
TM_EXPERT = 256
N_ASSIGN = 2 * LP
N_EXPERT_TILES = (N_ASSIGN + N_EXPERTS * (TM_EXPERT - 1)) // TM_EXPERT + 1
VMEM_LIMIT = V7X_VMEM_BYTES * 7 // 8
VMEM_LIMIT_OUT_PROJ = V7X_VMEM_BYTES * 31 // 32


def _cparams(sem, vmem_limit=VMEM_LIMIT):
    return pltpu.CompilerParams(dimension_semantics=sem, vmem_limit_bytes=vmem_limit)


def _sigmoid(x):
    return 1.0 / (1.0 + jnp.exp(-x))


def _log_sigmoid(x):
    return jnp.minimum(x, 0.0) - jnp.log1p(jnp.exp(-jnp.abs(x)))


def _layer_norm_rows(x, g, b):
    mu = jnp.mean(x, axis=-1, keepdims=True)
    xc = x - mu
    var = jnp.mean(xc * xc, axis=-1, keepdims=True)
    return xc * lax.rsqrt(var + LN_EPS) * g + b


def _dot(a, b):
    return jnp.dot(a, b, preferred_element_type=F32)


def _dot_nt(a, b):
    return lax.dot_general(a, b, (((1,), (1,)), ((), ())), preferred_element_type=F32)


def _dot_tn(a, b):
    return lax.dot_general(a, b, (((0,), (0,)), ((), ())), preferred_element_type=F32)


def _split_bf16(x):
    hi = x.astype(BF16)
    lo = (x - hi.astype(F32)).astype(BF16)
    return hi, lo


def _chunk_cumsum(x):
    r = lax.broadcasted_iota(jnp.int32, (CHUNK, CHUNK), 0)
    c = lax.broadcasted_iota(jnp.int32, (CHUNK, CHUNK), 1)
    tri = jnp.where(r >= c, 1.0, 0.0).astype(BF16)
    hi, lo = _split_bf16(x)
    return _dot(tri, hi) + _dot(tri, lo)


def _embed_kernel(x_ref, meta_ref, g_ref, b_ref, h_ref, hb_ref):
    i = pl.program_id(0)
    g = g_ref[...]
    b = b_ref[...]

    @pl.when(i < SEQ // TM_EMBED)
    def _():
        y = _layer_norm_rows(x_ref[...], g, b)
        h_ref[...] = y
        hb_ref[...] = y.astype(BF16)

    @pl.when(i == SEQ // TM_EMBED)
    def _():
        y = _layer_norm_rows(meta_ref[...], g, b)
        row = lax.broadcasted_iota(jnp.int32, (CHUNK, D_MODEL), 0)
        y = jnp.where(row >= META_PAD, y, 0.0)
        h_ref[0:CHUNK, :] = y
        hb_ref[0:CHUNK, :] = y.astype(BF16)
        h_ref[CHUNK:TM_EMBED, :] = jnp.zeros((TM_EMBED - CHUNK, D_MODEL), F32)
        hb_ref[CHUNK:TM_EMBED, :] = jnp.zeros((TM_EMBED - CHUNK, D_MODEL), BF16)


def _embed(x2d, meta_blk, g, b):
    nx = SEQ // TM_EMBED
    return pl.pallas_call(
        _embed_kernel,
        grid=(nx + 1,),
        in_specs=[
            pl.BlockSpec((TM_EMBED, D_MODEL), lambda i: (jnp.minimum(i, nx - 1), 0)),
            pl.BlockSpec((CHUNK, D_MODEL), lambda i: (0, 0)),
            pl.BlockSpec((1, D_MODEL), lambda i: (0, 0)),
            pl.BlockSpec((1, D_MODEL), lambda i: (0, 0)),
        ],
        out_specs=[
            pl.BlockSpec((TM_EMBED, D_MODEL), lambda i: (i, 0)),
            pl.BlockSpec((TM_EMBED, D_MODEL), lambda i: (i, 0)),
        ],
        out_shape=[
            jax.ShapeDtypeStruct((LP_EMBED, D_MODEL), F32),
            jax.ShapeDtypeStruct((LP_EMBED, D_MODEL), BF16),
        ],
        compiler_params=_cparams(("arbitrary",)),
        name="embed_ln",
    )(x2d, meta_blk, g, b)


K_TILES = D_MODEL // LANES
W_IN_MLSTM0 = D_POOL
W_IN_GATE0 = D_POOL + 4 * D_MLSTM
W_IN_HGRN0 = W_IN_GATE0 + 2 * N_HEADS
D_IN = W_IN_HGRN0 + 4 * D_HGRN


def _in_proj_kernel(layer, x_ref, w_ref, o_ref):
    w_t = jnp.concatenate([w_ref[:, DEPTH * kt + layer, :] for kt in range(K_TILES)], axis=1)
    o_ref[...] = _dot_nt(x_ref[...], w_t.astype(BF16))


def _in_proj(hb, w_in, layer):
    w_view = w_in.reshape(DEPTH, K_TILES, LANES, D_IN).transpose(3, 1, 0, 2).reshape(D_IN, K_TILES * DEPTH, LANES)

    def first_col(j):
        c = j * TN_IN
        return jnp.where(c < COL_HQ, W_IN_MLSTM0 + c,
                         jnp.where(c < COL_POOL, W_IN_HGRN0 + (c - COL_HQ),
                                   jnp.where(c < COL_GATE, c - COL_POOL, W_IN_GATE0)))

    w_block = (pl.Element(TN_IN), pl.Element(K_TILES * DEPTH), pl.Element(LANES))
    return pl.pallas_call(
        functools.partial(_in_proj_kernel, layer),
        grid=(LP // TM_BIG, P_COLS // TN_IN),
        in_specs=[
            pl.BlockSpec((TM_BIG, D_MODEL), lambda i, j: (i, 0)),
            pl.BlockSpec(w_block, lambda i, j: (first_col(j), 0, 0)),
        ],
        out_specs=pl.BlockSpec((TM_BIG, TN_IN), lambda i, j: (i, j)),
        out_shape=jax.ShapeDtypeStruct((LP, P_COLS), F32),
        compiler_params=_cparams(("arbitrary", "arbitrary")),
        name="in_proj",
    )(hb, w_view)


POOL_HALO = 16


def _pool_kernel(u_ref, halo_ref, pw_ref, ps_ref, y_ref, ubuf):
    i = pl.program_id(0)
    u = u_ref[...]
    ubuf[0:POOL_HALO, :] = halo_ref[...]
    ubuf[POOL_HALO:POOL_HALO + TM_MID, :] = u
    row = i * TM_MID + lax.broadcasted_iota(jnp.int32, (TM_MID, POOL_GROUP), 0)
    pos = jnp.where(row >= META_ROW0 + META_PAD, row - (META_ROW0 + META_PAD) + 1, 2 * POOL_HALO)
    scale = ps_ref[...]
    for gi, w in enumerate(POOL_WINDOWS):
        cs = slice(gi * POOL_GROUP, (gi + 1) * POOL_GROUP)
        ug = u[:, cs]
        acc = ug
        for j in range(1, w):
            acc = acc + ubuf[POOL_HALO - j:POOL_HALO - j + TM_MID, cs]
        div = jnp.minimum(pos, w).astype(F32)
        d = acc / div - ug
        yg = _dot(d.astype(BF16), pw_ref[gi].astype(BF16))
        y_ref[:, cs] = (yg * scale[:, cs]).astype(BF16)


def _pool(p, pool_w, pool_scale):
    blocks_per_tile = TM_MID // POOL_HALO
    last_meta_block = (LP - POOL_HALO) // POOL_HALO

    def halo_map(i):
        return (jnp.where(i == 0, last_meta_block, i * blocks_per_tile - 1), COL_POOL // D_POOL)

    return pl.pallas_call(
        _pool_kernel,
        grid=(LP // TM_MID,),
        in_specs=[
            pl.BlockSpec((TM_MID, D_POOL), lambda i: (i, COL_POOL // D_POOL)),
            pl.BlockSpec((POOL_HALO, D_POOL), halo_map),
            pl.BlockSpec((len(POOL_WINDOWS), POOL_GROUP, POOL_GROUP), lambda i: (0, 0, 0)),
            pl.BlockSpec((1, D_POOL), lambda i: (0, 0)),
        ],
        out_specs=pl.BlockSpec((TM_MID, D_POOL), lambda i: (i, 0)),
        out_shape=jax.ShapeDtypeStruct((LP, D_POOL), BF16),
        scratch_shapes=[pltpu.VMEM((POOL_HALO + TM_MID, D_POOL), F32)],
        compiler_params=_cparams(("arbitrary",)),
        name="pool_mixer",
    )(p, p, pool_w, pool_scale)


MLSTM_CHUNKS_PER_STEP = 8
HGRN_CHUNKS_PER_STEP = 8


def _step_block(c, chunks_per_step):
    return jnp.where(c == 0, SEQ // (chunks_per_step * CHUNK), c - 1)


CONV_HALO = 8


def _mlstm_kernel(mqk_ref, mv_ref, mo_ref, gt_ref, cw_ref, cb_ref, gb_ref, ng_ref, y_ref,
                  s_sc, m_sc, xbuf):
    c = pl.program_id(0)

    @pl.when(c == 0)
    def _():
        s_sc[...] = jnp.zeros(s_sc.shape, F32)
        m_sc[...] = jnp.full(m_sc.shape, NEG_INF, F32)
        xbuf[0:CONV_HALO, :] = jnp.zeros((CONV_HALO, 2 * D_MLSTM), F32)
        _mlstm_chunk(0, True, mqk_ref, mv_ref, mo_ref, gt_ref, cw_ref, cb_ref, gb_ref, ng_ref, y_ref,
                     s_sc, m_sc, xbuf)

    @pl.when(c > 0)
    def _():
        for g in range(MLSTM_CHUNKS_PER_STEP):
            _mlstm_chunk(g, False, mqk_ref, mv_ref, mo_ref, gt_ref, cw_ref, cb_ref, gb_ref, ng_ref, y_ref,
                         s_sc, m_sc, xbuf)


def _mlstm_chunk(g, is_meta, mqk_ref, mv_ref, mo_ref, gt_ref, cw_ref, cb_ref, gb_ref, ng_ref, y_ref,
                 s_sc, m_sc, xbuf):
    rows = pl.ds(g * CHUNK, CHUNK)
    x = mqk_ref[rows, :]
    xbuf[CONV_HALO:CONV_HALO + CHUNK, :] = x
    cw = cw_ref[...]
    conv = cb_ref[...] + cw[CONV_K - 1:CONV_K, :] * x
    for j in range(CONV_K - 1):
        off = CONV_HALO - (CONV_K - 1) + j
        conv = conv + cw[j:j + 1, :] * xbuf[off:off + CHUNK, :]
    xbuf[0:CONV_HALO, :] = x[CHUNK - CONV_HALO:CHUNK, :]
    qk = conv * _sigmoid(conv)
    v_all = mv_ref[rows, :]
    og_all = _sigmoid(mo_ref[rows, :])
    ng = ng_ref[...]

    z = gt_ref[rows, :] + gb_ref[...]
    if is_meta:
        valid = lax.broadcasted_iota(jnp.int32, (CHUNK, LANES), 0) >= META_PAD
        ig = jnp.where(valid, z, NEG_INF)
        lf = jnp.where(valid, _log_sigmoid(z), 0.0)
    else:
        ig = z
        lf = _log_sigmoid(z)
    g_t = pltpu.roll(_chunk_cumsum(lf), LANES - N_HEADS, axis=1)
    a = ig - g_t
    row = lax.broadcasted_iota(jnp.int32, (CHUNK, LANES), 0)
    a_max = a
    shift = 1
    while shift < CHUNK:
        a_max = jnp.maximum(a_max, jnp.where(row >= shift, pltpu.roll(a_max, shift, axis=0), NEG_INF))
        shift *= 2
    m_prev = m_sc[0:1, :]
    m_t = g_t + jnp.maximum(a_max, m_prev)
    m_ts = jnp.where(m_t == NEG_INF, 0.0, m_t)
    c_t = g_t - m_ts
    inter_all = jnp.exp(g_t + m_prev - m_ts)
    floor_all = jnp.exp(-m_ts)
    g_last = g_t[CHUNK - 1:CHUNK, :]
    m_new = g_last + jnp.maximum(a_max[CHUNK - 1:CHUNK, :], m_prev)
    decay_all = jnp.exp(g_last + m_prev - m_new)
    wexp_all = jnp.exp(g_last + a - m_new)
    m_sc[0:1, :] = m_new
    a_rows = a.T

    r64 = lax.broadcasted_iota(jnp.int32, (CHUNK, CHUNK), 0)
    c64 = lax.broadcasted_iota(jnp.int32, (CHUNK, CHUNK), 1)
    causal = r64 >= c64
    k_scale = HEAD_DIM ** -0.5
    ones_cols = jnp.ones((CHUNK, HEAD_DIM), BF16)
    mean_cols = jnp.full((HEAD_DIM, HEAD_DIM), 1.0 / HEAD_DIM, BF16)

    def row_mean(x):
        hi, lo = _split_bf16(x)
        return _dot(hi, mean_cols) + _dot(lo, mean_cols)

    heads = range(N_HEADS)
    hsl = [slice(h * HEAD_DIM, (h + 1) * HEAD_DIM) for h in heads]
    qb = [qk[:, hsl[h]].astype(BF16) for h in heads]
    k = [qk[:, D_MLSTM + h * HEAD_DIM:D_MLSTM + (h + 1) * HEAD_DIM] * k_scale for h in heads]
    v_aug = [jnp.concatenate([v_all[:, hsl[h]].astype(BF16), ones_cols], axis=1) for h in heads]
    state = [s_sc[h] for h in heads]

    s = []
    for h in heads:
        dexp = jnp.exp(jnp.where(causal, c_t[:, h:h + 1] + a_rows[h:h + 1, :], NEG_INF))
        s.append((_dot_nt(qb[h], k[h].astype(BF16)) * dexp).astype(BF16))
    hh = []
    for h in heads:
        nd = _dot(s[h], v_aug[h]) + inter_all[:, h:h + 1] * _dot(qb[h], state[h].astype(BF16))
        den = nd[:, HEAD_DIM:2 * HEAD_DIM]
        hh.append(nd[:, 0:HEAD_DIM] / jnp.maximum(jnp.abs(den), floor_all[:, h:h + 1]))
    for h in heads:
        wk = (k[h] * wexp_all[:, h:h + 1]).astype(BF16)
        s_sc[h] = decay_all[:, h:h + 1] * state[h] + _dot_tn(wk, v_aug[h])
    hc = [hh[h] - row_mean(hh[h]) for h in heads]
    var = [row_mean(hc[h] * hc[h]) for h in heads]
    for h in heads:
        y = hc[h] * lax.rsqrt(var[h] + LN_EPS) * ng[:, hsl[h]] * og_all[:, hsl[h]]
        y_ref[rows, hsl[h]] = y.astype(BF16)


def _mlstm(p, conv_w, conv_b, gate_bias, norm_g):
    n = MLSTM_CHUNKS_PER_STEP
    step_rows = n * CHUNK

    def col(block_w, start):
        return lambda c: (_step_block(c, n), start // block_w)

    const2 = lambda c: (0, 0)
    return pl.pallas_call(
        _mlstm_kernel,
        grid=(SEQ // step_rows + 1,),
        in_specs=[
            pl.BlockSpec((step_rows, 2 * D_MLSTM), col(2 * D_MLSTM, COL_MQK)),
            pl.BlockSpec((step_rows, D_MLSTM), col(D_MLSTM, COL_MV)),
            pl.BlockSpec((step_rows, D_MLSTM), col(D_MLSTM, COL_MO)),
            pl.BlockSpec((step_rows, LANES), col(LANES, COL_GATE)),
            pl.BlockSpec((CONV_K, 2 * D_MLSTM), const2),
            pl.BlockSpec((1, 2 * D_MLSTM), const2),
            pl.BlockSpec((1, LANES), const2),
            pl.BlockSpec((1, D_MLSTM), const2),
        ],
        out_specs=pl.BlockSpec((step_rows, D_MLSTM), lambda c: (_step_block(c, n), 0)),
        out_shape=jax.ShapeDtypeStruct((LP, D_MLSTM), BF16),
        scratch_shapes=[
            pltpu.VMEM((N_HEADS, HEAD_DIM, 2 * HEAD_DIM), F32),
            pltpu.VMEM((8, LANES), F32),
            pltpu.VMEM((CONV_HALO + CHUNK, 2 * D_MLSTM), F32),
        ],
        compiler_params=_cparams(("arbitrary",)),
        name="mlstm_mixer",
    )(p, p, p, p, conv_w, conv_b, gate_bias, norm_g)


N_LEVELS = 6


def _hgrn_tables():
    t = np.arange(CHUNK)
    sel = np.zeros((N_LEVELS * CHUNK, CHUNK), np.float32)
    mask = np.zeros((N_LEVELS + 1, CHUNK, CHUNK), np.float32)
    upper = np.zeros((CHUNK, LANES), np.float32)
    for l in range(N_LEVELS):
        half = 1 << l
        ref_row = (t // (2 * half)) * (2 * half) + half - 1
        sel[l * CHUNK + t, ref_row] = 1.0
        is_upper = (t // half) % 2 == 1
        upper[:, l] = is_upper
        same = (t[:, None] // (2 * half)) == (t[None, :] // (2 * half))
        mask[l] = same & is_upper[:, None] & ~is_upper[None, :]
    mask[N_LEVELS] = np.eye(CHUNK)
    return sel, mask, upper


def _hgrn_kernel(layer, hq_ref, hf_ref, hi_ref, hg_ref, lbl_ref, ng_ref, sel_ref, mask_ref, up_ref,
                 y_ref, st_sc):
    c = pl.program_id(0)

    lbl = lbl_ref[...]
    e = jnp.exp(lbl - jnp.max(lbl, axis=0, keepdims=True))
    sm = e / jnp.sum(e, axis=0, keepdims=True)
    lb = jnp.sum(sm[0:layer + 1, :], axis=0, keepdims=True) - sm[0:1, :]
    refs = (hq_ref, hf_ref, hi_ref, hg_ref, ng_ref, sel_ref, mask_ref, up_ref)

    @pl.when(c == 0)
    def _():
        st_sc[...] = jnp.zeros(st_sc.shape, F32)
        _hgrn_state(0, _hgrn_intra(0, lb, *refs), y_ref, st_sc)

    @pl.when(c > 0)
    def _():
        for g in range(HGRN_CHUNKS_PER_STEP):
            _hgrn_state(g, _hgrn_intra(g, lb, *refs), y_ref, st_sc)


def _hgrn_intra(g, lb, hq_ref, hf_ref, hi_ref, hg_ref, ng_ref, sel_ref, mask_ref, up_ref):
    rows = pl.ds(g * CHUNK, CHUNK)
    z = hf_ref[rows, :]
    a = jnp.log(lb)
    bb = jnp.log1p(-lb) + _log_sigmoid(z)
    mx = jnp.maximum(a, bb)
    log_f = mx + jnp.log(jnp.exp(a - mx) + jnp.exp(bb - mx))
    kk = (1.0 - lb) * _sigmoid(-z)
    hq = hq_ref[rows, :]
    q = hq * _sigmoid(hq)
    v = hi_ref[rows, :].astype(BF16)
    hg = hg_ref[rows, :]
    gate = hg * _sigmoid(hg) * ng_ref[...]

    b = _chunk_cumsum(log_f)
    b_hi, b_lo = _split_bf16(b)
    sel = sel_ref[...]
    refs = _dot(sel, b_hi) + _dot(sel, b_lo)
    up = up_ref[...]

    amats = [None] * N_HEADS
    for l in range(N_LEVELS + 1):
        if l < N_LEVELS:
            ref_l = refs[l * CHUNK:(l + 1) * CHUNK, :]
            sign = 2.0 * up[:, l:l + 1] - 1.0
            zl = jnp.exp(sign * (b - ref_l))
            ql = (q * zl).astype(BF16)
            kl = (kk * zl).astype(BF16)
        else:
            ql = q.astype(BF16)
            kl = kk.astype(BF16)
        ml = mask_ref[l]
        for h in range(N_HEADS):
            hs = slice(h * HEAD_DIM, (h + 1) * HEAD_DIM)
            part = ml * _dot_nt(ql[:, hs], kl[:, hs])
            amats[h] = part if amats[h] is None else amats[h] + part

    b_last = b[CHUNK - 1:CHUNK, :]
    qe = (q * jnp.exp(b)).astype(BF16)
    kd = (kk * jnp.exp(b_last - b)).astype(BF16)
    e_last = jnp.exp(b_last)
    return [a_h.astype(BF16) for a_h in amats], v, qe, kd, e_last, gate


def _hgrn_state(g, intra, y_ref, st_sc):
    amats, v, qe, kd, e_last, gate = intra
    rows = pl.ds(g * CHUNK, CHUNK)
    for h in range(N_HEADS):
        hs = slice(h * HEAD_DIM, (h + 1) * HEAD_DIM)
        st = st_sc[h]
        o = _dot(amats[h], v[:, hs]) + _dot_nt(qe[:, hs], st.astype(BF16))
        st_sc[h] = e_last[:, hs] * st + _dot_tn(v[:, hs], kd[:, hs])
        o = o * lax.rsqrt(jnp.mean(o * o, axis=1, keepdims=True) + LN_EPS)
        y_ref[rows, hs] = (o * gate[:, hs]).astype(BF16)


def _hgrn(p, lb_logits, norm_g, layer):
    n = HGRN_CHUNKS_PER_STEP
    step_rows = n * CHUNK

    def col(start):
        return lambda c: (_step_block(c, n), start // D_HGRN)

    sel, mask, upper = _hgrn_tables()
    const2 = lambda c: (0, 0)
    return pl.pallas_call(
        functools.partial(_hgrn_kernel, layer),
        grid=(SEQ // step_rows + 1,),
        in_specs=[
            pl.BlockSpec((step_rows, D_HGRN), col(COL_HQ)),
            pl.BlockSpec((step_rows, D_HGRN), col(COL_HF)),
            pl.BlockSpec((step_rows, D_HGRN), col(COL_HI)),
            pl.BlockSpec((step_rows, D_HGRN), col(COL_HG)),
            pl.BlockSpec((DEPTH, D_HGRN), const2),
            pl.BlockSpec((1, D_HGRN), const2),
            pl.BlockSpec((N_LEVELS * CHUNK, CHUNK), const2),
            pl.BlockSpec((N_LEVELS + 1, CHUNK, CHUNK), lambda c: (0, 0, 0)),
            pl.BlockSpec((CHUNK, LANES), const2),
        ],
        out_specs=pl.BlockSpec((step_rows, D_HGRN), lambda c: (_step_block(c, n), 0)),
        out_shape=jax.ShapeDtypeStruct((LP, D_HGRN), BF16),
        scratch_shapes=[pltpu.VMEM((N_HEADS, HEAD_DIM, HEAD_DIM), F32)],
        compiler_params=_cparams(("arbitrary",)),
        name="hgrn_mixer",
    )(p, p, p, p, lb_logits, norm_g, jnp.asarray(sel, BF16), jnp.asarray(mask, F32), jnp.asarray(upper, F32))


def _zero_pad_rows(y, row0):
    row = row0 + lax.broadcasted_iota(jnp.int32, y.shape, 0)
    is_pad = jnp.logical_and(row >= META_ROW0, row < META_ROW0 + META_PAD)
    return jnp.where(is_pad, 0.0, y)


def _first_argmax(x, lane, valid):
    xm = jnp.where(valid, x, NEG_INF)
    mx = jnp.max(xm, axis=1, keepdims=True)
    idx = jnp.min(jnp.where(jnp.logical_and(valid, xm == mx), lane, float(LANES)), axis=1, keepdims=True)
    return mx, idx


TOKEN_ROWS = D_MODEL // LANES
GATHER_PITCH = 24


def _store_token_linear(ref, x, first_token=0):
    n = x.shape[0]
    for k in range(TOKEN_ROWS):
        ref[pl.ds(first_token * TOKEN_ROWS + k, n, stride=TOKEN_ROWS), :] = x[:, k * LANES:(k + 1) * LANES]


def _load_gathered(ref, n):
    return jnp.concatenate([ref[pl.ds(k, n, stride=GATHER_PITCH), :] for k in range(TOKEN_ROWS)], axis=1)


OUT_HALF = -(-TM_MID // 32) * 16
OUT_SUB_BLOCKS = ((0, OUT_HALF), (OUT_HALF, TM_MID))


def _out_router_kernel(yp_ref, ym_ref, yh_ref, h_ref, wo_ref, g_ref, b_ref, wrh_ref, wrl_ref, br_ref,
                       h1_ref, hlin_ref, eid_ref, wts_ref):
    def project(r0, r1):
        y = jnp.concatenate([yp_ref[r0:r1, :], ym_ref[r0:r1, :], yh_ref[r0:r1, :]], axis=1)
        return _dot(y, wo_ref[...])

    acc = project(*OUT_SUB_BLOCKS[0])
    for n, (r0, r1) in enumerate(OUT_SUB_BLOCKS):
        acc_next = project(*OUT_SUB_BLOCKS[n + 1]) if n + 1 < len(OUT_SUB_BLOCKS) else None
        _out_router_rows(r0, r1, acc, h_ref, g_ref, b_ref, wrh_ref, wrl_ref, br_ref,
                         h1_ref, hlin_ref, eid_ref, wts_ref)
        acc = acc_next


def _out_router_rows(r0, r1, acc, h_ref, g_ref, b_ref, wrh_ref, wrl_ref, br_ref, h1_ref, hlin_ref, eid_ref, wts_ref):
    i = pl.program_id(0)
    h1 = _layer_norm_rows(ALPHA * h_ref[r0:r1, :] + acc, g_ref[...], b_ref[...])
    h1 = _zero_pad_rows(h1, i * TM_MID + r0)
    h1_ref[r0:r1, :] = h1
    _store_token_linear(hlin_ref, h1, r0)

    x_hi, x_lo = _split_bf16(h1)
    hi_both = _dot(x_hi, jnp.concatenate([wrh_ref[...], wrl_ref[...]], axis=1))
    logits = hi_both[:, 0:LANES] + hi_both[:, LANES:2 * LANES] + _dot(x_lo, wrh_ref[...]) + br_ref[...]
    lane = lax.broadcasted_iota(jnp.int32, logits.shape, 1).astype(F32)

    is_grp = lane < N_GROUPS
    g_max, g_idx = _first_argmax(logits, lane, is_grp)
    g_exp = jnp.where(is_grp, jnp.exp(logits - g_max), 0.0)
    p_grp = 1.0 / jnp.sum(g_exp, axis=1, keepdims=True)

    e_lo = N_GROUPS + g_idx * EXPERTS_PER_GROUP
    in_grp = jnp.logical_and(lane >= e_lo, lane < e_lo + EXPERTS_PER_GROUP)
    e_max, e1 = _first_argmax(logits, lane, in_grp)
    e_exp = jnp.where(in_grp, jnp.exp(logits - e_max), 0.0)
    p_exp = e_exp / jnp.sum(e_exp, axis=1, keepdims=True)
    p1, _ = _first_argmax(p_exp, lane, in_grp)
    rest = jnp.logical_and(in_grp, lane != e1)
    p2, e2 = _first_argmax(p_exp, lane, rest)
    psum = p1 + p2
    w1 = p_grp * p1 / psum
    w2 = p_grp * p2 / psum
    eid = jnp.where(lane == 0.0, e1 - N_GROUPS, jnp.where(lane == 1.0, e2 - N_GROUPS, 0.0))
    eid_ref[r0:r1, :] = eid.astype(jnp.int32)
    wts_ref[r0:r1, :] = jnp.where(lane == 0.0, w1, jnp.where(lane == 1.0, w2, 0.0))


def _out_router(y_pool, y_m, y_h, h, w_out, g, b, wr_hi, wr_lo, br):
    row = lambda i: (i, 0)
    const2 = lambda i: (0, 0)
    return pl.pallas_call(
        _out_router_kernel,
        grid=(LP // TM_MID,),
        in_specs=[
            pl.BlockSpec((TM_MID, D_POOL), row),
            pl.BlockSpec((TM_MID, D_MLSTM), row),
            pl.BlockSpec((TM_MID, D_HGRN), row),
            pl.BlockSpec((TM_MID, D_MODEL), row),
            pl.BlockSpec((D_MODEL, D_MODEL), const2, pipeline_mode=pl.Buffered(1)),
            pl.BlockSpec((1, D_MODEL), const2),
            pl.BlockSpec((1, D_MODEL), const2),
            pl.BlockSpec((D_MODEL, LANES), const2),
            pl.BlockSpec((D_MODEL, LANES), const2),
            pl.BlockSpec((1, LANES), const2),
        ],
        out_specs=[
            pl.BlockSpec((TM_MID, D_MODEL), row),
            pl.BlockSpec((TM_MID * TOKEN_ROWS, LANES), row),
            pl.BlockSpec((TM_MID, LANES), row),
            pl.BlockSpec((TM_MID, LANES), row),
        ],
        out_shape=[
            jax.ShapeDtypeStruct((LP, D_MODEL), F32),
            jax.ShapeDtypeStruct((LP * TOKEN_ROWS, LANES), F32),
            jax.ShapeDtypeStruct((LP, LANES), jnp.int32),
            jax.ShapeDtypeStruct((LP, LANES), F32),
        ],
        compiler_params=_cparams(("arbitrary",), VMEM_LIMIT_OUT_PROJ),
        name="out_proj_ln_router",
    )(y_pool, y_m, y_h, h, w_out, g, b, wr_hi, wr_lo, br)


ROUTE_ROWS = 256
SORTED_ROWS = N_EXPERT_TILES * TM_EXPERT
GATHER_UNROLL = 8


def _route_kernel(e_ref, dest_ref, tab_ref):
    e = e_ref[...]
    lane = lax.broadcasted_iota(jnp.int32, (ROUTE_ROWS, LANES), 1)
    kk = lax.broadcasted_iota(jnp.int32, (LANES, LANES), 0)
    ll = lax.broadcasted_iota(jnp.int32, (LANES, LANES), 1)
    before_lane = jnp.where(kk < ll, 1.0, 0.0).astype(BF16)
    all_lanes = jnp.ones((LANES, LANES), BF16)
    rr = lax.broadcasted_iota(jnp.int32, (ROUTE_ROWS, ROUTE_ROWS), 0)
    cc = lax.broadcasted_iota(jnp.int32, (ROUTE_ROWS, ROUTE_ROWS), 1)
    before_row = jnp.where(cc < rr, 1.0, 0.0).astype(BF16)

    lane1 = lane[0:1, :]
    rank = jnp.zeros((ROUTE_ROWS, LANES), F32)
    counts = jnp.zeros((1, LANES), F32)
    masks = []
    for x in range(N_EXPERTS):
        m = jnp.where(e == x, 1.0, 0.0)
        mb = m.astype(BF16)
        in_row = _dot(mb, before_lane)
        row_tot = _dot(mb, all_lanes)
        rows_before = _dot(before_row, row_tot.astype(BF16))
        rank = rank + m * (in_row + rows_before)
        total = rows_before[ROUTE_ROWS - 1:ROUTE_ROWS, :] + row_tot[ROUTE_ROWS - 1:ROUTE_ROWS, :]
        counts = counts + jnp.where(lane1 == x, total, 0.0)
        masks.append(m)

    padded = jnp.floor((counts + (TM_EXPERT - 1)) * (1.0 / TM_EXPERT)) * TM_EXPERT
    p_hi, p_lo = _split_bf16(padded)
    start = _dot(p_hi, before_lane) + _dot(p_lo, before_lane)
    end = start + padded

    dest = rank
    tile0 = (lane1 * TM_EXPERT).astype(F32)
    n_before = jnp.zeros((1, LANES), F32)
    for x in range(N_EXPERTS):
        dest = dest + masks[x] * start[:, x:x + 1]
        n_before = n_before + jnp.where(end[:, x:x + 1] <= tile0, 1.0, 0.0)
    dest_ref[...] = dest.astype(jnp.int32)

    tile_expert = jnp.minimum(n_before, float(N_EXPERTS - 1))
    n_tiles = end[:, N_EXPERTS - 1:N_EXPERTS] * (1.0 / TM_EXPERT)
    rows_left = jnp.zeros((1, LANES), F32)
    for x in range(N_EXPERTS):
        rows_left = rows_left + jnp.where(tile_expert == x, counts[:, x:x + 1] + start[:, x:x + 1] - tile0, 0.0)
    tile_rows = jnp.clip(rows_left, 0.0, float(TM_EXPERT))
    row = lax.broadcasted_iota(jnp.int32, (8, LANES), 0)
    tab = jnp.where(row == 0, tile_expert, jnp.where(row == 1, n_tiles, jnp.where(row == 2, tile_rows, 0.0)))
    tab_ref[...] = tab.astype(jnp.int32)


def _route(e_grid):
    return pl.pallas_call(
        _route_kernel,
        out_shape=[
            jax.ShapeDtypeStruct((ROUTE_ROWS, LANES), jnp.int32),
            jax.ShapeDtypeStruct((8, LANES), jnp.int32),
        ],
        compiler_params=pltpu.CompilerParams(vmem_limit_bytes=VMEM_LIMIT),
        name="moe_route",
    )(e_grid)


def _invert_kernel(dest_ref, tok_ref):
    def zero(p, carry):
        tok_ref[p] = 0
        return carry
    lax.fori_loop(0, SORTED_ROWS, zero, 0, unroll=4 * GATHER_UNROLL)

    def place_row(r, carry):
        first_token = r * (LANES // 2)
        for c in range(LANES):
            tok_ref[dest_ref[r, c]] = first_token + c // 2
        return carry
    lax.fori_loop(0, N_ASSIGN // LANES, place_row, 0)


def _invert(dest):
    used_rows = -(-N_ASSIGN // (8 * LANES)) * 8
    return pl.pallas_call(
        _invert_kernel,
        grid=(1,),
        in_specs=[pl.BlockSpec((used_rows, LANES), lambda i: (0, 0), memory_space=pltpu.SMEM)],
        out_specs=pl.BlockSpec((SORTED_ROWS,), lambda i: (0,), memory_space=pltpu.SMEM),
        out_shape=jax.ShapeDtypeStruct((SORTED_ROWS,), jnp.int32),
        name="moe_invert",
    )(dest)


EXPERT_ROW_STEP = 64
GATHER_PRIORITY = 0
WEIGHT_PRIORITY = 1


def _expert_kernel(layer, tile_expert, tile_rows, n_tiles, tok_cur_ref, tok_next_ref, h_hbm, w1_hbm, w3_hbm, w2_hbm,
                   y_ref, xbuf, w1f, w3f, w2f, w1b, w3b, w2b, gsem, wsem, wslot_ref):
    i = pl.program_id(0)
    n_used = n_tiles[0]
    slot = i % 2

    def weight_copies(expert, ws):
        return [pltpu.make_async_copy(src.at[layer, expert], dst.at[ws], wsem.at[ws])
                for src, dst in ((w1_hbm, w1f), (w3_hbm, w3f), (w2_hbm, w2f))]

    def n_groups(tile):
        return lax.shift_right_logical(tile_rows[tile] + (GATHER_UNROLL - 1), GATHER_UNROLL.bit_length() - 1)

    def start_gather(tok_ref, s, groups):
        def body(g, carry):
            for u in range(GATHER_UNROLL):
                r = g * GATHER_UNROLL + u
                src = h_hbm.at[pl.ds(pl.multiple_of(tok_ref[0, 0, r] * TOKEN_ROWS, TOKEN_ROWS), TOKEN_ROWS)]
                dst = xbuf.at[s, pl.ds(pl.multiple_of(r * GATHER_PITCH, 8), TOKEN_ROWS)]
                pltpu.make_async_copy(src, dst, gsem.at[s]).start(priority=GATHER_PRIORITY)
            return carry
        lax.fori_loop(0, groups, body, 0)

    def wait_gather(s, groups):
        n = GATHER_UNROLL * TOKEN_ROWS

        def body(g, carry):
            pltpu.make_async_copy(h_hbm.at[pl.ds(0, n)], xbuf.at[s, pl.ds(0, n)], gsem.at[s]).wait()
            return carry
        lax.fori_loop(0, groups, body, 0)

    last_tile = N_EXPERT_TILES - 1

    @pl.when(i == 0)
    def _():
        xbuf[...] = jnp.zeros(xbuf.shape, F32)
        start_gather(tok_cur_ref, 0, n_groups(0))
        wslot_ref[0] = 0
        for cp in weight_copies(tile_expert[0], 0):
            cp.start(priority=WEIGHT_PRIORITY)

    @pl.when(i < n_used)
    def _():
        wait_gather(slot, n_groups(i))
        start_gather(tok_next_ref, 1 - slot, n_groups(jnp.minimum(i + 1, last_tile)))
        expert = tile_expert[i]
        first_of_expert = jnp.logical_or(i == 0, expert != tile_expert[jnp.maximum(i - 1, 0)])

        @pl.when(first_of_expert)
        def _():
            ws = wslot_ref[0]
            for cp in weight_copies(expert, ws):
                cp.wait()
            nxt = lax.while_loop(
                lambda j: jnp.logical_and(j < n_used, tile_expert[jnp.minimum(j, last_tile)] == expert),
                lambda j: j + 1, i + 1)

            @pl.when(nxt < n_used)
            def _():
                for cp in weight_copies(tile_expert[jnp.minimum(nxt, last_tile)], 1 - ws):
                    cp.start(priority=WEIGHT_PRIORITY)

            w1b[...] = w1f[ws].astype(BF16)
            w3b[...] = w3f[ws].astype(BF16)
            w2b[...] = w2f[ws].astype(BF16)
            wslot_ref[0] = 1 - ws

        rows_here = tile_rows[i]
        for m in range(EXPERT_ROW_STEP, TM_EXPERT + 1, EXPERT_ROW_STEP):
            @pl.when(jnp.logical_and(rows_here > m - EXPERT_ROW_STEP, rows_here <= m))
            def _(m=m):
                x = _load_gathered(xbuf.at[slot], m).astype(BF16)
                a = _dot(x, w1b[...])
                g = _dot(x, w3b[...])
                he = (a * _sigmoid(a) * g).astype(BF16)
                _store_token_linear(y_ref, _dot(he, w2b[...]))
                if m < TM_EXPERT:
                    y_ref[m * TOKEN_ROWS:, :] = jnp.zeros(((TM_EXPERT - m) * TOKEN_ROWS, LANES), F32)

    @pl.when(i >= n_used)
    def _():
        y_ref[...] = jnp.zeros(y_ref.shape, F32)


def _experts(h1, w1, w3, w2, layer, tile_expert, tile_rows, n_tiles, row_token):
    def cur(i, te, tr, nt):
        return (i, 0, 0)

    def nxt(i, te, tr, nt):
        return (jnp.minimum(i + 1, N_EXPERT_TILES - 1), 0, 0)

    grid_spec = pltpu.PrefetchScalarGridSpec(
        num_scalar_prefetch=3,
        grid=(N_EXPERT_TILES,),
        in_specs=[
            pl.BlockSpec((1, 1, TM_EXPERT), cur, memory_space=pltpu.SMEM),
            pl.BlockSpec((1, 1, TM_EXPERT), nxt, memory_space=pltpu.SMEM),
            pl.BlockSpec(memory_space=pl.ANY),
            pl.BlockSpec(memory_space=pl.ANY),
            pl.BlockSpec(memory_space=pl.ANY),
            pl.BlockSpec(memory_space=pl.ANY),
        ],
        out_specs=pl.BlockSpec((TM_EXPERT * TOKEN_ROWS, LANES), lambda i, te, tr, nt: (i, 0)),
        scratch_shapes=[
            pltpu.VMEM((2, TM_EXPERT * GATHER_PITCH, LANES), F32),
            pltpu.VMEM((2, D_MODEL, D_EXPERT), F32),
            pltpu.VMEM((2, D_MODEL, D_EXPERT), F32),
            pltpu.VMEM((2, D_EXPERT, D_MODEL), F32),
            pltpu.VMEM((D_MODEL, D_EXPERT), BF16),
            pltpu.VMEM((D_MODEL, D_EXPERT), BF16),
            pltpu.VMEM((D_EXPERT, D_MODEL), BF16),
            pltpu.SemaphoreType.DMA((2,)),
            pltpu.SemaphoreType.DMA((2,)),
            pltpu.SMEM((1,), jnp.int32),
        ],
    )
    return pl.pallas_call(
        functools.partial(_expert_kernel, layer),
        grid_spec=grid_spec,
        out_shape=jax.ShapeDtypeStruct((SORTED_ROWS * TOKEN_ROWS, LANES), F32),
        compiler_params=_cparams(("arbitrary",)),
        name="moe_experts",
    )(tile_expert, tile_rows, n_tiles, row_token, row_token, h1, w1, w3, w2)


def _combine_kernel(tm, n_steps, final, pos_cur_ref, pos_next_ref, h1_ref, wts_ref, g_ref, b_ref, y_hbm,
                    *rest):
    out_refs, (ybuf, sem) = rest[:-2], rest[-2:]
    i = pl.program_id(0)
    slot = i % 2

    def start_gather(pos_ref, s):
        def body(r, carry):
            for j in range(2):
                row0 = pl.multiple_of(pos_ref[0, 0, 2 * r + j] * TOKEN_ROWS, TOKEN_ROWS)
                dst = ybuf.at[s, j, pl.ds(pl.multiple_of(r * GATHER_PITCH, 8), TOKEN_ROWS)]
                pltpu.make_async_copy(y_hbm.at[pl.ds(row0, TOKEN_ROWS)], dst, sem.at[s]).start()
            return carry
        lax.fori_loop(0, tm, body, 0, unroll=GATHER_UNROLL)

    @pl.when(i == 0)
    def _():
        start_gather(pos_cur_ref, 0)

    for j in range(2):
        n = tm * TOKEN_ROWS
        pltpu.make_async_copy(y_hbm.at[pl.ds(0, n)], ybuf.at[slot, j, pl.ds(0, n)], sem.at[slot]).wait()

    @pl.when(i + 1 < n_steps)
    def _():
        start_gather(pos_next_ref, 1 - slot)

    wts = wts_ref[...]
    y = wts[:, 0:1] * _load_gathered(ybuf.at[slot, 0], tm) + wts[:, 1:2] * _load_gathered(ybuf.at[slot, 1], tm)
    h2 = _layer_norm_rows(ALPHA * h1_ref[...] + y, g_ref[...], b_ref[...])
    if final:
        out_refs[0][...] = h2
    else:
        h2 = _zero_pad_rows(h2, i * tm)
        out_refs[0][...] = h2
        out_refs[1][...] = h2.astype(BF16)


def _combine(h1, y_sorted, pos, wts, g, b, final):
    tm = TM_FINAL if final else TM_LN
    rows = SEQ if final else LP
    n_steps = rows // tm
    pos3 = pos[:2 * rows].reshape(n_steps, 1, 2 * tm)
    row = lambda i: (i, 0)
    const2 = lambda i: (0, 0)
    if final:
        out_specs = [pl.BlockSpec((tm, D_MODEL), row)]
        out_shape = [jax.ShapeDtypeStruct((rows, D_MODEL), F32)]
    else:
        out_specs = [pl.BlockSpec((tm, D_MODEL), row), pl.BlockSpec((tm, D_MODEL), row)]
        out_shape = [jax.ShapeDtypeStruct((rows, D_MODEL), F32), jax.ShapeDtypeStruct((rows, D_MODEL), BF16)]
    return pl.pallas_call(
        functools.partial(_combine_kernel, tm, n_steps, final),
        grid=(n_steps,),
        in_specs=[
            pl.BlockSpec((1, 1, 2 * tm), lambda i: (i, 0, 0), memory_space=pltpu.SMEM),
            pl.BlockSpec((1, 1, 2 * tm), lambda i: (jnp.minimum(i + 1, n_steps - 1), 0, 0), memory_space=pltpu.SMEM),
            pl.BlockSpec((tm, D_MODEL), row),
            pl.BlockSpec((tm, LANES), row),
            pl.BlockSpec((1, D_MODEL), const2),
            pl.BlockSpec((1, D_MODEL), const2),
            pl.BlockSpec(memory_space=pl.ANY),
        ],
        out_specs=out_specs,
        out_shape=out_shape,
        scratch_shapes=[pltpu.VMEM((2, 2, tm * GATHER_PITCH, LANES), F32), pltpu.SemaphoreType.DMA((2,))],
        compiler_params=_cparams(("arbitrary",)),
        name="moe_combine_ln_final" if final else "moe_combine_ln",
    )(pos3, pos3, h1, wts, g, b, y_sorted)


def _pad_lanes(v):
    return jnp.pad(v, ((0, 0), (0, LANES - v.shape[1])))


def kernel(x, meta_tokens, emb_ln_g, emb_ln_b, hgrn_lb_logits, w_in, conv_w, conv_b, ig_b, fg_b,
           mlstm_norm_g, pool_w, pool_scale, hgrn_norm_g, w_out, ln1_g, ln1_b,
           w_router_group, b_router_group, w_router_expert, b_router_expert, w1, w3, w2,
           ln2_g, ln2_b):
    assert x.shape == (1, SEQ, D_MODEL) and x.dtype == F32
    row2 = lambda v: v.reshape(1, -1)
    meta_blk = jnp.pad(meta_tokens.astype(F32), ((META_PAD, 0), (0, 0)))
    h, hb = _embed(x.reshape(SEQ, D_MODEL), meta_blk, row2(emb_ln_g), row2(emb_ln_b))

    out = None
    for l in range(DEPTH):
        p = _in_proj(hb, w_in, l)
        y_pool = _pool(p, pool_w[l], row2(pool_scale[l]))
        gate_bias = _pad_lanes(jnp.concatenate([ig_b[l], fg_b[l]]).reshape(1, -1))
        y_m = _mlstm(p, conv_w[l], row2(conv_b[l]), gate_bias, row2(mlstm_norm_g[l]))
        y_h = _hgrn(p, hgrn_lb_logits, row2(hgrn_norm_g[l]), l)

        w_r = _pad_lanes(jnp.concatenate([w_router_group[l], w_router_expert[l]], axis=1))
        wr_hi, wr_lo = _split_bf16(w_r)
        b_r = _pad_lanes(jnp.concatenate([b_router_group[l], b_router_expert[l]]).reshape(1, -1))
        h1, h1_lin, eid, wts = _out_router(y_pool, y_m, y_h, h, w_out[l].astype(BF16), row2(ln1_g[l]),
                                           row2(ln1_b[l]), wr_hi, wr_lo, b_r)

        e_flat = jnp.pad(eid[:, 0:2].reshape(-1), (0, ROUTE_ROWS * LANES - N_ASSIGN), constant_values=-1)
        dest, tab = _route(e_flat.reshape(ROUTE_ROWS, LANES))
        row_token = _invert(dest).reshape(N_EXPERT_TILES, 1, TM_EXPERT)
        y_sorted = _experts(h1_lin, w1, w3, w2, l, tab[0, :N_EXPERT_TILES], tab[2, :N_EXPERT_TILES], tab[1, :1],
                            row_token)
        pos = dest.reshape(-1)
        if l + 1 < DEPTH:
            h, hb = _combine(h1, y_sorted, pos, wts, row2(ln2_g[l]), row2(ln2_b[l]), final=False)
        else:
            (out,) = _combine(h1, y_sorted, pos, wts, row2(ln2_g[l]), row2(ln2_b[l]), final=True)
    return out.reshape(1, SEQ, D_MODEL)
```

```python
import functools

import jax
import jax.numpy as jnp
import numpy as np
from jax import lax
from jax.experimental import pallas as pl
from jax.experimental.pallas import tpu as pltpu

F32 = jnp.float32
BF16 = jnp.bfloat16

D_MODEL = 2048
SEQ = 8192
DEPTH = 2
N_META = 16
CHUNK = 64
D_POOL = D_MODEL // 4
POOL_WINDOWS = (2, 4, 8, 16)
POOL_GROUP = D_POOL // len(POOL_WINDOWS)
D_MLSTM = 3 * D_MODEL // 8
N_HEADS = 6
HEAD_DIM = D_MLSTM // N_HEADS
CONV_K = 4
D_HGRN = D_MODEL - D_POOL - D_MLSTM
N_GROUPS = 4
EXPERTS_PER_GROUP = 8
N_EXPERTS = N_GROUPS * EXPERTS_PER_GROUP
D_EXPERT = D_MODEL // 4
ALPHA = (2 * DEPTH) ** 0.25
LN_EPS = 1e-5
NEG_INF = float("-inf")

LANES = 128
V7X_VMEM_BYTES = 64 * 1024 * 1024
LP = SEQ + CHUNK
META_ROW0 = SEQ
META_PAD = CHUNK - N_META
TM_EMBED = 512
LP_EMBED = (SEQ // TM_EMBED + 1) * TM_EMBED

COL_MQK = 0
COL_MV = 2 * D_MLSTM
COL_MO = COL_MV + D_MLSTM
COL_HQ = COL_MO + D_MLSTM
COL_HF = COL_HQ + D_HGRN
COL_HI = COL_HF + D_HGRN
COL_HG = COL_HI + D_HGRN
COL_POOL = COL_HG + D_HGRN
COL_GATE = COL_POOL + D_POOL
P_COLS = COL_GATE + 2 * LANES

TM_BIG = 2752
TN_IN = 256
TM_MID = 688
TM_LN = 192
TM_FINAL = 256
TM_EXPERT = 256
N_ASSIGN = 2 * LP
N_EXPERT_TILES = (N_ASSIGN + N_EXPERTS * (TM_EXPERT - 1)) // TM_EXPERT + 1
VMEM_LIMIT = V7X_VMEM_BYTES * 7 // 8
VMEM_LIMIT_OUT_PROJ = V7X_VMEM_BYTES * 31 // 32


def _cparams(sem, vmem_limit=VMEM_LIMIT):
    return pltpu.CompilerParams(dimension_semantics=sem, vmem_limit_bytes=vmem_limit)


def _sigmoid(x):
    return 1.0 / (1.0 + jnp.exp(-x))


def _log_sigmoid(x):
    return jnp.minimum(x, 0.0) - jnp.log1p(jnp.exp(-jnp.abs(x)))


def _layer_norm_rows(x, g, b):
    mu = jnp.mean(x, axis=-1, keepdims=True)
    xc = x - mu
    var = jnp.mean(xc * xc, axis=-1, keepdims=True)
    return xc * lax.rsqrt(var + LN_EPS) * g + b


def _dot(a, b):
    return jnp.dot(a, b, preferred_element_type=F32)


def _dot_nt(a, b):
    return lax.dot_general(a, b, (((1,), (1,)), ((), ())), preferred_element_type=F32)


def _dot_tn(a, b):
    return lax.dot_general(a, b, (((0,), (0,)), ((), ())), preferred_element_type=F32)


def _split_bf16(x):
    hi = x.astype(BF16)
    lo = (x - hi.astype(F32)).astype(BF16)
    return hi, lo


def _chunk_cumsum(x):
    r = lax.broadcasted_iota(jnp.int32, (CHUNK, CHUNK), 0)
    c = lax.broadcasted_iota(jnp.int32, (CHUNK, CHUNK), 1)
    tri = jnp.where(r >= c, 1.0, 0.0).astype(BF16)
    hi, lo = _split_bf16(x)
    return _dot(tri, hi) + _dot(tri, lo)


def _embed_kernel(x_ref, meta_ref, g_ref, b_ref, h_ref, hb_ref):
    i = pl.program_id(0)
    g = g_ref[...]
    b = b_ref[...]

    @pl.when(i < SEQ // TM_EMBED)
    def _():
        y = _layer_norm_rows(x_ref[...], g, b)
        h_ref[...] = y
        hb_ref[...] = y.astype(BF16)

    @pl.when(i == SEQ // TM_EMBED)
    def _():
        y = _layer_norm_rows(meta_ref[...], g, b)
        row = lax.broadcasted_iota(jnp.int32, (CHUNK, D_MODEL), 0)
        y = jnp.where(row >= META_PAD, y, 0.0)
        h_ref[0:CHUNK, :] = y
        hb_ref[0:CHUNK, :] = y.astype(BF16)
        h_ref[CHUNK:TM_EMBED, :] = jnp.zeros((TM_EMBED - CHUNK, D_MODEL), F32)
        hb_ref[CHUNK:TM_EMBED, :] = jnp.zeros((TM_EMBED - CHUNK, D_MODEL), BF16)


def _embed(x2d, meta_blk, g, b):
    nx = SEQ // TM_EMBED
    return pl.pallas_call(
        _embed_kernel,
        grid=(nx + 1,),
        in_specs=[
            pl.BlockSpec((TM_EMBED, D_MODEL), lambda i: (jnp.minimum(i, nx - 1), 0)),
            pl.BlockSpec((CHUNK, D_MODEL), lambda i: (0, 0)),
            pl.BlockSpec((1, D_MODEL), lambda i: (0, 0)),
            pl.BlockSpec((1, D_MODEL), lambda i: (0, 0)),
        ],
        out_specs=[
            pl.BlockSpec((TM_EMBED, D_MODEL), lambda i: (i, 0)),
            pl.BlockSpec((TM_EMBED, D_MODEL), lambda i: (i, 0)),
        ],
        out_shape=[
            jax.ShapeDtypeStruct((LP_EMBED, D_MODEL), F32),
            jax.ShapeDtypeStruct((LP_EMBED, D_MODEL), BF16),
        ],
        compiler_params=_cparams(("arbitrary",)),
        name="embed_ln",
    )(x2d, meta_blk, g, b)


K_TILES = D_MODEL // LANES
W_IN_MLSTM0 = D_POOL
W_IN_GATE0 = D_POOL + 4 * D_MLSTM
W_IN_HGRN0 = W_IN_GATE0 + 2 * N_HEADS
D_IN = W_IN_HGRN0 + 4 * D_HGRN


def _in_proj_kernel(layer, x_ref, w_ref, o_ref):
    w_t = jnp.concatenate([w_ref[:, DEPTH * kt + layer, :] for kt in range(K_TILES)], axis=1)
    o_ref[...] = _dot_nt(x_ref[...], w_t.astype(BF16))


def _in_proj(hb, w_in, layer):
    w_view = w_in.reshape(DEPTH, K_TILES, LANES, D_IN).transpose(3, 1, 0, 2).reshape(D_IN, K_TILES * DEPTH, LANES)

    def first_col(j):
        c = j * TN_IN
        return jnp.where(c < COL_HQ, W_IN_MLSTM0 + c,
                         jnp.where(c < COL_POOL, W_IN_HGRN0 + (c - COL_HQ),
                                   jnp.where(c < COL_GATE, c - COL_POOL, W_IN_GATE0)))

    w_block = (pl.Element(TN_IN), pl.Element(K_TILES * DEPTH), pl.Element(LANES))
    return pl.pallas_call(
        functools.partial(_in_proj_kernel, layer),
        grid=(LP // TM_BIG, P_COLS // TN_IN),
        in_specs=[
            pl.BlockSpec((TM_BIG, D_MODEL), lambda i, j: (i, 0)),
            pl.BlockSpec(w_block, lambda i, j: (first_col(j), 0, 0)),
        ],
        out_specs=pl.BlockSpec((TM_BIG, TN_IN), lambda i, j: (i, j)),
        out_shape=jax.ShapeDtypeStruct((LP, P_COLS), F32),
        compiler_params=_cparams(("arbitrary", "arbitrary")),
        name="in_proj",
    )(hb, w_view)


POOL_HALO = 16


def _pool_kernel(u_ref, halo_ref, pw_ref, ps_ref, y_ref, ubuf):
    i = pl.program_id(0)
    u = u_ref[...]
    ubuf[0:POOL_HALO, :] = halo_ref[...]
    ubuf[POOL_HALO:POOL_HALO + TM_MID, :] = u
    row = i * TM_MID + lax.broadcasted_iota(jnp.int32, (TM_MID, POOL_GROUP), 0)
    pos = jnp.where(row >= META_ROW0 + META_PAD, row - (META_ROW0 + META_PAD) + 1, 2 * POOL_HALO)
    scale = ps_ref[...]
    for gi, w in enumerate(POOL_WINDOWS):
        cs = slice(gi * POOL_GROUP, (gi + 1) * POOL_GROUP)
        ug = u[:, cs]
        acc = ug
        for j in range(1, w):
            acc = acc + ubuf[POOL_HALO - j:POOL_HALO - j + TM_MID, cs]
        div = jnp.minimum(pos, w).astype(F32)
        d = acc / div - ug
        yg = _dot(d.astype(BF16), pw_ref[gi].astype(BF16))
        y_ref[:, cs] = (yg * scale[:, cs]).astype(BF16)


def _pool(p, pool_w, pool_scale):
    blocks_per_tile = TM_MID // POOL_HALO
    last_meta_block = (LP - POOL_HALO) // POOL_HALO

    def halo_map(i):
        return (jnp.where(i == 0, last_meta_block, i * blocks_per_tile - 1), COL_POOL // D_POOL)

    return pl.pallas_call(
        _pool_kernel,
        grid=(LP // TM_MID,),
        in_specs=[
            pl.BlockSpec((TM_MID, D_POOL), lambda i: (i, COL_POOL // D_POOL)),
            pl.BlockSpec((POOL_HALO, D_POOL), halo_map),
            pl.BlockSpec((len(POOL_WINDOWS), POOL_GROUP, POOL_GROUP), lambda i: (0, 0, 0)),
            pl.BlockSpec((1, D_POOL), lambda i: (0, 0)),
        ],
        out_specs=pl.BlockSpec((TM_MID, D_POOL), lambda i: (i, 0)),
        out_shape=jax.ShapeDtypeStruct((LP, D_POOL), BF16),
        scratch_shapes=[pltpu.VMEM((POOL_HALO + TM_MID, D_POOL), F32)],
        compiler_params=_cparams(("arbitrary",)),
        name="pool_mixer",
    )(p, p, pool_w, pool_scale)


MLSTM_CHUNKS_PER_STEP = 8
HGRN_CHUNKS_PER_STEP = 8


def _step_block(c, chunks_per_step):
    return jnp.where(c == 0, SEQ // (chunks_per_step * CHUNK), c - 1)


CONV_HALO = 8


def _mlstm_kernel(mqk_ref, mv_ref, mo_ref, gt_ref, cw_ref, cb_ref, gb_ref, ng_ref, y_ref,
                  s_sc, m_sc, xbuf):
    c = pl.program_id(0)

    @pl.when(c == 0)
    def _():
        s_sc[...] = jnp.zeros(s_sc.shape, F32)
        m_sc[...] = jnp.full(m_sc.shape, NEG_INF, F32)
        xbuf[0:CONV_HALO, :] = jnp.zeros((CONV_HALO, 2 * D_MLSTM), F32)
        _mlstm_chunk(0, True, mqk_ref, mv_ref, mo_ref, gt_ref, cw_ref, cb_ref, gb_ref, ng_ref, y_ref,
                     s_sc, m_sc, xbuf)

    @pl.when(c > 0)
    def _():
        for g in range(MLSTM_CHUNKS_PER_STEP):
            _mlstm_chunk(g, False, mqk_ref, mv_ref, mo_ref, gt_ref, cw_ref, cb_ref, gb_ref, ng_ref, y_ref,
                         s_sc, m_sc, xbuf)


def _mlstm_chunk(g, is_meta, mqk_ref, mv_ref, mo_ref, gt_ref, cw_ref, cb_ref, gb_ref, ng_ref, y_ref,
                 s_sc, m_sc, xbuf):
    rows = pl.ds(g * CHUNK, CHUNK)
    x = mqk_ref[rows, :]
    xbuf[CONV_HALO:CONV_HALO + CHUNK, :] = x
    cw = cw_ref[...]
    conv = cb_ref[...] + cw[CONV_K - 1:CONV_K, :] * x
    for j in range(CONV_K - 1):
        off = CONV_HALO - (CONV_K - 1) + j
        conv = conv + cw[j:j + 1, :] * xbuf[off:off + CHUNK, :]
    xbuf[0:CONV_HALO, :] = x[CHUNK - CONV_HALO:CHUNK, :]
    qk = conv * _sigmoid(conv)
    v_all = mv_ref[rows, :]
    og_all = _sigmoid(mo_ref[rows, :])
    ng = ng_ref[...]

    z = gt_ref[rows, :] + gb_ref[...]
    if is_meta:
        valid = lax.broadcasted_iota(jnp.int32, (CHUNK, LANES), 0) >= META_PAD
        ig = jnp.where(valid, z, NEG_INF)
        lf = jnp.where(valid, _log_sigmoid(z), 0.0)
    else:
        ig = z
        lf = _log_sigmoid(z)
    g_t = pltpu.roll(_chunk_cumsum(lf), LANES - N_HEADS, axis=1)
    a = ig - g_t
    row = lax.broadcasted_iota(jnp.int32, (CHUNK, LANES), 0)
    a_max = a
    shift = 1
    while shift < CHUNK:
        a_max = jnp.maximum(a_max, jnp.where(row >= shift, pltpu.roll(a_max, shift, axis=0), NEG_INF))
        shift *= 2
    m_prev = m_sc[0:1, :]
    m_t = g_t + jnp.maximum(a_max, m_prev)
    m_ts = jnp.where(m_t == NEG_INF, 0.0, m_t)
    c_t = g_t - m_ts
    inter_all = jnp.exp(g_t + m_prev - m_ts)
    floor_all = jnp.exp(-m_ts)
    g_last = g_t[CHUNK - 1:CHUNK, :]
    m_new = g_last + jnp.maximum(a_max[CHUNK - 1:CHUNK, :], m_prev)
    decay_all = jnp.exp(g_last + m_prev - m_new)
    wexp_all = jnp.exp(g_last + a - m_new)
    m_sc[0:1, :] = m_new
    a_rows = a.T

    r64 = lax.broadcasted_iota(jnp.int32, (CHUNK, CHUNK), 0)
    c64 = lax.broadcasted_iota(jnp.int32, (CHUNK, CHUNK), 1)
    causal = r64 >= c64
    k_scale = HEAD_DIM ** -0.5
    ones_cols = jnp.ones((CHUNK, HEAD_DIM), BF16)
    mean_cols = jnp.full((HEAD_DIM, HEAD_DIM), 1.0 / HEAD_DIM, BF16)

    def row_mean(x):
        hi, lo = _split_bf16(x)
        return _dot(hi, mean_cols) + _dot(lo, mean_cols)

    heads = range(N_HEADS)
    hsl = [slice(h * HEAD_DIM, (h + 1) * HEAD_DIM) for h in heads]
    qb = [qk[:, hsl[h]].astype(BF16) for h in heads]
    k = [qk[:, D_MLSTM + h * HEAD_DIM:D_MLSTM + (h + 1) * HEAD_DIM] * k_scale for h in heads]
    v_aug = [jnp.concatenate([v_all[:, hsl[h]].astype(BF16), ones_cols], axis=1) for h in heads]
    state = [s_sc[h] for h in heads]

    s = []
    for h in heads:
        dexp = jnp.exp(jnp.where(causal, c_t[:, h:h + 1] + a_rows[h:h + 1, :], NEG_INF))
        s.append((_dot_nt(qb[h], k[h].astype(BF16)) * dexp).astype(BF16))
    hh = []
    for h in heads:
        nd = _dot(s[h], v_aug[h]) + inter_all[:, h:h + 1] * _dot(qb[h], state[h].astype(BF16))
        den = nd[:, HEAD_DIM:2 * HEAD_DIM]
        hh.append(nd[:, 0:HEAD_DIM] / jnp.maximum(jnp.abs(den), floor_all[:, h:h + 1]))
    for h in heads:
        wk = (k[h] * wexp_all[:, h:h + 1]).astype(BF16)
        s_sc[h] = decay_all[:, h:h + 1] * state[h] + _dot_tn(wk, v_aug[h])
    hc = [hh[h] - row_mean(hh[h]) for h in heads]
    var = [row_mean(hc[h] * hc[h]) for h in heads]
    for h in heads:
        y = hc[h] * lax.rsqrt(var[h] + LN_EPS) * ng[:, hsl[h]] * og_all[:, hsl[h]]
        y_ref[rows, hsl[h]] = y.astype(BF16)


def _mlstm(p, conv_w, conv_b, gate_bias, norm_g):
    n = MLSTM_CHUNKS_PER_STEP
    step_rows = n * CHUNK

    def col(block_w, start):
        return lambda c: (_step_block(c, n), start // block_w)

    const2 = lambda c: (0, 0)
    return pl.pallas_call(
        _mlstm_kernel,
        grid=(SEQ // step_rows + 1,),
        in_specs=[
            pl.BlockSpec((step_rows, 2 * D_MLSTM), col(2 * D_MLSTM, COL_MQK)),
            pl.BlockSpec((step_rows, D_MLSTM), col(D_MLSTM, COL_MV)),
            pl.BlockSpec((step_rows, D_MLSTM), col(D_MLSTM, COL_MO)),
            pl.BlockSpec((step_rows, LANES), col(LANES, COL_GATE)),
            pl.BlockSpec((CONV_K, 2 * D_MLSTM), const2),
            pl.BlockSpec((1, 2 * D_MLSTM), const2),
            pl.BlockSpec((1, LANES), const2),
            pl.BlockSpec((1, D_MLSTM), const2),
        ],
        out_specs=pl.BlockSpec((step_rows, D_MLSTM), lambda c: (_step_block(c, n), 0)),
        out_shape=jax.ShapeDtypeStruct((LP, D_MLSTM), BF16),
        scratch_shapes=[
            pltpu.VMEM((N_HEADS, HEAD_DIM, 2 * HEAD_DIM), F32),
            pltpu.VMEM((8, LANES), F32),
            pltpu.VMEM((CONV_HALO + CHUNK, 2 * D_MLSTM), F32),
        ],
        compiler_params=_cparams(("arbitrary",)),
        name="mlstm_mixer",
    )(p, p, p, p, conv_w, conv_b, gate_bias, norm_g)


N_LEVELS = 6
LOG2_E = 1.4426950408889634


def _hgrn_tables():
    t = np.arange(CHUNK)
    sel = np.zeros((N_LEVELS * CHUNK, CHUNK), np.float32)
    mask = np.zeros((N_LEVELS + 1, CHUNK, CHUNK), np.float32)
    for l in range(N_LEVELS):
        half = 1 << l
        ref_row = (t // (2 * half)) * (2 * half) + half - 1
        is_upper = (t // half) % 2 == 1
        sign = np.where(is_upper, 1.0, -1.0)
        np.add.at(sel, (l * CHUNK + t, t), sign)
        np.add.at(sel, (l * CHUNK + t, ref_row), -sign)
        same = (t[:, None] // (2 * half)) == (t[None, :] // (2 * half))
        mask[l] = same & is_upper[:, None] & ~is_upper[None, :]
    mask[N_LEVELS] = np.eye(CHUNK)
    return sel, mask


def _hgrn_kernel(layer, hq_ref, hf_ref, hi_ref, hg_ref, lbl_ref, ng_ref, sel_ref, mask_ref, y_ref, st_sc):
    c = pl.program_id(0)

    lbl = lbl_ref[...]
    e = jnp.exp(lbl - jnp.max(lbl, axis=0, keepdims=True))
    sm = e / jnp.sum(e, axis=0, keepdims=True)
    lb = jnp.sum(sm[0:layer + 1, :], axis=0, keepdims=True) - sm[0:1, :]
    refs = (hq_ref, hf_ref, hi_ref, hg_ref, ng_ref, sel_ref, mask_ref)

    @pl.when(c == 0)
    def _():
        st_sc[...] = jnp.zeros(st_sc.shape, F32)
        _hgrn_state(0, _hgrn_intra(0, lb, *refs), y_ref, st_sc)

    @pl.when(c > 0)
    def _():
        for g in range(HGRN_CHUNKS_PER_STEP):
            _hgrn_state(g, _hgrn_intra(g, lb, *refs), y_ref, st_sc)


def _hgrn_intra(g, lb, hq_ref, hf_ref, hi_ref, hg_ref, ng_ref, sel_ref, mask_ref):
    rows = pl.ds(g * CHUNK, CHUNK)
    z = hf_ref[rows, :]
    a = jnp.log(lb)
    bb = jnp.log1p(-lb) + _log_sigmoid(z)
    mx = jnp.maximum(a, bb)
    log_f = mx + jnp.log(jnp.exp(a - mx) + jnp.exp(bb - mx))
    kk = (1.0 - lb) * _sigmoid(-z)
    hq = hq_ref[rows, :]
    q = hq * _sigmoid(hq)
    v = hi_ref[rows, :].astype(BF16)
    hg = hg_ref[rows, :]
    gate = hg * _sigmoid(hg) * ng_ref[...]

    b = _chunk_cumsum(log_f) * LOG2_E
    b_hi, b_lo = _split_bf16(b)
    sel = sel_ref[...]
    diffs = _dot(sel, b_hi) + _dot(sel, b_lo)

    amats = [None] * N_HEADS
    for l in range(N_LEVELS + 1):
        if l < N_LEVELS:
            zl = jnp.exp2(diffs[l * CHUNK:(l + 1) * CHUNK, :])
            ql = (q * zl).astype(BF16)
            kl = (kk * zl).astype(BF16)
        else:
            ql = q.astype(BF16)
            kl = kk.astype(BF16)
        ml = mask_ref[l]
        for h in range(N_HEADS):
            hs = slice(h * HEAD_DIM, (h + 1) * HEAD_DIM)
            part = ml * _dot_nt(ql[:, hs], kl[:, hs])
            amats[h] = part if amats[h] is None else amats[h] + part

    b_last = b[CHUNK - 1:CHUNK, :]
    qe = (q * jnp.exp2(b)).astype(BF16)
    kd = (kk * jnp.exp2(b_last - b)).astype(BF16)
    e_last = jnp.exp2(b_last)
    return [a_h.astype(BF16) for a_h in amats], v, qe, kd, e_last, gate


def _hgrn_state(g, intra, y_ref, st_sc):
    amats, v, qe, kd, e_last, gate = intra
    rows = pl.ds(g * CHUNK, CHUNK)
    for h in range(N_HEADS):
        hs = slice(h * HEAD_DIM, (h + 1) * HEAD_DIM)
        st = st_sc[h]
        o = _dot(amats[h], v[:, hs]) + _dot_nt(qe[:, hs], st.astype(BF16))
        st_sc[h] = e_last[:, hs] * st + _dot_tn(v[:, hs], kd[:, hs])
        o = o * lax.rsqrt(jnp.mean(o * o, axis=1, keepdims=True) + LN_EPS)
        y_ref[rows, hs] = (o * gate[:, hs]).astype(BF16)


def _hgrn(p, lb_logits, norm_g, layer):
    n = HGRN_CHUNKS_PER_STEP
    step_rows = n * CHUNK

    def col(start):
        return lambda c: (_step_block(c, n), start // D_HGRN)

    sel, mask = _hgrn_tables()
    const2 = lambda c: (0, 0)
    return pl.pallas_call(
        functools.partial(_hgrn_kernel, layer),
        grid=(SEQ // step_rows + 1,),
        in_specs=[
            pl.BlockSpec((step_rows, D_HGRN), col(COL_HQ)),
            pl.BlockSpec((step_rows, D_HGRN), col(COL_HF)),
            pl.BlockSpec((step_rows, D_HGRN), col(COL_HI)),
            pl.BlockSpec((step_rows, D_HGRN), col(COL_HG)),
            pl.BlockSpec((DEPTH, D_HGRN), const2),
            pl.BlockSpec((1, D_HGRN), const2),
            pl.BlockSpec((N_LEVELS * CHUNK, CHUNK), const2),
            pl.BlockSpec((N_LEVELS + 1, CHUNK, CHUNK), lambda c: (0, 0, 0)),
        ],
        out_specs=pl.BlockSpec((step_rows, D_HGRN), lambda c: (_step_block(c, n), 0)),
        out_shape=jax.ShapeDtypeStruct((LP, D_HGRN), BF16),
        scratch_shapes=[pltpu.VMEM((N_HEADS, HEAD_DIM, HEAD_DIM), F32)],
        compiler_params=_cparams(("arbitrary",)),
        name="hgrn_mixer",
    )(p, p, p, p, lb_logits, norm_g, jnp.asarray(sel, BF16), jnp.asarray(mask, F32))


def _zero_pad_rows(y, row0):
    row = row0 + lax.broadcasted_iota(jnp.int32, y.shape, 0)
    is_pad = jnp.logical_and(row >= META_ROW0, row < META_ROW0 + META_PAD)
    return jnp.where(is_pad, 0.0, y)


def _first_argmax(x, lane, valid):
    xm = jnp.where(valid, x, NEG_INF)
    mx = jnp.max(xm, axis=1, keepdims=True)
    idx = jnp.min(jnp.where(jnp.logical_and(valid, xm == mx), lane, float(LANES)), axis=1, keepdims=True)
    return mx, idx


TOKEN_ROWS = D_MODEL // LANES
GATHER_PITCH = 24


def _store_token_linear(ref, x, first_token=0):
    n = x.shape[0]
    for k in range(TOKEN_ROWS):
        ref[pl.ds(first_token * TOKEN_ROWS + k, n, stride=TOKEN_ROWS), :] = x[:, k * LANES:(k + 1) * LANES]


def _load_gathered(ref, n):
    return jnp.concatenate([ref[pl.ds(k, n, stride=GATHER_PITCH), :] for k in range(TOKEN_ROWS)], axis=1)


OUT_HALF = -(-TM_MID // 32) * 16
OUT_SUB_BLOCKS = ((0, OUT_HALF), (OUT_HALF, TM_MID))


def _out_router_kernel(yp_ref, ym_ref, yh_ref, h_ref, wo_ref, g_ref, b_ref, wrh_ref, wrl_ref, br_ref,
                       h1_ref, hlin_ref, eid_ref, wts_ref):
    def project(r0, r1):
        y = jnp.concatenate([yp_ref[r0:r1, :], ym_ref[r0:r1, :], yh_ref[r0:r1, :]], axis=1)
        return _dot(y, wo_ref[...])

    acc = project(*OUT_SUB_BLOCKS[0])
    for n, (r0, r1) in enumerate(OUT_SUB_BLOCKS):
        acc_next = project(*OUT_SUB_BLOCKS[n + 1]) if n + 1 < len(OUT_SUB_BLOCKS) else None
        _out_router_rows(r0, r1, acc, h_ref, g_ref, b_ref, wrh_ref, wrl_ref, br_ref,
                         h1_ref, hlin_ref, eid_ref, wts_ref)
        acc = acc_next


def _out_router_rows(r0, r1, acc, h_ref, g_ref, b_ref, wrh_ref, wrl_ref, br_ref, h1_ref, hlin_ref, eid_ref, wts_ref):
    i = pl.program_id(0)
    h1 = _layer_norm_rows(ALPHA * h_ref[r0:r1, :] + acc, g_ref[...], b_ref[...])
    h1 = _zero_pad_rows(h1, i * TM_MID + r0)
    h1_ref[r0:r1, :] = h1
    _store_token_linear(hlin_ref, h1, r0)

    x_hi, x_lo = _split_bf16(h1)
    hi_both = _dot(x_hi, jnp.concatenate([wrh_ref[...], wrl_ref[...]], axis=1))
    logits = hi_both[:, 0:LANES] + hi_both[:, LANES:2 * LANES] + _dot(x_lo, wrh_ref[...]) + br_ref[...]
    lane = lax.broadcasted_iota(jnp.int32, logits.shape, 1).astype(F32)

    is_grp = lane < N_GROUPS
    g_max, g_idx = _first_argmax(logits, lane, is_grp)
    g_exp = jnp.where(is_grp, jnp.exp(logits - g_max), 0.0)
    p_grp = 1.0 / jnp.sum(g_exp, axis=1, keepdims=True)

    e_lo = N_GROUPS + g_idx * EXPERTS_PER_GROUP
    in_grp = jnp.logical_and(lane >= e_lo, lane < e_lo + EXPERTS_PER_GROUP)
    e_max, e1 = _first_argmax(logits, lane, in_grp)
    e_exp = jnp.where(in_grp, jnp.exp(logits - e_max), 0.0)
    p_exp = e_exp / jnp.sum(e_exp, axis=1, keepdims=True)
    p1, _ = _first_argmax(p_exp, lane, in_grp)
    rest = jnp.logical_and(in_grp, lane != e1)
    p2, e2 = _first_argmax(p_exp, lane, rest)
    psum = p1 + p2
    w1 = p_grp * p1 / psum
    w2 = p_grp * p2 / psum
    eid = jnp.where(lane == 0.0, e1 - N_GROUPS, jnp.where(lane == 1.0, e2 - N_GROUPS, 0.0))
    eid_ref[r0:r1, :] = eid.astype(jnp.int32)
    wts_ref[r0:r1, :] = jnp.where(lane == 0.0, w1, jnp.where(lane == 1.0, w2, 0.0))


def _out_router(y_pool, y_m, y_h, h, w_out, g, b, wr_hi, wr_lo, br):
    row = lambda i: (i, 0)
    const2 = lambda i: (0, 0)
    return pl.pallas_call(
        _out_router_kernel,
        grid=(LP // TM_MID,),
        in_specs=[
            pl.BlockSpec((TM_MID, D_POOL), row),
            pl.BlockSpec((TM_MID, D_MLSTM), row),
            pl.BlockSpec((TM_MID, D_HGRN), row),
            pl.BlockSpec((TM_MID, D_MODEL), row),
            pl.BlockSpec((D_MODEL, D_MODEL), const2, pipeline_mode=pl.Buffered(1)),
            pl.BlockSpec((1, D_MODEL), const2),
            pl.BlockSpec((1, D_MODEL), const2),
            pl.BlockSpec((D_MODEL, LANES), const2),
            pl.BlockSpec((D_MODEL, LANES), const2),
            pl.BlockSpec((1, LANES), const2),
        ],
        out_specs=[
            pl.BlockSpec((TM_MID, D_MODEL), row),
            pl.BlockSpec((TM_MID * TOKEN_ROWS, LANES), row),
            pl.BlockSpec((TM_MID, LANES), row),
            pl.BlockSpec((TM_MID, LANES), row),
        ],
        out_shape=[
            jax.ShapeDtypeStruct((LP, D_MODEL), F32),
            jax.ShapeDtypeStruct((LP * TOKEN_ROWS, LANES), F32),
            jax.ShapeDtypeStruct((LP, LANES), jnp.int32),
            jax.ShapeDtypeStruct((LP, LANES), F32),
        ],
        compiler_params=_cparams(("arbitrary",), VMEM_LIMIT_OUT_PROJ),
        name="out_proj_ln_router",
    )(y_pool, y_m, y_h, h, w_out, g, b, wr_hi, wr_lo, br)


ROUTE_ROWS = 256
SORTED_ROWS = N_EXPERT_TILES * TM_EXPERT
GATHER_UNROLL = 8


def _route_kernel(e_ref, dest_ref, tab_ref):
    e = e_ref[...]
    lane = lax.broadcasted_iota(jnp.int32, (ROUTE_ROWS, LANES), 1)
    kk = lax.broadcasted_iota(jnp.int32, (LANES, LANES), 0)
    ll = lax.broadcasted_iota(jnp.int32, (LANES, LANES), 1)
    before_lane = jnp.where(kk < ll, 1.0, 0.0).astype(BF16)
    all_lanes = jnp.ones((LANES, LANES), BF16)
    rr = lax.broadcasted_iota(jnp.int32, (ROUTE_ROWS, ROUTE_ROWS), 0)
    cc = lax.broadcasted_iota(jnp.int32, (ROUTE_ROWS, ROUTE_ROWS), 1)
    before_row = jnp.where(cc < rr, 1.0, 0.0).astype(BF16)

    lane1 = lane[0:1, :]
    rank = jnp.zeros((ROUTE_ROWS, LANES), F32)
    counts = jnp.zeros((1, LANES), F32)
    masks = []
    for x in range(N_EXPERTS):
        m = jnp.where(e == x, 1.0, 0.0)
        mb = m.astype(BF16)
        in_row = _dot(mb, before_lane)
        row_tot = _dot(mb, all_lanes)
        rows_before = _dot(before_row, row_tot.astype(BF16))
        rank = rank + m * (in_row + rows_before)
        total = rows_before[ROUTE_ROWS - 1:ROUTE_ROWS, :] + row_tot[ROUTE_ROWS - 1:ROUTE_ROWS, :]
        counts = counts + jnp.where(lane1 == x, total, 0.0)
        masks.append(m)

    padded = jnp.floor((counts + (TM_EXPERT - 1)) * (1.0 / TM_EXPERT)) * TM_EXPERT
    p_hi, p_lo = _split_bf16(padded)
    start = _dot(p_hi, before_lane) + _dot(p_lo, before_lane)
    end = start + padded

    dest = rank
    tile0 = (lane1 * TM_EXPERT).astype(F32)
    n_before = jnp.zeros((1, LANES), F32)
    for x in range(N_EXPERTS):
        dest = dest + masks[x] * start[:, x:x + 1]
        n_before = n_before + jnp.where(end[:, x:x + 1] <= tile0, 1.0, 0.0)
    dest_ref[...] = dest.astype(jnp.int32)

    tile_expert = jnp.minimum(n_before, float(N_EXPERTS - 1))
    n_tiles = end[:, N_EXPERTS - 1:N_EXPERTS] * (1.0 / TM_EXPERT)
    rows_left = jnp.zeros((1, LANES), F32)
    for x in range(N_EXPERTS):
        rows_left = rows_left + jnp.where(tile_expert == x, counts[:, x:x + 1] + start[:, x:x + 1] - tile0, 0.0)
    tile_rows = jnp.clip(rows_left, 0.0, float(TM_EXPERT))
    row = lax.broadcasted_iota(jnp.int32, (8, LANES), 0)
    tab = jnp.where(row == 0, tile_expert, jnp.where(row == 1, n_tiles, jnp.where(row == 2, tile_rows, 0.0)))
    tab_ref[...] = tab.astype(jnp.int32)


def _route(e_grid):
    return pl.pallas_call(
        _route_kernel,
        out_shape=[
            jax.ShapeDtypeStruct((ROUTE_ROWS, LANES), jnp.int32),
            jax.ShapeDtypeStruct((8, LANES), jnp.int32),
        ],
        compiler_params=pltpu.CompilerParams(vmem_limit_bytes=VMEM_LIMIT),
        name="moe_route",
    )(e_grid)


def _invert_kernel(dest_ref, tok_ref):
    def zero(p, carry):
        tok_ref[p] = 0
        return carry
    lax.fori_loop(0, SORTED_ROWS, zero, 0, unroll=4 * GATHER_UNROLL)

    def place_row(r, carry):
        first_token = r * (LANES // 2)
        for c in range(LANES):
            tok_ref[dest_ref[r, c]] = first_token + c // 2
        return carry
    lax.fori_loop(0, N_ASSIGN // LANES, place_row, 0)


def _invert(dest):
    used_rows = -(-N_ASSIGN // (8 * LANES)) * 8
    return pl.pallas_call(
        _invert_kernel,
        grid=(1,),
        in_specs=[pl.BlockSpec((used_rows, LANES), lambda i: (0, 0), memory_space=pltpu.SMEM)],
        out_specs=pl.BlockSpec((SORTED_ROWS,), lambda i: (0,), memory_space=pltpu.SMEM),
        out_shape=jax.ShapeDtypeStruct((SORTED_ROWS,), jnp.int32),
        name="moe_invert",
    )(dest)


EXPERT_ROW_STEP = 64
GATHER_PRIORITY = 0
WEIGHT_PRIORITY = 1


def _expert_kernel(layer, tile_expert, tile_rows, n_tiles, tok_cur_ref, tok_next_ref, h_hbm, w1_hbm, w3_hbm, w2_hbm,
                   y_ref, xbuf, w1f, w3f, w2f, w1b, w3b, w2b, gsem, wsem, wslot_ref):
    i = pl.program_id(0)
    n_used = n_tiles[0]
    slot = i % 2

    def weight_copies(expert, ws):
        return [pltpu.make_async_copy(src.at[layer, expert], dst.at[ws], wsem.at[ws])
                for src, dst in ((w1_hbm, w1f), (w3_hbm, w3f), (w2_hbm, w2f))]

    def n_groups(tile):
        return lax.shift_right_logical(tile_rows[tile] + (GATHER_UNROLL - 1), GATHER_UNROLL.bit_length() - 1)

    def start_gather(tok_ref, s, groups):
        def body(g, carry):
            for u in range(GATHER_UNROLL):
                r = g * GATHER_UNROLL + u
                src = h_hbm.at[pl.ds(pl.multiple_of(tok_ref[0, 0, r] * TOKEN_ROWS, TOKEN_ROWS), TOKEN_ROWS)]
                dst = xbuf.at[s, pl.ds(pl.multiple_of(r * GATHER_PITCH, 8), TOKEN_ROWS)]
                pltpu.make_async_copy(src, dst, gsem.at[s]).start(priority=GATHER_PRIORITY)
            return carry
        lax.fori_loop(0, groups, body, 0)

    def wait_gather(s, groups):
        n = GATHER_UNROLL * TOKEN_ROWS

        def body(g, carry):
            pltpu.make_async_copy(h_hbm.at[pl.ds(0, n)], xbuf.at[s, pl.ds(0, n)], gsem.at[s]).wait()
            return carry
        lax.fori_loop(0, groups, body, 0)

    last_tile = N_EXPERT_TILES - 1

    @pl.when(i == 0)
    def _():
        xbuf[...] = jnp.zeros(xbuf.shape, F32)
        start_gather(tok_cur_ref, 0, n_groups(0))
        wslot_ref[0] = 0
        for cp in weight_copies(tile_expert[0], 0):
            cp.start(priority=WEIGHT_PRIORITY)

    @pl.when(i < n_used)
    def _():
        wait_gather(slot, n_groups(i))
        start_gather(tok_next_ref, 1 - slot, n_groups(jnp.minimum(i + 1, last_tile)))
        expert = tile_expert[i]
        first_of_expert = jnp.logical_or(i == 0, expert != tile_expert[jnp.maximum(i - 1, 0)])

        @pl.when(first_of_expert)
        def _():
            ws = wslot_ref[0]
            for cp in weight_copies(expert, ws):
                cp.wait()
            nxt = lax.while_loop(
                lambda j: jnp.logical_and(j < n_used, tile_expert[jnp.minimum(j, last_tile)] == expert),
                lambda j: j + 1, i + 1)

            @pl.when(nxt < n_used)
            def _():
                for cp in weight_copies(tile_expert[jnp.minimum(nxt, last_tile)], 1 - ws):
                    cp.start(priority=WEIGHT_PRIORITY)

            w1b[...] = w1f[ws].astype(BF16)
            w3b[...] = w3f[ws].astype(BF16)
            w2b[...] = w2f[ws].astype(BF16)
            wslot_ref[0] = 1 - ws

        rows_here = tile_rows[i]
        for m in range(EXPERT_ROW_STEP, TM_EXPERT + 1, EXPERT_ROW_STEP):
            @pl.when(jnp.logical_and(rows_here > m - EXPERT_ROW_STEP, rows_here <= m))
            def _(m=m):
                x = _load_gathered(xbuf.at[slot], m).astype(BF16)
                a = _dot(x, w1b[...])
                g = _dot(x, w3b[...])
                he = (a * _sigmoid(a) * g).astype(BF16)
                _store_token_linear(y_ref, _dot(he, w2b[...]))
                if m < TM_EXPERT:
                    y_ref[m * TOKEN_ROWS:, :] = jnp.zeros(((TM_EXPERT - m) * TOKEN_ROWS, LANES), F32)

    @pl.when(i >= n_used)
    def _():
        y_ref[...] = jnp.zeros(y_ref.shape, F32)


def _experts(h1, w1, w3, w2, layer, tile_expert, tile_rows, n_tiles, row_token):
    def cur(i, te, tr, nt):
        return (i, 0, 0)

    def nxt(i, te, tr, nt):
        return (jnp.minimum(i + 1, N_EXPERT_TILES - 1), 0, 0)

    grid_spec = pltpu.PrefetchScalarGridSpec(
        num_scalar_prefetch=3,
        grid=(N_EXPERT_TILES,),
        in_specs=[
            pl.BlockSpec((1, 1, TM_EXPERT), cur, memory_space=pltpu.SMEM),
            pl.BlockSpec((1, 1, TM_EXPERT), nxt, memory_space=pltpu.SMEM),
            pl.BlockSpec(memory_space=pl.ANY),
            pl.BlockSpec(memory_space=pl.ANY),
            pl.BlockSpec(memory_space=pl.ANY),
            pl.BlockSpec(memory_space=pl.ANY),
        ],
        out_specs=pl.BlockSpec((TM_EXPERT * TOKEN_ROWS, LANES), lambda i, te, tr, nt: (i, 0)),
        scratch_shapes=[
            pltpu.VMEM((2, TM_EXPERT * GATHER_PITCH, LANES), F32),
            pltpu.VMEM((2, D_MODEL, D_EXPERT), F32),
            pltpu.VMEM((2, D_MODEL, D_EXPERT), F32),
            pltpu.VMEM((2, D_EXPERT, D_MODEL), F32),
            pltpu.VMEM((D_MODEL, D_EXPERT), BF16),
            pltpu.VMEM((D_MODEL, D_EXPERT), BF16),
            pltpu.VMEM((D_EXPERT, D_MODEL), BF16),
            pltpu.SemaphoreType.DMA((2,)),
            pltpu.SemaphoreType.DMA((2,)),
            pltpu.SMEM((1,), jnp.int32),
        ],
    )
    return pl.pallas_call(
        functools.partial(_expert_kernel, layer),
        grid_spec=grid_spec,
        out_shape=jax.ShapeDtypeStruct((SORTED_ROWS * TOKEN_ROWS, LANES), F32),
        compiler_params=_cparams(("arbitrary",)),
        name="moe_experts",
    )(tile_expert, tile_rows, n_tiles, row_token, row_token, h1, w1, w3, w2)


def _combine_kernel(tm, n_steps, final, pos_cur_ref, pos_next_ref, h1_ref, wts_ref, g_ref, b_ref, y_hbm,
                    *rest):
    out_refs, (ybuf, sem) = rest[:-2], rest[-2:]
    i = pl.program_id(0)
    slot = i % 2

    def start_gather(pos_ref, s):
        def body(r, carry):
            for j in range(2):
                row0 = pl.multiple_of(pos_ref[0, 0, 2 * r + j] * TOKEN_ROWS, TOKEN_ROWS)
                dst = ybuf.at[s, j, pl.ds(pl.multiple_of(r * GATHER_PITCH, 8), TOKEN_ROWS)]
                pltpu.make_async_copy(y_hbm.at[pl.ds(row0, TOKEN_ROWS)], dst, sem.at[s]).start()
            return carry
        lax.fori_loop(0, tm, body, 0, unroll=GATHER_UNROLL)

    @pl.when(i == 0)
    def _():
        start_gather(pos_cur_ref, 0)

    for j in range(2):
        n = tm * TOKEN_ROWS
        pltpu.make_async_copy(y_hbm.at[pl.ds(0, n)], ybuf.at[slot, j, pl.ds(0, n)], sem.at[slot]).wait()

    @pl.when(i + 1 < n_steps)
    def _():
        start_gather(pos_next_ref, 1 - slot)

    wts = wts_ref[...]
    y = wts[:, 0:1] * _load_gathered(ybuf.at[slot, 0], tm) + wts[:, 1:2] * _load_gathered(ybuf.at[slot, 1], tm)
    h2 = _layer_norm_rows(ALPHA * h1_ref[...] + y, g_ref[...], b_ref[...])
    if final:
        out_refs[0][...] = h2
    else:
        h2 = _zero_pad_rows(h2, i * tm)
        out_refs[0][...] = h2
        out_refs[1][...] = h2.astype(BF16)


def _combine(h1, y_sorted, pos, wts, g, b, final):
    tm = TM_FINAL if final else TM_LN
    rows = SEQ if final else LP
    n_steps = rows // tm
    pos3 = pos[:2 * rows].reshape(n_steps, 1, 2 * tm)
    row = lambda i: (i, 0)
    const2 = lambda i: (0, 0)
    if final:
        out_specs = [pl.BlockSpec((tm, D_MODEL), row)]
        out_shape = [jax.ShapeDtypeStruct((rows, D_MODEL), F32)]
    else:
        out_specs = [pl.BlockSpec((tm, D_MODEL), row), pl.BlockSpec((tm, D_MODEL), row)]
        out_shape = [jax.ShapeDtypeStruct((rows, D_MODEL), F32), jax.ShapeDtypeStruct((rows, D_MODEL), BF16)]
    return pl.pallas_call(
        functools.partial(_combine_kernel, tm, n_steps, final),
        grid=(n_steps,),
        in_specs=[
            pl.BlockSpec((1, 1, 2 * tm), lambda i: (i, 0, 0), memory_space=pltpu.SMEM),
            pl.BlockSpec((1, 1, 2 * tm), lambda i: (jnp.minimum(i + 1, n_steps - 1), 0, 0), memory_space=pltpu.SMEM),
            pl.BlockSpec((tm, D_MODEL), row),
            pl.BlockSpec((tm, LANES), row),
            pl.BlockSpec((1, D_MODEL), const2),
            pl.BlockSpec((1, D_MODEL), const2),
            pl.BlockSpec(memory_space=pl.ANY),
        ],
        out_specs=out_specs,
        out_shape=out_shape,
        scratch_shapes=[pltpu.VMEM((2, 2, tm * GATHER_PITCH, LANES), F32), pltpu.SemaphoreType.DMA((2,))],
        compiler_params=_cparams(("arbitrary",)),
        name="moe_combine_ln_final" if final else "moe_combine_ln",
    )(pos3, pos3, h1, wts, g, b, y_sorted)


def _pad_lanes(v):
    return jnp.pad(v, ((0, 0), (0, LANES - v.shape[1])))


def kernel(x, meta_tokens, emb_ln_g, emb_ln_b, hgrn_lb_logits, w_in, conv_w, conv_b, ig_b, fg_b,
           mlstm_norm_g, pool_w, pool_scale, hgrn_norm_g, w_out, ln1_g, ln1_b,
           w_router_group, b_router_group, w_router_expert, b_router_expert, w1, w3, w2,
           ln2_g, ln2_b):
    assert x.shape == (1, SEQ, D_MODEL) and x.dtype == F32
    row2 = lambda v: v.reshape(1, -1)
    meta_blk = jnp.pad(meta_tokens.astype(F32), ((META_PAD, 0), (0, 0)))
    h, hb = _embed(x.reshape(SEQ, D_MODEL), meta_blk, row2(emb_ln_g), row2(emb_ln_b))

    out = None
    for l in range(DEPTH):
        p = _in_proj(hb, w_in, l)
        y_pool = _pool(p, pool_w[l], row2(pool_scale[l]))
        gate_bias = _pad_lanes(jnp.concatenate([ig_b[l], fg_b[l]]).reshape(1, -1))
        y_m = _mlstm(p, conv_w[l], row2(conv_b[l]), gate_bias, row2(mlstm_norm_g[l]))
        y_h = _hgrn(p, hgrn_lb_logits, row2(hgrn_norm_g[l]), l)

        w_r = _pad_lanes(jnp.concatenate([w_router_group[l], w_router_expert[l]], axis=1))
        wr_hi, wr_lo = _split_bf16(w_r)
        b_r = _pad_lanes(jnp.concatenate([b_router_group[l], b_router_expert[l]]).reshape(1, -1))
        h1, h1_lin, eid, wts = _out_router(y_pool, y_m, y_h, h, w_out[l].astype(BF16), row2(ln1_g[l]),
                                           row2(ln1_b[l]), wr_hi, wr_lo, b_r)

        e_flat = jnp.pad(eid[:, 0:2].reshape(-1), (0, ROUTE_ROWS * LANES - N_ASSIGN), constant_values=-1)
        dest, tab = _route(e_flat.reshape(ROUTE_ROWS, LANES))
        row_token = _invert(dest).reshape(N_EXPERT_TILES, 1, TM_EXPERT)
        y_sorted = _experts(h1_lin, w1, w3, w2, l, tab[0, :N_EXPERT_TILES], tab[2, :N_EXPERT_TILES], tab[1, :1],
                            row_token)
        pos = dest.reshape(-1)
        if l + 1 < DEPTH:
            h, hb = _combine(h1, y_sorted, pos, wts, row2(ln2_g[l]), row2(ln2_b[l]), final=False)
        else:
            (out,) = _combine(h1, y_sorted, pos, wts, row2(ln2_g[l]), row2(ln2_b[l]), final=True)
    return out.reshape(1, SEQ, D_MODEL)
```

```python
import functools

import jax
import jax.numpy as jnp
import numpy as np
from jax import lax
from jax.experimental import pallas as pl
from jax.experimental.pallas import tpu as pltpu

F32 = jnp.float32
BF16 = jnp.bfloat16

D_MODEL = 2048
SEQ = 8192
DEPTH = 2
N_META = 16
CHUNK = 64
D_POOL = D_MODEL // 4
POOL_WINDOWS = (2, 4, 8, 16)
POOL_GROUP = D_POOL // len(POOL_WINDOWS)
D_MLSTM = 3 * D_MODEL // 8
N_HEADS = 6
HEAD_DIM = D_MLSTM // N_HEADS
CONV_K = 4
D_HGRN = D_MODEL - D_POOL - D_MLSTM
N_GROUPS = 4
EXPERTS_PER_GROUP = 8
N_EXPERTS = N_GROUPS * EXPERTS_PER_GROUP
D_EXPERT = D_MODEL // 4
ALPHA = (2 * DEPTH) ** 0.25
LN_EPS = 1e-5
NEG_INF = float("-inf")

LANES = 128
V7X_VMEM_BYTES = 64 * 1024 * 1024
LP = SEQ + CHUNK
META_ROW0 = SEQ
META_PAD = CHUNK - N_META
TM_EMBED = 512
LP_EMBED = (SEQ // TM_EMBED + 1) * TM_EMBED

COL_MQK = 0
COL_MV = 2 * D_MLSTM
COL_MO = COL_MV + D_MLSTM
COL_HQ = COL_MO + D_MLSTM
COL_HF = COL_HQ + D_HGRN
COL_HI = COL_HF + D_HGRN
COL_HG = COL_HI + D_HGRN
COL_POOL = COL_HG + D_HGRN
COL_GATE = COL_POOL + D_POOL
P_COLS = COL_GATE + 2 * LANES

TM_BIG = 2752
TN_IN = 256
TM_MID = 688
TM_LN = 192
TM_FINAL = 256
TM_EXPERT = 256
N_ASSIGN = 2 * LP
N_EXPERT_TILES = (N_ASSIGN + N_EXPERTS * (TM_EXPERT - 1)) // TM_EXPERT + 1
VMEM_LIMIT = V7X_VMEM_BYTES * 7 // 8
VMEM_LIMIT_OUT_PROJ = V7X_VMEM_BYTES * 31 // 32


def _cparams(sem, vmem_limit=VMEM_LIMIT):
    return pltpu.CompilerParams(dimension_semantics=sem, vmem_limit_bytes=vmem_limit)


def _sigmoid(x):
    return 1.0 / (1.0 + jnp.exp(-x))


def _log_sigmoid(x):
    return jnp.minimum(x, 0.0) - jnp.log1p(jnp.exp(-jnp.abs(x)))


def _layer_norm_rows(x, g, b):
    mu = jnp.mean(x, axis=-1, keepdims=True)
    xc = x - mu
    var = jnp.mean(xc * xc, axis=-1, keepdims=True)
    return xc * lax.rsqrt(var + LN_EPS) * g + b


def _dot(a, b):
    return jnp.dot(a, b, preferred_element_type=F32)


def _dot_nt(a, b):
    return lax.dot_general(a, b, (((1,), (1,)), ((), ())), preferred_element_type=F32)


def _dot_tn(a, b):
    return lax.dot_general(a, b, (((0,), (0,)), ((), ())), preferred_element_type=F32)


def _split_bf16(x):
    hi = x.astype(BF16)
    lo = (x - hi.astype(F32)).astype(BF16)
    return hi, lo


def _chunk_cumsum(x):
    r = lax.broadcasted_iota(jnp.int32, (CHUNK, CHUNK), 0)
    c = lax.broadcasted_iota(jnp.int32, (CHUNK, CHUNK), 1)
    tri = jnp.where(r >= c, 1.0, 0.0).astype(BF16)
    hi, lo = _split_bf16(x)
    return _dot(tri, hi) + _dot(tri, lo)


def _embed_kernel(x_ref, meta_ref, g_ref, b_ref, h_ref, hb_ref):
    i = pl.program_id(0)
    g = g_ref[...]
    b = b_ref[...]

    @pl.when(i < SEQ // TM_EMBED)
    def _():
        y = _layer_norm_rows(x_ref[...], g, b)
        h_ref[...] = y
        hb_ref[...] = y.astype(BF16)

    @pl.when(i == SEQ // TM_EMBED)
    def _():
        y = _layer_norm_rows(meta_ref[...], g, b)
        row = lax.broadcasted_iota(jnp.int32, (CHUNK, D_MODEL), 0)
        y = jnp.where(row >= META_PAD, y, 0.0)
        h_ref[0:CHUNK, :] = y
        hb_ref[0:CHUNK, :] = y.astype(BF16)
        h_ref[CHUNK:TM_EMBED, :] = jnp.zeros((TM_EMBED - CHUNK, D_MODEL), F32)
        hb_ref[CHUNK:TM_EMBED, :] = jnp.zeros((TM_EMBED - CHUNK, D_MODEL), BF16)


def _embed(x2d, meta_blk, g, b):
    nx = SEQ // TM_EMBED
    return pl.pallas_call(
        _embed_kernel,
        grid=(nx + 1,),
        in_specs=[
            pl.BlockSpec((TM_EMBED, D_MODEL), lambda i: (jnp.minimum(i, nx - 1), 0)),
            pl.BlockSpec((CHUNK, D_MODEL), lambda i: (0, 0)),
            pl.BlockSpec((1, D_MODEL), lambda i: (0, 0)),
            pl.BlockSpec((1, D_MODEL), lambda i: (0, 0)),
        ],
        out_specs=[
            pl.BlockSpec((TM_EMBED, D_MODEL), lambda i: (i, 0)),
            pl.BlockSpec((TM_EMBED, D_MODEL), lambda i: (i, 0)),
        ],
        out_shape=[
            jax.ShapeDtypeStruct((LP_EMBED, D_MODEL), F32),
            jax.ShapeDtypeStruct((LP_EMBED, D_MODEL), BF16),
        ],
        compiler_params=_cparams(("arbitrary",)),
        name="embed_ln",
    )(x2d, meta_blk, g, b)


K_TILES = D_MODEL // LANES
W_IN_MLSTM0 = D_POOL
W_IN_GATE0 = D_POOL + 4 * D_MLSTM
W_IN_HGRN0 = W_IN_GATE0 + 2 * N_HEADS
D_IN = W_IN_HGRN0 + 4 * D_HGRN


def _in_proj_kernel(layer, x_ref, w_ref, o_ref):
    w_t = jnp.concatenate([w_ref[:, DEPTH * kt + layer, :] for kt in range(K_TILES)], axis=1)
    o_ref[...] = _dot_nt(x_ref[...], w_t.astype(BF16))


def _in_proj(hb, w_in, layer):
    w_view = w_in.reshape(DEPTH, K_TILES, LANES, D_IN).transpose(3, 1, 0, 2).reshape(D_IN, K_TILES * DEPTH, LANES)

    def first_col(j):
        c = j * TN_IN
        return jnp.where(c < COL_HQ, W_IN_MLSTM0 + c,
                         jnp.where(c < COL_POOL, W_IN_HGRN0 + (c - COL_HQ),
                                   jnp.where(c < COL_GATE, c - COL_POOL, W_IN_GATE0)))

    w_block = (pl.Element(TN_IN), pl.Element(K_TILES * DEPTH), pl.Element(LANES))
    return pl.pallas_call(
        functools.partial(_in_proj_kernel, layer),
        grid=(LP // TM_BIG, P_COLS // TN_IN),
        in_specs=[
            pl.BlockSpec((TM_BIG, D_MODEL), lambda i, j: (i, 0)),
            pl.BlockSpec(w_block, lambda i, j: (first_col(j), 0, 0)),
        ],
        out_specs=pl.BlockSpec((TM_BIG, TN_IN), lambda i, j: (i, j)),
        out_shape=jax.ShapeDtypeStruct((LP, P_COLS), F32),
        compiler_params=_cparams(("arbitrary", "arbitrary")),
        name="in_proj",
    )(hb, w_view)


POOL_HALO = 16


def _pool_kernel(u_ref, halo_ref, pw_ref, ps_ref, y_ref, ubuf):
    i = pl.program_id(0)
    u = u_ref[...]
    ubuf[0:POOL_HALO, :] = halo_ref[...]
    ubuf[POOL_HALO:POOL_HALO + TM_MID, :] = u
    row = i * TM_MID + lax.broadcasted_iota(jnp.int32, (TM_MID, POOL_GROUP), 0)
    pos = jnp.where(row >= META_ROW0 + META_PAD, row - (META_ROW0 + META_PAD) + 1, 2 * POOL_HALO)
    scale = ps_ref[...]
    for gi, w in enumerate(POOL_WINDOWS):
        cs = slice(gi * POOL_GROUP, (gi + 1) * POOL_GROUP)
        ug = u[:, cs]
        acc = ug
        for j in range(1, w):
            acc = acc + ubuf[POOL_HALO - j:POOL_HALO - j + TM_MID, cs]
        div = jnp.minimum(pos, w).astype(F32)
        d = acc / div - ug
        yg = _dot(d.astype(BF16), pw_ref[gi].astype(BF16))
        y_ref[:, cs] = (yg * scale[:, cs]).astype(BF16)


def _pool(p, pool_w, pool_scale):
    blocks_per_tile = TM_MID // POOL_HALO
    last_meta_block = (LP - POOL_HALO) // POOL_HALO

    def halo_map(i):
        return (jnp.where(i == 0, last_meta_block, i * blocks_per_tile - 1), COL_POOL // D_POOL)

    return pl.pallas_call(
        _pool_kernel,
        grid=(LP // TM_MID,),
        in_specs=[
            pl.BlockSpec((TM_MID, D_POOL), lambda i: (i, COL_POOL // D_POOL)),
            pl.BlockSpec((POOL_HALO, D_POOL), halo_map),
            pl.BlockSpec((len(POOL_WINDOWS), POOL_GROUP, POOL_GROUP), lambda i: (0, 0, 0)),
            pl.BlockSpec((1, D_POOL), lambda i: (0, 0)),
        ],
        out_specs=pl.BlockSpec((TM_MID, D_POOL), lambda i: (i, 0)),
        out_shape=jax.ShapeDtypeStruct((LP, D_POOL), BF16),
        scratch_shapes=[pltpu.VMEM((POOL_HALO + TM_MID, D_POOL), F32)],
        compiler_params=_cparams(("arbitrary",)),
        name="pool_mixer",
    )(p, p, pool_w, pool_scale)


MLSTM_CHUNKS_PER_STEP = 8
HGRN_CHUNKS_PER_STEP = 8


def _step_block(c, chunks_per_step):
    return jnp.where(c == 0, SEQ // (chunks_per_step * CHUNK), c - 1)


CONV_HALO = 8


def _mlstm_kernel(mqk_ref, mv_ref, mo_ref, gt_ref, cw_ref, cb_ref, gb_ref, ng_ref, y_ref,
                  s_sc, m_sc, xbuf):
    c = pl.program_id(0)

    @pl.when(c == 0)
    def _():
        s_sc[...] = jnp.zeros(s_sc.shape, F32)
        m_sc[...] = jnp.full(m_sc.shape, NEG_INF, F32)
        xbuf[0:CONV_HALO, :] = jnp.zeros((CONV_HALO, 2 * D_MLSTM), F32)
        _mlstm_chunk(0, True, mqk_ref, mv_ref, mo_ref, gt_ref, cw_ref, cb_ref, gb_ref, ng_ref, y_ref,
                     s_sc, m_sc, xbuf)

    @pl.when(c > 0)
    def _():
        for g in range(MLSTM_CHUNKS_PER_STEP):
            _mlstm_chunk(g, False, mqk_ref, mv_ref, mo_ref, gt_ref, cw_ref, cb_ref, gb_ref, ng_ref, y_ref,
                         s_sc, m_sc, xbuf)


def _mlstm_chunk(g, is_meta, mqk_ref, mv_ref, mo_ref, gt_ref, cw_ref, cb_ref, gb_ref, ng_ref, y_ref,
                 s_sc, m_sc, xbuf):
    rows = pl.ds(g * CHUNK, CHUNK)
    x = mqk_ref[rows, :]
    xbuf[CONV_HALO:CONV_HALO + CHUNK, :] = x
    cw = cw_ref[...]
    conv = cb_ref[...] + cw[CONV_K - 1:CONV_K, :] * x
    for j in range(CONV_K - 1):
        off = CONV_HALO - (CONV_K - 1) + j
        conv = conv + cw[j:j + 1, :] * xbuf[off:off + CHUNK, :]
    xbuf[0:CONV_HALO, :] = x[CHUNK - CONV_HALO:CHUNK, :]
    qk = conv * _sigmoid(conv)
    v_all = mv_ref[rows, :]
    og_all = _sigmoid(mo_ref[rows, :])
    ng = ng_ref[...]

    z = gt_ref[rows, :] + gb_ref[...]
    if is_meta:
        valid = lax.broadcasted_iota(jnp.int32, (CHUNK, LANES), 0) >= META_PAD
        ig = jnp.where(valid, z, NEG_INF)
        lf = jnp.where(valid, _log_sigmoid(z), 0.0)
    else:
        ig = z
        lf = _log_sigmoid(z)
    g_t = pltpu.roll(_chunk_cumsum(lf), LANES - N_HEADS, axis=1)
    a = ig - g_t
    row = lax.broadcasted_iota(jnp.int32, (CHUNK, LANES), 0)
    a_max = a
    shift = 1
    while shift < CHUNK:
        a_max = jnp.maximum(a_max, jnp.where(row >= shift, pltpu.roll(a_max, shift, axis=0), NEG_INF))
        shift *= 2
    m_prev = m_sc[0:1, :]
    m_t = g_t + jnp.maximum(a_max, m_prev)
    m_ts = jnp.where(m_t == NEG_INF, 0.0, m_t)
    c_t = g_t - m_ts
    inter_all = jnp.exp(g_t + m_prev - m_ts)
    floor_all = jnp.exp(-m_ts)
    g_last = g_t[CHUNK - 1:CHUNK, :]
    m_new = g_last + jnp.maximum(a_max[CHUNK - 1:CHUNK, :], m_prev)
    decay_all = jnp.exp(g_last + m_prev - m_new)
    wexp_all = jnp.exp(g_last + a - m_new)
    m_sc[0:1, :] = m_new
    a_rows = a.T

    r64 = lax.broadcasted_iota(jnp.int32, (CHUNK, CHUNK), 0)
    c64 = lax.broadcasted_iota(jnp.int32, (CHUNK, CHUNK), 1)
    causal = r64 >= c64
    k_scale = HEAD_DIM ** -0.5
    ones_cols = jnp.ones((CHUNK, HEAD_DIM), BF16)
    mean_cols = jnp.full((HEAD_DIM, HEAD_DIM), 1.0 / HEAD_DIM, BF16)

    def row_mean(x):
        hi, lo = _split_bf16(x)
        return _dot(hi, mean_cols) + _dot(lo, mean_cols)

    heads = range(N_HEADS)
    hsl = [slice(h * HEAD_DIM, (h + 1) * HEAD_DIM) for h in heads]
    qb = [qk[:, hsl[h]].astype(BF16) for h in heads]
    k = [qk[:, D_MLSTM + h * HEAD_DIM:D_MLSTM + (h + 1) * HEAD_DIM] * k_scale for h in heads]
    v_aug = [jnp.concatenate([v_all[:, hsl[h]].astype(BF16), ones_cols], axis=1) for h in heads]
    state = [s_sc[h] for h in heads]

    s = []
    for h in heads:
        dexp = jnp.exp(jnp.where(causal, c_t[:, h:h + 1] + a_rows[h:h + 1, :], NEG_INF))
        s.append((_dot_nt(qb[h], k[h].astype(BF16)) * dexp).astype(BF16))
    hh = []
    for h in heads:
        nd = _dot(s[h], v_aug[h]) + inter_all[:, h:h + 1] * _dot(qb[h], state[h].astype(BF16))
        den = nd[:, HEAD_DIM:2 * HEAD_DIM]
        hh.append(nd[:, 0:HEAD_DIM] / jnp.maximum(jnp.abs(den), floor_all[:, h:h + 1]))
    for h in heads:
        wk = (k[h] * wexp_all[:, h:h + 1]).astype(BF16)
        s_sc[h] = decay_all[:, h:h + 1] * state[h] + _dot_tn(wk, v_aug[h])
    hc = [hh[h] - row_mean(hh[h]) for h in heads]
    var = [row_mean(hc[h] * hc[h]) for h in heads]
    for h in heads:
        y = hc[h] * lax.rsqrt(var[h] + LN_EPS) * ng[:, hsl[h]] * og_all[:, hsl[h]]
        y_ref[rows, hsl[h]] = y.astype(BF16)


def _mlstm(p, conv_w, conv_b, gate_bias, norm_g):
    n = MLSTM_CHUNKS_PER_STEP
    step_rows = n * CHUNK

    def col(block_w, start):
        return lambda c: (_step_block(c, n), start // block_w)

    const2 = lambda c: (0, 0)
    return pl.pallas_call(
        _mlstm_kernel,
        grid=(SEQ // step_rows + 1,),
        in_specs=[
            pl.BlockSpec((step_rows, 2 * D_MLSTM), col(2 * D_MLSTM, COL_MQK)),
            pl.BlockSpec((step_rows, D_MLSTM), col(D_MLSTM, COL_MV)),
            pl.BlockSpec((step_rows, D_MLSTM), col(D_MLSTM, COL_MO)),
            pl.BlockSpec((step_rows, LANES), col(LANES, COL_GATE)),
            pl.BlockSpec((CONV_K, 2 * D_MLSTM), const2),
            pl.BlockSpec((1, 2 * D_MLSTM), const2),
            pl.BlockSpec((1, LANES), const2),
            pl.BlockSpec((1, D_MLSTM), const2),
        ],
        out_specs=pl.BlockSpec((step_rows, D_MLSTM), lambda c: (_step_block(c, n), 0)),
        out_shape=jax.ShapeDtypeStruct((LP, D_MLSTM), BF16),
        scratch_shapes=[
            pltpu.VMEM((N_HEADS, HEAD_DIM, 2 * HEAD_DIM), F32),
            pltpu.VMEM((8, LANES), F32),
            pltpu.VMEM((CONV_HALO + CHUNK, 2 * D_MLSTM), F32),
        ],
        compiler_params=_cparams(("arbitrary",)),
        name="mlstm_mixer",
    )(p, p, p, p, conv_w, conv_b, gate_bias, norm_g)


N_LEVELS = 6
LOG2_E = 1.4426950408889634


def _hgrn_tables():
    t = np.arange(CHUNK)
    sel = np.zeros((N_LEVELS * CHUNK, CHUNK), np.float32)
    mask = np.zeros((N_LEVELS + 1, CHUNK, CHUNK), np.float32)
    for l in range(N_LEVELS):
        half = 1 << l
        ref_row = (t // (2 * half)) * (2 * half) + half - 1
        is_upper = (t // half) % 2 == 1
        sign = np.where(is_upper, 1.0, -1.0)
        np.add.at(sel, (l * CHUNK + t, t), sign)
        np.add.at(sel, (l * CHUNK + t, ref_row), -sign)
        same = (t[:, None] // (2 * half)) == (t[None, :] // (2 * half))
        mask[l] = same & is_upper[:, None] & ~is_upper[None, :]
    mask[N_LEVELS] = np.eye(CHUNK)
    return sel, mask


def _hgrn_kernel(layer, hq_ref, hf_ref, hi_ref, hg_ref, lbl_ref, ng_ref, sel_ref, mask_ref, y_ref, st_sc):
    c = pl.program_id(0)

    lbl = lbl_ref[...]
    e = jnp.exp(lbl - jnp.max(lbl, axis=0, keepdims=True))
    sm = e / jnp.sum(e, axis=0, keepdims=True)
    lb = jnp.sum(sm[0:layer + 1, :], axis=0, keepdims=True) - sm[0:1, :]
    refs = (hq_ref, hf_ref, hi_ref, hg_ref, ng_ref, sel_ref, mask_ref)

    @pl.when(c == 0)
    def _():
        st_sc[...] = jnp.zeros(st_sc.shape, F32)
        _hgrn_state(0, _hgrn_intra(0, lb, *refs), y_ref, st_sc)

    @pl.when(c > 0)
    def _():
        for g in range(HGRN_CHUNKS_PER_STEP):
            _hgrn_state(g, _hgrn_intra(g, lb, *refs), y_ref, st_sc)


def _hgrn_intra(g, lb, hq_ref, hf_ref, hi_ref, hg_ref, ng_ref, sel_ref, mask_ref):
    rows = pl.ds(g * CHUNK, CHUNK)
    z = hf_ref[rows, :]
    a = jnp.log(lb)
    bb = jnp.log1p(-lb) + _log_sigmoid(z)
    mx = jnp.maximum(a, bb)
    log_f = mx + jnp.log(jnp.exp(a - mx) + jnp.exp(bb - mx))
    kk = (1.0 - lb) * _sigmoid(-z)
    hq = hq_ref[rows, :]
    q = hq * _sigmoid(hq)
    v = hi_ref[rows, :].astype(BF16)
    hg = hg_ref[rows, :]
    gate = hg * _sigmoid(hg) * ng_ref[...]

    b = _chunk_cumsum(log_f) * LOG2_E
    b_hi, b_lo = _split_bf16(b)
    sel = sel_ref[...]
    diffs = _dot(sel, b_hi) + _dot(sel, b_lo)

    amats = [None] * N_HEADS
    for l in range(N_LEVELS + 1):
        if l < N_LEVELS:
            zl = jnp.exp2(diffs[l * CHUNK:(l + 1) * CHUNK, :])
            ql = (q * zl).astype(BF16)
            kl = (kk * zl).astype(BF16)
        else:
            ql = q.astype(BF16)
            kl = kk.astype(BF16)
        ml = mask_ref[l]
        for h in range(N_HEADS):
            hs = slice(h * HEAD_DIM, (h + 1) * HEAD_DIM)
            part = ml * _dot_nt(ql[:, hs], kl[:, hs])
            amats[h] = part if amats[h] is None else amats[h] + part

    b_last = b[CHUNK - 1:CHUNK, :]
    qe = (q * jnp.exp2(b)).astype(BF16)
    kd = (kk * jnp.exp2(b_last - b)).astype(BF16)
    e_last = jnp.exp2(b_last)
    return [a_h.astype(BF16) for a_h in amats], v, qe, kd, e_last, gate


def _hgrn_state(g, intra, y_ref, st_sc):
    amats, v, qe, kd, e_last, gate = intra
    rows = pl.ds(g * CHUNK, CHUNK)
    for h in range(N_HEADS):
        hs = slice(h * HEAD_DIM, (h + 1) * HEAD_DIM)
        st = st_sc[h]
        o = _dot(amats[h], v[:, hs]) + _dot_nt(qe[:, hs], st.astype(BF16))
        st_sc[h] = e_last[:, hs] * st + _dot_tn(v[:, hs], kd[:, hs])
        o = o * lax.rsqrt(jnp.mean(o * o, axis=1, keepdims=True) + LN_EPS)
        y_ref[rows, hs] = (o * gate[:, hs]).astype(BF16)


def _hgrn(p, lb_logits, norm_g, layer):
    n = HGRN_CHUNKS_PER_STEP
    step_rows = n * CHUNK

    def col(start):
        return lambda c: (_step_block(c, n), start // D_HGRN)

    sel, mask = _hgrn_tables()
    const2 = lambda c: (0, 0)
    return pl.pallas_call(
        functools.partial(_hgrn_kernel, layer),
        grid=(SEQ // step_rows + 1,),
        in_specs=[
            pl.BlockSpec((step_rows, D_HGRN), col(COL_HQ)),
            pl.BlockSpec((step_rows, D_HGRN), col(COL_HF)),
            pl.BlockSpec((step_rows, D_HGRN), col(COL_HI)),
            pl.BlockSpec((step_rows, D_HGRN), col(COL_HG)),
            pl.BlockSpec((DEPTH, D_HGRN), const2),
            pl.BlockSpec((1, D_HGRN), const2),
            pl.BlockSpec((N_LEVELS * CHUNK, CHUNK), const2),
            pl.BlockSpec((N_LEVELS + 1, CHUNK, CHUNK), lambda c: (0, 0, 0)),
        ],
        out_specs=pl.BlockSpec((step_rows, D_HGRN), lambda c: (_step_block(c, n), 0)),
        out_shape=jax.ShapeDtypeStruct((LP, D_HGRN), BF16),
        scratch_shapes=[pltpu.VMEM((N_HEADS, HEAD_DIM, HEAD_DIM), F32)],
        compiler_params=_cparams(("arbitrary",)),
        name="hgrn_mixer",
    )(p, p, p, p, lb_logits, norm_g, jnp.asarray(sel, BF16), jnp.asarray(mask, F32))


def _zero_pad_rows(y, row0):
    row = row0 + lax.broadcasted_iota(jnp.int32, y.shape, 0)
    is_pad = jnp.logical_and(row >= META_ROW0, row < META_ROW0 + META_PAD)
    return jnp.where(is_pad, 0.0, y)


def _first_argmax(x, lane, valid):
    xm = jnp.where(valid, x, NEG_INF)
    mx = jnp.max(xm, axis=1, keepdims=True)
    idx = jnp.min(jnp.where(jnp.logical_and(valid, xm == mx), lane, float(LANES)), axis=1, keepdims=True)
    return mx, idx


TOKEN_ROWS = D_MODEL // LANES
GATHER_PITCH = 24


def _store_token_linear(ref, x, first_token=0):
    n = x.shape[0]
    for k in range(TOKEN_ROWS):
        ref[pl.ds(first_token * TOKEN_ROWS + k, n, stride=TOKEN_ROWS), :] = x[:, k * LANES:(k + 1) * LANES]


def _load_gathered(ref, n):
    return jnp.concatenate([ref[pl.ds(k, n, stride=GATHER_PITCH), :] for k in range(TOKEN_ROWS)], axis=1)


OUT_HALF = -(-TM_MID // 32) * 16
OUT_SUB_BLOCKS = ((0, OUT_HALF), (OUT_HALF, TM_MID))


def _out_router_kernel(yp_ref, ym_ref, yh_ref, h_ref, wo_ref, g_ref, b_ref, wrh_ref, wrl_ref, br_ref,
                       h1_ref, hlin_ref, eid_ref, wts_ref):
    def project(r0, r1):
        y = jnp.concatenate([yp_ref[r0:r1, :], ym_ref[r0:r1, :], yh_ref[r0:r1, :]], axis=1)
        return _dot(y, wo_ref[...])

    acc = project(*OUT_SUB_BLOCKS[0])
    for n, (r0, r1) in enumerate(OUT_SUB_BLOCKS):
        acc_next = project(*OUT_SUB_BLOCKS[n + 1]) if n + 1 < len(OUT_SUB_BLOCKS) else None
        _out_router_rows(r0, r1, acc, h_ref, g_ref, b_ref, wrh_ref, wrl_ref, br_ref,
                         h1_ref, hlin_ref, eid_ref, wts_ref)
        acc = acc_next


def _out_router_rows(r0, r1, acc, h_ref, g_ref, b_ref, wrh_ref, wrl_ref, br_ref, h1_ref, hlin_ref, eid_ref, wts_ref):
    i = pl.program_id(0)
    h1 = _layer_norm_rows(ALPHA * h_ref[r0:r1, :] + acc, g_ref[...], b_ref[...])
    h1 = _zero_pad_rows(h1, i * TM_MID + r0)
    h1_ref[r0:r1, :] = h1
    _store_token_linear(hlin_ref, h1, r0)

    x_hi, x_lo = _split_bf16(h1)
    hi_both = _dot(x_hi, jnp.concatenate([wrh_ref[...], wrl_ref[...]], axis=1))
    logits = hi_both[:, 0:LANES] + hi_both[:, LANES:2 * LANES] + _dot(x_lo, wrh_ref[...]) + br_ref[...]
    lane = lax.broadcasted_iota(jnp.int32, logits.shape, 1).astype(F32)

    is_grp = lane < N_GROUPS
    g_max, g_idx = _first_argmax(logits, lane, is_grp)
    g_exp = jnp.where(is_grp, jnp.exp(logits - g_max), 0.0)
    p_grp = 1.0 / jnp.sum(g_exp, axis=1, keepdims=True)

    e_lo = N_GROUPS + g_idx * EXPERTS_PER_GROUP
    in_grp = jnp.logical_and(lane >= e_lo, lane < e_lo + EXPERTS_PER_GROUP)
    e_max, e1 = _first_argmax(logits, lane, in_grp)
    e_exp = jnp.where(in_grp, jnp.exp(logits - e_max), 0.0)
    p_exp = e_exp / jnp.sum(e_exp, axis=1, keepdims=True)
    p1, _ = _first_argmax(p_exp, lane, in_grp)
    rest = jnp.logical_and(in_grp, lane != e1)
    p2, e2 = _first_argmax(p_exp, lane, rest)
    psum = p1 + p2
    w1 = p_grp * p1 / psum
    w2 = p_grp * p2 / psum
    eid = jnp.where(lane == 0.0, e1 - N_GROUPS, jnp.where(lane == 1.0, e2 - N_GROUPS, 0.0))
    eid_ref[r0:r1, :] = eid.astype(jnp.int32)
    wts_ref[r0:r1, :] = jnp.where(lane == 0.0, w1, jnp.where(lane == 1.0, w2, 0.0))


def _out_router(y_pool, y_m, y_h, h, w_out, g, b, wr_hi, wr_lo, br):
    row = lambda i: (i, 0)
    const2 = lambda i: (0, 0)
    return pl.pallas_call(
        _out_router_kernel,
        grid=(LP // TM_MID,),
        in_specs=[
            pl.BlockSpec((TM_MID, D_POOL), row),
            pl.BlockSpec((TM_MID, D_MLSTM), row),
            pl.BlockSpec((TM_MID, D_HGRN), row),
            pl.BlockSpec((TM_MID, D_MODEL), row),
            pl.BlockSpec((D_MODEL, D_MODEL), const2, pipeline_mode=pl.Buffered(1)),
            pl.BlockSpec((1, D_MODEL), const2),
            pl.BlockSpec((1, D_MODEL), const2),
            pl.BlockSpec((D_MODEL, LANES), const2),
            pl.BlockSpec((D_MODEL, LANES), const2),
            pl.BlockSpec((1, LANES), const2),
        ],
        out_specs=[
            pl.BlockSpec((TM_MID, D_MODEL), row),
            pl.BlockSpec((TM_MID * TOKEN_ROWS, LANES), row),
            pl.BlockSpec((TM_MID, LANES), row),
            pl.BlockSpec((TM_MID, LANES), row),
        ],
        out_shape=[
            jax.ShapeDtypeStruct((LP, D_MODEL), F32),
            jax.ShapeDtypeStruct((LP * TOKEN_ROWS, LANES), F32),
            jax.ShapeDtypeStruct((LP, LANES), jnp.int32),
            jax.ShapeDtypeStruct((LP, LANES), F32),
        ],
        compiler_params=_cparams(("arbitrary",), VMEM_LIMIT_OUT_PROJ),
        name="out_proj_ln_router",
    )(y_pool, y_m, y_h, h, w_out, g, b, wr_hi, wr_lo, br)


ROUTE_ROWS = 256
SORTED_ROWS = N_EXPERT_TILES * TM_EXPERT
GATHER_UNROLL = 8


def _route_kernel(e_ref, dest_ref, tab_ref):
    e = e_ref[...]
    lane = lax.broadcasted_iota(jnp.int32, (ROUTE_ROWS, LANES), 1)
    kk = lax.broadcasted_iota(jnp.int32, (LANES, LANES), 0)
    ll = lax.broadcasted_iota(jnp.int32, (LANES, LANES), 1)
    before_lane = jnp.where(kk < ll, 1.0, 0.0).astype(BF16)
    all_lanes = jnp.ones((LANES, LANES), BF16)
    rr = lax.broadcasted_iota(jnp.int32, (ROUTE_ROWS, ROUTE_ROWS), 0)
    cc = lax.broadcasted_iota(jnp.int32, (ROUTE_ROWS, ROUTE_ROWS), 1)
    before_row = jnp.where(cc < rr, 1.0, 0.0).astype(BF16)

    lane1 = lane[0:1, :]
    rank = jnp.zeros((ROUTE_ROWS, LANES), F32)
    counts = jnp.zeros((1, LANES), F32)
    masks = []
    for x in range(N_EXPERTS):
        m = jnp.where(e == x, 1.0, 0.0)
        mb = m.astype(BF16)
        in_row = _dot(mb, before_lane)
        row_tot = _dot(mb, all_lanes)
        rows_before = _dot(before_row, row_tot.astype(BF16))
        rank = rank + m * (in_row + rows_before)
        total = rows_before[ROUTE_ROWS - 1:ROUTE_ROWS, :] + row_tot[ROUTE_ROWS - 1:ROUTE_ROWS, :]
        counts = counts + jnp.where(lane1 == x, total, 0.0)
        masks.append(m)

    padded = jnp.floor((counts + (TM_EXPERT - 1)) * (1.0 / TM_EXPERT)) * TM_EXPERT
    p_hi, p_lo = _split_bf16(padded)
    start = _dot(p_hi, before_lane) + _dot(p_lo, before_lane)
    end = start + padded

    dest = rank
    tile0 = (lane1 * TM_EXPERT).astype(F32)
    n_before = jnp.zeros((1, LANES), F32)
    for x in range(N_EXPERTS):
        dest = dest + masks[x] * start[:, x:x + 1]
        n_before = n_before + jnp.where(end[:, x:x + 1] <= tile0, 1.0, 0.0)
    dest_ref[...] = dest.astype(jnp.int32)

    tile_expert = jnp.minimum(n_before, float(N_EXPERTS - 1))
    n_tiles = end[:, N_EXPERTS - 1:N_EXPERTS] * (1.0 / TM_EXPERT)
    rows_left = jnp.zeros((1, LANES), F32)
    for x in range(N_EXPERTS):
        rows_left = rows_left + jnp.where(tile_expert == x, counts[:, x:x + 1] + start[:, x:x + 1] - tile0, 0.0)
    tile_rows = jnp.clip(rows_left, 0.0, float(TM_EXPERT))
    row = lax.broadcasted_iota(jnp.int32, (8, LANES), 0)
    tab = jnp.where(row == 0, tile_expert, jnp.where(row == 1, n_tiles, jnp.where(row == 2, tile_rows, 0.0)))
    tab_ref[...] = tab.astype(jnp.int32)


def _route(e_grid):
    return pl.pallas_call(
        _route_kernel,
        out_shape=[
            jax.ShapeDtypeStruct((ROUTE_ROWS, LANES), jnp.int32),
            jax.ShapeDtypeStruct((8, LANES), jnp.int32),
        ],
        compiler_params=pltpu.CompilerParams(vmem_limit_bytes=VMEM_LIMIT),
        name="moe_route",
    )(e_grid)


EXPERT_ROW_STEP = 64
WEIGHT_PRIORITY = 1
DISPATCH_TOKENS = 192
DISPATCH_UNROLL = 8


def _dispatch_kernel(n_steps, pos_ref, h_hbm, x_hbm, sem):
    i = pl.program_id(0)
    slot = i % 2

    def body(g, carry):
        for u in range(DISPATCH_UNROLL):
            r = g * DISPATCH_UNROLL + u
            src = h_hbm.at[pl.ds(pl.multiple_of((i * DISPATCH_TOKENS + r) * TOKEN_ROWS, TOKEN_ROWS), TOKEN_ROWS)]
            for j in range(2):
                row0 = pl.multiple_of(pos_ref[0, 0, 2 * r + j] * TOKEN_ROWS, TOKEN_ROWS)
                pltpu.make_async_copy(src, x_hbm.at[pl.ds(row0, TOKEN_ROWS)], sem.at[slot]).start()
        return carry
    lax.fori_loop(0, DISPATCH_TOKENS // DISPATCH_UNROLL, body, 0)

    def wait_step(s):
        n = 2 * DISPATCH_TOKENS * TOKEN_ROWS
        pltpu.make_async_copy(h_hbm.at[pl.ds(0, n)], x_hbm.at[pl.ds(0, n)], sem.at[s]).wait()

    @pl.when(i > 0)
    def _():
        wait_step(1 - slot)

    @pl.when(i == n_steps - 1)
    def _():
        wait_step(slot)


def _dispatch(h1_lin, pos):
    n_steps = LP // DISPATCH_TOKENS
    pos3 = pos.reshape(n_steps, 1, 2 * DISPATCH_TOKENS)
    return pl.pallas_call(
        functools.partial(_dispatch_kernel, n_steps),
        grid=(n_steps,),
        in_specs=[
            pl.BlockSpec((1, 1, 2 * DISPATCH_TOKENS), lambda i: (i, 0, 0), memory_space=pltpu.SMEM),
            pl.BlockSpec(memory_space=pl.ANY),
        ],
        out_specs=pl.BlockSpec(memory_space=pl.ANY),
        out_shape=jax.ShapeDtypeStruct((SORTED_ROWS * TOKEN_ROWS, LANES), F32),
        scratch_shapes=[pltpu.SemaphoreType.DMA((2,))],
        compiler_params=_cparams(("arbitrary",)),
        name="moe_dispatch",
    )(pos3, h1_lin)


def _expert_kernel(layer, tile_expert, tile_rows, n_tiles, x_ref, w1_hbm, w3_hbm, w2_hbm,
                   y_ref, w1f, w3f, w2f, w1b, w3b, w2b, wsem, wslot_ref):
    i = pl.program_id(0)
    n_used = n_tiles[0]

    def weight_copies(expert, ws):
        return [pltpu.make_async_copy(src.at[layer, expert], dst.at[ws], wsem.at[ws])
                for src, dst in ((w1_hbm, w1f), (w3_hbm, w3f), (w2_hbm, w2f))]

    last_tile = N_EXPERT_TILES - 1

    @pl.when(i == 0)
    def _():
        wslot_ref[0] = 0
        for cp in weight_copies(tile_expert[0], 0):
            cp.start(priority=WEIGHT_PRIORITY)

    @pl.when(i < n_used)
    def _():
        expert = tile_expert[i]
        first_of_expert = jnp.logical_or(i == 0, expert != tile_expert[jnp.maximum(i - 1, 0)])

        @pl.when(first_of_expert)
        def _():
            ws = wslot_ref[0]
            for cp in weight_copies(expert, ws):
                cp.wait()
            nxt = lax.while_loop(
                lambda j: jnp.logical_and(j < n_used, tile_expert[jnp.minimum(j, last_tile)] == expert),
                lambda j: j + 1, i + 1)

            @pl.when(nxt < n_used)
            def _():
                for cp in weight_copies(tile_expert[jnp.minimum(nxt, last_tile)], 1 - ws):
                    cp.start(priority=WEIGHT_PRIORITY)

            w1b[...] = w1f[ws].astype(BF16)
            w3b[...] = w3f[ws].astype(BF16)
            w2b[...] = w2f[ws].astype(BF16)
            wslot_ref[0] = 1 - ws

        rows_here = tile_rows[i]

        def run(m, masked):
            x = jnp.concatenate([x_ref[pl.ds(k, m, stride=TOKEN_ROWS), :] for k in range(TOKEN_ROWS)], axis=1)
            if masked:
                x = jnp.where(lax.broadcasted_iota(jnp.int32, (m, 1), 0) < rows_here, x, 0.0)
            x = x.astype(BF16)
            a = _dot(x, w1b[...])
            g = _dot(x, w3b[...])
            he = (a * _sigmoid(a) * g).astype(BF16)
            _store_token_linear(y_ref, _dot(he, w2b[...]))
            if m < TM_EXPERT:
                y_ref[m * TOKEN_ROWS:, :] = jnp.zeros(((TM_EXPERT - m) * TOKEN_ROWS, LANES), F32)

        @pl.when(rows_here == TM_EXPERT)
        def _():
            run(TM_EXPERT, False)

        for m in range(EXPERT_ROW_STEP, TM_EXPERT + 1, EXPERT_ROW_STEP):
            @pl.when(jnp.logical_and(rows_here > m - EXPERT_ROW_STEP, rows_here <= min(m, TM_EXPERT - 1)))
            def _(m=m):
                run(m, True)

    @pl.when(i >= n_used)
    def _():
        y_ref[...] = jnp.zeros(y_ref.shape, F32)


def _experts(x_sorted, w1, w3, w2, layer, tile_expert, tile_rows, n_tiles):
    def x_map(i, te, tr, nt):
        return (jnp.minimum(i, jnp.maximum(nt[0] - 1, 0)), 0)

    grid_spec = pltpu.PrefetchScalarGridSpec(
        num_scalar_prefetch=3,
        grid=(N_EXPERT_TILES,),
        in_specs=[
            pl.BlockSpec((TM_EXPERT * TOKEN_ROWS, LANES), x_map),
            pl.BlockSpec(memory_space=pl.ANY),
            pl.BlockSpec(memory_space=pl.ANY),
            pl.BlockSpec(memory_space=pl.ANY),
        ],
        out_specs=pl.BlockSpec((TM_EXPERT * TOKEN_ROWS, LANES), lambda i, te, tr, nt: (i, 0)),
        scratch_shapes=[
            pltpu.VMEM((2, D_MODEL, D_EXPERT), F32),
            pltpu.VMEM((2, D_MODEL, D_EXPERT), F32),
            pltpu.VMEM((2, D_EXPERT, D_MODEL), F32),
            pltpu.VMEM((D_MODEL, D_EXPERT), BF16),
            pltpu.VMEM((D_MODEL, D_EXPERT), BF16),
            pltpu.VMEM((D_EXPERT, D_MODEL), BF16),
            pltpu.SemaphoreType.DMA((2,)),
            pltpu.SMEM((1,), jnp.int32),
        ],
    )
    return pl.pallas_call(
        functools.partial(_expert_kernel, layer),
        grid_spec=grid_spec,
        out_shape=jax.ShapeDtypeStruct((SORTED_ROWS * TOKEN_ROWS, LANES), F32),
        compiler_params=_cparams(("arbitrary",)),
        name="moe_experts",
    )(tile_expert, tile_rows, n_tiles, x_sorted, w1, w3, w2)


def _combine_kernel(tm, n_steps, final, pos_cur_ref, pos_next_ref, h1_ref, wts_ref, g_ref, b_ref, y_hbm,
                    *rest):
    out_refs, (ybuf, sem) = rest[:-2], rest[-2:]
    i = pl.program_id(0)
    slot = i % 2

    def start_gather(pos_ref, s):
        def body(r, carry):
            for j in range(2):
                row0 = pl.multiple_of(pos_ref[0, 0, 2 * r + j] * TOKEN_ROWS, TOKEN_ROWS)
                dst = ybuf.at[s, j, pl.ds(pl.multiple_of(r * GATHER_PITCH, 8), TOKEN_ROWS)]
                pltpu.make_async_copy(y_hbm.at[pl.ds(row0, TOKEN_ROWS)], dst, sem.at[s]).start()
            return carry
        lax.fori_loop(0, tm, body, 0, unroll=GATHER_UNROLL)

    @pl.when(i == 0)
    def _():
        start_gather(pos_cur_ref, 0)

    for j in range(2):
        n = tm * TOKEN_ROWS
        pltpu.make_async_copy(y_hbm.at[pl.ds(0, n)], ybuf.at[slot, j, pl.ds(0, n)], sem.at[slot]).wait()

    @pl.when(i + 1 < n_steps)
    def _():
        start_gather(pos_next_ref, 1 - slot)

    wts = wts_ref[...]
    y = wts[:, 0:1] * _load_gathered(ybuf.at[slot, 0], tm) + wts[:, 1:2] * _load_gathered(ybuf.at[slot, 1], tm)
    h2 = _layer_norm_rows(ALPHA * h1_ref[...] + y, g_ref[...], b_ref[...])
    if final:
        out_refs[0][...] = h2
    else:
        h2 = _zero_pad_rows(h2, i * tm)
        out_refs[0][...] = h2
        out_refs[1][...] = h2.astype(BF16)


def _combine(h1, y_sorted, pos, wts, g, b, final):
    tm = TM_FINAL if final else TM_LN
    rows = SEQ if final else LP
    n_steps = rows // tm
    pos3 = pos[:2 * rows].reshape(n_steps, 1, 2 * tm)
    row = lambda i: (i, 0)
    const2 = lambda i: (0, 0)
    if final:
        out_specs = [pl.BlockSpec((tm, D_MODEL), row)]
        out_shape = [jax.ShapeDtypeStruct((rows, D_MODEL), F32)]
    else:
        out_specs = [pl.BlockSpec((tm, D_MODEL), row), pl.BlockSpec((tm, D_MODEL), row)]
        out_shape = [jax.ShapeDtypeStruct((rows, D_MODEL), F32), jax.ShapeDtypeStruct((rows, D_MODEL), BF16)]
    return pl.pallas_call(
        functools.partial(_combine_kernel, tm, n_steps, final),
        grid=(n_steps,),
        in_specs=[
            pl.BlockSpec((1, 1, 2 * tm), lambda i: (i, 0, 0), memory_space=pltpu.SMEM),
            pl.BlockSpec((1, 1, 2 * tm), lambda i: (jnp.minimum(i + 1, n_steps - 1), 0, 0), memory_space=pltpu.SMEM),
            pl.BlockSpec((tm, D_MODEL), row),
            pl.BlockSpec((tm, LANES), row),
            pl.BlockSpec((1, D_MODEL), const2),
            pl.BlockSpec((1, D_MODEL), const2),
            pl.BlockSpec(memory_space=pl.ANY),
        ],
        out_specs=out_specs,
        out_shape=out_shape,
        scratch_shapes=[pltpu.VMEM((2, 2, tm * GATHER_PITCH, LANES), F32), pltpu.SemaphoreType.DMA((2,))],
        compiler_params=_cparams(("arbitrary",)),
        name="moe_combine_ln_final" if final else "moe_combine_ln",
    )(pos3, pos3, h1, wts, g, b, y_sorted)


def _pad_lanes(v):
    return jnp.pad(v, ((0, 0), (0, LANES - v.shape[1])))


def kernel(x, meta_tokens, emb_ln_g, emb_ln_b, hgrn_lb_logits, w_in, conv_w, conv_b, ig_b, fg_b,
           mlstm_norm_g, pool_w, pool_scale, hgrn_norm_g, w_out, ln1_g, ln1_b,
           w_router_group, b_router_group, w_router_expert, b_router_expert, w1, w3, w2,
           ln2_g, ln2_b):
    assert x.shape == (1, SEQ, D_MODEL) and x.dtype == F32
    row2 = lambda v: v.reshape(1, -1)
    meta_blk = jnp.pad(meta_tokens.astype(F32), ((META_PAD, 0), (0, 0)))
    h, hb = _embed(x.reshape(SEQ, D_MODEL), meta_blk, row2(emb_ln_g), row2(emb_ln_b))

    out = None
    for l in range(DEPTH):
        p = _in_proj(hb, w_in, l)
        y_pool = _pool(p, pool_w[l], row2(pool_scale[l]))
        gate_bias = _pad_lanes(jnp.concatenate([ig_b[l], fg_b[l]]).reshape(1, -1))
        y_m = _mlstm(p, conv_w[l], row2(conv_b[l]), gate_bias, row2(mlstm_norm_g[l]))
        y_h = _hgrn(p, hgrn_lb_logits, row2(hgrn_norm_g[l]), l)

        w_r = _pad_lanes(jnp.concatenate([w_router_group[l], w_router_expert[l]], axis=1))
        wr_hi, wr_lo = _split_bf16(w_r)
        b_r = _pad_lanes(jnp.concatenate([b_router_group[l], b_router_expert[l]]).reshape(1, -1))
        h1, h1_lin, eid, wts = _out_router(y_pool, y_m, y_h, h, w_out[l].astype(BF16), row2(ln1_g[l]),
                                           row2(ln1_b[l]), wr_hi, wr_lo, b_r)

        e_flat = jnp.pad(eid[:, 0:2].reshape(-1), (0, ROUTE_ROWS * LANES - N_ASSIGN), constant_values=-1)
        dest, tab = _route(e_flat.reshape(ROUTE_ROWS, LANES))
        pos = dest.reshape(-1)
        x_sorted = _dispatch(h1_lin, pos[:N_ASSIGN])
        y_sorted = _experts(x_sorted, w1, w3, w2, l, tab[0, :N_EXPERT_TILES], tab[2, :N_EXPERT_TILES], tab[1, :1])
        if l + 1 < DEPTH:
            h, hb = _combine(h1, y_sorted, pos, wts, row2(ln2_g[l]), row2(ln2_b[l]), final=False)
        else:
            (out,) = _combine(h1, y_sorted, pos, wts, row2(ln2_g[l]), row2(ln2_b[l]), final=True)
    return out.reshape(1, SEQ, D_MODEL)
```

```python
import functools

import jax
import jax.numpy as jnp
import numpy as np
from jax import lax
from jax.experimental import pallas as pl
from jax.experimental.pallas import tpu as pltpu

F32 = jnp.float32
BF16 = jnp.bfloat16

D_MODEL = 2048
SEQ = 8192
DEPTH = 2
N_META = 16
CHUNK = 64
D_POOL = D_MODEL // 4
POOL_WINDOWS = (2, 4, 8, 16)
POOL_GROUP = D_POOL // len(POOL_WINDOWS)
D_MLSTM = 3 * D_MODEL // 8
N_HEADS = 6
HEAD_DIM = D_MLSTM // N_HEADS
CONV_K = 4
D_HGRN = D_MODEL - D_POOL - D_MLSTM
N_GROUPS = 4
EXPERTS_PER_GROUP = 8
N_EXPERTS = N_GROUPS * EXPERTS_PER_GROUP
D_EXPERT = D_MODEL // 4
ALPHA = (2 * DEPTH) ** 0.25
LN_EPS = 1e-5
NEG_INF = float("-inf")

LANES = 128
V7X_VMEM_BYTES = 64 * 1024 * 1024
LP = SEQ + CHUNK
META_ROW0 = SEQ
META_PAD = CHUNK - N_META
TM_EMBED = 512
LP_EMBED = (SEQ // TM_EMBED + 1) * TM_EMBED

COL_MQK = 0
COL_MV = 2 * D_MLSTM
COL_MO = COL_MV + D_MLSTM
COL_HQ = COL_MO + D_MLSTM
COL_HF = COL_HQ + D_HGRN
COL_HI = COL_HF + D_HGRN
COL_HG = COL_HI + D_HGRN
COL_POOL = COL_HG + D_HGRN
COL_GATE = COL_POOL + D_POOL
P_COLS = COL_GATE + 2 * LANES

TM_BIG = 2752
TN_IN = 256
TM_MID = 688
TM_LN = 192
TM_FINAL = 256
TM_EXPERT = 256
N_ASSIGN = 2 * LP
N_EXPERT_TILES = (N_ASSIGN + N_EXPERTS * (TM_EXPERT - 1)) // TM_EXPERT + 1
VMEM_LIMIT = V7X_VMEM_BYTES * 7 // 8
VMEM_LIMIT_OUT_PROJ = V7X_VMEM_BYTES * 31 // 32


def _cparams(sem, vmem_limit=VMEM_LIMIT):
    return pltpu.CompilerParams(dimension_semantics=sem, vmem_limit_bytes=vmem_limit)


def _sigmoid(x):
    return 1.0 / (1.0 + jnp.exp(-x))


def _log_sigmoid(x):
    return jnp.minimum(x, 0.0) - jnp.log1p(jnp.exp(-jnp.abs(x)))


def _layer_norm_rows(x, g, b):
    mu = jnp.mean(x, axis=-1, keepdims=True)
    xc = x - mu
    var = jnp.mean(xc * xc, axis=-1, keepdims=True)
    return xc * lax.rsqrt(var + LN_EPS) * g + b


def _dot(a, b):
    return jnp.dot(a, b, preferred_element_type=F32)


def _dot_nt(a, b):
    return lax.dot_general(a, b, (((1,), (1,)), ((), ())), preferred_element_type=F32)


def _dot_tn(a, b):
    return lax.dot_general(a, b, (((0,), (0,)), ((), ())), preferred_element_type=F32)


def _split_bf16(x):
    hi = x.astype(BF16)
    lo = (x - hi.astype(F32)).astype(BF16)
    return hi, lo


def _chunk_cumsum(x):
    r = lax.broadcasted_iota(jnp.int32, (CHUNK, CHUNK), 0)
    c = lax.broadcasted_iota(jnp.int32, (CHUNK, CHUNK), 1)
    tri = jnp.where(r >= c, 1.0, 0.0).astype(BF16)
    hi, lo = _split_bf16(x)
    return _dot(tri, hi) + _dot(tri, lo)


def _embed_kernel(x_ref, meta_ref, g_ref, b_ref, h_ref, hb_ref):
    i = pl.program_id(0)
    g = g_ref[...]
    b = b_ref[...]

    @pl.when(i < SEQ // TM_EMBED)
    def _():
        y = _layer_norm_rows(x_ref[...], g, b)
        h_ref[...] = y
        hb_ref[...] = y.astype(BF16)

    @pl.when(i == SEQ // TM_EMBED)
    def _():
        y = _layer_norm_rows(meta_ref[...], g, b)
        row = lax.broadcasted_iota(jnp.int32, (CHUNK, D_MODEL), 0)
        y = jnp.where(row >= META_PAD, y, 0.0)
        h_ref[0:CHUNK, :] = y
        hb_ref[0:CHUNK, :] = y.astype(BF16)
        h_ref[CHUNK:TM_EMBED, :] = jnp.zeros((TM_EMBED - CHUNK, D_MODEL), F32)
        hb_ref[CHUNK:TM_EMBED, :] = jnp.zeros((TM_EMBED - CHUNK, D_MODEL), BF16)


def _embed(x2d, meta_blk, g, b):
    nx = SEQ // TM_EMBED
    return pl.pallas_call(
        _embed_kernel,
        grid=(nx + 1,),
        in_specs=[
            pl.BlockSpec((TM_EMBED, D_MODEL), lambda i: (jnp.minimum(i, nx - 1), 0)),
            pl.BlockSpec((CHUNK, D_MODEL), lambda i: (0, 0)),
            pl.BlockSpec((1, D_MODEL), lambda i: (0, 0)),
            pl.BlockSpec((1, D_MODEL), lambda i: (0, 0)),
        ],
        out_specs=[
            pl.BlockSpec((TM_EMBED, D_MODEL), lambda i: (i, 0)),
            pl.BlockSpec((TM_EMBED, D_MODEL), lambda i: (i, 0)),
        ],
        out_shape=[
            jax.ShapeDtypeStruct((LP_EMBED, D_MODEL), F32),
            jax.ShapeDtypeStruct((LP_EMBED, D_MODEL), BF16),
        ],
        compiler_params=_cparams(("arbitrary",)),
        name="embed_ln",
    )(x2d, meta_blk, g, b)


K_TILES = D_MODEL // LANES
W_IN_MLSTM0 = D_POOL
W_IN_GATE0 = D_POOL + 4 * D_MLSTM
W_IN_HGRN0 = W_IN_GATE0 + 2 * N_HEADS
D_IN = W_IN_HGRN0 + 4 * D_HGRN


def _in_proj_kernel(layer, x_ref, w_ref, o_ref):
    w_t = jnp.concatenate([w_ref[:, DEPTH * kt + layer, :] for kt in range(K_TILES)], axis=1)
    o_ref[...] = _dot_nt(x_ref[...], w_t.astype(BF16))


def _in_proj(hb, w_in, layer):
    w_view = w_in.reshape(DEPTH, K_TILES, LANES, D_IN).transpose(3, 1, 0, 2).reshape(D_IN, K_TILES * DEPTH, LANES)

    def first_col(j):
        c = j * TN_IN
        return jnp.where(c < COL_HQ, W_IN_MLSTM0 + c,
                         jnp.where(c < COL_POOL, W_IN_HGRN0 + (c - COL_HQ),
                                   jnp.where(c < COL_GATE, c - COL_POOL, W_IN_GATE0)))

    w_block = (pl.Element(TN_IN), pl.Element(K_TILES * DEPTH), pl.Element(LANES))
    return pl.pallas_call(
        functools.partial(_in_proj_kernel, layer),
        grid=(LP // TM_BIG, P_COLS // TN_IN),
        in_specs=[
            pl.BlockSpec((TM_BIG, D_MODEL), lambda i, j: (i, 0)),
            pl.BlockSpec(w_block, lambda i, j: (first_col(j), 0, 0)),
        ],
        out_specs=pl.BlockSpec((TM_BIG, TN_IN), lambda i, j: (i, j)),
        out_shape=jax.ShapeDtypeStruct((LP, P_COLS), F32),
        compiler_params=_cparams(("arbitrary", "arbitrary")),
        name="in_proj",
    )(hb, w_view)


POOL_HALO = 16


def _pool_kernel(u_ref, halo_ref, pw_ref, ps_ref, y_ref, ubuf):
    i = pl.program_id(0)
    u = u_ref[...]
    ubuf[0:POOL_HALO, :] = halo_ref[...]
    ubuf[POOL_HALO:POOL_HALO + TM_MID, :] = u
    row = i * TM_MID + lax.broadcasted_iota(jnp.int32, (TM_MID, POOL_GROUP), 0)
    pos = jnp.where(row >= META_ROW0 + META_PAD, row - (META_ROW0 + META_PAD) + 1, 2 * POOL_HALO)
    scale = ps_ref[...]
    for gi, w in enumerate(POOL_WINDOWS):
        cs = slice(gi * POOL_GROUP, (gi + 1) * POOL_GROUP)
        ug = u[:, cs]
        acc = ug
        for j in range(1, w):
            acc = acc + ubuf[POOL_HALO - j:POOL_HALO - j + TM_MID, cs]
        div = jnp.minimum(pos, w).astype(F32)
        d = acc / div - ug
        yg = _dot(d.astype(BF16), pw_ref[gi].astype(BF16))
        y_ref[:, cs] = (yg * scale[:, cs]).astype(BF16)


def _pool(p, pool_w, pool_scale):
    blocks_per_tile = TM_MID // POOL_HALO
    last_meta_block = (LP - POOL_HALO) // POOL_HALO

    def halo_map(i):
        return (jnp.where(i == 0, last_meta_block, i * blocks_per_tile - 1), COL_POOL // D_POOL)

    return pl.pallas_call(
        _pool_kernel,
        grid=(LP // TM_MID,),
        in_specs=[
            pl.BlockSpec((TM_MID, D_POOL), lambda i: (i, COL_POOL // D_POOL)),
            pl.BlockSpec((POOL_HALO, D_POOL), halo_map),
            pl.BlockSpec((len(POOL_WINDOWS), POOL_GROUP, POOL_GROUP), lambda i: (0, 0, 0)),
            pl.BlockSpec((1, D_POOL), lambda i: (0, 0)),
        ],
        out_specs=pl.BlockSpec((TM_MID, D_POOL), lambda i: (i, 0)),
        out_shape=jax.ShapeDtypeStruct((LP, D_POOL), BF16),
        scratch_shapes=[pltpu.VMEM((POOL_HALO + TM_MID, D_POOL), F32)],
        compiler_params=_cparams(("arbitrary",)),
        name="pool_mixer",
    )(p, p, pool_w, pool_scale)


MLSTM_CHUNKS_PER_STEP = 8
HGRN_CHUNKS_PER_STEP = 8


def _step_block(c, chunks_per_step):
    return jnp.where(c == 0, SEQ // (chunks_per_step * CHUNK), c - 1)


CONV_HALO = 8


def _mlstm_kernel(mqk_ref, mv_ref, mo_ref, gt_ref, cw_ref, cb_ref, gb_ref, ng_ref, y_ref,
                  s_sc, m_sc, xbuf):
    c = pl.program_id(0)

    @pl.when(c == 0)
    def _():
        s_sc[...] = jnp.zeros(s_sc.shape, F32)
        m_sc[...] = jnp.full(m_sc.shape, NEG_INF, F32)
        xbuf[0:CONV_HALO, :] = jnp.zeros((CONV_HALO, 2 * D_MLSTM), F32)
        _mlstm_chunk(0, True, mqk_ref, mv_ref, mo_ref, gt_ref, cw_ref, cb_ref, gb_ref, ng_ref, y_ref,
                     s_sc, m_sc, xbuf)

    @pl.when(c > 0)
    def _():
        for g in range(MLSTM_CHUNKS_PER_STEP):
            _mlstm_chunk(g, False, mqk_ref, mv_ref, mo_ref, gt_ref, cw_ref, cb_ref, gb_ref, ng_ref, y_ref,
                         s_sc, m_sc, xbuf)


def _mlstm_chunk(g, is_meta, mqk_ref, mv_ref, mo_ref, gt_ref, cw_ref, cb_ref, gb_ref, ng_ref, y_ref,
                 s_sc, m_sc, xbuf):
    rows = pl.ds(g * CHUNK, CHUNK)
    x = mqk_ref[rows, :]
    xbuf[CONV_HALO:CONV_HALO + CHUNK, :] = x
    cw = cw_ref[...]
    conv = cb_ref[...] + cw[CONV_K - 1:CONV_K, :] * x
    for j in range(CONV_K - 1):
        off = CONV_HALO - (CONV_K - 1) + j
        conv = conv + cw[j:j + 1, :] * xbuf[off:off + CHUNK, :]
    xbuf[0:CONV_HALO, :] = x[CHUNK - CONV_HALO:CHUNK, :]
    qk = conv * _sigmoid(conv)
    v_all = mv_ref[rows, :]
    og_all = _sigmoid(mo_ref[rows, :])
    ng = ng_ref[...]

    z = gt_ref[rows, :] + gb_ref[...]
    if is_meta:
        valid = lax.broadcasted_iota(jnp.int32, (CHUNK, LANES), 0) >= META_PAD
        ig = jnp.where(valid, z, NEG_INF)
        lf = jnp.where(valid, _log_sigmoid(z), 0.0)
    else:
        ig = z
        lf = _log_sigmoid(z)
    g_t = pltpu.roll(_chunk_cumsum(lf), LANES - N_HEADS, axis=1)
    a = ig - g_t
    row = lax.broadcasted_iota(jnp.int32, (CHUNK, LANES), 0)
    a_max = a
    shift = 1
    while shift < CHUNK:
        a_max = jnp.maximum(a_max, jnp.where(row >= shift, pltpu.roll(a_max, shift, axis=0), NEG_INF))
        shift *= 2
    m_prev = m_sc[0:1, :]
    m_t = g_t + jnp.maximum(a_max, m_prev)
    m_ts = jnp.where(m_t == NEG_INF, 0.0, m_t)
    c_t = g_t - m_ts
    inter_all = jnp.exp(g_t + m_prev - m_ts)
    floor_all = jnp.exp(-m_ts)
    g_last = g_t[CHUNK - 1:CHUNK, :]
    m_new = g_last + jnp.maximum(a_max[CHUNK - 1:CHUNK, :], m_prev)
    decay_all = jnp.exp(g_last + m_prev - m_new)
    wexp_all = jnp.exp(g_last + a - m_new)
    m_sc[0:1, :] = m_new
    a_rows = a.T

    r64 = lax.broadcasted_iota(jnp.int32, (CHUNK, CHUNK), 0)
    c64 = lax.broadcasted_iota(jnp.int32, (CHUNK, CHUNK), 1)
    causal = r64 >= c64
    k_scale = HEAD_DIM ** -0.5
    ones_cols = jnp.ones((CHUNK, HEAD_DIM), BF16)
    mean_cols = jnp.full((HEAD_DIM, HEAD_DIM), 1.0 / HEAD_DIM, BF16)

    def row_mean(x):
        hi, lo = _split_bf16(x)
        return _dot(hi, mean_cols) + _dot(lo, mean_cols)

    heads = range(N_HEADS)
    hsl = [slice(h * HEAD_DIM, (h + 1) * HEAD_DIM) for h in heads]
    qb = [qk[:, hsl[h]].astype(BF16) for h in heads]
    k = [qk[:, D_MLSTM + h * HEAD_DIM:D_MLSTM + (h + 1) * HEAD_DIM] * k_scale for h in heads]
    v_aug = [jnp.concatenate([v_all[:, hsl[h]].astype(BF16), ones_cols], axis=1) for h in heads]
    state = [s_sc[h] for h in heads]

    s = []
    for h in heads:
        dexp = jnp.exp(jnp.where(causal, c_t[:, h:h + 1] + a_rows[h:h + 1, :], NEG_INF))
        s.append((_dot_nt(qb[h], k[h].astype(BF16)) * dexp).astype(BF16))
    hh = []
    for h in heads:
        nd = _dot(s[h], v_aug[h]) + inter_all[:, h:h + 1] * _dot(qb[h], state[h].astype(BF16))
        den = nd[:, HEAD_DIM:2 * HEAD_DIM]
        hh.append(nd[:, 0:HEAD_DIM] / jnp.maximum(jnp.abs(den), floor_all[:, h:h + 1]))
    for h in heads:
        wk = (k[h] * wexp_all[:, h:h + 1]).astype(BF16)
        s_sc[h] = decay_all[:, h:h + 1] * state[h] + _dot_tn(wk, v_aug[h])
    hc = [hh[h] - row_mean(hh[h]) for h in heads]
    var = [row_mean(hc[h] * hc[h]) for h in heads]
    for h in heads:
        y = hc[h] * lax.rsqrt(var[h] + LN_EPS) * ng[:, hsl[h]] * og_all[:, hsl[h]]
        y_ref[rows, hsl[h]] = y.astype(BF16)


def _mlstm(p, conv_w, conv_b, gate_bias, norm_g):
    n = MLSTM_CHUNKS_PER_STEP
    step_rows = n * CHUNK

    def col(block_w, start):
        return lambda c: (_step_block(c, n), start // block_w)

    const2 = lambda c: (0, 0)
    return pl.pallas_call(
        _mlstm_kernel,
        grid=(SEQ // step_rows + 1,),
        in_specs=[
            pl.BlockSpec((step_rows, 2 * D_MLSTM), col(2 * D_MLSTM, COL_MQK)),
            pl.BlockSpec((step_rows, D_MLSTM), col(D_MLSTM, COL_MV)),
            pl.BlockSpec((step_rows, D_MLSTM), col(D_MLSTM, COL_MO)),
            pl.BlockSpec((step_rows, LANES), col(LANES, COL_GATE)),
            pl.BlockSpec((CONV_K, 2 * D_MLSTM), const2),
            pl.BlockSpec((1, 2 * D_MLSTM), const2),
            pl.BlockSpec((1, LANES), const2),
            pl.BlockSpec((1, D_MLSTM), const2),
        ],
        out_specs=pl.BlockSpec((step_rows, D_MLSTM), lambda c: (_step_block(c, n), 0)),
        out_shape=jax.ShapeDtypeStruct((LP, D_MLSTM), BF16),
        scratch_shapes=[
            pltpu.VMEM((N_HEADS, HEAD_DIM, 2 * HEAD_DIM), F32),
            pltpu.VMEM((8, LANES), F32),
            pltpu.VMEM((CONV_HALO + CHUNK, 2 * D_MLSTM), F32),
        ],
        compiler_params=_cparams(("arbitrary",)),
        name="mlstm_mixer",
    )(p, p, p, p, conv_w, conv_b, gate_bias, norm_g)


N_LEVELS = 6
LOG2_E = 1.4426950408889634


def _hgrn_tables():
    t = np.arange(CHUNK)
    sel = np.zeros((N_LEVELS * CHUNK, CHUNK), np.float32)
    mask = np.zeros((N_LEVELS + 1, CHUNK, CHUNK), np.float32)
    for l in range(N_LEVELS):
        half = 1 << l
        ref_row = (t // (2 * half)) * (2 * half) + half - 1
        is_upper = (t // half) % 2 == 1
        sign = np.where(is_upper, 1.0, -1.0)
        np.add.at(sel, (l * CHUNK + t, t), sign)
        np.add.at(sel, (l * CHUNK + t, ref_row), -sign)
        same = (t[:, None] // (2 * half)) == (t[None, :] // (2 * half))
        mask[l] = same & is_upper[:, None] & ~is_upper[None, :]
    mask[N_LEVELS] = np.eye(CHUNK)
    return sel, mask


def _hgrn_kernel(layer, hq_ref, hf_ref, hi_ref, hg_ref, lbl_ref, ng_ref, sel_ref, mask_ref, y_ref, st_sc):
    c = pl.program_id(0)

    lbl = lbl_ref[...]
    e = jnp.exp(lbl - jnp.max(lbl, axis=0, keepdims=True))
    sm = e / jnp.sum(e, axis=0, keepdims=True)
    lb = jnp.sum(sm[0:layer + 1, :], axis=0, keepdims=True) - sm[0:1, :]
    refs = (hq_ref, hf_ref, hi_ref, hg_ref, ng_ref, sel_ref, mask_ref)

    @pl.when(c == 0)
    def _():
        st_sc[...] = jnp.zeros(st_sc.shape, F32)
        _hgrn_state(0, _hgrn_intra(0, lb, *refs), y_ref, st_sc)

    @pl.when(c > 0)
    def _():
        for g in range(HGRN_CHUNKS_PER_STEP):
            _hgrn_state(g, _hgrn_intra(g, lb, *refs), y_ref, st_sc)


def _hgrn_intra(g, lb, hq_ref, hf_ref, hi_ref, hg_ref, ng_ref, sel_ref, mask_ref):
    rows = pl.ds(g * CHUNK, CHUNK)
    z = hf_ref[rows, :]
    a = jnp.log(lb)
    bb = jnp.log1p(-lb) + _log_sigmoid(z)
    mx = jnp.maximum(a, bb)
    log_f = mx + jnp.log(jnp.exp(a - mx) + jnp.exp(bb - mx))
    kk = (1.0 - lb) * _sigmoid(-z)
    hq = hq_ref[rows, :]
    q = hq * _sigmoid(hq)
    v = hi_ref[rows, :].astype(BF16)
    hg = hg_ref[rows, :]
    gate = hg * _sigmoid(hg) * ng_ref[...]

    b = _chunk_cumsum(log_f) * LOG2_E
    b_hi, b_lo = _split_bf16(b)
    sel = sel_ref[...]
    diffs = _dot(sel, b_hi) + _dot(sel, b_lo)

    amats = [None] * N_HEADS
    for l in range(N_LEVELS + 1):
        if l < N_LEVELS:
            zl = jnp.exp2(diffs[l * CHUNK:(l + 1) * CHUNK, :])
            ql = (q * zl).astype(BF16)
            kl = (kk * zl).astype(BF16)
        else:
            ql = q.astype(BF16)
            kl = kk.astype(BF16)
        ml = mask_ref[l]
        for h in range(N_HEADS):
            hs = slice(h * HEAD_DIM, (h + 1) * HEAD_DIM)
            part = ml * _dot_nt(ql[:, hs], kl[:, hs])
            amats[h] = part if amats[h] is None else amats[h] + part

    b_last = b[CHUNK - 1:CHUNK, :]
    qe = (q * jnp.exp2(b)).astype(BF16)
    kd = (kk * jnp.exp2(b_last - b)).astype(BF16)
    e_last = jnp.exp2(b_last)
    return [a_h.astype(BF16) for a_h in amats], v, qe, kd, e_last, gate


def _hgrn_state(g, intra, y_ref, st_sc):
    amats, v, qe, kd, e_last, gate = intra
    rows = pl.ds(g * CHUNK, CHUNK)
    for h in range(N_HEADS):
        hs = slice(h * HEAD_DIM, (h + 1) * HEAD_DIM)
        st = st_sc[h]
        o = _dot(amats[h], v[:, hs]) + _dot_nt(qe[:, hs], st.astype(BF16))
        st_sc[h] = e_last[:, hs] * st + _dot_tn(v[:, hs], kd[:, hs])
        o = o * lax.rsqrt(jnp.mean(o * o, axis=1, keepdims=True) + LN_EPS)
        y_ref[rows, hs] = (o * gate[:, hs]).astype(BF16)


def _hgrn(p, lb_logits, norm_g, layer):
    n = HGRN_CHUNKS_PER_STEP
    step_rows = n * CHUNK

    def col(start):
        return lambda c: (_step_block(c, n), start // D_HGRN)

    sel, mask = _hgrn_tables()
    const2 = lambda c: (0, 0)
    return pl.pallas_call(
        functools.partial(_hgrn_kernel, layer),
        grid=(SEQ // step_rows + 1,),
        in_specs=[
            pl.BlockSpec((step_rows, D_HGRN), col(COL_HQ)),
            pl.BlockSpec((step_rows, D_HGRN), col(COL_HF)),
            pl.BlockSpec((step_rows, D_HGRN), col(COL_HI)),
            pl.BlockSpec((step_rows, D_HGRN), col(COL_HG)),
            pl.BlockSpec((DEPTH, D_HGRN), const2),
            pl.BlockSpec((1, D_HGRN), const2),
            pl.BlockSpec((N_LEVELS * CHUNK, CHUNK), const2),
            pl.BlockSpec((N_LEVELS + 1, CHUNK, CHUNK), lambda c: (0, 0, 0)),
        ],
        out_specs=pl.BlockSpec((step_rows, D_HGRN), lambda c: (_step_block(c, n), 0)),
        out_shape=jax.ShapeDtypeStruct((LP, D_HGRN), BF16),
        scratch_shapes=[pltpu.VMEM((N_HEADS, HEAD_DIM, HEAD_DIM), F32)],
        compiler_params=_cparams(("arbitrary",)),
        name="hgrn_mixer",
    )(p, p, p, p, lb_logits, norm_g, jnp.asarray(sel, BF16), jnp.asarray(mask, F32))


def _zero_pad_rows(y, row0):
    row = row0 + lax.broadcasted_iota(jnp.int32, y.shape, 0)
    is_pad = jnp.logical_and(row >= META_ROW0, row < META_ROW0 + META_PAD)
    return jnp.where(is_pad, 0.0, y)


def _first_argmax(x, lane, valid):
    xm = jnp.where(valid, x, NEG_INF)
    mx = jnp.max(xm, axis=1, keepdims=True)
    idx = jnp.min(jnp.where(jnp.logical_and(valid, xm == mx), lane, float(LANES)), axis=1, keepdims=True)
    return mx, idx


TOKEN_ROWS = D_MODEL // LANES
GATHER_PITCH = 24


def _store_token_linear(ref, x, first_token=0):
    n = x.shape[0]
    for k in range(TOKEN_ROWS):
        ref[pl.ds(first_token * TOKEN_ROWS + k, n, stride=TOKEN_ROWS), :] = x[:, k * LANES:(k + 1) * LANES]


def _load_gathered(ref, n):
    return jnp.concatenate([ref[pl.ds(k, n, stride=GATHER_PITCH), :] for k in range(TOKEN_ROWS)], axis=1)


OUT_HALF = -(-TM_MID // 32) * 16
OUT_SUB_BLOCKS = ((0, OUT_HALF), (OUT_HALF, TM_MID))


def _out_router_kernel(yp_ref, ym_ref, yh_ref, h_ref, wo_ref, g_ref, b_ref, wrh_ref, wrl_ref, br_ref,
                       h1_ref, hlin_ref, eid_ref, wts_ref):
    def project(r0, r1):
        y = jnp.concatenate([yp_ref[r0:r1, :], ym_ref[r0:r1, :], yh_ref[r0:r1, :]], axis=1)
        return _dot(y, wo_ref[...])

    acc = project(*OUT_SUB_BLOCKS[0])
    for n, (r0, r1) in enumerate(OUT_SUB_BLOCKS):
        acc_next = project(*OUT_SUB_BLOCKS[n + 1]) if n + 1 < len(OUT_SUB_BLOCKS) else None
        _out_router_rows(r0, r1, acc, h_ref, g_ref, b_ref, wrh_ref, wrl_ref, br_ref,
                         h1_ref, hlin_ref, eid_ref, wts_ref)
        acc = acc_next


def _out_router_rows(r0, r1, acc, h_ref, g_ref, b_ref, wrh_ref, wrl_ref, br_ref, h1_ref, hlin_ref, eid_ref, wts_ref):
    i = pl.program_id(0)
    h1 = _layer_norm_rows(ALPHA * h_ref[r0:r1, :] + acc, g_ref[...], b_ref[...])
    h1 = _zero_pad_rows(h1, i * TM_MID + r0)
    h1_ref[r0:r1, :] = h1
    _store_token_linear(hlin_ref, h1, r0)

    x_hi, x_lo = _split_bf16(h1)
    hi_both = _dot(x_hi, jnp.concatenate([wrh_ref[...], wrl_ref[...]], axis=1))
    logits = hi_both[:, 0:LANES] + hi_both[:, LANES:2 * LANES] + _dot(x_lo, wrh_ref[...]) + br_ref[...]
    lane = lax.broadcasted_iota(jnp.int32, logits.shape, 1).astype(F32)

    is_grp = lane < N_GROUPS
    g_max, g_idx = _first_argmax(logits, lane, is_grp)
    g_exp = jnp.where(is_grp, jnp.exp(logits - g_max), 0.0)
    p_grp = 1.0 / jnp.sum(g_exp, axis=1, keepdims=True)

    e_lo = N_GROUPS + g_idx * EXPERTS_PER_GROUP
    in_grp = jnp.logical_and(lane >= e_lo, lane < e_lo + EXPERTS_PER_GROUP)
    e_max, e1 = _first_argmax(logits, lane, in_grp)
    e_exp = jnp.where(in_grp, jnp.exp(logits - e_max), 0.0)
    p_exp = e_exp / jnp.sum(e_exp, axis=1, keepdims=True)
    p1, _ = _first_argmax(p_exp, lane, in_grp)
    rest = jnp.logical_and(in_grp, lane != e1)
    p2, e2 = _first_argmax(p_exp, lane, rest)
    psum = p1 + p2
    w1 = p_grp * p1 / psum
    w2 = p_grp * p2 / psum
    eid = jnp.where(lane == 0.0, e1 - N_GROUPS, jnp.where(lane == 1.0, e2 - N_GROUPS, 0.0))
    eid_ref[r0:r1, :] = eid.astype(jnp.int32)
    wts_ref[r0:r1, :] = jnp.where(lane == 0.0, w1, jnp.where(lane == 1.0, w2, 0.0))


def _out_router(y_pool, y_m, y_h, h, w_out, g, b, wr_hi, wr_lo, br):
    row = lambda i: (i, 0)
    const2 = lambda i: (0, 0)
    return pl.pallas_call(
        _out_router_kernel,
        grid=(LP // TM_MID,),
        in_specs=[
            pl.BlockSpec((TM_MID, D_POOL), row),
            pl.BlockSpec((TM_MID, D_MLSTM), row),
            pl.BlockSpec((TM_MID, D_HGRN), row),
            pl.BlockSpec((TM_MID, D_MODEL), row),
            pl.BlockSpec((D_MODEL, D_MODEL), const2, pipeline_mode=pl.Buffered(1)),
            pl.BlockSpec((1, D_MODEL), const2),
            pl.BlockSpec((1, D_MODEL), const2),
            pl.BlockSpec((D_MODEL, LANES), const2),
            pl.BlockSpec((D_MODEL, LANES), const2),
            pl.BlockSpec((1, LANES), const2),
        ],
        out_specs=[
            pl.BlockSpec((TM_MID, D_MODEL), row),
            pl.BlockSpec((TM_MID * TOKEN_ROWS, LANES), row),
            pl.BlockSpec((TM_MID, LANES), row),
            pl.BlockSpec((TM_MID, LANES), row),
        ],
        out_shape=[
            jax.ShapeDtypeStruct((LP, D_MODEL), F32),
            jax.ShapeDtypeStruct((LP * TOKEN_ROWS, LANES), F32),
            jax.ShapeDtypeStruct((LP, LANES), jnp.int32),
            jax.ShapeDtypeStruct((LP, LANES), F32),
        ],
        compiler_params=_cparams(("arbitrary",), VMEM_LIMIT_OUT_PROJ),
        name="out_proj_ln_router",
    )(y_pool, y_m, y_h, h, w_out, g, b, wr_hi, wr_lo, br)


ROUTE_ROWS = 256
SORTED_ROWS = N_EXPERT_TILES * TM_EXPERT
GATHER_UNROLL = 8


def _route_kernel(e_ref, dest_ref, tab_ref):
    e = e_ref[...]
    lane = lax.broadcasted_iota(jnp.int32, (ROUTE_ROWS, LANES), 1)
    kk = lax.broadcasted_iota(jnp.int32, (LANES, LANES), 0)
    ll = lax.broadcasted_iota(jnp.int32, (LANES, LANES), 1)
    before_lane = jnp.where(kk < ll, 1.0, 0.0).astype(BF16)
    all_lanes = jnp.ones((LANES, LANES), BF16)
    rr = lax.broadcasted_iota(jnp.int32, (ROUTE_ROWS, ROUTE_ROWS), 0)
    cc = lax.broadcasted_iota(jnp.int32, (ROUTE_ROWS, ROUTE_ROWS), 1)
    before_row = jnp.where(cc < rr, 1.0, 0.0).astype(BF16)

    lane1 = lane[0:1, :]
    rank = jnp.zeros((ROUTE_ROWS, LANES), F32)
    counts = jnp.zeros((1, LANES), F32)
    masks = []
    for x in range(N_EXPERTS):
        m = jnp.where(e == x, 1.0, 0.0)
        mb = m.astype(BF16)
        in_row = _dot(mb, before_lane)
        row_tot = _dot(mb, all_lanes)
        rows_before = _dot(before_row, row_tot.astype(BF16))
        rank = rank + m * (in_row + rows_before)
        total = rows_before[ROUTE_ROWS - 1:ROUTE_ROWS, :] + row_tot[ROUTE_ROWS - 1:ROUTE_ROWS, :]
        counts = counts + jnp.where(lane1 == x, total, 0.0)
        masks.append(m)

    padded = jnp.floor((counts + (TM_EXPERT - 1)) * (1.0 / TM_EXPERT)) * TM_EXPERT
    p_hi, p_lo = _split_bf16(padded)
    start = _dot(p_hi, before_lane) + _dot(p_lo, before_lane)
    end = start + padded

    dest = rank
    tile0 = (lane1 * TM_EXPERT).astype(F32)
    n_before = jnp.zeros((1, LANES), F32)
    for x in range(N_EXPERTS):
        dest = dest + masks[x] * start[:, x:x + 1]
        n_before = n_before + jnp.where(end[:, x:x + 1] <= tile0, 1.0, 0.0)
    dest_ref[...] = dest.astype(jnp.int32)

    tile_expert = jnp.minimum(n_before, float(N_EXPERTS - 1))
    n_tiles = end[:, N_EXPERTS - 1:N_EXPERTS] * (1.0 / TM_EXPERT)
    rows_left = jnp.zeros((1, LANES), F32)
    for x in range(N_EXPERTS):
        rows_left = rows_left + jnp.where(tile_expert == x, counts[:, x:x + 1] + start[:, x:x + 1] - tile0, 0.0)
    tile_rows = jnp.clip(rows_left, 0.0, float(TM_EXPERT))
    row = lax.broadcasted_iota(jnp.int32, (8, LANES), 0)
    tab = jnp.where(row == 0, tile_expert, jnp.where(row == 1, n_tiles, jnp.where(row == 2, tile_rows, 0.0)))
    tab_ref[...] = tab.astype(jnp.int32)


def _route(e_grid):
    return pl.pallas_call(
        _route_kernel,
        out_shape=[
            jax.ShapeDtypeStruct((ROUTE_ROWS, LANES), jnp.int32),
            jax.ShapeDtypeStruct((8, LANES), jnp.int32),
        ],
        compiler_params=pltpu.CompilerParams(vmem_limit_bytes=VMEM_LIMIT),
        name="moe_route",
    )(e_grid)


EXPERT_ROW_STEP = 64
WEIGHT_PRIORITY = 1
DISPATCH_TOKENS = 1376
DISPATCH_UNROLL = 8


def _dispatch_kernel(pos_ref, h_ref, x_hbm, sem):
    def body(g, carry):
        for u in range(DISPATCH_UNROLL):
            r = g * DISPATCH_UNROLL + u
            src = h_ref.at[pl.ds(pl.multiple_of(r * TOKEN_ROWS, TOKEN_ROWS), TOKEN_ROWS)]
            for j in range(2):
                row0 = pl.multiple_of(pos_ref[0, 0, 2 * r + j] * TOKEN_ROWS, TOKEN_ROWS)
                pltpu.make_async_copy(src, x_hbm.at[pl.ds(row0, TOKEN_ROWS)], sem.at[0]).start()
        return carry
    lax.fori_loop(0, DISPATCH_TOKENS // DISPATCH_UNROLL, body, 0)

    n = DISPATCH_TOKENS * TOKEN_ROWS
    for _ in range(2):
        pltpu.make_async_copy(h_ref, x_hbm.at[pl.ds(0, n)], sem.at[0]).wait()


def _dispatch(h1_lin, pos):
    n_steps = LP // DISPATCH_TOKENS
    pos3 = pos.reshape(n_steps, 1, 2 * DISPATCH_TOKENS)
    return pl.pallas_call(
        _dispatch_kernel,
        grid=(n_steps,),
        in_specs=[
            pl.BlockSpec((1, 1, 2 * DISPATCH_TOKENS), lambda i: (i, 0, 0), memory_space=pltpu.SMEM),
            pl.BlockSpec((DISPATCH_TOKENS * TOKEN_ROWS, LANES), lambda i: (i, 0)),
        ],
        out_specs=pl.BlockSpec(memory_space=pl.ANY),
        out_shape=jax.ShapeDtypeStruct((SORTED_ROWS * TOKEN_ROWS, LANES), F32),
        scratch_shapes=[pltpu.SemaphoreType.DMA((1,))],
        compiler_params=_cparams(("arbitrary",)),
        name="moe_dispatch",
    )(pos3, h1_lin)


def _expert_kernel(layer, tile_expert, tile_rows, n_tiles, x_ref, w1_hbm, w3_hbm, w2_hbm,
                   y_ref, w1f, w3f, w2f, w1b, w3b, w2b, wsem, wslot_ref):
    i = pl.program_id(0)
    n_used = n_tiles[0]

    def weight_copies(expert, ws):
        return [pltpu.make_async_copy(src.at[layer, expert], dst.at[ws], wsem.at[ws])
                for src, dst in ((w1_hbm, w1f), (w3_hbm, w3f), (w2_hbm, w2f))]

    last_tile = N_EXPERT_TILES - 1

    @pl.when(i == 0)
    def _():
        wslot_ref[0] = 0
        for cp in weight_copies(tile_expert[0], 0):
            cp.start(priority=WEIGHT_PRIORITY)

    @pl.when(i < n_used)
    def _():
        expert = tile_expert[i]
        first_of_expert = jnp.logical_or(i == 0, expert != tile_expert[jnp.maximum(i - 1, 0)])

        @pl.when(first_of_expert)
        def _():
            ws = wslot_ref[0]
            for cp in weight_copies(expert, ws):
                cp.wait()
            nxt = lax.while_loop(
                lambda j: jnp.logical_and(j < n_used, tile_expert[jnp.minimum(j, last_tile)] == expert),
                lambda j: j + 1, i + 1)

            @pl.when(nxt < n_used)
            def _():
                for cp in weight_copies(tile_expert[jnp.minimum(nxt, last_tile)], 1 - ws):
                    cp.start(priority=WEIGHT_PRIORITY)

            w1b[...] = w1f[ws].astype(BF16)
            w3b[...] = w3f[ws].astype(BF16)
            w2b[...] = w2f[ws].astype(BF16)
            wslot_ref[0] = 1 - ws

        rows_here = tile_rows[i]

        def run(m, masked):
            x = jnp.concatenate([x_ref[pl.ds(k, m, stride=TOKEN_ROWS), :] for k in range(TOKEN_ROWS)], axis=1)
            if masked:
                x = jnp.where(lax.broadcasted_iota(jnp.int32, (m, 1), 0) < rows_here, x, 0.0)
            x = x.astype(BF16)
            a = _dot(x, w1b[...])
            g = _dot(x, w3b[...])
            he = (a * _sigmoid(a) * g).astype(BF16)
            _store_token_linear(y_ref, _dot(he, w2b[...]))
            if m < TM_EXPERT:
                y_ref[m * TOKEN_ROWS:, :] = jnp.zeros(((TM_EXPERT - m) * TOKEN_ROWS, LANES), F32)

        @pl.when(rows_here == TM_EXPERT)
        def _():
            run(TM_EXPERT, False)

        for m in range(EXPERT_ROW_STEP, TM_EXPERT + 1, EXPERT_ROW_STEP):
            @pl.when(jnp.logical_and(rows_here > m - EXPERT_ROW_STEP, rows_here <= min(m, TM_EXPERT - 1)))
            def _(m=m):
                run(m, True)

    @pl.when(i >= n_used)
    def _():
        y_ref[...] = jnp.zeros(y_ref.shape, F32)


def _experts(x_sorted, w1, w3, w2, layer, tile_expert, tile_rows, n_tiles):
    def x_map(i, te, tr, nt):
        return (jnp.minimum(i, jnp.maximum(nt[0] - 1, 0)), 0)

    grid_spec = pltpu.PrefetchScalarGridSpec(
        num_scalar_prefetch=3,
        grid=(N_EXPERT_TILES,),
        in_specs=[
            pl.BlockSpec((TM_EXPERT * TOKEN_ROWS, LANES), x_map),
            pl.BlockSpec(memory_space=pl.ANY),
            pl.BlockSpec(memory_space=pl.ANY),
            pl.BlockSpec(memory_space=pl.ANY),
        ],
        out_specs=pl.BlockSpec((TM_EXPERT * TOKEN_ROWS, LANES), lambda i, te, tr, nt: (i, 0)),
        scratch_shapes=[
            pltpu.VMEM((2, D_MODEL, D_EXPERT), F32),
            pltpu.VMEM((2, D_MODEL, D_EXPERT), F32),
            pltpu.VMEM((2, D_EXPERT, D_MODEL), F32),
            pltpu.VMEM((D_MODEL, D_EXPERT), BF16),
            pltpu.VMEM((D_MODEL, D_EXPERT), BF16),
            pltpu.VMEM((D_EXPERT, D_MODEL), BF16),
            pltpu.SemaphoreType.DMA((2,)),
            pltpu.SMEM((1,), jnp.int32),
        ],
    )
    return pl.pallas_call(
        functools.partial(_expert_kernel, layer),
        grid_spec=grid_spec,
        out_shape=jax.ShapeDtypeStruct((SORTED_ROWS * TOKEN_ROWS, LANES), F32),
        compiler_params=_cparams(("arbitrary",)),
        name="moe_experts",
    )(tile_expert, tile_rows, n_tiles, x_sorted, w1, w3, w2)


def _combine_kernel(tm, n_steps, final, pos_cur_ref, pos_next_ref, h1_ref, wts_ref, g_ref, b_ref, y_hbm,
                    *rest):
    out_refs, (ybuf, sem) = rest[:-2], rest[-2:]
    i = pl.program_id(0)
    slot = i % 2

    def start_gather(pos_ref, s):
        def body(r, carry):
            for j in range(2):
                row0 = pl.multiple_of(pos_ref[0, 0, 2 * r + j] * TOKEN_ROWS, TOKEN_ROWS)
                dst = ybuf.at[s, j, pl.ds(pl.multiple_of(r * GATHER_PITCH, 8), TOKEN_ROWS)]
                pltpu.make_async_copy(y_hbm.at[pl.ds(row0, TOKEN_ROWS)], dst, sem.at[s]).start()
            return carry
        lax.fori_loop(0, tm, body, 0, unroll=GATHER_UNROLL)

    @pl.when(i == 0)
    def _():
        start_gather(pos_cur_ref, 0)

    for j in range(2):
        n = tm * TOKEN_ROWS
        pltpu.make_async_copy(y_hbm.at[pl.ds(0, n)], ybuf.at[slot, j, pl.ds(0, n)], sem.at[slot]).wait()

    @pl.when(i + 1 < n_steps)
    def _():
        start_gather(pos_next_ref, 1 - slot)

    wts = wts_ref[...]
    y = wts[:, 0:1] * _load_gathered(ybuf.at[slot, 0], tm) + wts[:, 1:2] * _load_gathered(ybuf.at[slot, 1], tm)
    h2 = _layer_norm_rows(ALPHA * h1_ref[...] + y, g_ref[...], b_ref[...])
    if final:
        out_refs[0][...] = h2
    else:
        h2 = _zero_pad_rows(h2, i * tm)
        out_refs[0][...] = h2
        out_refs[1][...] = h2.astype(BF16)


def _combine(h1, y_sorted, pos, wts, g, b, final):
    tm = TM_FINAL if final else TM_LN
    rows = SEQ if final else LP
    n_steps = rows // tm
    pos3 = pos[:2 * rows].reshape(n_steps, 1, 2 * tm)
    row = lambda i: (i, 0)
    const2 = lambda i: (0, 0)
    if final:
        out_specs = [pl.BlockSpec((tm, D_MODEL), row)]
        out_shape = [jax.ShapeDtypeStruct((rows, D_MODEL), F32)]
    else:
        out_specs = [pl.BlockSpec((tm, D_MODEL), row), pl.BlockSpec((tm, D_MODEL), row)]
        out_shape = [jax.ShapeDtypeStruct((rows, D_MODEL), F32), jax.ShapeDtypeStruct((rows, D_MODEL), BF16)]
    return pl.pallas_call(
        functools.partial(_combine_kernel, tm, n_steps, final),
        grid=(n_steps,),
        in_specs=[
            pl.BlockSpec((1, 1, 2 * tm), lambda i: (i, 0, 0), memory_space=pltpu.SMEM),
            pl.BlockSpec((1, 1, 2 * tm), lambda i: (jnp.minimum(i + 1, n_steps - 1), 0, 0), memory_space=pltpu.SMEM),
            pl.BlockSpec((tm, D_MODEL), row),
            pl.BlockSpec((tm, LANES), row),
            pl.BlockSpec((1, D_MODEL), const2),
            pl.BlockSpec((1, D_MODEL), const2),
            pl.BlockSpec(memory_space=pl.ANY),
        ],
        out_specs=out_specs,
        out_shape=out_shape,
        scratch_shapes=[pltpu.VMEM((2, 2, tm * GATHER_PITCH, LANES), F32), pltpu.SemaphoreType.DMA((2,))],
        compiler_params=_cparams(("arbitrary",)),
        name="moe_combine_ln_final" if final else "moe_combine_ln",
    )(pos3, pos3, h1, wts, g, b, y_sorted)


def _pad_lanes(v):
    return jnp.pad(v, ((0, 0), (0, LANES - v.shape[1])))


def kernel(x, meta_tokens, emb_ln_g, emb_ln_b, hgrn_lb_logits, w_in, conv_w, conv_b, ig_b, fg_b,
           mlstm_norm_g, pool_w, pool_scale, hgrn_norm_g, w_out, ln1_g, ln1_b,
           w_router_group, b_router_group, w_router_expert, b_router_expert, w1, w3, w2,
           ln2_g, ln2_b):
    assert x.shape == (1, SEQ, D_MODEL) and x.dtype == F32
    row2 = lambda v: v.reshape(1, -1)
    meta_blk = jnp.pad(meta_tokens.astype(F32), ((META_PAD, 0), (0, 0)))
    h, hb = _embed(x.reshape(SEQ, D_MODEL), meta_blk, row2(emb_ln_g), row2(emb_ln_b))

    out = None
    for l in range(DEPTH):
        p = _in_proj(hb, w_in, l)
        y_pool = _pool(p, pool_w[l], row2(pool_scale[l]))
        gate_bias = _pad_lanes(jnp.concatenate([ig_b[l], fg_b[l]]).reshape(1, -1))
        y_m = _mlstm(p, conv_w[l], row2(conv_b[l]), gate_bias, row2(mlstm_norm_g[l]))
        y_h = _hgrn(p, hgrn_lb_logits, row2(hgrn_norm_g[l]), l)

        w_r = _pad_lanes(jnp.concatenate([w_router_group[l], w_router_expert[l]], axis=1))
        wr_hi, wr_lo = _split_bf16(w_r)
        b_r = _pad_lanes(jnp.concatenate([b_router_group[l], b_router_expert[l]]).reshape(1, -1))
        h1, h1_lin, eid, wts = _out_router(y_pool, y_m, y_h, h, w_out[l].astype(BF16), row2(ln1_g[l]),
                                           row2(ln1_b[l]), wr_hi, wr_lo, b_r)

        e_flat = jnp.pad(eid[:, 0:2].reshape(-1), (0, ROUTE_ROWS * LANES - N_ASSIGN), constant_values=-1)
        dest, tab = _route(e_flat.reshape(ROUTE_ROWS, LANES))
        pos = dest.reshape(-1)
        x_sorted = _dispatch(h1_lin, pos[:N_ASSIGN])
        y_sorted = _experts(x_sorted, w1, w3, w2, l, tab[0, :N_EXPERT_TILES], tab[2, :N_EXPERT_TILES], tab[1, :1])
        if l + 1 < DEPTH:
            h, hb = _combine(h1, y_sorted, pos, wts, row2(ln2_g[l]), row2(ln2_b[l]), final=False)
        else:
            (out,) = _combine(h1, y_sorted, pos, wts, row2(ln2_g[l]), row2(ln2_b[l]), final=True)
    return out.reshape(1, SEQ, D_MODEL)
```

```python
import functools

import jax
import jax.numpy as jnp
import numpy as np
from jax import lax
from jax.experimental import pallas as pl
from jax.experimental.pallas import tpu as pltpu

F32 = jnp.float32
BF16 = jnp.bfloat16

D_MODEL = 2048
SEQ = 8192
DEPTH = 2
N_META = 16
CHUNK = 64
D_POOL = D_MODEL // 4
POOL_WINDOWS = (2, 4, 8, 16)
POOL_GROUP = D_POOL // len(POOL_WINDOWS)
D_MLSTM = 3 * D_MODEL // 8
N_HEADS = 6
HEAD_DIM = D_MLSTM // N_HEADS
CONV_K = 4
D_HGRN = D_MODEL - D_POOL - D_MLSTM
N_GROUPS = 4
EXPERTS_PER_GROUP = 8
N_EXPERTS = N_GROUPS * EXPERTS_PER_GROUP
D_EXPERT = D_MODEL // 4
ALPHA = (2 * DEPTH) ** 0.25
LN_EPS = 1e-5
NEG_INF = float("-inf")

LANES = 128
V7X_VMEM_BYTES = 64 * 1024 * 1024
LP = SEQ + CHUNK
META_ROW0 = SEQ
META_PAD = CHUNK - N_META
TM_EMBED = 512
LP_EMBED = (SEQ // TM_EMBED + 1) * TM_EMBED

COL_MQK = 0
COL_MV = 2 * D_MLSTM
COL_MO = COL_MV + D_MLSTM
COL_HQ = COL_MO + D_MLSTM
COL_HF = COL_HQ + D_HGRN
COL_HI = COL_HF + D_HGRN
COL_HG = COL_HI + D_HGRN
COL_POOL = COL_HG + D_HGRN
COL_GATE = COL_POOL + D_POOL
P_COLS = COL_GATE + 2 * LANES

TM_BIG = 2752
TN_IN = 256
TM_MID = 688
TM_LN = 192
TM_FINAL = 256
TM_EXPERT = 256
N_ASSIGN = 2 * LP
N_EXPERT_TILES = (N_ASSIGN + N_EXPERTS * (TM_EXPERT - 1)) // TM_EXPERT + 1
VMEM_LIMIT = V7X_VMEM_BYTES * 7 // 8
VMEM_LIMIT_OUT_PROJ = V7X_VMEM_BYTES * 31 // 32


def _cparams(sem, vmem_limit=VMEM_LIMIT):
    return pltpu.CompilerParams(dimension_semantics=sem, vmem_limit_bytes=vmem_limit)


def _sigmoid(x):
    return 1.0 / (1.0 + jnp.exp(-x))


def _log_sigmoid(x):
    return jnp.minimum(x, 0.0) - jnp.log1p(jnp.exp(-jnp.abs(x)))


def _layer_norm_rows(x, g, b):
    mu = jnp.mean(x, axis=-1, keepdims=True)
    xc = x - mu
    var = jnp.mean(xc * xc, axis=-1, keepdims=True)
    return xc * lax.rsqrt(var + LN_EPS) * g + b


def _dot(a, b):
    return jnp.dot(a, b, preferred_element_type=F32)


def _dot_nt(a, b):
    return lax.dot_general(a, b, (((1,), (1,)), ((), ())), preferred_element_type=F32)


def _dot_tn(a, b):
    return lax.dot_general(a, b, (((0,), (0,)), ((), ())), preferred_element_type=F32)


def _split_bf16(x):
    hi = x.astype(BF16)
    lo = (x - hi.astype(F32)).astype(BF16)
    return hi, lo


def _chunk_cumsum(x):
    r = lax.broadcasted_iota(jnp.int32, (CHUNK, CHUNK), 0)
    c = lax.broadcasted_iota(jnp.int32, (CHUNK, CHUNK), 1)
    tri = jnp.where(r >= c, 1.0, 0.0).astype(BF16)
    hi, lo = _split_bf16(x)
    return _dot(tri, hi) + _dot(tri, lo)


def _embed_kernel(x_ref, meta_ref, g_ref, b_ref, h_ref, hb_ref):
    i = pl.program_id(0)
    g = g_ref[...]
    b = b_ref[...]

    @pl.when(i < SEQ // TM_EMBED)
    def _():
        y = _layer_norm_rows(x_ref[...], g, b)
        h_ref[...] = y
        hb_ref[...] = y.astype(BF16)

    @pl.when(i == SEQ // TM_EMBED)
    def _():
        y = _layer_norm_rows(meta_ref[...], g, b)
        row = lax.broadcasted_iota(jnp.int32, (CHUNK, D_MODEL), 0)
        y = jnp.where(row >= META_PAD, y, 0.0)
        h_ref[0:CHUNK, :] = y
        hb_ref[0:CHUNK, :] = y.astype(BF16)
        h_ref[CHUNK:TM_EMBED, :] = jnp.zeros((TM_EMBED - CHUNK, D_MODEL), F32)
        hb_ref[CHUNK:TM_EMBED, :] = jnp.zeros((TM_EMBED - CHUNK, D_MODEL), BF16)


def _embed(x2d, meta_blk, g, b):
    nx = SEQ // TM_EMBED
    return pl.pallas_call(
        _embed_kernel,
        grid=(nx + 1,),
        in_specs=[
            pl.BlockSpec((TM_EMBED, D_MODEL), lambda i: (jnp.minimum(i, nx - 1), 0)),
            pl.BlockSpec((CHUNK, D_MODEL), lambda i: (0, 0)),
            pl.BlockSpec((1, D_MODEL), lambda i: (0, 0)),
            pl.BlockSpec((1, D_MODEL), lambda i: (0, 0)),
        ],
        out_specs=[
            pl.BlockSpec((TM_EMBED, D_MODEL), lambda i: (i, 0)),
            pl.BlockSpec((TM_EMBED, D_MODEL), lambda i: (i, 0)),
        ],
        out_shape=[
            jax.ShapeDtypeStruct((LP_EMBED, D_MODEL), F32),
            jax.ShapeDtypeStruct((LP_EMBED, D_MODEL), BF16),
        ],
        compiler_params=_cparams(("arbitrary",)),
        name="embed_ln",
    )(x2d, meta_blk, g, b)


K_TILES = D_MODEL // LANES
W_IN_MLSTM0 = D_POOL
W_IN_GATE0 = D_POOL + 4 * D_MLSTM
W_IN_HGRN0 = W_IN_GATE0 + 2 * N_HEADS
D_IN = W_IN_HGRN0 + 4 * D_HGRN


def _in_proj_kernel(layer, x_ref, w_ref, o_ref):
    w_t = jnp.concatenate([w_ref[:, DEPTH * kt + layer, :] for kt in range(K_TILES)], axis=1)
    o_ref[...] = _dot_nt(x_ref[...], w_t.astype(BF16))


def _in_proj(hb, w_in, layer):
    w_view = w_in.reshape(DEPTH, K_TILES, LANES, D_IN).transpose(3, 1, 0, 2).reshape(D_IN, K_TILES * DEPTH, LANES)

    def first_col(j):
        c = j * TN_IN
        return jnp.where(c < COL_HQ, W_IN_MLSTM0 + c,
                         jnp.where(c < COL_POOL, W_IN_HGRN0 + (c - COL_HQ),
                                   jnp.where(c < COL_GATE, c - COL_POOL, W_IN_GATE0)))

    w_block = (pl.Element(TN_IN), pl.Element(K_TILES * DEPTH), pl.Element(LANES))
    return pl.pallas_call(
        functools.partial(_in_proj_kernel, layer),
        grid=(LP // TM_BIG, P_COLS // TN_IN),
        in_specs=[
            pl.BlockSpec((TM_BIG, D_MODEL), lambda i, j: (i, 0)),
            pl.BlockSpec(w_block, lambda i, j: (first_col(j), 0, 0)),
        ],
        out_specs=pl.BlockSpec((TM_BIG, TN_IN), lambda i, j: (i, j)),
        out_shape=jax.ShapeDtypeStruct((LP, P_COLS), F32),
        compiler_params=_cparams(("arbitrary", "arbitrary")),
        name="in_proj",
    )(hb, w_view)


POOL_HALO = 16


def _pool_kernel(u_ref, halo_ref, pw_ref, ps_ref, y_ref, ubuf):
    i = pl.program_id(0)
    u = u_ref[...]
    ubuf[0:POOL_HALO, :] = halo_ref[...]
    ubuf[POOL_HALO:POOL_HALO + TM_MID, :] = u
    row = i * TM_MID + lax.broadcasted_iota(jnp.int32, (TM_MID, POOL_GROUP), 0)
    pos = jnp.where(row >= META_ROW0 + META_PAD, row - (META_ROW0 + META_PAD) + 1, 2 * POOL_HALO)
    scale = ps_ref[...]
    for gi, w in enumerate(POOL_WINDOWS):
        cs = slice(gi * POOL_GROUP, (gi + 1) * POOL_GROUP)
        ug = u[:, cs]
        acc = ug
        for j in range(1, w):
            acc = acc + ubuf[POOL_HALO - j:POOL_HALO - j + TM_MID, cs]
        div = jnp.minimum(pos, w).astype(F32)
        d = acc / div - ug
        yg = _dot(d.astype(BF16), pw_ref[gi].astype(BF16))
        y_ref[:, cs] = (yg * scale[:, cs]).astype(BF16)


def _pool(p, pool_w, pool_scale):
    blocks_per_tile = TM_MID // POOL_HALO
    last_meta_block = (LP - POOL_HALO) // POOL_HALO

    def halo_map(i):
        return (jnp.where(i == 0, last_meta_block, i * blocks_per_tile - 1), COL_POOL // D_POOL)

    return pl.pallas_call(
        _pool_kernel,
        grid=(LP // TM_MID,),
        in_specs=[
            pl.BlockSpec((TM_MID, D_POOL), lambda i: (i, COL_POOL // D_POOL)),
            pl.BlockSpec((POOL_HALO, D_POOL), halo_map),
            pl.BlockSpec((len(POOL_WINDOWS), POOL_GROUP, POOL_GROUP), lambda i: (0, 0, 0)),
            pl.BlockSpec((1, D_POOL), lambda i: (0, 0)),
        ],
        out_specs=pl.BlockSpec((TM_MID, D_POOL), lambda i: (i, 0)),
        out_shape=jax.ShapeDtypeStruct((LP, D_POOL), BF16),
        scratch_shapes=[pltpu.VMEM((POOL_HALO + TM_MID, D_POOL), F32)],
        compiler_params=_cparams(("arbitrary",)),
        name="pool_mixer",
    )(p, p, pool_w, pool_scale)


MLSTM_CHUNKS_PER_STEP = 8
HGRN_CHUNKS_PER_STEP = 8


def _step_block(c, chunks_per_step):
    return jnp.where(c == 0, SEQ // (chunks_per_step * CHUNK), c - 1)


CONV_HALO = 8


def _mlstm_kernel(mqk_ref, mv_ref, mo_ref, gt_ref, cw_ref, cb_ref, gb_ref, ng_ref, y_ref,
                  s_sc, m_sc, xbuf):
    c = pl.program_id(0)

    @pl.when(c == 0)
    def _():
        s_sc[...] = jnp.zeros(s_sc.shape, F32)
        m_sc[...] = jnp.full(m_sc.shape, NEG_INF, F32)
        xbuf[0:CONV_HALO, :] = jnp.zeros((CONV_HALO, 2 * D_MLSTM), F32)
        _mlstm_chunk(0, True, mqk_ref, mv_ref, mo_ref, gt_ref, cw_ref, cb_ref, gb_ref, ng_ref, y_ref,
                     s_sc, m_sc, xbuf)

    @pl.when(c > 0)
    def _():
        for g in range(MLSTM_CHUNKS_PER_STEP):
            _mlstm_chunk(g, False, mqk_ref, mv_ref, mo_ref, gt_ref, cw_ref, cb_ref, gb_ref, ng_ref, y_ref,
                         s_sc, m_sc, xbuf)


def _mlstm_chunk(g, is_meta, mqk_ref, mv_ref, mo_ref, gt_ref, cw_ref, cb_ref, gb_ref, ng_ref, y_ref,
                 s_sc, m_sc, xbuf):
    rows = pl.ds(g * CHUNK, CHUNK)
    x = mqk_ref[rows, :]
    xbuf[CONV_HALO:CONV_HALO + CHUNK, :] = x
    cw = cw_ref[...]
    conv = cb_ref[...] + cw[CONV_K - 1:CONV_K, :] * x
    for j in range(CONV_K - 1):
        off = CONV_HALO - (CONV_K - 1) + j
        conv = conv + cw[j:j + 1, :] * xbuf[off:off + CHUNK, :]
    xbuf[0:CONV_HALO, :] = x[CHUNK - CONV_HALO:CHUNK, :]
    qk = conv * _sigmoid(conv)
    v_all = mv_ref[rows, :]
    og_all = _sigmoid(mo_ref[rows, :])
    ng = ng_ref[...]

    z = gt_ref[rows, :] + gb_ref[...]
    if is_meta:
        valid = lax.broadcasted_iota(jnp.int32, (CHUNK, LANES), 0) >= META_PAD
        ig = jnp.where(valid, z, NEG_INF)
        lf = jnp.where(valid, _log_sigmoid(z), 0.0)
    else:
        ig = z
        lf = _log_sigmoid(z)
    g_t = pltpu.roll(_chunk_cumsum(lf), LANES - N_HEADS, axis=1)
    a = ig - g_t
    row = lax.broadcasted_iota(jnp.int32, (CHUNK, LANES), 0)
    a_max = a
    shift = 1
    while shift < CHUNK:
        a_max = jnp.maximum(a_max, jnp.where(row >= shift, pltpu.roll(a_max, shift, axis=0), NEG_INF))
        shift *= 2
    m_prev = m_sc[0:1, :]
    m_t = g_t + jnp.maximum(a_max, m_prev)
    m_ts = jnp.where(m_t == NEG_INF, 0.0, m_t)
    c_t = g_t - m_ts
    inter_all = jnp.exp(g_t + m_prev - m_ts)
    floor_all = jnp.exp(-m_ts)
    g_last = g_t[CHUNK - 1:CHUNK, :]
    m_new = g_last + jnp.maximum(a_max[CHUNK - 1:CHUNK, :], m_prev)
    decay_all = jnp.exp(g_last + m_prev - m_new)
    wexp_all = jnp.exp(g_last + a - m_new)
    m_sc[0:1, :] = m_new
    a_rows = a.T

    r64 = lax.broadcasted_iota(jnp.int32, (CHUNK, CHUNK), 0)
    c64 = lax.broadcasted_iota(jnp.int32, (CHUNK, CHUNK), 1)
    causal = r64 >= c64
    k_scale = HEAD_DIM ** -0.5
    ones_cols = jnp.ones((CHUNK, HEAD_DIM), BF16)
    mean_cols = jnp.full((HEAD_DIM, HEAD_DIM), 1.0 / HEAD_DIM, BF16)

    def row_mean(x):
        hi, lo = _split_bf16(x)
        return _dot(hi, mean_cols) + _dot(lo, mean_cols)

    heads = range(N_HEADS)
    hsl = [slice(h * HEAD_DIM, (h + 1) * HEAD_DIM) for h in heads]
    qb = [qk[:, hsl[h]].astype(BF16) for h in heads]
    k = [qk[:, D_MLSTM + h * HEAD_DIM:D_MLSTM + (h + 1) * HEAD_DIM] * k_scale for h in heads]
    v_aug = [jnp.concatenate([v_all[:, hsl[h]].astype(BF16), ones_cols], axis=1) for h in heads]
    state = [s_sc[h] for h in heads]

    s = []
    for h in heads:
        dexp = jnp.exp(jnp.where(causal, c_t[:, h:h + 1] + a_rows[h:h + 1, :], NEG_INF))
        s.append((_dot_nt(qb[h], k[h].astype(BF16)) * dexp).astype(BF16))
    hh = []
    for h in heads:
        nd = _dot(s[h], v_aug[h]) + inter_all[:, h:h + 1] * _dot(qb[h], state[h].astype(BF16))
        den = nd[:, HEAD_DIM:2 * HEAD_DIM]
        hh.append(nd[:, 0:HEAD_DIM] / jnp.maximum(jnp.abs(den), floor_all[:, h:h + 1]))
    for h in heads:
        wk = (k[h] * wexp_all[:, h:h + 1]).astype(BF16)
        s_sc[h] = decay_all[:, h:h + 1] * state[h] + _dot_tn(wk, v_aug[h])
    hc = [hh[h] - row_mean(hh[h]) for h in heads]
    var = [row_mean(hc[h] * hc[h]) for h in heads]
    for h in heads:
        y = hc[h] * lax.rsqrt(var[h] + LN_EPS) * ng[:, hsl[h]] * og_all[:, hsl[h]]
        y_ref[rows, hsl[h]] = y.astype(BF16)


def _mlstm(p, conv_w, conv_b, gate_bias, norm_g):
    n = MLSTM_CHUNKS_PER_STEP
    step_rows = n * CHUNK

    def col(block_w, start):
        return lambda c: (_step_block(c, n), start // block_w)

    const2 = lambda c: (0, 0)
    return pl.pallas_call(
        _mlstm_kernel,
        grid=(SEQ // step_rows + 1,),
        in_specs=[
            pl.BlockSpec((step_rows, 2 * D_MLSTM), col(2 * D_MLSTM, COL_MQK)),
            pl.BlockSpec((step_rows, D_MLSTM), col(D_MLSTM, COL_MV)),
            pl.BlockSpec((step_rows, D_MLSTM), col(D_MLSTM, COL_MO)),
            pl.BlockSpec((step_rows, LANES), col(LANES, COL_GATE)),
            pl.BlockSpec((CONV_K, 2 * D_MLSTM), const2),
            pl.BlockSpec((1, 2 * D_MLSTM), const2),
            pl.BlockSpec((1, LANES), const2),
            pl.BlockSpec((1, D_MLSTM), const2),
        ],
        out_specs=pl.BlockSpec((step_rows, D_MLSTM), lambda c: (_step_block(c, n), 0)),
        out_shape=jax.ShapeDtypeStruct((LP, D_MLSTM), BF16),
        scratch_shapes=[
            pltpu.VMEM((N_HEADS, HEAD_DIM, 2 * HEAD_DIM), F32),
            pltpu.VMEM((8, LANES), F32),
            pltpu.VMEM((CONV_HALO + CHUNK, 2 * D_MLSTM), F32),
        ],
        compiler_params=_cparams(("arbitrary",)),
        name="mlstm_mixer",
    )(p, p, p, p, conv_w, conv_b, gate_bias, norm_g)


N_LEVELS = 6
LOG2_E = 1.4426950408889634


def _hgrn_tables():
    t = np.arange(CHUNK)
    sel = np.zeros((N_LEVELS * CHUNK, CHUNK), np.float32)
    mask = np.zeros((N_LEVELS + 1, CHUNK, CHUNK), np.float32)
    for l in range(N_LEVELS):
        half = 1 << l
        ref_row = (t // (2 * half)) * (2 * half) + half - 1
        is_upper = (t // half) % 2 == 1
        sign = np.where(is_upper, 1.0, -1.0)
        np.add.at(sel, (l * CHUNK + t, t), sign)
        np.add.at(sel, (l * CHUNK + t, ref_row), -sign)
        same = (t[:, None] // (2 * half)) == (t[None, :] // (2 * half))
        mask[l] = same & is_upper[:, None] & ~is_upper[None, :]
    mask[N_LEVELS] = np.eye(CHUNK)
    return sel, mask


def _hgrn_kernel(layer, hq_ref, hf_ref, hi_ref, hg_ref, lbl_ref, ng_ref, sel_ref, mask_ref, y_ref, st_sc):
    c = pl.program_id(0)

    lbl = lbl_ref[...]
    e = jnp.exp(lbl - jnp.max(lbl, axis=0, keepdims=True))
    sm = e / jnp.sum(e, axis=0, keepdims=True)
    lb = jnp.sum(sm[0:layer + 1, :], axis=0, keepdims=True) - sm[0:1, :]
    refs = (hq_ref, hf_ref, hi_ref, hg_ref, ng_ref, sel_ref, mask_ref)

    @pl.when(c == 0)
    def _():
        st_sc[...] = jnp.zeros(st_sc.shape, F32)
        _hgrn_state(0, _hgrn_intra(0, lb, *refs), y_ref, st_sc)

    @pl.when(c > 0)
    def _():
        for g in range(HGRN_CHUNKS_PER_STEP):
            _hgrn_state(g, _hgrn_intra(g, lb, *refs), y_ref, st_sc)


def _hgrn_intra(g, lb, hq_ref, hf_ref, hi_ref, hg_ref, ng_ref, sel_ref, mask_ref):
    rows = pl.ds(g * CHUNK, CHUNK)
    z = hf_ref[rows, :]
    a = jnp.log(lb)
    bb = jnp.log1p(-lb) + _log_sigmoid(z)
    mx = jnp.maximum(a, bb)
    log_f = mx + jnp.log(jnp.exp(a - mx) + jnp.exp(bb - mx))
    kk = (1.0 - lb) * _sigmoid(-z)
    hq = hq_ref[rows, :]
    q = hq * _sigmoid(hq)
    v = hi_ref[rows, :].astype(BF16)
    hg = hg_ref[rows, :]
    gate = hg * _sigmoid(hg) * ng_ref[...]

    b = _chunk_cumsum(log_f) * LOG2_E
    b_hi, b_lo = _split_bf16(b)
    sel = sel_ref[...]
    diffs = _dot(sel, b_hi) + _dot(sel, b_lo)

    amats = [None] * N_HEADS
    for l in range(N_LEVELS + 1):
        if l < N_LEVELS:
            zl = jnp.exp2(diffs[l * CHUNK:(l + 1) * CHUNK, :])
            ql = (q * zl).astype(BF16)
            kl = (kk * zl).astype(BF16)
        else:
            ql = q.astype(BF16)
            kl = kk.astype(BF16)
        ml = mask_ref[l]
        for h in range(N_HEADS):
            hs = slice(h * HEAD_DIM, (h + 1) * HEAD_DIM)
            part = ml * _dot_nt(ql[:, hs], kl[:, hs])
            amats[h] = part if amats[h] is None else amats[h] + part

    b_last = b[CHUNK - 1:CHUNK, :]
    qe = (q * jnp.exp2(b)).astype(BF16)
    kd = (kk * jnp.exp2(b_last - b)).astype(BF16)
    e_last = jnp.exp2(b_last)
    return [a_h.astype(BF16) for a_h in amats], v, qe, kd, e_last, gate


def _hgrn_state(g, intra, y_ref, st_sc):
    amats, v, qe, kd, e_last, gate = intra
    rows = pl.ds(g * CHUNK, CHUNK)
    for h in range(N_HEADS):
        hs = slice(h * HEAD_DIM, (h + 1) * HEAD_DIM)
        st = st_sc[h]
        o = _dot(amats[h], v[:, hs]) + _dot_nt(qe[:, hs], st.astype(BF16))
        st_sc[h] = e_last[:, hs] * st + _dot_tn(v[:, hs], kd[:, hs])
        o = o * lax.rsqrt(jnp.mean(o * o, axis=1, keepdims=True) + LN_EPS)
        y_ref[rows, hs] = (o * gate[:, hs]).astype(BF16)


def _hgrn(p, lb_logits, norm_g, layer):
    n = HGRN_CHUNKS_PER_STEP
    step_rows = n * CHUNK

    def col(start):
        return lambda c: (_step_block(c, n), start // D_HGRN)

    sel, mask = _hgrn_tables()
    const2 = lambda c: (0, 0)
    return pl.pallas_call(
        functools.partial(_hgrn_kernel, layer),
        grid=(SEQ // step_rows + 1,),
        in_specs=[
            pl.BlockSpec((step_rows, D_HGRN), col(COL_HQ)),
            pl.BlockSpec((step_rows, D_HGRN), col(COL_HF)),
            pl.BlockSpec((step_rows, D_HGRN), col(COL_HI)),
            pl.BlockSpec((step_rows, D_HGRN), col(COL_HG)),
            pl.BlockSpec((DEPTH, D_HGRN), const2),
            pl.BlockSpec((1, D_HGRN), const2),
            pl.BlockSpec((N_LEVELS * CHUNK, CHUNK), const2),
            pl.BlockSpec((N_LEVELS + 1, CHUNK, CHUNK), lambda c: (0, 0, 0)),
        ],
        out_specs=pl.BlockSpec((step_rows, D_HGRN), lambda c: (_step_block(c, n), 0)),
        out_shape=jax.ShapeDtypeStruct((LP, D_HGRN), BF16),
        scratch_shapes=[pltpu.VMEM((N_HEADS, HEAD_DIM, HEAD_DIM), F32)],
        compiler_params=_cparams(("arbitrary",)),
        name="hgrn_mixer",
    )(p, p, p, p, lb_logits, norm_g, jnp.asarray(sel, BF16), jnp.asarray(mask, F32))


def _zero_pad_rows(y, row0):
    row = row0 + lax.broadcasted_iota(jnp.int32, y.shape, 0)
    is_pad = jnp.logical_and(row >= META_ROW0, row < META_ROW0 + META_PAD)
    return jnp.where(is_pad, 0.0, y)


def _first_argmax(x, lane, valid):
    xm = jnp.where(valid, x, NEG_INF)
    mx = jnp.max(xm, axis=1, keepdims=True)
    idx = jnp.min(jnp.where(jnp.logical_and(valid, xm == mx), lane, float(LANES)), axis=1, keepdims=True)
    return mx, idx


TOKEN_ROWS = D_MODEL // LANES
GATHER_PITCH = 24


def _store_token_linear(ref, x, first_token=0):
    n = x.shape[0]
    for k in range(TOKEN_ROWS):
        ref[pl.ds(first_token * TOKEN_ROWS + k, n, stride=TOKEN_ROWS), :] = x[:, k * LANES:(k + 1) * LANES]


def _load_gathered(ref, n):
    return jnp.concatenate([ref[pl.ds(k, n, stride=GATHER_PITCH), :] for k in range(TOKEN_ROWS)], axis=1)


OUT_HALF = -(-TM_MID // 32) * 16
OUT_SUB_BLOCKS = ((0, OUT_HALF), (OUT_HALF, TM_MID))


def _out_router_kernel(yp_ref, ym_ref, yh_ref, h_ref, wo_ref, g_ref, b_ref, wrh_ref, wrl_ref, br_ref,
                       h1_ref, hlin_ref, eid_ref, wts_ref):
    def project(r0, r1):
        y = jnp.concatenate([yp_ref[r0:r1, :], ym_ref[r0:r1, :], yh_ref[r0:r1, :]], axis=1)
        return _dot(y, wo_ref[...])

    acc = project(*OUT_SUB_BLOCKS[0])
    for n, (r0, r1) in enumerate(OUT_SUB_BLOCKS):
        acc_next = project(*OUT_SUB_BLOCKS[n + 1]) if n + 1 < len(OUT_SUB_BLOCKS) else None
        _out_router_rows(r0, r1, acc, h_ref, g_ref, b_ref, wrh_ref, wrl_ref, br_ref,
                         h1_ref, hlin_ref, eid_ref, wts_ref)
        acc = acc_next


def _out_router_rows(r0, r1, acc, h_ref, g_ref, b_ref, wrh_ref, wrl_ref, br_ref, h1_ref, hlin_ref, eid_ref, wts_ref):
    i = pl.program_id(0)
    h1 = _layer_norm_rows(ALPHA * h_ref[r0:r1, :] + acc, g_ref[...], b_ref[...])
    h1 = _zero_pad_rows(h1, i * TM_MID + r0)
    h1_ref[r0:r1, :] = h1
    _store_token_linear(hlin_ref, h1, r0)

    x_hi, x_lo = _split_bf16(h1)
    hi_both = _dot(x_hi, jnp.concatenate([wrh_ref[...], wrl_ref[...]], axis=1))
    logits = hi_both[:, 0:LANES] + hi_both[:, LANES:2 * LANES] + _dot(x_lo, wrh_ref[...]) + br_ref[...]
    lane = lax.broadcasted_iota(jnp.int32, logits.shape, 1).astype(F32)

    is_grp = lane < N_GROUPS
    g_max, g_idx = _first_argmax(logits, lane, is_grp)
    g_exp = jnp.where(is_grp, jnp.exp(logits - g_max), 0.0)
    p_grp = 1.0 / jnp.sum(g_exp, axis=1, keepdims=True)

    e_lo = N_GROUPS + g_idx * EXPERTS_PER_GROUP
    in_grp = jnp.logical_and(lane >= e_lo, lane < e_lo + EXPERTS_PER_GROUP)
    e_max, e1 = _first_argmax(logits, lane, in_grp)
    e_exp = jnp.where(in_grp, jnp.exp(logits - e_max), 0.0)
    p_exp = e_exp / jnp.sum(e_exp, axis=1, keepdims=True)
    p1, _ = _first_argmax(p_exp, lane, in_grp)
    rest = jnp.logical_and(in_grp, lane != e1)
    p2, e2 = _first_argmax(p_exp, lane, rest)
    psum = p1 + p2
    w1 = p_grp * p1 / psum
    w2 = p_grp * p2 / psum
    eid = jnp.where(lane == 0.0, e1 - N_GROUPS, jnp.where(lane == 1.0, e2 - N_GROUPS, 0.0))
    eid_ref[r0:r1, :] = eid.astype(jnp.int32)
    wts_ref[r0:r1, :] = jnp.where(lane == 0.0, w1, jnp.where(lane == 1.0, w2, 0.0))


def _out_router(y_pool, y_m, y_h, h, w_out, g, b, wr_hi, wr_lo, br):
    row = lambda i: (i, 0)
    const2 = lambda i: (0, 0)
    return pl.pallas_call(
        _out_router_kernel,
        grid=(LP // TM_MID,),
        in_specs=[
            pl.BlockSpec((TM_MID, D_POOL), row),
            pl.BlockSpec((TM_MID, D_MLSTM), row),
            pl.BlockSpec((TM_MID, D_HGRN), row),
            pl.BlockSpec((TM_MID, D_MODEL), row),
            pl.BlockSpec((D_MODEL, D_MODEL), const2, pipeline_mode=pl.Buffered(1)),
            pl.BlockSpec((1, D_MODEL), const2),
            pl.BlockSpec((1, D_MODEL), const2),
            pl.BlockSpec((D_MODEL, LANES), const2),
            pl.BlockSpec((D_MODEL, LANES), const2),
            pl.BlockSpec((1, LANES), const2),
        ],
        out_specs=[
            pl.BlockSpec((TM_MID, D_MODEL), row),
            pl.BlockSpec((TM_MID * TOKEN_ROWS, LANES), row),
            pl.BlockSpec((TM_MID, LANES), row),
            pl.BlockSpec((TM_MID, LANES), row),
        ],
        out_shape=[
            jax.ShapeDtypeStruct((LP, D_MODEL), F32),
            jax.ShapeDtypeStruct((LP * TOKEN_ROWS, LANES), F32),
            jax.ShapeDtypeStruct((LP, LANES), jnp.int32),
            jax.ShapeDtypeStruct((LP, LANES), F32),
        ],
        compiler_params=_cparams(("arbitrary",), VMEM_LIMIT_OUT_PROJ),
        name="out_proj_ln_router",
    )(y_pool, y_m, y_h, h, w_out, g, b, wr_hi, wr_lo, br)


ROUTE_ROWS = 256
SORTED_ROWS = N_EXPERT_TILES * TM_EXPERT
GATHER_UNROLL = 8


def _route_kernel(e_ref, dest_ref, tab_ref):
    e = e_ref[...]
    lane = lax.broadcasted_iota(jnp.int32, (ROUTE_ROWS, LANES), 1)
    kk = lax.broadcasted_iota(jnp.int32, (LANES, LANES), 0)
    ll = lax.broadcasted_iota(jnp.int32, (LANES, LANES), 1)
    before_lane = jnp.where(kk < ll, 1.0, 0.0).astype(BF16)
    all_lanes = jnp.ones((LANES, LANES), BF16)
    rr = lax.broadcasted_iota(jnp.int32, (ROUTE_ROWS, ROUTE_ROWS), 0)
    cc = lax.broadcasted_iota(jnp.int32, (ROUTE_ROWS, ROUTE_ROWS), 1)
    before_row = jnp.where(cc < rr, 1.0, 0.0).astype(BF16)

    lane1 = lane[0:1, :]
    rank = jnp.zeros((ROUTE_ROWS, LANES), F32)
    counts = jnp.zeros((1, LANES), F32)
    masks = []
    for x in range(N_EXPERTS):
        m = jnp.where(e == x, 1.0, 0.0)
        mb = m.astype(BF16)
        in_row = _dot(mb, before_lane)
        row_tot = _dot(mb, all_lanes)
        rows_before = _dot(before_row, row_tot.astype(BF16))
        rank = rank + m * (in_row + rows_before)
        total = rows_before[ROUTE_ROWS - 1:ROUTE_ROWS, :] + row_tot[ROUTE_ROWS - 1:ROUTE_ROWS, :]
        counts = counts + jnp.where(lane1 == x, total, 0.0)
        masks.append(m)

    padded = jnp.floor((counts + (TM_EXPERT - 1)) * (1.0 / TM_EXPERT)) * TM_EXPERT
    p_hi, p_lo = _split_bf16(padded)
    start = _dot(p_hi, before_lane) + _dot(p_lo, before_lane)
    end = start + padded

    dest = rank
    tile0 = (lane1 * TM_EXPERT).astype(F32)
    n_before = jnp.zeros((1, LANES), F32)
    for x in range(N_EXPERTS):
        dest = dest + masks[x] * start[:, x:x + 1]
        n_before = n_before + jnp.where(end[:, x:x + 1] <= tile0, 1.0, 0.0)
    dest_ref[...] = dest.astype(jnp.int32)

    tile_expert = jnp.minimum(n_before, float(N_EXPERTS - 1))
    n_tiles = end[:, N_EXPERTS - 1:N_EXPERTS] * (1.0 / TM_EXPERT)
    rows_left = jnp.zeros((1, LANES), F32)
    for x in range(N_EXPERTS):
        rows_left = rows_left + jnp.where(tile_expert == x, counts[:, x:x + 1] + start[:, x:x + 1] - tile0, 0.0)
    tile_rows = jnp.clip(rows_left, 0.0, float(TM_EXPERT))
    row = lax.broadcasted_iota(jnp.int32, (8, LANES), 0)
    tab = jnp.where(row == 0, tile_expert, jnp.where(row == 1, n_tiles, jnp.where(row == 2, tile_rows, 0.0)))
    tab_ref[...] = tab.astype(jnp.int32)


def _route(e_grid):
    return pl.pallas_call(
        _route_kernel,
        out_shape=[
            jax.ShapeDtypeStruct((ROUTE_ROWS, LANES), jnp.int32),
            jax.ShapeDtypeStruct((8, LANES), jnp.int32),
        ],
        compiler_params=pltpu.CompilerParams(vmem_limit_bytes=VMEM_LIMIT),
        name="moe_route",
    )(e_grid)


EXPERT_ROW_STEP = 64
WEIGHT_PRIORITY = 1
DISPATCH_TOKENS = 1376
DISPATCH_UNROLL = 8


def _dispatch_kernel(pos_ref, h_ref, x_hbm, sem):
    def body(g, carry):
        for u in range(DISPATCH_UNROLL):
            r = g * DISPATCH_UNROLL + u
            src = h_ref.at[pl.ds(pl.multiple_of(r * TOKEN_ROWS, TOKEN_ROWS), TOKEN_ROWS)]
            for j in range(2):
                row0 = pl.multiple_of(pos_ref[0, 0, 2 * r + j] * GATHER_PITCH, 8)
                pltpu.make_async_copy(src, x_hbm.at[pl.ds(row0, TOKEN_ROWS)], sem.at[0]).start()
        return carry
    lax.fori_loop(0, DISPATCH_TOKENS // DISPATCH_UNROLL, body, 0)

    n = DISPATCH_TOKENS * TOKEN_ROWS
    for _ in range(2):
        pltpu.make_async_copy(h_ref, x_hbm.at[pl.ds(0, n)], sem.at[0]).wait()


def _dispatch(h1_lin, pos):
    n_steps = LP // DISPATCH_TOKENS
    pos3 = pos.reshape(n_steps, 1, 2 * DISPATCH_TOKENS)
    return pl.pallas_call(
        _dispatch_kernel,
        grid=(n_steps,),
        in_specs=[
            pl.BlockSpec((1, 1, 2 * DISPATCH_TOKENS), lambda i: (i, 0, 0), memory_space=pltpu.SMEM),
            pl.BlockSpec((DISPATCH_TOKENS * TOKEN_ROWS, LANES), lambda i: (i, 0)),
        ],
        out_specs=pl.BlockSpec(memory_space=pl.ANY),
        out_shape=jax.ShapeDtypeStruct((SORTED_ROWS * GATHER_PITCH, LANES), F32),
        scratch_shapes=[pltpu.SemaphoreType.DMA((1,))],
        compiler_params=_cparams(("arbitrary",)),
        name="moe_dispatch",
    )(pos3, h1_lin)


def _expert_kernel(layer, tile_expert, tile_rows, n_tiles, x_ref, w1_hbm, w3_hbm, w2_hbm,
                   y_ref, w1f, w3f, w2f, w1b, w3b, w2b, wsem, wslot_ref):
    i = pl.program_id(0)
    n_used = n_tiles[0]

    def weight_copies(expert, ws):
        return [pltpu.make_async_copy(src.at[layer, expert], dst.at[ws], wsem.at[ws])
                for src, dst in ((w1_hbm, w1f), (w3_hbm, w3f), (w2_hbm, w2f))]

    last_tile = N_EXPERT_TILES - 1

    @pl.when(i == 0)
    def _():
        wslot_ref[0] = 0
        for cp in weight_copies(tile_expert[0], 0):
            cp.start(priority=WEIGHT_PRIORITY)

    @pl.when(i < n_used)
    def _():
        expert = tile_expert[i]
        first_of_expert = jnp.logical_or(i == 0, expert != tile_expert[jnp.maximum(i - 1, 0)])

        @pl.when(first_of_expert)
        def _():
            ws = wslot_ref[0]
            for cp in weight_copies(expert, ws):
                cp.wait()
            nxt = lax.while_loop(
                lambda j: jnp.logical_and(j < n_used, tile_expert[jnp.minimum(j, last_tile)] == expert),
                lambda j: j + 1, i + 1)

            @pl.when(nxt < n_used)
            def _():
                for cp in weight_copies(tile_expert[jnp.minimum(nxt, last_tile)], 1 - ws):
                    cp.start(priority=WEIGHT_PRIORITY)

            w1b[...] = w1f[ws].astype(BF16)
            w3b[...] = w3f[ws].astype(BF16)
            w2b[...] = w2f[ws].astype(BF16)
            wslot_ref[0] = 1 - ws

        rows_here = tile_rows[i]

        def run(m, masked):
            x = _load_gathered(x_ref, m)
            if masked:
                x = jnp.where(lax.broadcasted_iota(jnp.int32, (m, 1), 0) < rows_here, x, 0.0)
            x = x.astype(BF16)
            a = _dot(x, w1b[...])
            g = _dot(x, w3b[...])
            he = (a * _sigmoid(a) * g).astype(BF16)
            _store_token_linear(y_ref, _dot(he, w2b[...]))
            if m < TM_EXPERT:
                y_ref[m * TOKEN_ROWS:, :] = jnp.zeros(((TM_EXPERT - m) * TOKEN_ROWS, LANES), F32)

        @pl.when(rows_here == TM_EXPERT)
        def _():
            run(TM_EXPERT, False)

        for m in range(EXPERT_ROW_STEP, TM_EXPERT + 1, EXPERT_ROW_STEP):
            @pl.when(jnp.logical_and(rows_here > m - EXPERT_ROW_STEP, rows_here <= min(m, TM_EXPERT - 1)))
            def _(m=m):
                run(m, True)

    @pl.when(i >= n_used)
    def _():
        y_ref[...] = jnp.zeros(y_ref.shape, F32)


def _experts(x_sorted, w1, w3, w2, layer, tile_expert, tile_rows, n_tiles):
    def x_map(i, te, tr, nt):
        return (jnp.minimum(i, jnp.maximum(nt[0] - 1, 0)), 0)

    grid_spec = pltpu.PrefetchScalarGridSpec(
        num_scalar_prefetch=3,
        grid=(N_EXPERT_TILES,),
        in_specs=[
            pl.BlockSpec((TM_EXPERT * GATHER_PITCH, LANES), x_map),
            pl.BlockSpec(memory_space=pl.ANY),
            pl.BlockSpec(memory_space=pl.ANY),
            pl.BlockSpec(memory_space=pl.ANY),
        ],
        out_specs=pl.BlockSpec((TM_EXPERT * TOKEN_ROWS, LANES), lambda i, te, tr, nt: (i, 0)),
        scratch_shapes=[
            pltpu.VMEM((2, D_MODEL, D_EXPERT), F32),
            pltpu.VMEM((2, D_MODEL, D_EXPERT), F32),
            pltpu.VMEM((2, D_EXPERT, D_MODEL), F32),
            pltpu.VMEM((D_MODEL, D_EXPERT), BF16),
            pltpu.VMEM((D_MODEL, D_EXPERT), BF16),
            pltpu.VMEM((D_EXPERT, D_MODEL), BF16),
            pltpu.SemaphoreType.DMA((2,)),
            pltpu.SMEM((1,), jnp.int32),
        ],
    )
    return pl.pallas_call(
        functools.partial(_expert_kernel, layer),
        grid_spec=grid_spec,
        out_shape=jax.ShapeDtypeStruct((SORTED_ROWS * TOKEN_ROWS, LANES), F32),
        compiler_params=_cparams(("arbitrary",)),
        name="moe_experts",
    )(tile_expert, tile_rows, n_tiles, x_sorted, w1, w3, w2)


def _combine_kernel(tm, n_steps, final, pos_cur_ref, pos_next_ref, h1_ref, wts_ref, g_ref, b_ref, y_hbm,
                    *rest):
    out_refs, (ybuf, sem) = rest[:-2], rest[-2:]
    i = pl.program_id(0)
    slot = i % 2

    def start_gather(pos_ref, s):
        def body(r, carry):
            for j in range(2):
                row0 = pl.multiple_of(pos_ref[0, 0, 2 * r + j] * TOKEN_ROWS, TOKEN_ROWS)
                dst = ybuf.at[s, j, pl.ds(pl.multiple_of(r * GATHER_PITCH, 8), TOKEN_ROWS)]
                pltpu.make_async_copy(y_hbm.at[pl.ds(row0, TOKEN_ROWS)], dst, sem.at[s]).start()
            return carry
        lax.fori_loop(0, tm, body, 0, unroll=GATHER_UNROLL)

    @pl.when(i == 0)
    def _():
        start_gather(pos_cur_ref, 0)

    for j in range(2):
        n = tm * TOKEN_ROWS
        pltpu.make_async_copy(y_hbm.at[pl.ds(0, n)], ybuf.at[slot, j, pl.ds(0, n)], sem.at[slot]).wait()

    @pl.when(i + 1 < n_steps)
    def _():
        start_gather(pos_next_ref, 1 - slot)

    wts = wts_ref[...]
    y = wts[:, 0:1] * _load_gathered(ybuf.at[slot, 0], tm) + wts[:, 1:2] * _load_gathered(ybuf.at[slot, 1], tm)
    h2 = _layer_norm_rows(ALPHA * h1_ref[...] + y, g_ref[...], b_ref[...])
    if final:
        out_refs[0][...] = h2
    else:
        h2 = _zero_pad_rows(h2, i * tm)
        out_refs[0][...] = h2
        out_refs[1][...] = h2.astype(BF16)


def _combine(h1, y_sorted, pos, wts, g, b, final):
    tm = TM_FINAL if final else TM_LN
    rows = SEQ if final else LP
    n_steps = rows // tm
    pos3 = pos[:2 * rows].reshape(n_steps, 1, 2 * tm)
    row = lambda i: (i, 0)
    const2 = lambda i: (0, 0)
    if final:
        out_specs = [pl.BlockSpec((tm, D_MODEL), row)]
        out_shape = [jax.ShapeDtypeStruct((rows, D_MODEL), F32)]
    else:
        out_specs = [pl.BlockSpec((tm, D_MODEL), row), pl.BlockSpec((tm, D_MODEL), row)]
        out_shape = [jax.ShapeDtypeStruct((rows, D_MODEL), F32), jax.ShapeDtypeStruct((rows, D_MODEL), BF16)]
    return pl.pallas_call(
        functools.partial(_combine_kernel, tm, n_steps, final),
        grid=(n_steps,),
        in_specs=[
            pl.BlockSpec((1, 1, 2 * tm), lambda i: (i, 0, 0), memory_space=pltpu.SMEM),
            pl.BlockSpec((1, 1, 2 * tm), lambda i: (jnp.minimum(i + 1, n_steps - 1), 0, 0), memory_space=pltpu.SMEM),
            pl.BlockSpec((tm, D_MODEL), row),
            pl.BlockSpec((tm, LANES), row),
            pl.BlockSpec((1, D_MODEL), const2),
            pl.BlockSpec((1, D_MODEL), const2),
            pl.BlockSpec(memory_space=pl.ANY),
        ],
        out_specs=out_specs,
        out_shape=out_shape,
        scratch_shapes=[pltpu.VMEM((2, 2, tm * GATHER_PITCH, LANES), F32), pltpu.SemaphoreType.DMA((2,))],
        compiler_params=_cparams(("arbitrary",)),
        name="moe_combine_ln_final" if final else "moe_combine_ln",
    )(pos3, pos3, h1, wts, g, b, y_sorted)


def _pad_lanes(v):
    return jnp.pad(v, ((0, 0), (0, LANES - v.shape[1])))


def kernel(x, meta_tokens, emb_ln_g, emb_ln_b, hgrn_lb_logits, w_in, conv_w, conv_b, ig_b, fg_b,
           mlstm_norm_g, pool_w, pool_scale, hgrn_norm_g, w_out, ln1_g, ln1_b,
           w_router_group, b_router_group, w_router_expert, b_router_expert, w1, w3, w2,
           ln2_g, ln2_b):
    assert x.shape == (1, SEQ, D_MODEL) and x.dtype == F32
    row2 = lambda v: v.reshape(1, -1)
    meta_blk = jnp.pad(meta_tokens.astype(F32), ((META_PAD, 0), (0, 0)))
    h, hb = _embed(x.reshape(SEQ, D_MODEL), meta_blk, row2(emb_ln_g), row2(emb_ln_b))

    out = None
    for l in range(DEPTH):
        p = _in_proj(hb, w_in, l)
        y_pool = _pool(p, pool_w[l], row2(pool_scale[l]))
        gate_bias = _pad_lanes(jnp.concatenate([ig_b[l], fg_b[l]]).reshape(1, -1))
        y_m = _mlstm(p, conv_w[l], row2(conv_b[l]), gate_bias, row2(mlstm_norm_g[l]))
        y_h = _hgrn(p, hgrn_lb_logits, row2(hgrn_norm_g[l]), l)

        w_r = _pad_lanes(jnp.concatenate([w_router_group[l], w_router_expert[l]], axis=1))
        wr_hi, wr_lo = _split_bf16(w_r)
        b_r = _pad_lanes(jnp.concatenate([b_router_group[l], b_router_expert[l]]).reshape(1, -1))
        h1, h1_lin, eid, wts = _out_router(y_pool, y_m, y_h, h, w_out[l].astype(BF16), row2(ln1_g[l]),
                                           row2(ln1_b[l]), wr_hi, wr_lo, b_r)

        e_flat = jnp.pad(eid[:, 0:2].reshape(-1), (0, ROUTE_ROWS * LANES - N_ASSIGN), constant_values=-1)
        dest, tab = _route(e_flat.reshape(ROUTE_ROWS, LANES))
        pos = dest.reshape(-1)
        x_sorted = _dispatch(h1_lin, pos[:N_ASSIGN])
        y_sorted = _experts(x_sorted, w1, w3, w2, l, tab[0, :N_EXPERT_TILES], tab[2, :N_EXPERT_TILES], tab[1, :1])
        if l + 1 < DEPTH:
            h, hb = _combine(h1, y_sorted, pos, wts, row2(ln2_g[l]), row2(ln2_b[l]), final=False)
        else:
            (out,) = _combine(h1, y_sorted, pos, wts, row2(ln2_g[l]), row2(ln2_b[l]), final=True)
    return out.reshape(1, SEQ, D_MODEL)
```
